```python
import jax, jax.numpy as jnp
from jax import lax
import numpy as np

D_MODEL = 4096
BATCH = 2
SEQ = 8192
DEPTH = 2
DEC_BATCH = 32
DEC_SEQ = 64
PAST_LEN = 4096

CHUNK = 64
N_MEM = 256
EPS = 1e-6

ML_HEADS = 4
ML_DQK = 128
ML_DV = 256
ML_QK = ML_HEADS * ML_DQK
ML_WIDTH = ML_HEADS * ML_DV
ML_COLS = 2 * ML_QK + 2 * ML_WIDTH + 2 * ML_HEADS

RW_HEADS = 16
RW_N = 64
RW_WIDTH = RW_HEADS * RW_N
RW_W_RANK = 64
RW_A_RANK = 64
RW_G_RANK = 128
RW_COLS = 3 * RW_WIDTH + RW_W_RANK + RW_A_RANK + RW_G_RANK
RW_GN_EPS = 64e-5

GD_HEADS = 8
GD_DK = 128
GD_DV = 128
GD_QK = GD_HEADS * GD_DK
GD_WIDTH = GD_HEADS * GD_DV
GD_QKV = 2 * GD_QK + GD_WIDTH
GD_CONV = 4
GD_COLS = GD_QKV + GD_WIDTH + 2 * GD_HEADS

N_BRANCH = 3
BR_WIDTH = 1024
N_IN = ML_COLS + RW_COLS + GD_COLS + N_BRANCH * D_MODEL

CA_HEADS = 4
CA_HEAD_DIM = 256
CA_WIDTH = CA_HEADS * CA_HEAD_DIM

D_FF = 2 * D_MODEL
FFN_CONV = 3

kernel_name = "hybrid_stream_encoder_step"


def _offsets(sizes):
    return [int(s) for s in np.cumsum(sizes)]


def _rmsnorm(x, g):
    xf = x.astype(jnp.float32)
    y = xf * lax.rsqrt(jnp.mean(xf * xf, axis=-1, keepdims=True) + EPS)
    return (y * g.astype(jnp.float32)).astype(x.dtype)


def _l2norm(x):
    xf = x.astype(jnp.float32)
    return xf * lax.rsqrt(jnp.sum(xf * xf, axis=-1, keepdims=True) + 1e-6)


def _causal_dwconv(x, buf, w):
    T = x.shape[1]
    W = w.shape[0]
    xp = jnp.concatenate([buf.astype(x.dtype), x], axis=1)
    y = xp[:, 0:T] * w[0]
    for i in range(1, W):
        y = y + xp[:, i:i + T] * w[i]
    return y, xp[:, T:]


def _to_chunks(x, c):
    B, T = x.shape[:2]
    x = x.reshape((B, T // c, c) + x.shape[2:])
    return x.transpose((1, 0, 3, 2) + tuple(range(4, x.ndim)))


def _from_chunks(y):
    NC, B, H, c = y.shape[:4]
    y = y.transpose((1, 0, 3, 2) + tuple(range(4, y.ndim)))
    return y.reshape((B, NC * c, H) + y.shape[4:])


def _mlstm(q, k, v, ig, logf, C0, n0, m0, c):
    ar = jnp.arange(c)
    causal = ar[:, None] >= ar[None, :]

    def step(carry, xs):
        C, n, m = carry
        qc, kc, vc, ic, fc = xs
        b = jnp.cumsum(fc, axis=-1)
        dmat = jnp.where(causal, b[..., :, None] - b[..., None, :] + ic[..., None, :], -jnp.inf)
        mt = jnp.maximum(b + m[..., None], jnp.max(dmat, axis=-1))
        pmat = jnp.exp(dmat - mt[..., None]) * jnp.einsum('bhtd,bhsd->bhts', qc, kc)
        inter = jnp.exp(b + m[..., None] - mt)
        num = inter[..., None] * jnp.einsum('bhtd,bhde->bhte', qc, C) + jnp.einsum('bhts,bhse->bhte', pmat, vc)
        den = inter * jnp.einsum('bhtd,bhd->bht', qc, n) + jnp.sum(pmat, axis=-1)
        h = num / jnp.maximum(jnp.abs(den), jnp.exp(-mt))[..., None]
        m_new = mt[..., -1]
        wj = jnp.exp(b[..., -1:] - b + ic - m_new[..., None])
        dec = jnp.exp(b[..., -1] + m - m_new)
        C_new = dec[..., None, None] * C + jnp.einsum('bhs,bhsd,bhse->bhde', wj, kc, vc)
        n_new = dec[..., None] * n + jnp.einsum('bhs,bhsd->bhd', wj, kc)
        return (C_new, n_new, m_new), h

    xs = (_to_chunks(q, c), _to_chunks(k, c), _to_chunks(v, c), _to_chunks(ig, c), _to_chunks(logf, c))
    (C, n, m), h = lax.scan(step, (C0, n0, m0), xs)
    return _from_chunks(h), C, n, m


def _rwkv7(r, w, k, v, kk, a, S0):
    def step(S, xs):
        rt, wt, kt, vt, kkt, at = xs
        sa = jnp.einsum('bhij,bhj->bhi', S, -kkt)
        S = S * wt[:, :, None, :] + sa[..., None] * (kkt * at)[:, :, None, :] + vt[..., None] * kt[:, :, None, :]
        y = jnp.einsum('bhij,bhj->bhi', S, rt)
        return S, y

    xs = tuple(jnp.swapaxes(t, 0, 1) for t in (r, w, k, v, kk, a))
    S, y = lax.scan(step, S0, xs)
    return jnp.swapaxes(y, 0, 1), S


def _gated_delta(q, k, v, beta, g, S0, c):
    ar = jnp.arange(c)
    incl = ar[:, None] >= ar[None, :]
    strict = ar[:, None] > ar[None, :]
    eye = jnp.eye(c, dtype=jnp.float32)

    def step(S, xs):
        qc, kc, vc, bc, gch = xs
        gc = jnp.cumsum(gch, axis=-1)
        dec = jnp.exp(jnp.where(incl, gc[..., :, None] - gc[..., None, :], -jnp.inf))
        kkt = jnp.einsum('bhtd,bhsd->bhts', kc, kc)
        amat = eye + jnp.where(strict, bc[..., :, None] * kkt * dec, 0.0)
        rhs = bc[..., None] * (vc - jnp.exp(gc)[..., None] * jnp.einsum('bhtd,bhde->bhte', kc, S))
        u = lax.linalg.triangular_solve(amat, rhs, left_side=True, lower=True)
        qkt = jnp.einsum('bhtd,bhsd->bhts', qc, kc) * dec
        o = jnp.exp(gc)[..., None] * jnp.einsum('bhtd,bhde->bhte', qc, S) + jnp.einsum('bhts,bhse->bhte', qkt, u)
        glast = gc[..., -1]
        S_new = jnp.exp(glast)[..., None, None] * S + jnp.einsum(
            'bhsd,bhse->bhde', kc * jnp.exp(glast[..., None] - gc)[..., None], u)
        return S_new, o

    xs = (_to_chunks(q, c), _to_chunks(k, c), _to_chunks(v, c), _to_chunks(beta, c), _to_chunks(g, c))
    S, o = lax.scan(step, S0, xs)
    return _from_chunks(o), S


def _mem_kv(mem, p):
    B, M, _ = mem.shape
    kv = _rmsnorm(mem, p["g_mem"]) @ p["w_ca_kv"]
    k, v = jnp.split(kv, 2, axis=-1)
    return k.reshape(B, M, CA_HEADS, CA_HEAD_DIM), v.reshape(B, M, CA_HEADS, CA_HEAD_DIM)


def _zero_state(B, dt):
    f32 = jnp.float32
    return dict(
        ml_C=jnp.zeros((B, ML_HEADS, ML_DQK, ML_DV), f32),
        ml_n=jnp.zeros((B, ML_HEADS, ML_DQK), f32),
        ml_m=jnp.zeros((B, ML_HEADS), f32),
        rw_S=jnp.zeros((B, RW_HEADS, RW_N, RW_N), f32),
        rw_shift=jnp.zeros((B, RW_COLS), dt),
        gd_S=jnp.zeros((B, GD_HEADS, GD_DK, GD_DV), f32),
        gd_conv=jnp.zeros((B, GD_CONV - 1, GD_QKV), dt),
        ffn_conv=jnp.zeros((B, FFN_CONV - 1, 2 * D_FF), dt),
    )


def _layer(h, mem_k, mem_v, st, p):
    B, T, D = h.shape
    c = min(CHUNK, T)
    dt = h.dtype
    f32 = jnp.float32

    u = _rmsnorm(h, p["g_mix"])
    proj = u @ p["w_in"]
    ml_p, rw_p, gd_p, gate_p = jnp.split(proj, _offsets([ML_COLS, RW_COLS, GD_COLS]), axis=-1)

    mq, mk, mv, mo, mif = jnp.split(ml_p, _offsets([ML_QK, ML_QK, ML_WIDTH, ML_WIDTH]), axis=-1)
    mif = mif.astype(f32) + p["ml_b_if"]
    hm, ml_C, ml_n, ml_m = _mlstm(
        mq.reshape(B, T, ML_HEADS, ML_DQK).astype(f32),
        mk.reshape(B, T, ML_HEADS, ML_DQK).astype(f32) * ML_DQK ** -0.5,
        mv.reshape(B, T, ML_HEADS, ML_DV).astype(f32),
        mif[..., :ML_HEADS], jax.nn.log_sigmoid(mif[..., ML_HEADS:]),
        st["ml_C"].astype(f32), st["ml_n"].astype(f32), st["ml_m"].astype(f32), c)
    hm = hm * lax.rsqrt(jnp.mean(hm * hm, axis=-1, keepdims=True) + EPS)
    y_ml = hm.reshape(B, T, ML_WIDTH) * p["ml_norm"] * jax.nn.sigmoid(mo.astype(f32))

    prev = jnp.concatenate([st["rw_shift"][:, None].astype(dt), rw_p[:, :-1]], axis=1)
    rw_x = rw_p + (prev - rw_p) * p["rw_mu"]
    rw_shift = rw_p[:, -1]
    rr, rk, rv, xw, xa, xg = [t.astype(f32) for t in jnp.split(
        rw_x, _offsets([RW_WIDTH, RW_WIDTH, RW_WIDTH, RW_W_RANK, RW_A_RANK]), axis=-1)]
    w_pre = p["rw_w0"] + jnp.tanh(xw) @ p["rw_w2"]
    decay = jnp.exp(-jnp.exp(-jax.nn.softplus(-w_pre) - 0.5))
    a = jax.nn.sigmoid(p["rw_a0"] + xa @ p["rw_a2"])
    g_out = jax.nn.sigmoid(xg) @ p["rw_g2"]

    def heads(t):
        return t.reshape(B, T, RW_HEADS, RW_N)

    kk = _l2norm(heads(rk * p["rw_k_k"]))
    kmod = rk * (1.0 + (a - 1.0) * p["rw_k_a"])
    yr, rw_S = _rwkv7(heads(rr), heads(decay), heads(kmod), heads(rv), kk, heads(a), st["rw_S"].astype(f32))
    mu_ = jnp.mean(yr, axis=-1, keepdims=True)
    var = jnp.mean(jnp.square(yr - mu_), axis=-1, keepdims=True)
    yr = (yr - mu_) * lax.rsqrt(var + RW_GN_EPS)
    bonus = jnp.sum(heads(rr) * heads(kmod) * p["rw_r_k"].reshape(RW_HEADS, RW_N),
                    axis=-1, keepdims=True) * heads(rv)
    y_rw = (yr.reshape(B, T, RW_WIDTH) * p["rw_ln"] + bonus.reshape(B, T, RW_WIDTH)) * g_out

    g_qkv, g_z, g_b, g_a = jnp.split(gd_p, _offsets([GD_QKV, GD_WIDTH, GD_HEADS]), axis=-1)
    qkv, gd_conv = _causal_dwconv(g_qkv, st["gd_conv"], p["gd_conv_w"])
    qkv = jax.nn.silu(qkv.astype(f32))
    gq, gk, gv = jnp.split(qkv, _offsets([GD_QK, GD_QK]), axis=-1)
    gq = _l2norm(gq.reshape(B, T, GD_HEADS, GD_DK)) * GD_DK ** -0.5
    gk = _l2norm(gk.reshape(B, T, GD_HEADS, GD_DK))
    beta = jax.nn.sigmoid(g_b.astype(f32))
    g_log = -jnp.exp(p["gd_a_log"].astype(f32)) * jax.nn.softplus(g_a.astype(f32) + p["gd_dt_bias"])
    og, gd_S = _gated_delta(gq, gk, gv.reshape(B, T, GD_HEADS, GD_DV), beta, g_log,
                            st["gd_S"].astype(f32), c)
    og = og * lax.rsqrt(jnp.mean(og * og, axis=-1, keepdims=True) + EPS) * p["gd_norm"]
    y_gd = og.reshape(B, T, GD_WIDTH) * jax.nn.silu(g_z.astype(f32))

    parts = [jax.nn.sigmoid(gate_p[..., i * D:(i + 1) * D]) * (yb.astype(dt) @ p["w_branch"][i])
             for i, yb in enumerate((y_ml, y_rw, y_gd))]
    merged = parts[0] + parts[1] + parts[2]
    h = h + merged @ p["w_out"]

    u = _rmsnorm(h, p["g_ca"])
    q = (u @ p["w_ca_q"]).reshape(B, T, CA_HEADS, CA_HEAD_DIM)
    s = jnp.einsum('bthd,bshd->bhts', q, mem_k.astype(dt)).astype(f32) * CA_HEAD_DIM ** -0.5
    attn = jax.nn.softmax(s, axis=-1).astype(dt)
    o = jnp.einsum('bhts,bshd->bthd', attn, mem_v.astype(dt)).reshape(B, T, CA_WIDTH)
    h = h + o @ p["w_ca_o"]

    u = _rmsnorm(h, p["g_ffn"])
    up, ffn_conv = _causal_dwconv(u @ p["w_up"], st["ffn_conv"], p["ffn_conv_w"])
    fa, fg = jnp.split(up, 2, axis=-1)
    h = h + (jax.nn.silu(fg) * fa) @ p["w_down"]

    new_st = dict(ml_C=ml_C, ml_n=ml_n, ml_m=ml_m, rw_S=rw_S, rw_shift=rw_shift,
                  gd_S=gd_S, gd_conv=gd_conv, ffn_conv=ffn_conv)
    return h, new_st


def setup_inputs(seed: int = 0) -> dict:
    key = jax.random.key(seed)
    keys = jax.random.split(key, 64)
    counter = [0]
    f32 = jnp.float32

    def nk():
        counter[0] += 1
        return keys[counter[0] - 1]

    def nrm(shape, scale=1.0):
        return jax.random.normal(nk(), shape, f32) * scale

    def gain(shape):
        return 1.0 + 0.05 * jax.random.normal(nk(), shape, f32)

    def unif(shape, lo, hi):
        return jax.random.uniform(nk(), shape, f32, lo, hi)

    L, D = DEPTH, D_MODEL
    inp = {}
    inp["x_prompt"] = nrm((BATCH, SEQ, D))
    inp["x_sample"] = nrm((DEC_BATCH, DEC_SEQ, D))
    inp["cache_mem_k"] = nrm((L, DEC_BATCH, N_MEM, CA_HEADS, CA_HEAD_DIM))
    inp["cache_mem_v"] = nrm((L, DEC_BATCH, N_MEM, CA_HEADS, CA_HEAD_DIM))
    inp["state_mlstm_C"] = nrm((L, DEC_BATCH, ML_HEADS, ML_DQK, ML_DV), 0.1)
    inp["state_mlstm_n"] = nrm((L, DEC_BATCH, ML_HEADS, ML_DQK), 0.1)
    inp["state_mlstm_m"] = nrm((L, DEC_BATCH, ML_HEADS))
    inp["state_rwkv_S"] = nrm((L, DEC_BATCH, RW_HEADS, RW_N, RW_N), 0.5)
    inp["state_rwkv_shift"] = nrm((L, DEC_BATCH, RW_COLS))
    inp["state_gdn_S"] = nrm((L, DEC_BATCH, GD_HEADS, GD_DK, GD_DV), 0.3)
    inp["state_gdn_conv"] = nrm((L, DEC_BATCH, GD_CONV - 1, GD_QKV))
    inp["state_ffn_conv"] = nrm((L, DEC_BATCH, FFN_CONV - 1, 2 * D_FF))
    inp["mem_prompt"] = nrm((BATCH, N_MEM, D))
    inp["g_mix"] = gain((L, D))
    inp["w_in"] = nrm((L, D, N_IN), D ** -0.5)
    inp["ml_b_if"] = jnp.concatenate(
        [nrm((L, ML_HEADS), 0.1), jnp.linspace(3.0, 6.0, ML_HEADS)[None] + nrm((L, ML_HEADS), 0.1)], axis=-1)
    inp["ml_norm"] = gain((L, ML_WIDTH))
    inp["rw_mu"] = unif((L, RW_COLS), 0.0, 1.0)
    inp["rw_w0"] = unif((L, RW_WIDTH), -5.0, -0.5)
    inp["rw_w2"] = nrm((L, RW_W_RANK, RW_WIDTH), 0.1)
    inp["rw_a0"] = nrm((L, RW_WIDTH), 0.1)
    inp["rw_a2"] = nrm((L, RW_A_RANK, RW_WIDTH), 0.1)
    inp["rw_g2"] = nrm((L, RW_G_RANK, RW_WIDTH), RW_G_RANK ** -0.5)
    inp["rw_k_k"] = 0.85 + nrm((L, RW_WIDTH), 0.05)
    inp["rw_k_a"] = gain((L, RW_WIDTH))
    inp["rw_r_k"] = nrm((L, RW_WIDTH), 0.1)
    inp["rw_ln"] = gain((L, RW_WIDTH))
    inp["gd_conv_w"] = nrm((L, GD_CONV, GD_QKV), GD_CONV ** -0.5)
    inp["gd_a_log"] = jnp.log(unif((L, GD_HEADS), 1.0, 16.0))
    inp["gd_dt_bias"] = jnp.log(jnp.expm1(unif((L, GD_HEADS), 0.001, 0.1)))
    inp["gd_norm"] = gain((L, GD_DV))
    inp["w_branch"] = nrm((L, N_BRANCH, BR_WIDTH, D), BR_WIDTH ** -0.5)
    inp["w_out"] = nrm((L, D, D), D ** -0.5)
    inp["g_ca"] = gain((L, D))
    inp["g_mem"] = gain((L, D))
    inp["w_ca_q"] = nrm((L, D, CA_WIDTH), D ** -0.5)
    inp["w_ca_kv"] = nrm((L, D, 2 * CA_WIDTH), D ** -0.5)
    inp["w_ca_o"] = nrm((L, CA_WIDTH, D), CA_WIDTH ** -0.5)
    inp["g_ffn"] = gain((L, D))
    inp["w_up"] = nrm((L, D, 2 * D_FF), D ** -0.5)
    inp["ffn_conv_w"] = nrm((L, FFN_CONV, 2 * D_FF), FFN_CONV ** -0.5)
    inp["w_down"] = nrm((L, D_FF, D), D_FF ** -0.5)
    inp["g_final"] = gain((D,))
    return inp


def reference(x_prompt, x_sample, cache_mem_k, cache_mem_v, state_mlstm_C, state_mlstm_n, state_mlstm_m,
              state_rwkv_S, state_rwkv_shift, state_gdn_S, state_gdn_conv, state_ffn_conv, mem_prompt,
              g_mix, w_in, ml_b_if, ml_norm, rw_mu, rw_w0, rw_w2, rw_a0, rw_a2, rw_g2, rw_k_k, rw_k_a,
              rw_r_k, rw_ln, gd_conv_w, gd_a_log, gd_dt_bias, gd_norm, w_branch, w_out, g_ca, g_mem,
              w_ca_q, w_ca_kv, w_ca_o, g_ffn, w_up, ffn_conv_w, w_down, g_final):
    keys = ("ml_C", "ml_n", "ml_m", "rw_S", "rw_shift", "gd_S", "gd_conv", "ffn_conv")
    new_p = {k: [] for k in keys}
    new_s = {k: [] for k in keys}
    mem_k_list, mem_v_list = [], []
    hp, hs = x_prompt, x_sample
    for l in range(DEPTH):
        p = dict(g_mix=g_mix[l], w_in=w_in[l], ml_b_if=ml_b_if[l], ml_norm=ml_norm[l], rw_mu=rw_mu[l],
                 rw_w0=rw_w0[l], rw_w2=rw_w2[l], rw_a0=rw_a0[l], rw_a2=rw_a2[l], rw_g2=rw_g2[l],
                 rw_k_k=rw_k_k[l], rw_k_a=rw_k_a[l], rw_r_k=rw_r_k[l], rw_ln=rw_ln[l],
                 gd_conv_w=gd_conv_w[l], gd_a_log=gd_a_log[l], gd_dt_bias=gd_dt_bias[l], gd_norm=gd_norm[l],
                 w_branch=w_branch[l], w_out=w_out[l], g_ca=g_ca[l], g_mem=g_mem[l], w_ca_q=w_ca_q[l],
                 w_ca_kv=w_ca_kv[l], w_ca_o=w_ca_o[l], g_ffn=g_ffn[l], w_up=w_up[l],
                 ffn_conv_w=ffn_conv_w[l], w_down=w_down[l])
        mk, mv = _mem_kv(mem_prompt, p)
        hp, stp = _layer(hp, mk, mv, _zero_state(hp.shape[0], hp.dtype), p)
        mem_k_list.append(mk)
        mem_v_list.append(mv)
        st_in = dict(ml_C=state_mlstm_C[l], ml_n=state_mlstm_n[l], ml_m=state_mlstm_m[l],
                     rw_S=state_rwkv_S[l], rw_shift=state_rwkv_shift[l], gd_S=state_gdn_S[l],
                     gd_conv=state_gdn_conv[l], ffn_conv=state_ffn_conv[l])
        hs, sts = _layer(hs, cache_mem_k[l], cache_mem_v[l], st_in, p)
        for k in keys:
            new_p[k].append(stp[k])
            new_s[k].append(sts[k])
    y_prompt = _rmsnorm(hp, g_final)
    y_sample = _rmsnorm(hs, g_final)
    mem_k_p = jnp.stack(mem_k_list)
    mem_v_p = jnp.stack(mem_v_list)
    mlstm_C_p = jnp.stack(new_p["ml_C"])
    mlstm_n_p = jnp.stack(new_p["ml_n"])
    mlstm_m_p = jnp.stack(new_p["ml_m"])
    rwkv_S_p = jnp.stack(new_p["rw_S"])
    rwkv_shift_p = jnp.stack(new_p["rw_shift"])
    gdn_S_p = jnp.stack(new_p["gd_S"])
    gdn_conv_p = jnp.stack(new_p["gd_conv"])
    ffn_conv_p = jnp.stack(new_p["ffn_conv"])
    mlstm_C_s = jnp.stack(new_s["ml_C"])
    mlstm_n_s = jnp.stack(new_s["ml_n"])
    mlstm_m_s = jnp.stack(new_s["ml_m"])
    rwkv_S_s = jnp.stack(new_s["rw_S"])
    rwkv_shift_s = jnp.stack(new_s["rw_shift"])
    gdn_S_s = jnp.stack(new_s["gd_S"])
    gdn_conv_s = jnp.stack(new_s["gd_conv"])
    ffn_conv_s = jnp.stack(new_s["ffn_conv"])
    return (y_prompt, y_sample, mem_k_p, mem_v_p, mlstm_C_p, mlstm_n_p, mlstm_m_p, rwkv_S_p, rwkv_shift_p,
            gdn_S_p, gdn_conv_p, ffn_conv_p, mlstm_C_s, mlstm_n_s, mlstm_m_s, rwkv_S_s, rwkv_shift_s,
            gdn_S_s, gdn_conv_s, ffn_conv_s)
```

```python
import functools

import jax
import jax.numpy as jnp
from jax import lax
from jax.experimental import pallas as pl
from jax.experimental.pallas import tpu as pltpu

F32 = jnp.float32
BF16 = jnp.bfloat16

EPS = 1e-6
CHUNK = 64

ML_HEADS, ML_DQK, ML_DV = 4, 128, 256
ML_QK = ML_HEADS * ML_DQK
ML_WIDTH = ML_HEADS * ML_DV
ML_MAIN = 2 * ML_QK + 2 * ML_WIDTH

RW_HEADS, RW_N = 16, 64
RW_WIDTH = RW_HEADS * RW_N
RW_W_RANK, RW_A_RANK, RW_G_RANK = 64, 64, 128
RW_COLS = 3 * RW_WIDTH + RW_W_RANK + RW_A_RANK + RW_G_RANK
RW_GN_EPS = 64e-5

GD_HEADS, GD_DK, GD_DV = 8, 128, 128
GD_QK = GD_HEADS * GD_DK
GD_WIDTH = GD_HEADS * GD_DV
GD_QKV = 2 * GD_QK + GD_WIDTH
GD_CONV = 4
GD_MAIN = GD_QKV + GD_WIDTH

BR_WIDTH = 1024
CA_HEADS, CA_HEAD_DIM = 4, 256
CA_WIDTH = CA_HEADS * CA_HEAD_DIM
FFN_CONV = 3

SMALL_W = 128
SM_ML_I, SM_ML_F, SM_GD_B, SM_GD_A = 0, ML_HEADS, 2 * ML_HEADS, 2 * ML_HEADS + GD_HEADS

V7X_VMEM_LIMIT = 56 * 1024 * 1024
TRI_BLOCK = 16


def _cparams(sem):
    return pltpu.CompilerParams(dimension_semantics=sem, vmem_limit_bytes=V7X_VMEM_LIMIT)


def _tile(dim, pref, quantum):
    if dim <= pref:
        return dim
    t = (pref // quantum) * quantum
    while t > quantum and dim % t:
        t -= quantum
    assert dim % t == 0, (dim, pref, quantum)
    return t


def _split2(a):
    hi = a.astype(BF16)
    lo = (a - hi.astype(F32)).astype(BF16)
    return hi, lo


def _dg(a, b, ca, cb):
    return lax.dot_general(a, b, (((ca,), (cb,)), ((), ())), preferred_element_type=F32)


def _dot3(a, b, ca=1, cb=0):
    ah, al = _split2(a)
    bh, bl = _split2(b)
    return _dg(ah, bh, ca, cb) + (_dg(al, bh, ca, cb) + _dg(ah, bl, ca, cb))


def _dot_nt(a, b):
    return _dot3(a, b, 1, 1)


def _dot_tn(a, b):
    return _dot3(a.T, b, 1, 0)


def _tri_masks(c):
    row = lax.broadcasted_iota(jnp.int32, (c, c), 0)
    col = lax.broadcasted_iota(jnp.int32, (c, c), 1)
    return row, col


def _cumsum_rows(x):
    c = x.shape[0]
    row, col = _tri_masks(c)
    tri = jnp.where(row >= col, 1.0, 0.0).astype(BF16)
    x0 = x.astype(BF16)
    r1 = x - x0.astype(F32)
    x1 = r1.astype(BF16)
    x2 = (r1 - x1.astype(F32)).astype(BF16)
    return _dg(tri, x0, 1, 0) + (_dg(tri, x1, 1, 0) + _dg(tri, x2, 1, 0))


def _tri_solve(low, rhs):
    c = low.shape[0]
    row, col = _tri_masks(c)
    same = (row // TRI_BLOCK) == (col // TRI_BLOCK)
    eye = jnp.where(row == col, 1.0, 0.0).astype(F32)
    n = jnp.where(same, -low, 0.0)
    off = jnp.where(same, 0.0, low)
    x = eye + n
    p = n
    steps = 1
    while steps * 2 < TRI_BLOCK:
        p = _dot3(p, p)
        x = x + _dot3(x, p)
        steps *= 2
    m = _dot3(x, off)
    u = _dot3(x, rhs)
    nblk = c // TRI_BLOCK
    terms = []
    pw = m
    k = 1
    while k < nblk:
        terms.append(pw)
        k *= 2
        if k < nblk:
            pw = _dot3(pw, pw)
    for i in range(len(terms) - 1, 0, -1):
        u = u + _dot3(terms[i], u)
    if terms:
        u = u - _dot3(terms[0], u)
    return u


def _softplus(x):
    return jnp.maximum(x, 0.0) + jnp.log1p(jnp.exp(-jnp.abs(x)))


def _sigmoid(x):
    return 1.0 / (1.0 + jnp.exp(-x))


def _silu(x):
    return x * _sigmoid(x)


def _shift_rows(x, prev8, k):
    xr = pltpu.roll(x, k, 0)
    pr = pltpu.roll(prev8, k, 0)
    row = lax.broadcasted_iota(jnp.int32, (8, x.shape[1]), 0)
    head = jnp.where(row < k, pr, xr[0:8])
    if x.shape[0] == 8:
        return head
    return jnp.concatenate([head, xr[8:]], axis=0)


def _rmsnorm_kernel(x_ref, g_ref, o_ref):
    x = x_ref[...]
    y = x * lax.rsqrt(jnp.mean(x * x, axis=-1, keepdims=True) + EPS)
    o_ref[...] = (y * g_ref[...]).astype(o_ref.dtype)


def _rmsnorm(x, g, out_dtype):
    m, d = x.shape
    tr = _tile(m, 256, 8)
    return pl.pallas_call(
        _rmsnorm_kernel,
        grid=(m // tr,),
        in_specs=[pl.BlockSpec((tr, d), lambda i: (i, 0)), pl.BlockSpec((1, d), lambda i: (0, 0))],
        out_specs=pl.BlockSpec((tr, d), lambda i: (i, 0)),
        out_shape=jax.ShapeDtypeStruct((m, d), out_dtype),
        compiler_params=_cparams(("parallel",)),
        name="rmsnorm",
    )(x, g.reshape(1, d))


def _mm_kernel(a_ref, w_ref, o_ref):
    o_ref[...] = jnp.dot(a_ref[...], w_ref[...], preferred_element_type=F32).astype(o_ref.dtype)


def _mm_res_kernel(a_ref, w_ref, r_ref, o_ref):
    o_ref[...] = r_ref[...] + jnp.dot(a_ref[...], w_ref[...], preferred_element_type=F32)


def _mm_tiles(m, k, n):
    tm = _tile(m, 1024 if k <= 4096 else 512, 8)
    tn = _tile(n, 512, 128)
    return tm, tn


def _matmul(a, w, residual=None, out_dtype=F32):
    m, k = a.shape
    n = w.shape[1]
    tm, tn = _mm_tiles(m, k, n)
    in_specs = [pl.BlockSpec((tm, k), lambda i, j: (i, 0)), pl.BlockSpec((k, tn), lambda i, j: (0, j))]
    args = [a, w]
    body = _mm_kernel
    if residual is not None:
        in_specs.append(pl.BlockSpec((tm, tn), lambda i, j: (i, j)))
        args.append(residual)
        body = _mm_res_kernel
    return pl.pallas_call(
        body,
        grid=(m // tm, n // tn),
        in_specs=in_specs,
        out_specs=pl.BlockSpec((tm, tn), lambda i, j: (i, j)),
        out_shape=jax.ShapeDtypeStruct((m, n), out_dtype),
        compiler_params=_cparams(("parallel", "arbitrary")),
        name="matmul",
    )(*args)


def _mlstm_kernel(q_ref, k_ref, v_ref, og_ref, sm_ref, bias_ref, norm_ref, c0_ref, n0_ref, m0_ref,
                  y_ref, c_ref, n_ref, m_ref):
    @pl.when(pl.program_id(1) == 0)
    def _():
        c_ref[...] = c0_ref[...]
        n_ref[...] = n0_ref[...]
        m_ref[...] = m0_ref[...]

    c = q_ref.shape[0]
    pre = sm_ref[...] + bias_ref[...]
    logf = -_softplus(-pre)
    cum = _cumsum_rows(logf)
    pre_t = pre.T
    cum_t = cum.T
    row, col = _tri_masks(c)
    causal = row >= col
    for h in range(ML_HEADS):
        q = q_ref[:, h * ML_DQK:(h + 1) * ML_DQK]
        k = k_ref[:, h * ML_DQK:(h + 1) * ML_DQK] * (ML_DQK ** -0.5)
        v = v_ref[:, h * ML_DV:(h + 1) * ML_DV]
        fi = SM_ML_F + h
        ii = SM_ML_I + h
        b_col = cum[:, fi:fi + 1]
        b_row = cum_t[fi:fi + 1, :]
        i_col = pre[:, ii:ii + 1]
        i_row = pre_t[ii:ii + 1, :]
        m_old = m_ref[0, :, h:h + 1]
        c_old = c_ref[0, h]
        n_old = n_ref[0, h:h + 1, :]

        dmat = jnp.where(causal, b_col - b_row + i_row, -jnp.inf)
        mt = jnp.maximum(b_col + m_old, jnp.max(dmat, axis=-1, keepdims=True))
        pmat = jnp.exp(dmat - mt) * _dot_nt(q, k)
        inter = jnp.exp(b_col + m_old - mt)
        num = inter * _dot3(q, c_old) + _dot3(pmat, v)
        den = inter * jnp.sum(q * n_old, axis=-1, keepdims=True) + jnp.sum(pmat, axis=-1, keepdims=True)
        hh = num / jnp.maximum(jnp.abs(den), jnp.exp(-mt))

        m_new = mt[c - 1:c, :]
        b_last = b_col[c - 1:c, :]
        wj = jnp.exp(b_last - b_col + i_col - m_new)
        dec = jnp.exp(b_last + m_old - m_new)
        kw = k * wj
        c_ref[0, h] = dec * c_old + _dot_tn(kw, v)
        n_ref[0, h:h + 1, :] = dec * n_old + jnp.sum(kw, axis=0, keepdims=True)
        m_ref[0, :, h:h + 1] = m_new

        hn = hh * lax.rsqrt(jnp.mean(hh * hh, axis=-1, keepdims=True) + EPS)
        og = og_ref[:, h * ML_DV:(h + 1) * ML_DV]
        y = hn * norm_ref[:, h * ML_DV:(h + 1) * ML_DV] * _sigmoid(og)
        y_ref[:, h * ML_DV:(h + 1) * ML_DV] = y.astype(y_ref.dtype)


def _mlstm(p_main, p_small, bias_row, ml_norm, c0, n0, m0, bsz, t):
    nc = t // CHUNK
    rows = bsz * t
    rmap = lambda b, c: b * nc + c
    return pl.pallas_call(
        _mlstm_kernel,
        grid=(bsz, nc),
        in_specs=[
            pl.BlockSpec((CHUNK, ML_QK), lambda b, c: (rmap(b, c), 0)),
            pl.BlockSpec((CHUNK, ML_QK), lambda b, c: (rmap(b, c), 1)),
            pl.BlockSpec((CHUNK, ML_WIDTH), lambda b, c: (rmap(b, c), 1)),
            pl.BlockSpec((CHUNK, ML_WIDTH), lambda b, c: (rmap(b, c), 2)),
            pl.BlockSpec((CHUNK, SMALL_W), lambda b, c: (rmap(b, c), 0)),
            pl.BlockSpec((1, SMALL_W), lambda b, c: (0, 0)),
            pl.BlockSpec((1, ML_WIDTH), lambda b, c: (0, 0)),
            pl.BlockSpec((1, ML_HEADS, ML_DQK, ML_DV), lambda b, c: (b, 0, 0, 0)),
            pl.BlockSpec((1, ML_HEADS, ML_DQK), lambda b, c: (b, 0, 0)),
            pl.BlockSpec((1, 1, ML_HEADS), lambda b, c: (b, 0, 0)),
        ],
        out_specs=[
            pl.BlockSpec((CHUNK, ML_WIDTH), lambda b, c: (rmap(b, c), 0)),
            pl.BlockSpec((1, ML_HEADS, ML_DQK, ML_DV), lambda b, c: (b, 0, 0, 0)),
            pl.BlockSpec((1, ML_HEADS, ML_DQK), lambda b, c: (b, 0, 0)),
            pl.BlockSpec((1, 1, ML_HEADS), lambda b, c: (b, 0, 0)),
        ],
        out_shape=[
            jax.ShapeDtypeStruct((rows, ML_WIDTH), BF16),
            jax.ShapeDtypeStruct(c0.shape, F32),
            jax.ShapeDtypeStruct(n0.shape, F32),
            jax.ShapeDtypeStruct((bsz, 1, ML_HEADS), F32),
        ],
        compiler_params=_cparams(("parallel", "arbitrary")),
        name="mlstm",
    )(p_main, p_main, p_main, p_main, p_small, bias_row, ml_norm.reshape(1, ML_WIDTH),
      c0, n0, m0.reshape(bsz, 1, ML_HEADS))


def _rwkv_kernel(p_ref, shift_ref, mu_ref, w0_ref, w2_ref, a0_ref, a2_ref, g2_ref, kk_ref, ka_ref,
                 rk_ref, ln_ref, s0_ref, y_ref, s_ref, prev_ref):
    @pl.when(pl.program_id(1) == 0)
    def _():
        s_ref[...] = s0_ref[...]
        prev_ref[...] = jnp.broadcast_to(shift_ref[0], prev_ref.shape)

    c = p_ref.shape[0]
    w = RW_WIDTH
    x = p_ref[...]
    xprev = _shift_rows(x, prev_ref[...], 1)
    prev_ref[...] = x[c - 8:c]
    xm = x + (xprev - x) * mu_ref[...]
    rr = xm[:, 0:w]
    rk = xm[:, w:2 * w]
    rv = xm[:, 2 * w:3 * w]
    xw = xm[:, 3 * w:3 * w + RW_W_RANK]
    xa = xm[:, 3 * w + RW_W_RANK:3 * w + RW_W_RANK + RW_A_RANK]
    xg = xm[:, 3 * w + RW_W_RANK + RW_A_RANK:]

    w_pre = w0_ref[...] + _dot3(jnp.tanh(xw), w2_ref[...])
    lw = -jnp.exp(-_softplus(-w_pre) - 0.5)
    a = _sigmoid(a0_ref[...] + _dot3(xa, a2_ref[...]))
    g_out = _dot3(_sigmoid(xg), g2_ref[...])
    kk_raw = rk * kk_ref[...]
    kmod = rk * (1.0 + (a - 1.0) * ka_ref[...])
    bon = rr * kmod * rk_ref[...]

    lc = _cumsum_rows(lw)
    l_end = lc[c - 1:c, :]
    l_mid = lc[c // 2 - 1:c // 2, :]
    lcc = lc - l_mid
    p_mid = jnp.exp(l_mid)
    p_in = jnp.exp(lcc)
    p_prev = jnp.exp(lcc - lw)
    p_inv = jnp.exp(-lcc)
    p_end = jnp.exp(l_end - lc)
    p_all = jnp.exp(l_end)

    row, col = _tri_masks(c)
    strict = row > col
    incl = row >= col
    for h in range(RW_HEADS):
        sl = slice(h * RW_N, (h + 1) * RW_N)
        kk = kk_raw[:, sl]
        kk = kk * lax.rsqrt(jnp.sum(kk * kk, axis=-1, keepdims=True) + 1e-6)
        bb = kk * a[:, sl]
        r_h = rr[:, sl]
        k_h = kmod[:, sl]
        v_h = rv[:, sl]
        kkp = kk * p_prev[:, sl]
        rp = r_h * p_in[:, sl]
        kd = k_h * p_inv[:, sl]
        bd = bb * p_inv[:, sl]
        s_old = s_ref[0, h]
        s_mid = s_old * p_mid[:, sl]

        a_k = jnp.where(strict, _dot_nt(kkp, kd), 0.0)
        a_b = jnp.where(strict, _dot_nt(kkp, bd), 0.0)
        r_k = jnp.where(incl, _dot_nt(rp, kd), 0.0)
        r_b = jnp.where(incl, _dot_nt(rp, bd), 0.0)
        rhs = _dot_nt(kkp, s_mid) + _dot3(a_k, v_h)
        u = _tri_solve(a_b, rhs)
        yh = _dot_nt(rp, s_mid) + _dot3(r_k, v_h) - _dot3(r_b, u)
        s_ref[0, h] = (s_old * p_all[:, sl] + _dot_tn(v_h, k_h * p_end[:, sl])
                       - _dot_tn(u, bb * p_end[:, sl]))

        mu_ = jnp.mean(yh, axis=-1, keepdims=True)
        yc = yh - mu_
        var = jnp.mean(yc * yc, axis=-1, keepdims=True)
        yn = yc * lax.rsqrt(var + RW_GN_EPS)
        bonus = jnp.sum(bon[:, sl], axis=-1, keepdims=True) * v_h
        out = (yn * ln_ref[:, sl] + bonus) * g_out[:, sl]
        y_ref[:, sl] = out.astype(y_ref.dtype)


def _rwkv(p_rw, shift0, s0, p, bsz, t):
    nc = t // CHUNK
    rows = bsz * t
    rmap = lambda b, c: b * nc + c
    full = lambda shape: pl.BlockSpec(shape, lambda b, c: (0,) * len(shape))
    w = RW_WIDTH
    return pl.pallas_call(
        _rwkv_kernel,
        grid=(bsz, nc),
        in_specs=[
            pl.BlockSpec((CHUNK, RW_COLS), lambda b, c: (rmap(b, c), 0)),
            pl.BlockSpec((1, 1, RW_COLS), lambda b, c: (b, 0, 0)),
            full((1, RW_COLS)), full((1, w)), full((RW_W_RANK, w)), full((1, w)), full((RW_A_RANK, w)),
            full((RW_G_RANK, w)), full((1, w)), full((1, w)), full((1, w)), full((1, w)),
            pl.BlockSpec((1, RW_HEADS, RW_N, RW_N), lambda b, c: (b, 0, 0, 0)),
        ],
        out_specs=[
            pl.BlockSpec((CHUNK, w), lambda b, c: (rmap(b, c), 0)),
            pl.BlockSpec((1, RW_HEADS, RW_N, RW_N), lambda b, c: (b, 0, 0, 0)),
        ],
        out_shape=[jax.ShapeDtypeStruct((rows, w), BF16), jax.ShapeDtypeStruct(s0.shape, F32)],
        scratch_shapes=[pltpu.VMEM((8, RW_COLS), F32)],
        compiler_params=_cparams(("parallel", "arbitrary")),
        name="rwkv7",
    )(p_rw, shift0.reshape(bsz, 1, RW_COLS), p["rw_mu"].reshape(1, RW_COLS), p["rw_w0"].reshape(1, w),
      p["rw_w2"], p["rw_a0"].reshape(1, w), p["rw_a2"], p["rw_g2"], p["rw_k_k"].reshape(1, w),
      p["rw_k_a"].reshape(1, w), p["rw_r_k"].reshape(1, w), p["rw_ln"].reshape(1, w), s0)


def _gdn_kernel(q_ref, k_ref, v_ref, z_ref, sm_ref, cw_ref, conv0_ref, alog_ref, dtb_ref, norm_ref, s0_ref,
                y_ref, s_ref, prev_ref):
    @pl.when(pl.program_id(1) == 0)
    def _():
        s_ref[...] = s0_ref[...]
        prev_ref[...] = jnp.zeros(prev_ref.shape, F32)
        prev_ref[8 - (GD_CONV - 1):8, :] = conv0_ref[0]

    c = q_ref.shape[0]
    acts = []
    for sec, ref in enumerate((q_ref, k_ref, v_ref)):
        cs = slice(sec * GD_QK, (sec + 1) * GD_QK)
        x = ref[...]
        prev8 = prev_ref[:, cs]
        y = _shift_rows(x, prev8, GD_CONV - 1) * cw_ref[0:1, cs]
        for i in range(1, GD_CONV - 1):
            y = y + _shift_rows(x, prev8, GD_CONV - 1 - i) * cw_ref[i:i + 1, cs]
        y = y + x * cw_ref[GD_CONV - 1:GD_CONV, cs]
        prev_ref[:, cs] = x[c - 8:c]
        acts.append(_silu(y))
    qa, ka, va = acts

    sm = sm_ref[...]
    beta = _sigmoid(sm)
    g = -jnp.exp(alog_ref[...]) * _softplus(sm + dtb_ref[...])
    gc = _cumsum_rows(g)
    gc_t = gc.T
    row, col = _tri_masks(c)
    strict = row > col
    incl = row >= col
    for h in range(GD_HEADS):
        ks = slice(h * GD_DK, (h + 1) * GD_DK)
        vs = slice(h * GD_DV, (h + 1) * GD_DV)
        q = qa[:, ks]
        q = q * lax.rsqrt(jnp.sum(q * q, axis=-1, keepdims=True) + 1e-6) * (GD_DK ** -0.5)
        k = ka[:, ks]
        k = k * lax.rsqrt(jnp.sum(k * k, axis=-1, keepdims=True) + 1e-6)
        v = va[:, vs]
        bi = SM_GD_B + h
        ai = SM_GD_A + h
        b_col = beta[:, bi:bi + 1]
        g_col = gc[:, ai:ai + 1]
        g_row = gc_t[ai:ai + 1, :]
        s_old = s_ref[0, h]

        dec = jnp.exp(jnp.where(incl, g_col - g_row, -jnp.inf))
        low = jnp.where(strict, b_col * _dot_nt(k, k) * dec, 0.0)
        eg = jnp.exp(g_col)
        rhs = b_col * (v - eg * _dot3(k, s_old))
        u = _tri_solve(low, rhs)
        qkt = _dot_nt(q, k) * dec
        o = eg * _dot3(q, s_old) + _dot3(qkt, u)
        g_last = g_col[c - 1:c, :]
        s_ref[0, h] = jnp.exp(g_last) * s_old + _dot_tn(k * jnp.exp(g_last - g_col), u)

        og = o * lax.rsqrt(jnp.mean(o * o, axis=-1, keepdims=True) + EPS) * norm_ref[...]
        y = og * _silu(z_ref[:, vs])
        y_ref[:, vs] = y.astype(y_ref.dtype)


def _gdn(p_main, p_small, conv_w, conv0, alog_row, dtb_row, gd_norm, s0, bsz, t):
    nc = t // CHUNK
    rows = bsz * t
    rmap = lambda b, c: b * nc + c
    base = ML_MAIN // GD_QK
    assert ML_MAIN % GD_QK == 0
    full = lambda shape: pl.BlockSpec(shape, lambda b, c: (0,) * len(shape))
    return pl.pallas_call(
        _gdn_kernel,
        grid=(bsz, nc),
        in_specs=[
            pl.BlockSpec((CHUNK, GD_QK), lambda b, c: (rmap(b, c), base)),
            pl.BlockSpec((CHUNK, GD_QK), lambda b, c: (rmap(b, c), base + 1)),
            pl.BlockSpec((CHUNK, GD_WIDTH), lambda b, c: (rmap(b, c), base + 2)),
            pl.BlockSpec((CHUNK, GD_WIDTH), lambda b, c: (rmap(b, c), base + 3)),
            pl.BlockSpec((CHUNK, SMALL_W), lambda b, c: (rmap(b, c), 0)),
            full((GD_CONV, GD_QKV)),
            pl.BlockSpec((1, GD_CONV - 1, GD_QKV), lambda b, c: (b, 0, 0)),
            full((1, SMALL_W)), full((1, SMALL_W)), full((1, GD_DV)),
            pl.BlockSpec((1, GD_HEADS, GD_DK, GD_DV), lambda b, c: (b, 0, 0, 0)),
        ],
        out_specs=[
            pl.BlockSpec((CHUNK, GD_WIDTH), lambda b, c: (rmap(b, c), 0)),
            pl.BlockSpec((1, GD_HEADS, GD_DK, GD_DV), lambda b, c: (b, 0, 0, 0)),
        ],
        out_shape=[jax.ShapeDtypeStruct((rows, GD_WIDTH), BF16), jax.ShapeDtypeStruct(s0.shape, F32)],
        scratch_shapes=[pltpu.VMEM((8, GD_QKV), F32)],
        compiler_params=_cparams(("parallel", "arbitrary")),
        name="gdn",
    )(p_main, p_main, p_main, p_main, p_small, conv_w, conv0, alog_row, dtb_row, gd_norm.reshape(1, GD_DV), s0)


def _merge_kernel(y0_ref, y1_ref, y2_ref, w_ref, g0_ref, g1_ref, g2_ref, o_ref):
    acc = _sigmoid(g0_ref[...]) * jnp.dot(y0_ref[...], w_ref[0], preferred_element_type=F32)
    acc = acc + _sigmoid(g1_ref[...]) * jnp.dot(y1_ref[...], w_ref[1], preferred_element_type=F32)
    acc = acc + _sigmoid(g2_ref[...]) * jnp.dot(y2_ref[...], w_ref[2], preferred_element_type=F32)
    o_ref[...] = acc.astype(o_ref.dtype)


def _merge(ys, w_branch, p_main, d):
    m = ys[0].shape[0]
    tm = _tile(m, 512, 8)
    tn = _tile(d, 512, 128)
    gate0 = ML_MAIN + GD_MAIN
    assert gate0 % tn == 0
    gb = gate0 // tn
    nb = d // tn
    yspec = pl.BlockSpec((tm, BR_WIDTH), lambda i, j: (i, 0))
    gspec = lambda b: pl.BlockSpec((tm, tn), lambda i, j: (i, gb + b * nb + j))
    return pl.pallas_call(
        _merge_kernel,
        grid=(m // tm, nb),
        in_specs=[yspec, yspec, yspec, pl.BlockSpec((3, BR_WIDTH, tn), lambda i, j: (0, 0, j)),
                  gspec(0), gspec(1), gspec(2)],
        out_specs=pl.BlockSpec((tm, tn), lambda i, j: (i, j)),
        out_shape=jax.ShapeDtypeStruct((m, d), BF16),
        compiler_params=_cparams(("parallel", "arbitrary")),
        name="merge",
    )(ys[0], ys[1], ys[2], w_branch, p_main, p_main, p_main)


def _attn_kernel(q_ref, k_ref, v_ref, o_ref):
    for h in range(CA_HEADS):
        hs = slice(h * CA_HEAD_DIM, (h + 1) * CA_HEAD_DIM)
        s = _dg(q_ref[:, hs], k_ref[0, :, hs], 1, 1) * (CA_HEAD_DIM ** -0.5)
        s = s - jnp.max(s, axis=-1, keepdims=True)
        e = jnp.exp(s)
        pr = e / jnp.sum(e, axis=-1, keepdims=True)
        o = jnp.dot(pr.astype(BF16), v_ref[0, :, hs], preferred_element_type=F32)
        o_ref[:, hs] = o.astype(o_ref.dtype)


def _attention(q, mem_k, mem_v, bsz, t):
    tq = _tile(t, 512, 8)
    nt = t // tq
    n_mem = mem_k.shape[1]
    return pl.pallas_call(
        _attn_kernel,
        grid=(bsz, nt),
        in_specs=[
            pl.BlockSpec((tq, CA_WIDTH), lambda b, i: (b * nt + i, 0)),
            pl.BlockSpec((1, n_mem, CA_WIDTH), lambda b, i: (b, 0, 0)),
            pl.BlockSpec((1, n_mem, CA_WIDTH), lambda b, i: (b, 0, 0)),
        ],
        out_specs=pl.BlockSpec((tq, CA_WIDTH), lambda b, i: (b * nt + i, 0)),
        out_shape=jax.ShapeDtypeStruct((bsz * t, CA_WIDTH), BF16),
        compiler_params=_cparams(("parallel", "arbitrary")),
        name="mem_attention",
    )(q, mem_k, mem_v)


def _ffn_conv(x, prev, w):
    row = lax.broadcasted_iota(jnp.int32, x.shape, 0)
    x1 = jnp.where(row == 0, prev[1:2], pltpu.roll(x, 1, 0))
    x2 = jnp.where(row == 0, prev[0:1], jnp.where(row == 1, prev[1:2], pltpu.roll(x, 2, 0)))
    return x2 * w[0:1] + x1 * w[1:2] + x * w[2:3]


def _ffn_act_kernel(a_ref, g_ref, pa_ref, pg_ref, wa_ref, wg_ref, o_ref):
    fa = _ffn_conv(a_ref[...], pa_ref[0, 0], wa_ref[...])
    fg = _ffn_conv(g_ref[...], pg_ref[0, 0], wg_ref[...])
    o_ref[...] = (_silu(fg) * fa).astype(o_ref.dtype)


def _ffn_act(up, conv0, conv_w, bsz, t):
    c2 = up.shape[1]
    f = c2 // 2
    tr = _tile(t, 512, 8)
    nt = t // tr
    tc = _tile(f, 1024, 128)
    nj = f // tc
    tails = up.reshape(bsz, nt, tr, c2)[:, :nt - 1, tr - (FFN_CONV - 1):, :]
    prev = jnp.concatenate([conv0[:, None], tails], axis=1)
    return pl.pallas_call(
        _ffn_act_kernel,
        grid=(bsz, nt, nj),
        in_specs=[
            pl.BlockSpec((tr, tc), lambda b, i, j: (b * nt + i, j)),
            pl.BlockSpec((tr, tc), lambda b, i, j: (b * nt + i, nj + j)),
            pl.BlockSpec((1, 1, FFN_CONV - 1, tc), lambda b, i, j: (b, i, 0, j)),
            pl.BlockSpec((1, 1, FFN_CONV - 1, tc), lambda b, i, j: (b, i, 0, nj + j)),
            pl.BlockSpec((FFN_CONV, tc), lambda b, i, j: (0, j)),
            pl.BlockSpec((FFN_CONV, tc), lambda b, i, j: (0, nj + j)),
        ],
        out_specs=pl.BlockSpec((tr, tc), lambda b, i, j: (b * nt + i, j)),
        out_shape=jax.ShapeDtypeStruct((bsz * t, f), BF16),
        compiler_params=_cparams(("parallel", "parallel", "arbitrary")),
        name="ffn_conv_act",
    )(up, up, prev, prev, conv_w, conv_w)


def _prep_layer(p, d):
    w_in = p["w_in"]
    o_ml = 0
    o_if = ML_MAIN
    o_rw = o_if + 2 * ML_HEADS
    o_gd = o_rw + RW_COLS
    o_ba = o_gd + GD_MAIN
    o_gate = o_ba + 2 * GD_HEADS
    w_main = jnp.concatenate(
        [w_in[:, o_ml:o_ml + ML_MAIN], w_in[:, o_gd:o_gd + GD_MAIN], w_in[:, o_gate:]], axis=1).astype(BF16)
    w_rw = w_in[:, o_rw:o_rw + RW_COLS].astype(BF16)
    n_small = 2 * ML_HEADS + 2 * GD_HEADS
    w_small = jnp.concatenate(
        [w_in[:, o_if:o_if + 2 * ML_HEADS], w_in[:, o_ba:o_ba + 2 * GD_HEADS],
         jnp.zeros((d, SMALL_W - n_small), F32)], axis=1).astype(BF16)
    zrow = jnp.zeros((SMALL_W,), F32)
    q = dict(p)
    q.update(
        w_main=w_main, w_rw=w_rw, w_small=w_small,
        ml_bias_row=zrow.at[SM_ML_I:SM_ML_I + 2 * ML_HEADS].set(p["ml_b_if"]).reshape(1, SMALL_W),
        gd_alog_row=zrow.at[SM_GD_A:SM_GD_A + GD_HEADS].set(p["gd_a_log"]).reshape(1, SMALL_W),
        gd_dtb_row=zrow.at[SM_GD_A:SM_GD_A + GD_HEADS].set(p["gd_dt_bias"]).reshape(1, SMALL_W),
        w_branch_b=p["w_branch"].astype(BF16), w_out_b=p["w_out"].astype(BF16),
        w_ca_q_b=p["w_ca_q"].astype(BF16), w_ca_kv_b=p["w_ca_kv"].astype(BF16),
        w_ca_o_b=p["w_ca_o"].astype(BF16), w_up_b=p["w_up"].astype(BF16), w_down_b=p["w_down"].astype(BF16))
    return q


def _layer(h, mem_k, mem_v, st, p, bsz, t):
    d = h.shape[1]
    u = _rmsnorm(h, p["g_mix"], BF16)
    p_main = _matmul(u, p["w_main"])
    p_rw = _matmul(u, p["w_rw"])
    p_small = _matmul(u, p["w_small"])

    y_ml, ml_c, ml_n, ml_m = _mlstm(p_main, p_small, p["ml_bias_row"], p["ml_norm"],
                                    st["ml_C"], st["ml_n"], st["ml_m"], bsz, t)
    y_rw, rw_s = _rwkv(p_rw, st["rw_shift"], st["rw_S"], p, bsz, t)
    y_gd, gd_s = _gdn(p_main, p_small, p["gd_conv_w"], st["gd_conv"], p["gd_alog_row"], p["gd_dtb_row"],
                      p["gd_norm"], st["gd_S"], bsz, t)
    merged = _merge((y_ml, y_rw, y_gd), p["w_branch_b"], p_main, d)
    h = _matmul(merged, p["w_out_b"], residual=h)

    u = _rmsnorm(h, p["g_ca"], BF16)
    q = _matmul(u, p["w_ca_q_b"], out_dtype=BF16)
    o = _attention(q, mem_k, mem_v, bsz, t)
    h = _matmul(o, p["w_ca_o_b"], residual=h)

    u = _rmsnorm(h, p["g_ffn"], BF16)
    up = _matmul(u, p["w_up_b"])
    act = _ffn_act(up, st["ffn_conv"], p["ffn_conv_w"], bsz, t)
    h = _matmul(act, p["w_down_b"], residual=h)

    gd0 = ML_MAIN
    new_st = dict(
        ml_C=ml_c, ml_n=ml_n, ml_m=ml_m.reshape(bsz, ML_HEADS), rw_S=rw_s,
        rw_shift=p_rw.reshape(bsz, t, RW_COLS)[:, t - 1],
        gd_S=gd_s,
        gd_conv=p_main.reshape(bsz, t, -1)[:, t - (GD_CONV - 1):, gd0:gd0 + GD_QKV],
        ffn_conv=up.reshape(bsz, t, -1)[:, t - (FFN_CONV - 1):])
    return h, new_st


def _zero_state(bsz, d_ff2):
    return dict(
        ml_C=jnp.zeros((bsz, ML_HEADS, ML_DQK, ML_DV), F32), ml_n=jnp.zeros((bsz, ML_HEADS, ML_DQK), F32),
        ml_m=jnp.zeros((bsz, ML_HEADS), F32), rw_S=jnp.zeros((bsz, RW_HEADS, RW_N, RW_N), F32),
        rw_shift=jnp.zeros((bsz, RW_COLS), F32), gd_S=jnp.zeros((bsz, GD_HEADS, GD_DK, GD_DV), F32),
        gd_conv=jnp.zeros((bsz, GD_CONV - 1, GD_QKV), F32), ffn_conv=jnp.zeros((bsz, FFN_CONV - 1, d_ff2), F32))


def kernel(x_prompt, x_sample, cache_mem_k, cache_mem_v, state_mlstm_C, state_mlstm_n, state_mlstm_m, state_rwkv_S, state_rwkv_shift, state_gdn_S, state_gdn_conv, state_ffn_conv, mem_prompt, g_mix, w_in, ml_b_if, ml_norm, rw_mu, rw_w0, rw_w2, rw_a0, rw_a2, rw_g2, rw_k_k, rw_k_a, rw_r_k, rw_ln, gd_conv_w, gd_a_log, gd_dt_bias, gd_norm, w_branch, w_out, g_ca, g_mem, w_ca_q, w_ca_kv, w_ca_o, g_ffn, w_up, ffn_conv_w, w_down, g_final):
    bp, tp, d = x_prompt.shape
    bs, ts, _ = x_sample.shape
    depth = w_in.shape[0]
    n_mem = mem_prompt.shape[1]
    assert tp % CHUNK == 0 and ts % CHUNK == 0
    stacked = dict(g_mix=g_mix, w_in=w_in, ml_b_if=ml_b_if, ml_norm=ml_norm, rw_mu=rw_mu, rw_w0=rw_w0,
                   rw_w2=rw_w2, rw_a0=rw_a0, rw_a2=rw_a2, rw_g2=rw_g2, rw_k_k=rw_k_k, rw_k_a=rw_k_a,
                   rw_r_k=rw_r_k, rw_ln=rw_ln, gd_conv_w=gd_conv_w, gd_a_log=gd_a_log, gd_dt_bias=gd_dt_bias,
                   gd_norm=gd_norm, w_branch=w_branch, w_out=w_out, g_ca=g_ca, g_mem=g_mem, w_ca_q=w_ca_q,
                   w_ca_kv=w_ca_kv, w_ca_o=w_ca_o, g_ffn=g_ffn, w_up=w_up, ffn_conv_w=ffn_conv_w,
                   w_down=w_down)
    keys = ("ml_C", "ml_n", "ml_m", "rw_S", "rw_shift", "gd_S", "gd_conv", "ffn_conv")
    new_p = {k: [] for k in keys}
    new_s = {k: [] for k in keys}
    mem_k_list, mem_v_list = [], []
    hp = x_prompt.reshape(bp * tp, d)
    hs = x_sample.reshape(bs * ts, d)
    mem2d = mem_prompt.reshape(bp * n_mem, d)
    for l in range(depth):
        p = _prep_layer({k: v[l] for k, v in stacked.items()}, d)
        kv = _matmul(_rmsnorm(mem2d, p["g_mem"], BF16), p["w_ca_kv_b"])
        mk = kv[:, :CA_WIDTH].reshape(bp, n_mem, CA_WIDTH)
        mv = kv[:, CA_WIDTH:].reshape(bp, n_mem, CA_WIDTH)
        hp, stp = _layer(hp, mk.astype(BF16), mv.astype(BF16), _zero_state(bp, w_up.shape[2]), p, bp, tp)
        mem_k_list.append(mk.reshape(bp, n_mem, CA_HEADS, CA_HEAD_DIM))
        mem_v_list.append(mv.reshape(bp, n_mem, CA_HEADS, CA_HEAD_DIM))
        st_in = dict(ml_C=state_mlstm_C[l], ml_n=state_mlstm_n[l], ml_m=state_mlstm_m[l],
                     rw_S=state_rwkv_S[l], rw_shift=state_rwkv_shift[l], gd_S=state_gdn_S[l],
                     gd_conv=state_gdn_conv[l], ffn_conv=state_ffn_conv[l])
        ck = cache_mem_k[l].reshape(bs, n_mem, CA_WIDTH).astype(BF16)
        cv = cache_mem_v[l].reshape(bs, n_mem, CA_WIDTH).astype(BF16)
        hs, sts = _layer(hs, ck, cv, st_in, p, bs, ts)
        for k in keys:
            new_p[k].append(stp[k])
            new_s[k].append(sts[k])
    y_prompt = _rmsnorm(hp, g_final, F32).reshape(bp, tp, d)
    y_sample = _rmsnorm(hs, g_final, F32).reshape(bs, ts, d)
    outs = [y_prompt, y_sample, jnp.stack(mem_k_list), jnp.stack(mem_v_list)]
    outs += [jnp.stack(new_p[k]) for k in keys]
    outs += [jnp.stack(new_s[k]) for k in keys]
    return tuple(outs)
```

```python
import functools

import jax
import jax.numpy as jnp
from jax import lax
from jax.experimental import pallas as pl
from jax.experimental.pallas import tpu as pltpu

F32 = jnp.float32
BF16 = jnp.bfloat16

EPS = 1e-6
CHUNK = 64

ML_HEADS, ML_DQK, ML_DV = 4, 128, 256
ML_QK = ML_HEADS * ML_DQK
ML_WIDTH = ML_HEADS * ML_DV
ML_MAIN = 2 * ML_QK + 2 * ML_WIDTH
ML_PASSES = 3

RW_HEADS, RW_N = 16, 64
RW_WIDTH = RW_HEADS * RW_N
RW_W_RANK, RW_A_RANK, RW_G_RANK = 64, 64, 128
RW_COLS = 3 * RW_WIDTH + RW_W_RANK + RW_A_RANK + RW_G_RANK
RW_GN_EPS = 64e-5
RW_GROUP = 4
RW_PASSES = 1

GD_HEADS, GD_DK, GD_DV = 8, 128, 128
GD_QK = GD_HEADS * GD_DK
GD_WIDTH = GD_HEADS * GD_DV
GD_QKV = 2 * GD_QK + GD_WIDTH
GD_CONV = 4
GD_MAIN = GD_QKV + GD_WIDTH
GD_GROUP = 4
GD_PASSES = 1

BR_WIDTH = 1024
CA_HEADS, CA_HEAD_DIM = 4, 256
CA_WIDTH = CA_HEADS * CA_HEAD_DIM
FFN_CONV = 3

SMALL_W = 128
SM_ML_I, SM_ML_F, SM_GD_B, SM_GD_A = 0, ML_HEADS, 2 * ML_HEADS, 2 * ML_HEADS + GD_HEADS

V7X_VMEM_LIMIT = 56 * 1024 * 1024
TRI_BLOCK = 16


def _cparams(sem):
    return pltpu.CompilerParams(dimension_semantics=sem, vmem_limit_bytes=V7X_VMEM_LIMIT)


def _tile(dim, pref, quantum):
    if dim <= pref:
        return dim
    t = (pref // quantum) * quantum
    while t > quantum and dim % t:
        t -= quantum
    assert dim % t == 0, (dim, pref, quantum)
    return t


def _split2(a):
    hi = a.astype(BF16)
    lo = (a - hi.astype(F32)).astype(BF16)
    return hi, lo


def _dg(a, b, ca, cb):
    return lax.dot_general(a, b, (((ca,), (cb,)), ((), ())), preferred_element_type=F32)


def _dot3(a, b, ca=1, cb=0):
    ah, al = _split2(a)
    bh, bl = _split2(b)
    return _dg(ah, bh, ca, cb) + (_dg(al, bh, ca, cb) + _dg(ah, bl, ca, cb))


def _dotp(a, b, ca, cb, passes):
    if passes == 1:
        return _dg(a.astype(BF16), b.astype(BF16), ca, cb)
    return _dot3(a, b, ca, cb)


def _tri_masks(c):
    row = lax.broadcasted_iota(jnp.int32, (c, c), 0)
    col = lax.broadcasted_iota(jnp.int32, (c, c), 1)
    return row, col


def _cumsum_rows(x):
    c = x.shape[0]
    row, col = _tri_masks(c)
    tri = jnp.where(row >= col, 1.0, 0.0).astype(BF16)
    x0 = x.astype(BF16)
    r1 = x - x0.astype(F32)
    x1 = r1.astype(BF16)
    x2 = (r1 - x1.astype(F32)).astype(BF16)
    return _dg(tri, x0, 1, 0) + (_dg(tri, x1, 1, 0) + _dg(tri, x2, 1, 0))


def _tri_solve(low, rhs, c, passes):
    row, col = _tri_masks(low.shape[0])
    same = (row // TRI_BLOCK) == (col // TRI_BLOCK)
    eye = jnp.where(row == col, 1.0, 0.0).astype(F32)
    n = jnp.where(same, -low, 0.0)
    off = jnp.where(same, 0.0, low)
    x = eye + n
    p = n
    steps = 1
    while steps * 2 < TRI_BLOCK:
        p = _dotp(p, p, 1, 0, passes)
        x = x + _dotp(x, p, 1, 0, passes)
        steps *= 2
    m = _dotp(x, off, 1, 0, passes)
    u = _dotp(x, rhs, 1, 0, passes)
    nblk = c // TRI_BLOCK
    terms = []
    pw = m
    k = 1
    while k < nblk:
        terms.append(pw)
        k *= 2
        if k < nblk:
            pw = _dotp(pw, pw, 1, 0, passes)
    for i in range(len(terms) - 1, 0, -1):
        u = u + _dotp(terms[i], u, 1, 0, passes)
    if terms:
        u = u - _dotp(terms[0], u, 1, 0, passes)
    return u


def _head_sums(x, hw):
    assert 2 * hw == 128
    lane = lax.broadcasted_iota(jnp.int32, (x.shape[0], 128), 1)
    lo = lane < hw
    out = []
    for s in range(x.shape[1] // 128):
        xs = x[:, s * 128:(s + 1) * 128]
        s_lo = jnp.sum(jnp.where(lo, xs, 0.0), axis=-1, keepdims=True)
        s_hi = jnp.sum(jnp.where(lo, 0.0, xs), axis=-1, keepdims=True)
        out.append(jnp.where(lo, s_lo, s_hi))
    return jnp.concatenate(out, axis=1)


def _softplus(x):
    return jnp.maximum(x, 0.0) + jnp.log1p(jnp.exp(-jnp.abs(x)))


def _sigmoid(x):
    return 1.0 / (1.0 + jnp.exp(-x))


def _silu(x):
    return x * _sigmoid(x)


def _shift_rows(x, prev8, k):
    xr = pltpu.roll(x, k, 0)
    pr = pltpu.roll(prev8, k, 0)
    row = lax.broadcasted_iota(jnp.int32, (8, x.shape[1]), 0)
    head = jnp.where(row < k, pr, xr[0:8])
    if x.shape[0] == 8:
        return head
    return jnp.concatenate([head, xr[8:]], axis=0)


def _rmsnorm_kernel(x_ref, g_ref, o_ref):
    x = x_ref[...]
    y = x * lax.rsqrt(jnp.mean(x * x, axis=-1, keepdims=True) + EPS)
    o_ref[...] = (y * g_ref[...]).astype(o_ref.dtype)


def _rmsnorm(x, g, out_dtype):
    m, d = x.shape
    tr = _tile(m, 256, 8)
    return pl.pallas_call(
        _rmsnorm_kernel,
        grid=(m // tr,),
        in_specs=[pl.BlockSpec((tr, d), lambda i: (i, 0)), pl.BlockSpec((1, d), lambda i: (0, 0))],
        out_specs=pl.BlockSpec((tr, d), lambda i: (i, 0)),
        out_shape=jax.ShapeDtypeStruct((m, d), out_dtype),
        compiler_params=_cparams(("parallel",)),
        name="rmsnorm",
    )(x, g.reshape(1, d))


def _mm_kernel(a_ref, w_ref, o_ref):
    o_ref[...] = jnp.dot(a_ref[...], w_ref[...], preferred_element_type=F32).astype(o_ref.dtype)


def _mm_res_kernel(a_ref, w_ref, r_ref, o_ref):
    o_ref[...] = r_ref[...] + jnp.dot(a_ref[...], w_ref[...], preferred_element_type=F32)


def _mm_tiles(m, k, n):
    tm = _tile(m, 1024 if k <= 4096 else 512, 8)
    tn = _tile(n, 512, 128)
    return tm, tn


def _matmul(a, w, residual=None, out_dtype=F32):
    m, k = a.shape
    n = w.shape[1]
    tm, tn = _mm_tiles(m, k, n)
    in_specs = [pl.BlockSpec((tm, k), lambda i, j: (i, 0)), pl.BlockSpec((k, tn), lambda i, j: (0, j))]
    args = [a, w]
    body = _mm_kernel
    if residual is not None:
        in_specs.append(pl.BlockSpec((tm, tn), lambda i, j: (i, j)))
        args.append(residual)
        body = _mm_res_kernel
    return pl.pallas_call(
        body,
        grid=(m // tm, n // tn),
        in_specs=in_specs,
        out_specs=pl.BlockSpec((tm, tn), lambda i, j: (i, j)),
        out_shape=jax.ShapeDtypeStruct((m, n), out_dtype),
        compiler_params=_cparams(("parallel", "arbitrary")),
        name="matmul",
    )(*args)


def _mlstm_kernel(q_ref, k_ref, v_ref, og_ref, sm_ref, bias_ref, norm_ref, c0_ref, n0_ref, m0_ref,
                  y_ref, c_ref, n_ref, m_ref):
    @pl.when(pl.program_id(1) == 0)
    def _():
        c_ref[...] = c0_ref[...]
        n_ref[...] = n0_ref[...]
        m_ref[...] = m0_ref[...]

    c = q_ref.shape[0]
    pre = sm_ref[...] + bias_ref[...]
    logf = -_softplus(-pre)
    cum = _cumsum_rows(logf)
    pre_t = pre.T
    cum_t = cum.T
    row, col = _tri_masks(c)
    causal = row >= col
    for h in range(ML_HEADS):
        q = q_ref[:, h * ML_DQK:(h + 1) * ML_DQK]
        k = k_ref[:, h * ML_DQK:(h + 1) * ML_DQK] * (ML_DQK ** -0.5)
        v = v_ref[:, h * ML_DV:(h + 1) * ML_DV]
        fi = SM_ML_F + h
        ii = SM_ML_I + h
        b_col = cum[:, fi:fi + 1]
        b_row = cum_t[fi:fi + 1, :]
        i_col = pre[:, ii:ii + 1]
        i_row = pre_t[ii:ii + 1, :]
        m_old = m_ref[0, :, h:h + 1]
        c_old = c_ref[0, h]
        n_old = n_ref[0, h:h + 1, :]

        dmat = jnp.where(causal, b_col - b_row + i_row, -jnp.inf)
        mt = jnp.maximum(b_col + m_old, jnp.max(dmat, axis=-1, keepdims=True))
        pmat = jnp.exp(dmat - mt) * _dotp(q, k, 1, 1, ML_PASSES)
        inter = jnp.exp(b_col + m_old - mt)
        num = inter * _dotp(q, c_old, 1, 0, ML_PASSES) + _dotp(pmat, v, 1, 0, ML_PASSES)
        den = inter * jnp.sum(q * n_old, axis=-1, keepdims=True) + jnp.sum(pmat, axis=-1, keepdims=True)
        hh = num / jnp.maximum(jnp.abs(den), jnp.exp(-mt))

        m_new = mt[c - 1:c, :]
        b_last = b_col[c - 1:c, :]
        wj = jnp.exp(b_last - b_col + i_col - m_new)
        dec = jnp.exp(b_last + m_old - m_new)
        kw = k * wj
        c_ref[0, h] = dec * c_old + _dotp(kw.T, v, 1, 0, ML_PASSES)
        n_ref[0, h:h + 1, :] = dec * n_old + jnp.sum(kw, axis=0, keepdims=True)
        m_ref[0, :, h:h + 1] = m_new

        hn = hh * lax.rsqrt(jnp.mean(hh * hh, axis=-1, keepdims=True) + EPS)
        og = og_ref[:, h * ML_DV:(h + 1) * ML_DV]
        y = hn * norm_ref[:, h * ML_DV:(h + 1) * ML_DV] * _sigmoid(og)
        y_ref[:, h * ML_DV:(h + 1) * ML_DV] = y.astype(y_ref.dtype)


def _mlstm(p_main, p_small, bias_row, ml_norm, c0, n0, m0, bsz, t):
    nc = t // CHUNK
    rows = bsz * t
    rmap = lambda b, c: b * nc + c
    return pl.pallas_call(
        _mlstm_kernel,
        grid=(bsz, nc),
        in_specs=[
            pl.BlockSpec((CHUNK, ML_QK), lambda b, c: (rmap(b, c), 0)),
            pl.BlockSpec((CHUNK, ML_QK), lambda b, c: (rmap(b, c), 1)),
            pl.BlockSpec((CHUNK, ML_WIDTH), lambda b, c: (rmap(b, c), 1)),
            pl.BlockSpec((CHUNK, ML_WIDTH), lambda b, c: (rmap(b, c), 2)),
            pl.BlockSpec((CHUNK, SMALL_W), lambda b, c: (rmap(b, c), 0)),
            pl.BlockSpec((1, SMALL_W), lambda b, c: (0, 0)),
            pl.BlockSpec((1, ML_WIDTH), lambda b, c: (0, 0)),
            pl.BlockSpec((1, ML_HEADS, ML_DQK, ML_DV), lambda b, c: (b, 0, 0, 0)),
            pl.BlockSpec((1, ML_HEADS, ML_DQK), lambda b, c: (b, 0, 0)),
            pl.BlockSpec((1, 1, ML_HEADS), lambda b, c: (b, 0, 0)),
        ],
        out_specs=[
            pl.BlockSpec((CHUNK, ML_WIDTH), lambda b, c: (rmap(b, c), 0)),
            pl.BlockSpec((1, ML_HEADS, ML_DQK, ML_DV), lambda b, c: (b, 0, 0, 0)),
            pl.BlockSpec((1, ML_HEADS, ML_DQK), lambda b, c: (b, 0, 0)),
            pl.BlockSpec((1, 1, ML_HEADS), lambda b, c: (b, 0, 0)),
        ],
        out_shape=[
            jax.ShapeDtypeStruct((rows, ML_WIDTH), BF16),
            jax.ShapeDtypeStruct(c0.shape, F32),
            jax.ShapeDtypeStruct(n0.shape, F32),
            jax.ShapeDtypeStruct((bsz, 1, ML_HEADS), F32),
        ],
        compiler_params=_cparams(("parallel", "arbitrary")),
        name="mlstm",
    )(p_main, p_main, p_main, p_main, p_small, bias_row, ml_norm.reshape(1, ML_WIDTH),
      c0, n0, m0.reshape(bsz, 1, ML_HEADS))


def _rwkv_kernel(p_ref, shift_ref, mu_ref, w0_ref, w2_ref, a0_ref, a2_ref, g2_ref, kk_ref, ka_ref,
                 rk_ref, ln_ref, s0_ref, y_ref, s_ref, prev_ref):
    @pl.when(pl.program_id(1) == 0)
    def _():
        s_ref[...] = s0_ref[...]
        prev_ref[...] = jnp.broadcast_to(shift_ref[0], prev_ref.shape)

    c = p_ref.shape[0]
    w = RW_WIDTH
    x = p_ref[...]
    xprev = _shift_rows(x, prev_ref[...], 1)
    prev_ref[...] = x[c - 8:c]
    xm = x + (xprev - x) * mu_ref[...]
    rr = xm[:, 0:w]
    rk = xm[:, w:2 * w]
    rv = xm[:, 2 * w:3 * w]
    xw = xm[:, 3 * w:3 * w + RW_W_RANK]
    xa = xm[:, 3 * w + RW_W_RANK:3 * w + RW_W_RANK + RW_A_RANK]
    xg = xm[:, 3 * w + RW_W_RANK + RW_A_RANK:]

    w_pre = w0_ref[...] + _dot3(jnp.tanh(xw), w2_ref[...])
    lw = -jnp.exp(-_softplus(-w_pre) - 0.5)
    a = _sigmoid(a0_ref[...] + _dot3(xa, a2_ref[...]))
    g_out = _dot3(_sigmoid(xg), g2_ref[...])
    kk_raw = rk * kk_ref[...]
    kmod = rk * (1.0 + (a - 1.0) * ka_ref[...])
    bon = rr * kmod * rk_ref[...]

    lc = _cumsum_rows(lw)
    l_end = lc[c - 1:c, :]
    l_mid = lc[c // 2 - 1:c // 2, :]
    lcc = lc - l_mid
    p_mid = jnp.exp(l_mid)
    p_in = jnp.exp(lcc)
    p_prev = jnp.exp(lcc - lw)
    p_inv = jnp.exp(-lcc)
    p_end = jnp.exp(l_end - lc)
    p_all = jnp.exp(l_end)

    kkn = kk_raw * lax.rsqrt(_head_sums(kk_raw * kk_raw, RW_N) + 1e-6)
    bb = kkn * a
    kkp = kkn * p_prev
    rp = rr * p_in
    kd = kmod * p_inv
    bd = bb * p_inv
    k_end = kmod * p_end
    b_end = bb * p_end

    gw = RW_GROUP * RW_N
    gr = RW_GROUP * c
    r4 = lax.broadcasted_iota(jnp.int32, (gr, gw), 0)
    c4 = lax.broadcasted_iota(jnp.int32, (gr, gw), 1)
    head_eq = (r4 // c) == (c4 // RW_N)
    m_strict = jnp.where(head_eq, (r4 % c) - (c4 % RW_N), -1) > 0
    m_incl = jnp.where(head_eq, (r4 % c) - (c4 % RW_N), -1) >= 0

    def rep(z):
        return jnp.concatenate([z] * RW_GROUP, axis=0)

    def blockdiag(z):
        return jnp.where(head_eq, rep(z), 0.0)

    def collapse(z):
        zm = jnp.where(head_eq, z, 0.0)
        out = zm[0:c]
        for i in range(1, RW_GROUP):
            out = out + zm[i * c:(i + 1) * c]
        return out

    y_parts = []
    for g in range(RW_HEADS // RW_GROUP):
        cs = slice(g * gw, (g + 1) * gw)
        s_old = s_ref[0, :, cs]
        lhs2 = jnp.concatenate([blockdiag(kkp[:, cs]), blockdiag(rp[:, cs])], axis=0)
        v_rep = rep(rv[:, cs])
        ab_k = _dotp(lhs2, rep(kd[:, cs]), 1, 1, RW_PASSES)
        ab_b = _dotp(lhs2, rep(bd[:, cs]), 1, 1, RW_PASSES)
        ab_s = _dotp(lhs2, rep(s_old * p_mid[:, cs]), 1, 1, RW_PASSES)
        a_k = jnp.where(m_strict, ab_k[0:gr], 0.0)
        r_k = jnp.where(m_incl, ab_k[gr:], 0.0)
        a_b = jnp.where(m_strict, ab_b[0:gr], 0.0)
        r_b = jnp.where(m_incl, ab_b[gr:], 0.0)
        rhs = ab_s[0:gr] + _dotp(a_k, v_rep, 1, 0, RW_PASSES)
        u = _tri_solve(a_b, rhs, c, RW_PASSES)
        yg = ab_s[gr:] + _dotp(r_k, v_rep, 1, 0, RW_PASSES) - _dotp(r_b, u, 1, 0, RW_PASSES)
        u_nat = collapse(u)
        upd = _dotp(jnp.concatenate([rv[:, cs], -u_nat], axis=0).T,
                    jnp.concatenate([k_end[:, cs], b_end[:, cs]], axis=0), 1, 0, RW_PASSES)
        s_ref[0, :, cs] = s_old * p_all[:, cs] + collapse(upd)
        y_parts.append(collapse(yg))

    yh = jnp.concatenate(y_parts, axis=1)
    mu_ = _head_sums(yh, RW_N) * (1.0 / RW_N)
    yc = yh - mu_
    var = _head_sums(yc * yc, RW_N) * (1.0 / RW_N)
    yn = yc * lax.rsqrt(var + RW_GN_EPS)
    out = (yn * ln_ref[...] + _head_sums(bon, RW_N) * rv) * g_out
    y_ref[...] = out.astype(y_ref.dtype)


def _rwkv(p_rw, shift0, s0, p, bsz, t):
    nc = t // CHUNK
    rows = bsz * t
    rmap = lambda b, c: b * nc + c
    full = lambda shape: pl.BlockSpec(shape, lambda b, c: (0,) * len(shape))
    w = RW_WIDTH
    s_nat = s0.transpose(0, 2, 1, 3).reshape(bsz, RW_N, w)
    y, s = pl.pallas_call(
        _rwkv_kernel,
        grid=(bsz, nc),
        in_specs=[
            pl.BlockSpec((CHUNK, RW_COLS), lambda b, c: (rmap(b, c), 0)),
            pl.BlockSpec((1, 1, RW_COLS), lambda b, c: (b, 0, 0)),
            full((1, RW_COLS)), full((1, w)), full((RW_W_RANK, w)), full((1, w)), full((RW_A_RANK, w)),
            full((RW_G_RANK, w)), full((1, w)), full((1, w)), full((1, w)), full((1, w)),
            pl.BlockSpec((1, RW_N, w), lambda b, c: (b, 0, 0)),
        ],
        out_specs=[
            pl.BlockSpec((CHUNK, w), lambda b, c: (rmap(b, c), 0)),
            pl.BlockSpec((1, RW_N, w), lambda b, c: (b, 0, 0)),
        ],
        out_shape=[jax.ShapeDtypeStruct((rows, w), BF16), jax.ShapeDtypeStruct(s_nat.shape, F32)],
        scratch_shapes=[pltpu.VMEM((8, RW_COLS), F32)],
        compiler_params=_cparams(("parallel", "arbitrary")),
        name="rwkv7",
    )(p_rw, shift0.reshape(bsz, 1, RW_COLS), p["rw_mu"].reshape(1, RW_COLS), p["rw_w0"].reshape(1, w),
      p["rw_w2"], p["rw_a0"].reshape(1, w), p["rw_a2"], p["rw_g2"], p["rw_k_k"].reshape(1, w),
      p["rw_k_a"].reshape(1, w), p["rw_r_k"].reshape(1, w), p["rw_ln"].reshape(1, w), s_nat)
    return y, s.reshape(bsz, RW_N, RW_HEADS, RW_N).transpose(0, 2, 1, 3)


def _gdn_kernel(q_ref, k_ref, v_ref, z_ref, sm_ref, cw_ref, conv0_ref, alog_ref, dtb_ref, norm_ref, s0_ref,
                y_ref, s_ref, prev_ref):
    @pl.when(pl.program_id(1) == 0)
    def _():
        s_ref[...] = s0_ref[...]
        prev_ref[...] = jnp.zeros(prev_ref.shape, F32)
        prev_ref[8 - (GD_CONV - 1):8, :] = conv0_ref[0]

    c = q_ref.shape[0]
    acts = []
    for sec, ref in enumerate((q_ref, k_ref, v_ref)):
        cs = slice(sec * GD_QK, (sec + 1) * GD_QK)
        x = ref[...]
        prev8 = prev_ref[:, cs]
        y = _shift_rows(x, prev8, GD_CONV - 1) * cw_ref[0:1, cs]
        for i in range(1, GD_CONV - 1):
            y = y + _shift_rows(x, prev8, GD_CONV - 1 - i) * cw_ref[i:i + 1, cs]
        y = y + x * cw_ref[GD_CONV - 1:GD_CONV, cs]
        prev_ref[:, cs] = x[c - 8:c]
        acts.append(_silu(y))
    qa, ka, va = acts

    sm = sm_ref[...]
    beta = _sigmoid(sm)
    g = -jnp.exp(alog_ref[...]) * _softplus(sm + dtb_ref[...])
    gc = _cumsum_rows(g)
    gc_t = gc.T

    gr = GD_GROUP * c
    gk = GD_GROUP * GD_DK
    r4 = lax.broadcasted_iota(jnp.int32, (gr, gr), 0)
    c4 = lax.broadcasted_iota(jnp.int32, (gr, gr), 1)
    t_minus_s = jnp.where((r4 // c) == (c4 // c), (r4 % c) - (c4 % c), -1)
    m_strict = t_minus_s > 0
    m_incl = t_minus_s >= 0
    head_eq = (lax.broadcasted_iota(jnp.int32, (gr, gk), 0) // c
               == lax.broadcasted_iota(jnp.int32, (gr, gk), 1) // GD_DK)

    def rep(z):
        return jnp.concatenate([z] * GD_GROUP, axis=0)

    def stack(parts):
        return jnp.concatenate(parts, axis=0)

    for g0 in range(0, GD_HEADS, GD_GROUP):
        heads = range(g0, g0 + GD_GROUP)
        qn, kn = [], []
        for h in heads:
            ks = slice(h * GD_DK, (h + 1) * GD_DK)
            q = qa[:, ks]
            qn.append(q * lax.rsqrt(jnp.sum(q * q, axis=-1, keepdims=True) + 1e-6) * (GD_DK ** -0.5))
            k = ka[:, ks]
            kn.append(k * lax.rsqrt(jnp.sum(k * k, axis=-1, keepdims=True) + 1e-6))
        k_nat = jnp.concatenate(kn, axis=1)
        q_nat = jnp.concatenate(qn, axis=1)
        k_bd = jnp.where(head_eq, rep(k_nat), 0.0)
        q_bd = jnp.where(head_eq, rep(q_nat), 0.0)
        lhs2 = stack([k_bd, q_bd])
        v_st = stack([va[:, h * GD_DV:(h + 1) * GD_DV] for h in heads])
        b_col = stack([beta[:, SM_GD_B + h:SM_GD_B + h + 1] for h in heads])
        g_col = stack([gc[:, SM_GD_A + h:SM_GD_A + h + 1] for h in heads])
        g_row = jnp.concatenate([gc_t[SM_GD_A + h:SM_GD_A + h + 1, :] for h in heads], axis=1)
        s_old = s_ref[0, g0:g0 + GD_GROUP]
        s_stack = s_old.reshape(gk, GD_DV)

        dec = jnp.exp(jnp.where(m_incl, g_col - g_row, -jnp.inf))
        kq = _dotp(lhs2, rep(k_nat), 1, 1, GD_PASSES)
        ks_qs = _dotp(lhs2, s_stack, 1, 0, GD_PASSES)
        low = jnp.where(m_strict, b_col * kq[0:gr] * dec, 0.0)
        eg = jnp.exp(g_col)
        rhs = b_col * (v_st - eg * ks_qs[0:gr])
        u = _tri_solve(low, rhs, c, GD_PASSES)
        qkt = kq[gr:] * dec
        o = eg * ks_qs[gr:] + _dotp(qkt, u, 1, 0, GD_PASSES)
        for i, h in enumerate(heads):
            rs = slice(i * c, (i + 1) * c)
            gl = gc[c - 1:c, SM_GD_A + h:SM_GD_A + h + 1]
            kw_h = kn[i] * jnp.exp(gl - g_col[rs])
            s_ref[0, h] = jnp.exp(gl) * s_old[i] + _dotp(kw_h.T, u[rs], 1, 0, GD_PASSES)
            o_h = o[rs]
            og = o_h * lax.rsqrt(jnp.mean(o_h * o_h, axis=-1, keepdims=True) + EPS) * norm_ref[...]
            vs = slice(h * GD_DV, (h + 1) * GD_DV)
            y_ref[:, vs] = (og * _silu(z_ref[:, vs])).astype(y_ref.dtype)


def _gdn(p_main, p_small, conv_w, conv0, alog_row, dtb_row, gd_norm, s0, bsz, t):
    nc = t // CHUNK
    rows = bsz * t
    rmap = lambda b, c: b * nc + c
    base = ML_MAIN // GD_QK
    assert ML_MAIN % GD_QK == 0
    full = lambda shape: pl.BlockSpec(shape, lambda b, c: (0,) * len(shape))
    return pl.pallas_call(
        _gdn_kernel,
        grid=(bsz, nc),
        in_specs=[
            pl.BlockSpec((CHUNK, GD_QK), lambda b, c: (rmap(b, c), base)),
            pl.BlockSpec((CHUNK, GD_QK), lambda b, c: (rmap(b, c), base + 1)),
            pl.BlockSpec((CHUNK, GD_WIDTH), lambda b, c: (rmap(b, c), base + 2)),
            pl.BlockSpec((CHUNK, GD_WIDTH), lambda b, c: (rmap(b, c), base + 3)),
            pl.BlockSpec((CHUNK, SMALL_W), lambda b, c: (rmap(b, c), 0)),
            full((GD_CONV, GD_QKV)),
            pl.BlockSpec((1, GD_CONV - 1, GD_QKV), lambda b, c: (b, 0, 0)),
            full((1, SMALL_W)), full((1, SMALL_W)), full((1, GD_DV)),
            pl.BlockSpec((1, GD_HEADS, GD_DK, GD_DV), lambda b, c: (b, 0, 0, 0)),
        ],
        out_specs=[
            pl.BlockSpec((CHUNK, GD_WIDTH), lambda b, c: (rmap(b, c), 0)),
            pl.BlockSpec((1, GD_HEADS, GD_DK, GD_DV), lambda b, c: (b, 0, 0, 0)),
        ],
        out_shape=[jax.ShapeDtypeStruct((rows, GD_WIDTH), BF16), jax.ShapeDtypeStruct(s0.shape, F32)],
        scratch_shapes=[pltpu.VMEM((8, GD_QKV), F32)],
        compiler_params=_cparams(("parallel", "arbitrary")),
        name="gdn",
    )(p_main, p_main, p_main, p_main, p_small, conv_w, conv0, alog_row, dtb_row, gd_norm.reshape(1, GD_DV), s0)


def _merge_kernel(y0_ref, y1_ref, y2_ref, w_ref, g0_ref, g1_ref, g2_ref, o_ref):
    acc = _sigmoid(g0_ref[...]) * jnp.dot(y0_ref[...], w_ref[0], preferred_element_type=F32)
    acc = acc + _sigmoid(g1_ref[...]) * jnp.dot(y1_ref[...], w_ref[1], preferred_element_type=F32)
    acc = acc + _sigmoid(g2_ref[...]) * jnp.dot(y2_ref[...], w_ref[2], preferred_element_type=F32)
    o_ref[...] = acc.astype(o_ref.dtype)


def _merge(ys, w_branch, p_main, d):
    m = ys[0].shape[0]
    tm = _tile(m, 512, 8)
    tn = _tile(d, 512, 128)
    gate0 = ML_MAIN + GD_MAIN
    assert gate0 % tn == 0
    gb = gate0 // tn
    nb = d // tn
    yspec = pl.BlockSpec((tm, BR_WIDTH), lambda i, j: (i, 0))
    gspec = lambda b: pl.BlockSpec((tm, tn), lambda i, j: (i, gb + b * nb + j))
    return pl.pallas_call(
        _merge_kernel,
        grid=(m // tm, nb),
        in_specs=[yspec, yspec, yspec, pl.BlockSpec((3, BR_WIDTH, tn), lambda i, j: (0, 0, j)),
                  gspec(0), gspec(1), gspec(2)],
        out_specs=pl.BlockSpec((tm, tn), lambda i, j: (i, j)),
        out_shape=jax.ShapeDtypeStruct((m, d), BF16),
        compiler_params=_cparams(("parallel", "arbitrary")),
        name="merge",
    )(ys[0], ys[1], ys[2], w_branch, p_main, p_main, p_main)


def _attn_kernel(q_ref, k_ref, v_ref, o_ref):
    for h in range(CA_HEADS):
        hs = slice(h * CA_HEAD_DIM, (h + 1) * CA_HEAD_DIM)
        s = _dg(q_ref[:, hs], k_ref[0, :, hs], 1, 1) * (CA_HEAD_DIM ** -0.5)
        s = s - jnp.max(s, axis=-1, keepdims=True)
        e = jnp.exp(s)
        pr = e / jnp.sum(e, axis=-1, keepdims=True)
        o = jnp.dot(pr.astype(BF16), v_ref[0, :, hs], preferred_element_type=F32)
        o_ref[:, hs] = o.astype(o_ref.dtype)


def _attention(q, mem_k, mem_v, bsz, t):
    tq = _tile(t, 512, 8)
    nt = t // tq
    n_mem = mem_k.shape[1]
    return pl.pallas_call(
        _attn_kernel,
        grid=(bsz, nt),
        in_specs=[
            pl.BlockSpec((tq, CA_WIDTH), lambda b, i: (b * nt + i, 0)),
            pl.BlockSpec((1, n_mem, CA_WIDTH), lambda b, i: (b, 0, 0)),
            pl.BlockSpec((1, n_mem, CA_WIDTH), lambda b, i: (b, 0, 0)),
        ],
        out_specs=pl.BlockSpec((tq, CA_WIDTH), lambda b, i: (b * nt + i, 0)),
        out_shape=jax.ShapeDtypeStruct((bsz * t, CA_WIDTH), BF16),
        compiler_params=_cparams(("parallel", "arbitrary")),
        name="mem_attention",
    )(q, mem_k, mem_v)


def _ffn_conv(x, prev, w):
    row = lax.broadcasted_iota(jnp.int32, x.shape, 0)
    x1 = jnp.where(row == 0, prev[1:2], pltpu.roll(x, 1, 0))
    x2 = jnp.where(row == 0, prev[0:1], jnp.where(row == 1, prev[1:2], pltpu.roll(x, 2, 0)))
    return x2 * w[0:1] + x1 * w[1:2] + x * w[2:3]


def _ffn_act_kernel(a_ref, g_ref, pa_ref, pg_ref, wa_ref, wg_ref, o_ref):
    fa = _ffn_conv(a_ref[...], pa_ref[0, 0], wa_ref[...])
    fg = _ffn_conv(g_ref[...], pg_ref[0, 0], wg_ref[...])
    o_ref[...] = (_silu(fg) * fa).astype(o_ref.dtype)


def _ffn_act(up, conv0, conv_w, bsz, t):
    c2 = up.shape[1]
    f = c2 // 2
    tr = _tile(t, 512, 8)
    nt = t // tr
    tc = _tile(f, 1024, 128)
    nj = f // tc
    tails = up.reshape(bsz, nt, tr, c2)[:, :nt - 1, tr - (FFN_CONV - 1):, :]
    prev = jnp.concatenate([conv0[:, None], tails], axis=1)
    return pl.pallas_call(
        _ffn_act_kernel,
        grid=(bsz, nt, nj),
        in_specs=[
            pl.BlockSpec((tr, tc), lambda b, i, j: (b * nt + i, j)),
            pl.BlockSpec((tr, tc), lambda b, i, j: (b * nt + i, nj + j)),
            pl.BlockSpec((1, 1, FFN_CONV - 1, tc), lambda b, i, j: (b, i, 0, j)),
            pl.BlockSpec((1, 1, FFN_CONV - 1, tc), lambda b, i, j: (b, i, 0, nj + j)),
            pl.BlockSpec((FFN_CONV, tc), lambda b, i, j: (0, j)),
            pl.BlockSpec((FFN_CONV, tc), lambda b, i, j: (0, nj + j)),
        ],
        out_specs=pl.BlockSpec((tr, tc), lambda b, i, j: (b * nt + i, j)),
        out_shape=jax.ShapeDtypeStruct((bsz * t, f), BF16),
        compiler_params=_cparams(("parallel", "parallel", "arbitrary")),
        name="ffn_conv_act",
    )(up, up, prev, prev, conv_w, conv_w)


def _prep_layer(p, d):
    w_in = p["w_in"]
    o_ml = 0
    o_if = ML_MAIN
    o_rw = o_if + 2 * ML_HEADS
    o_gd = o_rw + RW_COLS
    o_ba = o_gd + GD_MAIN
    o_gate = o_ba + 2 * GD_HEADS
    w_main = jnp.concatenate(
        [w_in[:, o_ml:o_ml + ML_MAIN], w_in[:, o_gd:o_gd + GD_MAIN], w_in[:, o_gate:]], axis=1).astype(BF16)
    w_rw = w_in[:, o_rw:o_rw + RW_COLS].astype(BF16)
    n_small = 2 * ML_HEADS + 2 * GD_HEADS
    w_small = jnp.concatenate(
        [w_in[:, o_if:o_if + 2 * ML_HEADS], w_in[:, o_ba:o_ba + 2 * GD_HEADS],
         jnp.zeros((d, SMALL_W - n_small), F32)], axis=1).astype(BF16)
    zrow = jnp.zeros((SMALL_W,), F32)
    q = dict(p)
    q.update(
        w_main=w_main, w_rw=w_rw, w_small=w_small,
        ml_bias_row=zrow.at[SM_ML_I:SM_ML_I + 2 * ML_HEADS].set(p["ml_b_if"]).reshape(1, SMALL_W),
        gd_alog_row=zrow.at[SM_GD_A:SM_GD_A + GD_HEADS].set(p["gd_a_log"]).reshape(1, SMALL_W),
        gd_dtb_row=zrow.at[SM_GD_A:SM_GD_A + GD_HEADS].set(p["gd_dt_bias"]).reshape(1, SMALL_W),
        w_branch_b=p["w_branch"].astype(BF16), w_out_b=p["w_out"].astype(BF16),
        w_ca_q_b=p["w_ca_q"].astype(BF16), w_ca_kv_b=p["w_ca_kv"].astype(BF16),
        w_ca_o_b=p["w_ca_o"].astype(BF16), w_up_b=p["w_up"].astype(BF16), w_down_b=p["w_down"].astype(BF16))
    return q


def _layer(h, mem_k, mem_v, st, p, bsz, t):
    d = h.shape[1]
    u = _rmsnorm(h, p["g_mix"], BF16)
    p_main = _matmul(u, p["w_main"])
    p_rw = _matmul(u, p["w_rw"])
    p_small = _matmul(u, p["w_small"])

    y_ml, ml_c, ml_n, ml_m = _mlstm(p_main, p_small, p["ml_bias_row"], p["ml_norm"],
                                    st["ml_C"], st["ml_n"], st["ml_m"], bsz, t)
    y_rw, rw_s = _rwkv(p_rw, st["rw_shift"], st["rw_S"], p, bsz, t)
    y_gd, gd_s = _gdn(p_main, p_small, p["gd_conv_w"], st["gd_conv"], p["gd_alog_row"], p["gd_dtb_row"],
                      p["gd_norm"], st["gd_S"], bsz, t)
    merged = _merge((y_ml, y_rw, y_gd), p["w_branch_b"], p_main, d)
    h = _matmul(merged, p["w_out_b"], residual=h)

    u = _rmsnorm(h, p["g_ca"], BF16)
    q = _matmul(u, p["w_ca_q_b"], out_dtype=BF16)
    o = _attention(q, mem_k, mem_v, bsz, t)
    h = _matmul(o, p["w_ca_o_b"], residual=h)

    u = _rmsnorm(h, p["g_ffn"], BF16)
    up = _matmul(u, p["w_up_b"])
    act = _ffn_act(up, st["ffn_conv"], p["ffn_conv_w"], bsz, t)
    h = _matmul(act, p["w_down_b"], residual=h)

    gd0 = ML_MAIN
    new_st = dict(
        ml_C=ml_c, ml_n=ml_n, ml_m=ml_m.reshape(bsz, ML_HEADS), rw_S=rw_s,
        rw_shift=p_rw.reshape(bsz, t, RW_COLS)[:, t - 1],
        gd_S=gd_s,
        gd_conv=p_main.reshape(bsz, t, -1)[:, t - (GD_CONV - 1):, gd0:gd0 + GD_QKV],
        ffn_conv=up.reshape(bsz, t, -1)[:, t - (FFN_CONV - 1):])
    return h, new_st


def _zero_state(bsz, d_ff2):
    return dict(
        ml_C=jnp.zeros((bsz, ML_HEADS, ML_DQK, ML_DV), F32), ml_n=jnp.zeros((bsz, ML_HEADS, ML_DQK), F32),
        ml_m=jnp.zeros((bsz, ML_HEADS), F32), rw_S=jnp.zeros((bsz, RW_HEADS, RW_N, RW_N), F32),
        rw_shift=jnp.zeros((bsz, RW_COLS), F32), gd_S=jnp.zeros((bsz, GD_HEADS, GD_DK, GD_DV), F32),
        gd_conv=jnp.zeros((bsz, GD_CONV - 1, GD_QKV), F32), ffn_conv=jnp.zeros((bsz, FFN_CONV - 1, d_ff2), F32))


def kernel(x_prompt, x_sample, cache_mem_k, cache_mem_v, state_mlstm_C, state_mlstm_n, state_mlstm_m, state_rwkv_S, state_rwkv_shift, state_gdn_S, state_gdn_conv, state_ffn_conv, mem_prompt, g_mix, w_in, ml_b_if, ml_norm, rw_mu, rw_w0, rw_w2, rw_a0, rw_a2, rw_g2, rw_k_k, rw_k_a, rw_r_k, rw_ln, gd_conv_w, gd_a_log, gd_dt_bias, gd_norm, w_branch, w_out, g_ca, g_mem, w_ca_q, w_ca_kv, w_ca_o, g_ffn, w_up, ffn_conv_w, w_down, g_final):
    bp, tp, d = x_prompt.shape
    bs, ts, _ = x_sample.shape
    depth = w_in.shape[0]
    n_mem = mem_prompt.shape[1]
    assert tp % CHUNK == 0 and ts % CHUNK == 0
    stacked = dict(g_mix=g_mix, w_in=w_in, ml_b_if=ml_b_if, ml_norm=ml_norm, rw_mu=rw_mu, rw_w0=rw_w0,
                   rw_w2=rw_w2, rw_a0=rw_a0, rw_a2=rw_a2, rw_g2=rw_g2, rw_k_k=rw_k_k, rw_k_a=rw_k_a,
                   rw_r_k=rw_r_k, rw_ln=rw_ln, gd_conv_w=gd_conv_w, gd_a_log=gd_a_log, gd_dt_bias=gd_dt_bias,
                   gd_norm=gd_norm, w_branch=w_branch, w_out=w_out, g_ca=g_ca, g_mem=g_mem, w_ca_q=w_ca_q,
                   w_ca_kv=w_ca_kv, w_ca_o=w_ca_o, g_ffn=g_ffn, w_up=w_up, ffn_conv_w=ffn_conv_w,
                   w_down=w_down)
    keys = ("ml_C", "ml_n", "ml_m", "rw_S", "rw_shift", "gd_S", "gd_conv", "ffn_conv")
    new_p = {k: [] for k in keys}
    new_s = {k: [] for k in keys}
    mem_k_list, mem_v_list = [], []
    hp = x_prompt.reshape(bp * tp, d)
    hs = x_sample.reshape(bs * ts, d)
    mem2d = mem_prompt.reshape(bp * n_mem, d)
    for l in range(depth):
        p = _prep_layer({k: v[l] for k, v in stacked.items()}, d)
        kv = _matmul(_rmsnorm(mem2d, p["g_mem"], BF16), p["w_ca_kv_b"])
        mk = kv[:, :CA_WIDTH].reshape(bp, n_mem, CA_WIDTH)
        mv = kv[:, CA_WIDTH:].reshape(bp, n_mem, CA_WIDTH)
        hp, stp = _layer(hp, mk.astype(BF16), mv.astype(BF16), _zero_state(bp, w_up.shape[2]), p, bp, tp)
        mem_k_list.append(mk.reshape(bp, n_mem, CA_HEADS, CA_HEAD_DIM))
        mem_v_list.append(mv.reshape(bp, n_mem, CA_HEADS, CA_HEAD_DIM))
        st_in = dict(ml_C=state_mlstm_C[l], ml_n=state_mlstm_n[l], ml_m=state_mlstm_m[l],
                     rw_S=state_rwkv_S[l], rw_shift=state_rwkv_shift[l], gd_S=state_gdn_S[l],
                     gd_conv=state_gdn_conv[l], ffn_conv=state_ffn_conv[l])
        ck = cache_mem_k[l].reshape(bs, n_mem, CA_WIDTH).astype(BF16)
        cv = cache_mem_v[l].reshape(bs, n_mem, CA_WIDTH).astype(BF16)
        hs, sts = _layer(hs, ck, cv, st_in, p, bs, ts)
        for k in keys:
            new_p[k].append(stp[k])
            new_s[k].append(sts[k])
    y_prompt = _rmsnorm(hp, g_final, F32).reshape(bp, tp, d)
    y_sample = _rmsnorm(hs, g_final, F32).reshape(bs, ts, d)
    outs = [y_prompt, y_sample, jnp.stack(mem_k_list), jnp.stack(mem_v_list)]
    outs += [jnp.stack(new_p[k]) for k in keys]
    outs += [jnp.stack(new_s[k]) for k in keys]
    return tuple(outs)
```

```python
import functools

import jax
import jax.numpy as jnp
from jax import lax
from jax.experimental import pallas as pl
from jax.experimental.pallas import tpu as pltpu

F32 = jnp.float32
BF16 = jnp.bfloat16

EPS = 1e-6
CHUNK = 64

ML_HEADS, ML_DQK, ML_DV = 4, 128, 256
ML_QK = ML_HEADS * ML_DQK
ML_WIDTH = ML_HEADS * ML_DV
ML_MAIN = 2 * ML_QK + 2 * ML_WIDTH
ML_PASSES = 3

RW_HEADS, RW_N = 16, 64
RW_WIDTH = RW_HEADS * RW_N
RW_W_RANK, RW_A_RANK, RW_G_RANK = 64, 64, 128
RW_COLS = 3 * RW_WIDTH + RW_W_RANK + RW_A_RANK + RW_G_RANK
RW_GN_EPS = 64e-5
RW_GROUP = 4
RW_PASSES = 1

GD_HEADS, GD_DK, GD_DV = 8, 128, 128
GD_QK = GD_HEADS * GD_DK
GD_WIDTH = GD_HEADS * GD_DV
GD_QKV = 2 * GD_QK + GD_WIDTH
GD_CONV = 4
GD_MAIN = GD_QKV + GD_WIDTH
GD_GROUP = 4
GD_PASSES = 1

BR_WIDTH = 1024
CA_HEADS, CA_HEAD_DIM = 4, 256
CA_WIDTH = CA_HEADS * CA_HEAD_DIM
FFN_CONV = 3

SMALL_W = 128
SM_ML_I, SM_ML_F, SM_GD_B, SM_GD_A = 0, ML_HEADS, 2 * ML_HEADS, 2 * ML_HEADS + GD_HEADS

V7X_VMEM_LIMIT = 56 * 1024 * 1024
TRI_BLOCK = 16


def _cparams(sem):
    return pltpu.CompilerParams(dimension_semantics=sem, vmem_limit_bytes=V7X_VMEM_LIMIT)


def _tile(dim, pref, quantum):
    if dim <= pref:
        return dim
    t = (pref // quantum) * quantum
    while t > quantum and dim % t:
        t -= quantum
    assert dim % t == 0, (dim, pref, quantum)
    return t


def _split2(a):
    hi = a.astype(BF16)
    lo = (a - hi.astype(F32)).astype(BF16)
    return hi, lo


def _dg(a, b, ca, cb):
    return lax.dot_general(a, b, (((ca,), (cb,)), ((), ())), preferred_element_type=F32)


def _dot3(a, b, ca=1, cb=0):
    ah, al = _split2(a)
    bh, bl = _split2(b)
    return _dg(ah, bh, ca, cb) + (_dg(al, bh, ca, cb) + _dg(ah, bl, ca, cb))


def _dotp(a, b, ca, cb, passes):
    if passes == 1:
        return _dg(a.astype(BF16), b.astype(BF16), ca, cb)
    return _dot3(a, b, ca, cb)


def _tri_masks(c):
    row = lax.broadcasted_iota(jnp.int32, (c, c), 0)
    col = lax.broadcasted_iota(jnp.int32, (c, c), 1)
    return row, col


def _cumsum_rows(x):
    c = x.shape[0]
    row, col = _tri_masks(c)
    tri = jnp.where(row >= col, 1.0, 0.0).astype(BF16)
    x0 = x.astype(BF16)
    r1 = x - x0.astype(F32)
    x1 = r1.astype(BF16)
    x2 = (r1 - x1.astype(F32)).astype(BF16)
    return _dg(tri, x0, 1, 0) + (_dg(tri, x1, 1, 0) + _dg(tri, x2, 1, 0))


def _tri_solve(low, rhs, c, passes):
    return _tri_solve_many([low], [rhs], c, passes)[0]


def _tri_solve_many(lows, rhss, c, passes):
    mm = lambda a, b: _dotp(a, b, 1, 0, passes)
    row, col = _tri_masks(lows[0].shape[0])
    same = (row // TRI_BLOCK) == (col // TRI_BLOCK)
    eye = jnp.where(row == col, 1.0, 0.0).astype(F32)
    ps = [jnp.where(same, -low, 0.0) for low in lows]
    offs = [jnp.where(same, 0.0, low) for low in lows]
    xs = [eye + p for p in ps]
    steps = 1
    while steps * 2 < TRI_BLOCK:
        ps = [mm(p, p) for p in ps]
        xs = [x + mm(x, p) for x, p in zip(xs, ps)]
        steps *= 2
    ms = [mm(x, off) for x, off in zip(xs, offs)]
    us = [mm(x, rhs) for x, rhs in zip(xs, rhss)]
    nblk = c // TRI_BLOCK
    terms = []
    pws = ms
    k = 1
    while k < nblk:
        terms.append(pws)
        k *= 2
        if k < nblk:
            pws = [mm(pw, pw) for pw in pws]
    for i in range(len(terms) - 1, 0, -1):
        us = [u + mm(t, u) for t, u in zip(terms[i], us)]
    if terms:
        us = [u - mm(t, u) for t, u in zip(terms[0], us)]
    return us


def _head_sums(x, hw):
    assert 2 * hw == 128
    lane = lax.broadcasted_iota(jnp.int32, (x.shape[0], 128), 1)
    lo = lane < hw
    out = []
    for s in range(x.shape[1] // 128):
        xs = x[:, s * 128:(s + 1) * 128]
        s_lo = jnp.sum(jnp.where(lo, xs, 0.0), axis=-1, keepdims=True)
        s_hi = jnp.sum(jnp.where(lo, 0.0, xs), axis=-1, keepdims=True)
        out.append(jnp.where(lo, s_lo, s_hi))
    return jnp.concatenate(out, axis=1)


def _softplus(x):
    return jnp.maximum(x, 0.0) + jnp.log1p(jnp.exp(-jnp.abs(x)))


def _sigmoid(x):
    return 1.0 / (1.0 + jnp.exp(-x))


def _silu(x):
    return x * _sigmoid(x)


def _shift_rows(x, prev8, k):
    xr = pltpu.roll(x, k, 0)
    pr = pltpu.roll(prev8, k, 0)
    row = lax.broadcasted_iota(jnp.int32, (8, x.shape[1]), 0)
    head = jnp.where(row < k, pr, xr[0:8])
    if x.shape[0] == 8:
        return head
    return jnp.concatenate([head, xr[8:]], axis=0)


def _rmsnorm_kernel(x_ref, g_ref, o_ref):
    x = x_ref[...]
    y = x * lax.rsqrt(jnp.mean(x * x, axis=-1, keepdims=True) + EPS)
    o_ref[...] = (y * g_ref[...]).astype(o_ref.dtype)


def _rmsnorm(x, g, out_dtype):
    m, d = x.shape
    tr = _tile(m, 256, 8)
    return pl.pallas_call(
        _rmsnorm_kernel,
        grid=(m // tr,),
        in_specs=[pl.BlockSpec((tr, d), lambda i: (i, 0)), pl.BlockSpec((1, d), lambda i: (0, 0))],
        out_specs=pl.BlockSpec((tr, d), lambda i: (i, 0)),
        out_shape=jax.ShapeDtypeStruct((m, d), out_dtype),
        compiler_params=_cparams(("parallel",)),
        name="rmsnorm",
    )(x, g.reshape(1, d))


def _mm_kernel(a_ref, w_ref, o_ref):
    o_ref[...] = jnp.dot(a_ref[...], w_ref[...], preferred_element_type=F32).astype(o_ref.dtype)


def _mm_res_kernel(a_ref, w_ref, r_ref, o_ref):
    o_ref[...] = r_ref[...] + jnp.dot(a_ref[...], w_ref[...], preferred_element_type=F32)


def _mm_tiles(m, k, n):
    tm = _tile(m, 1024 if k <= 4096 else 512, 8)
    tn = _tile(n, 512, 128)
    return tm, tn


def _matmul(a, w, residual=None, out_dtype=F32):
    m, k = a.shape
    n = w.shape[1]
    tm, tn = _mm_tiles(m, k, n)
    in_specs = [pl.BlockSpec((tm, k), lambda i, j: (i, 0)), pl.BlockSpec((k, tn), lambda i, j: (0, j))]
    args = [a, w]
    body = _mm_kernel
    if residual is not None:
        in_specs.append(pl.BlockSpec((tm, tn), lambda i, j: (i, j)))
        args.append(residual)
        body = _mm_res_kernel
    return pl.pallas_call(
        body,
        grid=(m // tm, n // tn),
        in_specs=in_specs,
        out_specs=pl.BlockSpec((tm, tn), lambda i, j: (i, j)),
        out_shape=jax.ShapeDtypeStruct((m, n), out_dtype),
        compiler_params=_cparams(("parallel", "arbitrary")),
        name="matmul",
    )(*args)


def _mlstm_kernel(q_ref, k_ref, v_ref, og_ref, sm_ref, bias_ref, norm_ref, c0_ref, n0_ref, m0_ref,
                  y_ref, c_ref, n_ref, m_ref):
    @pl.when(pl.program_id(1) == 0)
    def _():
        c_ref[...] = c0_ref[...]
        n_ref[...] = n0_ref[...]
        m_ref[...] = m0_ref[...]

    c = q_ref.shape[0]
    pre = sm_ref[...] + bias_ref[...]
    logf = -_softplus(-pre)
    cum = _cumsum_rows(logf)
    pre_t = pre.T
    cum_t = cum.T
    row, col = _tri_masks(c)
    causal = row >= col
    hs = range(ML_HEADS)
    q = [q_ref[:, h * ML_DQK:(h + 1) * ML_DQK] for h in hs]
    k = [k_ref[:, h * ML_DQK:(h + 1) * ML_DQK] * (ML_DQK ** -0.5) for h in hs]
    v = [v_ref[:, h * ML_DV:(h + 1) * ML_DV] for h in hs]
    b_col = [cum[:, SM_ML_F + h:SM_ML_F + h + 1] for h in hs]
    b_row = [cum_t[SM_ML_F + h:SM_ML_F + h + 1, :] for h in hs]
    i_col = [pre[:, SM_ML_I + h:SM_ML_I + h + 1] for h in hs]
    i_row = [pre_t[SM_ML_I + h:SM_ML_I + h + 1, :] for h in hs]
    m_old = [m_ref[0, :, h:h + 1] for h in hs]
    c_old = [c_ref[0, h] for h in hs]
    n_old = [n_ref[0, h:h + 1, :] for h in hs]

    qk = [_dotp(q[h], k[h], 1, 1, ML_PASSES) for h in hs]
    qc = [_dotp(q[h], c_old[h], 1, 0, ML_PASSES) for h in hs]
    dmat = [jnp.where(causal, b_col[h] - b_row[h] + i_row[h], -jnp.inf) for h in hs]
    mt = [jnp.maximum(b_col[h] + m_old[h], jnp.max(dmat[h], axis=-1, keepdims=True)) for h in hs]
    pmat = [jnp.exp(dmat[h] - mt[h]) * qk[h] for h in hs]
    inter = [jnp.exp(b_col[h] + m_old[h] - mt[h]) for h in hs]
    num = [inter[h] * qc[h] + _dotp(pmat[h], v[h], 1, 0, ML_PASSES) for h in hs]
    den = [inter[h] * jnp.sum(q[h] * n_old[h], axis=-1, keepdims=True)
           + jnp.sum(pmat[h], axis=-1, keepdims=True) for h in hs]
    hh = [num[h] / jnp.maximum(jnp.abs(den[h]), jnp.exp(-mt[h])) for h in hs]

    m_new = [mt[h][c - 1:c, :] for h in hs]
    b_last = [b_col[h][c - 1:c, :] for h in hs]
    kw = [k[h] * jnp.exp(b_last[h] - b_col[h] + i_col[h] - m_new[h]) for h in hs]
    dec = [jnp.exp(b_last[h] + m_old[h] - m_new[h]) for h in hs]
    c_new = [dec[h] * c_old[h] + _dotp(kw[h].T, v[h], 1, 0, ML_PASSES) for h in hs]
    for h in hs:
        c_ref[0, h] = c_new[h]
        n_ref[0, h:h + 1, :] = dec[h] * n_old[h] + jnp.sum(kw[h], axis=0, keepdims=True)
        m_ref[0, :, h:h + 1] = m_new[h]
        hn = hh[h] * lax.rsqrt(jnp.mean(hh[h] * hh[h], axis=-1, keepdims=True) + EPS)
        og = og_ref[:, h * ML_DV:(h + 1) * ML_DV]
        y = hn * norm_ref[:, h * ML_DV:(h + 1) * ML_DV] * _sigmoid(og)
        y_ref[:, h * ML_DV:(h + 1) * ML_DV] = y.astype(y_ref.dtype)


def _mlstm(p_main, p_small, bias_row, ml_norm, c0, n0, m0, bsz, t):
    nc = t // CHUNK
    rows = bsz * t
    rmap = lambda b, c: b * nc + c
    return pl.pallas_call(
        _mlstm_kernel,
        grid=(bsz, nc),
        in_specs=[
            pl.BlockSpec((CHUNK, ML_QK), lambda b, c: (rmap(b, c), 0)),
            pl.BlockSpec((CHUNK, ML_QK), lambda b, c: (rmap(b, c), 1)),
            pl.BlockSpec((CHUNK, ML_WIDTH), lambda b, c: (rmap(b, c), 1)),
            pl.BlockSpec((CHUNK, ML_WIDTH), lambda b, c: (rmap(b, c), 2)),
            pl.BlockSpec((CHUNK, SMALL_W), lambda b, c: (rmap(b, c), 0)),
            pl.BlockSpec((1, SMALL_W), lambda b, c: (0, 0)),
            pl.BlockSpec((1, ML_WIDTH), lambda b, c: (0, 0)),
            pl.BlockSpec((1, ML_HEADS, ML_DQK, ML_DV), lambda b, c: (b, 0, 0, 0)),
            pl.BlockSpec((1, ML_HEADS, ML_DQK), lambda b, c: (b, 0, 0)),
            pl.BlockSpec((1, 1, ML_HEADS), lambda b, c: (b, 0, 0)),
        ],
        out_specs=[
            pl.BlockSpec((CHUNK, ML_WIDTH), lambda b, c: (rmap(b, c), 0)),
            pl.BlockSpec((1, ML_HEADS, ML_DQK, ML_DV), lambda b, c: (b, 0, 0, 0)),
            pl.BlockSpec((1, ML_HEADS, ML_DQK), lambda b, c: (b, 0, 0)),
            pl.BlockSpec((1, 1, ML_HEADS), lambda b, c: (b, 0, 0)),
        ],
        out_shape=[
            jax.ShapeDtypeStruct((rows, ML_WIDTH), BF16),
            jax.ShapeDtypeStruct(c0.shape, F32),
            jax.ShapeDtypeStruct(n0.shape, F32),
            jax.ShapeDtypeStruct((bsz, 1, ML_HEADS), F32),
        ],
        compiler_params=_cparams(("parallel", "arbitrary")),
        name="mlstm",
    )(p_main, p_main, p_main, p_main, p_small, bias_row, ml_norm.reshape(1, ML_WIDTH),
      c0, n0, m0.reshape(bsz, 1, ML_HEADS))


def _rwkv_kernel(p_ref, shift_ref, mu_ref, w0_ref, w2_ref, a0_ref, a2_ref, g2_ref, kk_ref, ka_ref,
                 rk_ref, ln_ref, s0_ref, y_ref, s_ref, prev_ref):
    @pl.when(pl.program_id(1) == 0)
    def _():
        s_ref[...] = s0_ref[...]
        prev_ref[...] = jnp.broadcast_to(shift_ref[0], prev_ref.shape)

    c = p_ref.shape[0]
    w = RW_WIDTH
    x = p_ref[...]
    xprev = _shift_rows(x, prev_ref[...], 1)
    prev_ref[...] = x[c - 8:c]
    xm = x + (xprev - x) * mu_ref[...]
    rr = xm[:, 0:w]
    rk = xm[:, w:2 * w]
    rv = xm[:, 2 * w:3 * w]
    xw = xm[:, 3 * w:3 * w + RW_W_RANK]
    xa = xm[:, 3 * w + RW_W_RANK:3 * w + RW_W_RANK + RW_A_RANK]
    xg = xm[:, 3 * w + RW_W_RANK + RW_A_RANK:]

    w_pre = w0_ref[...] + _dot3(jnp.tanh(xw), w2_ref[...])
    lw = -jnp.exp(-_softplus(-w_pre) - 0.5)
    a = _sigmoid(a0_ref[...] + _dot3(xa, a2_ref[...]))
    g_out = _dot3(_sigmoid(xg), g2_ref[...])
    kk_raw = rk * kk_ref[...]
    kmod = rk * (1.0 + (a - 1.0) * ka_ref[...])
    bon = rr * kmod * rk_ref[...]

    lc = _cumsum_rows(lw)
    l_end = lc[c - 1:c, :]
    l_mid = lc[c // 2 - 1:c // 2, :]
    lcc = lc - l_mid
    p_mid = jnp.exp(l_mid)
    p_in = jnp.exp(lcc)
    p_prev = jnp.exp(lcc - lw)
    p_inv = jnp.exp(-lcc)
    p_end = jnp.exp(l_end - lc)
    p_all = jnp.exp(l_end)

    kkn = kk_raw * lax.rsqrt(_head_sums(kk_raw * kk_raw, RW_N) + 1e-6)
    bb = kkn * a
    kkp = kkn * p_prev
    rp = rr * p_in
    kd = kmod * p_inv
    bd = bb * p_inv
    k_end = kmod * p_end
    b_end = bb * p_end

    gw = RW_GROUP * RW_N
    gr = RW_GROUP * c
    r4 = lax.broadcasted_iota(jnp.int32, (gr, gw), 0)
    c4 = lax.broadcasted_iota(jnp.int32, (gr, gw), 1)
    head_eq = (r4 // c) == (c4 // RW_N)
    m_strict = jnp.where(head_eq, (r4 % c) - (c4 % RW_N), -1) > 0
    m_incl = jnp.where(head_eq, (r4 % c) - (c4 % RW_N), -1) >= 0

    def rep(z):
        return jnp.concatenate([z] * RW_GROUP, axis=0)

    def blockdiag(z):
        return jnp.where(head_eq, rep(z), 0.0)

    def collapse(z):
        zm = jnp.where(head_eq, z, 0.0)
        out = zm[0:c]
        for i in range(1, RW_GROUP):
            out = out + zm[i * c:(i + 1) * c]
        return out

    groups = [slice(g * gw, (g + 1) * gw) for g in range(RW_HEADS // RW_GROUP)]
    nt = lambda x, y: _dotp(x, y, 1, 1, RW_PASSES)
    nn = lambda x, y: _dotp(x, y, 1, 0, RW_PASSES)
    s_olds = [s_ref[0, :, cs] for cs in groups]
    lhs2 = [jnp.concatenate([blockdiag(kkp[:, cs]), blockdiag(rp[:, cs])], axis=0) for cs in groups]
    v_reps = [rep(rv[:, cs]) for cs in groups]
    ab_b = [nt(l, rep(bd[:, cs])) for l, cs in zip(lhs2, groups)]
    ab_k = [nt(l, rep(kd[:, cs])) for l, cs in zip(lhs2, groups)]
    ab_s = [nt(l, rep(s * p_mid[:, cs])) for l, s, cs in zip(lhs2, s_olds, groups)]
    a_b = [jnp.where(m_strict, z[0:gr], 0.0) for z in ab_b]
    a_k = [jnp.where(m_strict, z[0:gr], 0.0) for z in ab_k]
    rhs = [z[0:gr] + nn(ak, v) for z, ak, v in zip(ab_s, a_k, v_reps)]
    us = _tri_solve_many(a_b, rhs, c, RW_PASSES)
    r_k = [jnp.where(m_incl, z[gr:], 0.0) for z in ab_k]
    r_b = [jnp.where(m_incl, z[gr:], 0.0) for z in ab_b]
    ygs = [z[gr:] + nn(rk_, v) - nn(rb_, u) for z, rk_, v, rb_, u in zip(ab_s, r_k, v_reps, r_b, us)]
    upds = [nn(jnp.concatenate([rv[:, cs], -collapse(u)], axis=0).T,
               jnp.concatenate([k_end[:, cs], b_end[:, cs]], axis=0)) for cs, u in zip(groups, us)]
    for cs, s, upd in zip(groups, s_olds, upds):
        s_ref[0, :, cs] = s * p_all[:, cs] + collapse(upd)

    yh = jnp.concatenate([collapse(z) for z in ygs], axis=1)
    mu_ = _head_sums(yh, RW_N) * (1.0 / RW_N)
    yc = yh - mu_
    var = _head_sums(yc * yc, RW_N) * (1.0 / RW_N)
    yn = yc * lax.rsqrt(var + RW_GN_EPS)
    out = (yn * ln_ref[...] + _head_sums(bon, RW_N) * rv) * g_out
    y_ref[...] = out.astype(y_ref.dtype)


def _rwkv(p_rw, shift0, s0, p, bsz, t):
    nc = t // CHUNK
    rows = bsz * t
    rmap = lambda b, c: b * nc + c
    full = lambda shape: pl.BlockSpec(shape, lambda b, c: (0,) * len(shape))
    w = RW_WIDTH
    s_nat = s0.transpose(0, 2, 1, 3).reshape(bsz, RW_N, w)
    y, s = pl.pallas_call(
        _rwkv_kernel,
        grid=(bsz, nc),
        in_specs=[
            pl.BlockSpec((CHUNK, RW_COLS), lambda b, c: (rmap(b, c), 0)),
            pl.BlockSpec((1, 1, RW_COLS), lambda b, c: (b, 0, 0)),
            full((1, RW_COLS)), full((1, w)), full((RW_W_RANK, w)), full((1, w)), full((RW_A_RANK, w)),
            full((RW_G_RANK, w)), full((1, w)), full((1, w)), full((1, w)), full((1, w)),
            pl.BlockSpec((1, RW_N, w), lambda b, c: (b, 0, 0)),
        ],
        out_specs=[
            pl.BlockSpec((CHUNK, w), lambda b, c: (rmap(b, c), 0)),
            pl.BlockSpec((1, RW_N, w), lambda b, c: (b, 0, 0)),
        ],
        out_shape=[jax.ShapeDtypeStruct((rows, w), BF16), jax.ShapeDtypeStruct(s_nat.shape, F32)],
        scratch_shapes=[pltpu.VMEM((8, RW_COLS), F32)],
        compiler_params=_cparams(("parallel", "arbitrary")),
        name="rwkv7",
    )(p_rw, shift0.reshape(bsz, 1, RW_COLS), p["rw_mu"].reshape(1, RW_COLS), p["rw_w0"].reshape(1, w),
      p["rw_w2"], p["rw_a0"].reshape(1, w), p["rw_a2"], p["rw_g2"], p["rw_k_k"].reshape(1, w),
      p["rw_k_a"].reshape(1, w), p["rw_r_k"].reshape(1, w), p["rw_ln"].reshape(1, w), s_nat)
    return y, s.reshape(bsz, RW_N, RW_HEADS, RW_N).transpose(0, 2, 1, 3)


def _gdn_kernel(q_ref, k_ref, v_ref, z_ref, sm_ref, cw_ref, conv0_ref, alog_ref, dtb_ref, norm_ref, s0_ref,
                y_ref, s_ref, prev_ref):
    @pl.when(pl.program_id(1) == 0)
    def _():
        s_ref[...] = s0_ref[...]
        prev_ref[...] = jnp.zeros(prev_ref.shape, F32)
        prev_ref[8 - (GD_CONV - 1):8, :] = conv0_ref[0]

    c = q_ref.shape[0]
    acts = []
    for sec, ref in enumerate((q_ref, k_ref, v_ref)):
        cs = slice(sec * GD_QK, (sec + 1) * GD_QK)
        x = ref[...]
        prev8 = prev_ref[:, cs]
        y = _shift_rows(x, prev8, GD_CONV - 1) * cw_ref[0:1, cs]
        for i in range(1, GD_CONV - 1):
            y = y + _shift_rows(x, prev8, GD_CONV - 1 - i) * cw_ref[i:i + 1, cs]
        y = y + x * cw_ref[GD_CONV - 1:GD_CONV, cs]
        prev_ref[:, cs] = x[c - 8:c]
        acts.append(_silu(y))
    qa, ka, va = acts

    sm = sm_ref[...]
    beta = _sigmoid(sm)
    g = -jnp.exp(alog_ref[...]) * _softplus(sm + dtb_ref[...])
    gc = _cumsum_rows(g)
    gc_t = gc.T

    gr = GD_GROUP * c
    gk = GD_GROUP * GD_DK
    r4 = lax.broadcasted_iota(jnp.int32, (gr, gr), 0)
    c4 = lax.broadcasted_iota(jnp.int32, (gr, gr), 1)
    t_minus_s = jnp.where((r4 // c) == (c4 // c), (r4 % c) - (c4 % c), -1)
    m_strict = t_minus_s > 0
    m_incl = t_minus_s >= 0
    head_eq = (lax.broadcasted_iota(jnp.int32, (gr, gk), 0) // c
               == lax.broadcasted_iota(jnp.int32, (gr, gk), 1) // GD_DK)

    def rep(z):
        return jnp.concatenate([z] * GD_GROUP, axis=0)

    def stack(parts):
        return jnp.concatenate(parts, axis=0)

    kn = []
    qn = []
    for h in range(GD_HEADS):
        ks = slice(h * GD_DK, (h + 1) * GD_DK)
        q = qa[:, ks]
        qn.append(q * lax.rsqrt(jnp.sum(q * q, axis=-1, keepdims=True) + 1e-6) * (GD_DK ** -0.5))
        k = ka[:, ks]
        kn.append(k * lax.rsqrt(jnp.sum(k * k, axis=-1, keepdims=True) + 1e-6))
    groups = [range(g0, g0 + GD_GROUP) for g0 in range(0, GD_HEADS, GD_GROUP)]
    k_nat = [jnp.concatenate([kn[h] for h in hs], axis=1) for hs in groups]
    q_nat = [jnp.concatenate([qn[h] for h in hs], axis=1) for hs in groups]
    lhs2 = [stack([jnp.where(head_eq, rep(kk_), 0.0), jnp.where(head_eq, rep(qq_), 0.0)])
            for kk_, qq_ in zip(k_nat, q_nat)]
    v_st = [stack([va[:, h * GD_DV:(h + 1) * GD_DV] for h in hs]) for hs in groups]
    b_col = [stack([beta[:, SM_GD_B + h:SM_GD_B + h + 1] for h in hs]) for hs in groups]
    g_col = [stack([gc[:, SM_GD_A + h:SM_GD_A + h + 1] for h in hs]) for hs in groups]
    g_row = [jnp.concatenate([gc_t[SM_GD_A + h:SM_GD_A + h + 1, :] for h in hs], axis=1) for hs in groups]
    s_old = [s_ref[0, hs[0]:hs[0] + GD_GROUP] for hs in groups]

    dec = [jnp.exp(jnp.where(m_incl, gc_ - gr_, -jnp.inf)) for gc_, gr_ in zip(g_col, g_row)]
    kq = [_dotp(l, rep(kk_), 1, 1, GD_PASSES) for l, kk_ in zip(lhs2, k_nat)]
    ks_qs = [_dotp(l, s.reshape(gk, GD_DV), 1, 0, GD_PASSES) for l, s in zip(lhs2, s_old)]
    low = [jnp.where(m_strict, b * z[0:gr] * d, 0.0) for b, z, d in zip(b_col, kq, dec)]
    eg = [jnp.exp(gc_) for gc_ in g_col]
    rhs = [b * (v - e * z[0:gr]) for b, v, e, z in zip(b_col, v_st, eg, ks_qs)]
    us = _tri_solve_many(low, rhs, c, GD_PASSES)
    os_ = [e * z[gr:] + _dotp(zq[gr:] * d, u, 1, 0, GD_PASSES)
           for e, z, zq, d, u in zip(eg, ks_qs, kq, dec, us)]
    for gi, hs in enumerate(groups):
        for i, h in enumerate(hs):
            rs = slice(i * c, (i + 1) * c)
            gl = gc[c - 1:c, SM_GD_A + h:SM_GD_A + h + 1]
            kw_h = kn[h] * jnp.exp(gl - g_col[gi][rs])
            s_ref[0, h] = jnp.exp(gl) * s_old[gi][i] + _dotp(kw_h.T, us[gi][rs], 1, 0, GD_PASSES)
            o_h = os_[gi][rs]
            og = o_h * lax.rsqrt(jnp.mean(o_h * o_h, axis=-1, keepdims=True) + EPS) * norm_ref[...]
            vs = slice(h * GD_DV, (h + 1) * GD_DV)
            y_ref[:, vs] = (og * _silu(z_ref[:, vs])).astype(y_ref.dtype)


def _gdn(p_main, p_small, conv_w, conv0, alog_row, dtb_row, gd_norm, s0, bsz, t):
    nc = t // CHUNK
    rows = bsz * t
    rmap = lambda b, c: b * nc + c
    base = ML_MAIN // GD_QK
    assert ML_MAIN % GD_QK == 0
    full = lambda shape: pl.BlockSpec(shape, lambda b, c: (0,) * len(shape))
    return pl.pallas_call(
        _gdn_kernel,
        grid=(bsz, nc),
        in_specs=[
            pl.BlockSpec((CHUNK, GD_QK), lambda b, c: (rmap(b, c), base)),
            pl.BlockSpec((CHUNK, GD_QK), lambda b, c: (rmap(b, c), base + 1)),
            pl.BlockSpec((CHUNK, GD_WIDTH), lambda b, c: (rmap(b, c), base + 2)),
            pl.BlockSpec((CHUNK, GD_WIDTH), lambda b, c: (rmap(b, c), base + 3)),
            pl.BlockSpec((CHUNK, SMALL_W), lambda b, c: (rmap(b, c), 0)),
            full((GD_CONV, GD_QKV)),
            pl.BlockSpec((1, GD_CONV - 1, GD_QKV), lambda b, c: (b, 0, 0)),
            full((1, SMALL_W)), full((1, SMALL_W)), full((1, GD_DV)),
            pl.BlockSpec((1, GD_HEADS, GD_DK, GD_DV), lambda b, c: (b, 0, 0, 0)),
        ],
        out_specs=[
            pl.BlockSpec((CHUNK, GD_WIDTH), lambda b, c: (rmap(b, c), 0)),
            pl.BlockSpec((1, GD_HEADS, GD_DK, GD_DV), lambda b, c: (b, 0, 0, 0)),
        ],
        out_shape=[jax.ShapeDtypeStruct((rows, GD_WIDTH), BF16), jax.ShapeDtypeStruct(s0.shape, F32)],
        scratch_shapes=[pltpu.VMEM((8, GD_QKV), F32)],
        compiler_params=_cparams(("parallel", "arbitrary")),
        name="gdn",
    )(p_main, p_main, p_main, p_main, p_small, conv_w, conv0, alog_row, dtb_row, gd_norm.reshape(1, GD_DV), s0)


def _merge_kernel(y0_ref, y1_ref, y2_ref, w_ref, g0_ref, g1_ref, g2_ref, o_ref):
    acc = _sigmoid(g0_ref[...]) * jnp.dot(y0_ref[...], w_ref[0], preferred_element_type=F32)
    acc = acc + _sigmoid(g1_ref[...]) * jnp.dot(y1_ref[...], w_ref[1], preferred_element_type=F32)
    acc = acc + _sigmoid(g2_ref[...]) * jnp.dot(y2_ref[...], w_ref[2], preferred_element_type=F32)
    o_ref[...] = acc.astype(o_ref.dtype)


def _merge(ys, w_branch, p_main, d):
    m = ys[0].shape[0]
    tm = _tile(m, 1024, 8)
    tn = _tile(d, 512, 128)
    gate0 = ML_MAIN + GD_MAIN
    assert gate0 % tn == 0
    gb = gate0 // tn
    nb = d // tn
    yspec = pl.BlockSpec((tm, BR_WIDTH), lambda i, j: (i, 0))
    gspec = lambda b: pl.BlockSpec((tm, tn), lambda i, j: (i, gb + b * nb + j))
    return pl.pallas_call(
        _merge_kernel,
        grid=(m // tm, nb),
        in_specs=[yspec, yspec, yspec, pl.BlockSpec((3, BR_WIDTH, tn), lambda i, j: (0, 0, j)),
                  gspec(0), gspec(1), gspec(2)],
        out_specs=pl.BlockSpec((tm, tn), lambda i, j: (i, j)),
        out_shape=jax.ShapeDtypeStruct((m, d), BF16),
        compiler_params=_cparams(("parallel", "arbitrary")),
        name="merge",
    )(ys[0], ys[1], ys[2], w_branch, p_main, p_main, p_main)


def _attn_kernel(q_ref, k_ref, v_ref, o_ref):
    for h in range(CA_HEADS):
        hs = slice(h * CA_HEAD_DIM, (h + 1) * CA_HEAD_DIM)
        s = _dg(q_ref[:, hs], k_ref[0, :, hs], 1, 1) * (CA_HEAD_DIM ** -0.5)
        s = s - jnp.max(s, axis=-1, keepdims=True)
        e = jnp.exp(s)
        pr = e / jnp.sum(e, axis=-1, keepdims=True)
        o = jnp.dot(pr.astype(BF16), v_ref[0, :, hs], preferred_element_type=F32)
        o_ref[:, hs] = o.astype(o_ref.dtype)


def _attention(q, mem_k, mem_v, bsz, t):
    tq = _tile(t, 512, 8)
    nt = t // tq
    n_mem = mem_k.shape[1]
    return pl.pallas_call(
        _attn_kernel,
        grid=(bsz, nt),
        in_specs=[
            pl.BlockSpec((tq, CA_WIDTH), lambda b, i: (b * nt + i, 0)),
            pl.BlockSpec((1, n_mem, CA_WIDTH), lambda b, i: (b, 0, 0)),
            pl.BlockSpec((1, n_mem, CA_WIDTH), lambda b, i: (b, 0, 0)),
        ],
        out_specs=pl.BlockSpec((tq, CA_WIDTH), lambda b, i: (b * nt + i, 0)),
        out_shape=jax.ShapeDtypeStruct((bsz * t, CA_WIDTH), BF16),
        compiler_params=_cparams(("parallel", "arbitrary")),
        name="mem_attention",
    )(q, mem_k, mem_v)


def _ffn_conv(x, prev, w):
    row = lax.broadcasted_iota(jnp.int32, x.shape, 0)
    x1 = jnp.where(row == 0, prev[1:2], pltpu.roll(x, 1, 0))
    x2 = jnp.where(row == 0, prev[0:1], jnp.where(row == 1, prev[1:2], pltpu.roll(x, 2, 0)))
    return x2 * w[0:1] + x1 * w[1:2] + x * w[2:3]


def _ffn_act_kernel(a_ref, g_ref, pa_ref, pg_ref, wa_ref, wg_ref, o_ref):
    fa = _ffn_conv(a_ref[...], pa_ref[0, 0], wa_ref[...])
    fg = _ffn_conv(g_ref[...], pg_ref[0, 0], wg_ref[...])
    o_ref[...] = (_silu(fg) * fa).astype(o_ref.dtype)


def _ffn_act(up, conv0, conv_w, bsz, t):
    c2 = up.shape[1]
    f = c2 // 2
    tr = _tile(t, 512, 8)
    nt = t // tr
    tc = _tile(f, 1024, 128)
    nj = f // tc
    tails = up.reshape(bsz, nt, tr, c2)[:, :nt - 1, tr - (FFN_CONV - 1):, :]
    prev = jnp.concatenate([conv0[:, None], tails], axis=1)
    return pl.pallas_call(
        _ffn_act_kernel,
        grid=(bsz, nt, nj),
        in_specs=[
            pl.BlockSpec((tr, tc), lambda b, i, j: (b * nt + i, j)),
            pl.BlockSpec((tr, tc), lambda b, i, j: (b * nt + i, nj + j)),
            pl.BlockSpec((1, 1, FFN_CONV - 1, tc), lambda b, i, j: (b, i, 0, j)),
            pl.BlockSpec((1, 1, FFN_CONV - 1, tc), lambda b, i, j: (b, i, 0, nj + j)),
            pl.BlockSpec((FFN_CONV, tc), lambda b, i, j: (0, j)),
            pl.BlockSpec((FFN_CONV, tc), lambda b, i, j: (0, nj + j)),
        ],
        out_specs=pl.BlockSpec((tr, tc), lambda b, i, j: (b * nt + i, j)),
        out_shape=jax.ShapeDtypeStruct((bsz * t, f), BF16),
        compiler_params=_cparams(("parallel", "parallel", "arbitrary")),
        name="ffn_conv_act",
    )(up, up, prev, prev, conv_w, conv_w)


def _prep_layer(p, d):
    w_in = p["w_in"]
    o_ml = 0
    o_if = ML_MAIN
    o_rw = o_if + 2 * ML_HEADS
    o_gd = o_rw + RW_COLS
    o_ba = o_gd + GD_MAIN
    o_gate = o_ba + 2 * GD_HEADS
    w_main = jnp.concatenate(
        [w_in[:, o_ml:o_ml + ML_MAIN], w_in[:, o_gd:o_gd + GD_MAIN], w_in[:, o_gate:]], axis=1).astype(BF16)
    w_rw = w_in[:, o_rw:o_rw + RW_COLS].astype(BF16)
    n_small = 2 * ML_HEADS + 2 * GD_HEADS
    w_small = jnp.concatenate(
        [w_in[:, o_if:o_if + 2 * ML_HEADS], w_in[:, o_ba:o_ba + 2 * GD_HEADS],
         jnp.zeros((d, SMALL_W - n_small), F32)], axis=1).astype(BF16)
    zrow = jnp.zeros((SMALL_W,), F32)
    q = dict(p)
    q.update(
        w_main=w_main, w_rw=w_rw, w_small=w_small,
        ml_bias_row=zrow.at[SM_ML_I:SM_ML_I + 2 * ML_HEADS].set(p["ml_b_if"]).reshape(1, SMALL_W),
        gd_alog_row=zrow.at[SM_GD_A:SM_GD_A + GD_HEADS].set(p["gd_a_log"]).reshape(1, SMALL_W),
        gd_dtb_row=zrow.at[SM_GD_A:SM_GD_A + GD_HEADS].set(p["gd_dt_bias"]).reshape(1, SMALL_W),
        w_branch_b=p["w_branch"].astype(BF16), w_out_b=p["w_out"].astype(BF16),
        w_ca_q_b=p["w_ca_q"].astype(BF16), w_ca_kv_b=p["w_ca_kv"].astype(BF16),
        w_ca_o_b=p["w_ca_o"].astype(BF16), w_up_b=p["w_up"].astype(BF16), w_down_b=p["w_down"].astype(BF16))
    return q


def _layer(h, mem_k, mem_v, st, p, bsz, t):
    d = h.shape[1]
    u = _rmsnorm(h, p["g_mix"], BF16)
    p_main = _matmul(u, p["w_main"])
    p_rw = _matmul(u, p["w_rw"])
    p_small = _matmul(u, p["w_small"])

    y_ml, ml_c, ml_n, ml_m = _mlstm(p_main, p_small, p["ml_bias_row"], p["ml_norm"],
                                    st["ml_C"], st["ml_n"], st["ml_m"], bsz, t)
    y_rw, rw_s = _rwkv(p_rw, st["rw_shift"], st["rw_S"], p, bsz, t)
    y_gd, gd_s = _gdn(p_main, p_small, p["gd_conv_w"], st["gd_conv"], p["gd_alog_row"], p["gd_dtb_row"],
                      p["gd_norm"], st["gd_S"], bsz, t)
    merged = _merge((y_ml, y_rw, y_gd), p["w_branch_b"], p_main, d)
    h = _matmul(merged, p["w_out_b"], residual=h)

    u = _rmsnorm(h, p["g_ca"], BF16)
    q = _matmul(u, p["w_ca_q_b"], out_dtype=BF16)
    o = _attention(q, mem_k, mem_v, bsz, t)
    h = _matmul(o, p["w_ca_o_b"], residual=h)

    u = _rmsnorm(h, p["g_ffn"], BF16)
    up = _matmul(u, p["w_up_b"])
    act = _ffn_act(up, st["ffn_conv"], p["ffn_conv_w"], bsz, t)
    h = _matmul(act, p["w_down_b"], residual=h)

    gd0 = ML_MAIN
    new_st = dict(
        ml_C=ml_c, ml_n=ml_n, ml_m=ml_m.reshape(bsz, ML_HEADS), rw_S=rw_s,
        rw_shift=p_rw.reshape(bsz, t, RW_COLS)[:, t - 1],
        gd_S=gd_s,
        gd_conv=p_main.reshape(bsz, t, -1)[:, t - (GD_CONV - 1):, gd0:gd0 + GD_QKV],
        ffn_conv=up.reshape(bsz, t, -1)[:, t - (FFN_CONV - 1):])
    return h, new_st


def _zero_state(bsz, d_ff2):
    return dict(
        ml_C=jnp.zeros((bsz, ML_HEADS, ML_DQK, ML_DV), F32), ml_n=jnp.zeros((bsz, ML_HEADS, ML_DQK), F32),
        ml_m=jnp.zeros((bsz, ML_HEADS), F32), rw_S=jnp.zeros((bsz, RW_HEADS, RW_N, RW_N), F32),
        rw_shift=jnp.zeros((bsz, RW_COLS), F32), gd_S=jnp.zeros((bsz, GD_HEADS, GD_DK, GD_DV), F32),
        gd_conv=jnp.zeros((bsz, GD_CONV - 1, GD_QKV), F32), ffn_conv=jnp.zeros((bsz, FFN_CONV - 1, d_ff2), F32))


def kernel(x_prompt, x_sample, cache_mem_k, cache_mem_v, state_mlstm_C, state_mlstm_n, state_mlstm_m, state_rwkv_S, state_rwkv_shift, state_gdn_S, state_gdn_conv, state_ffn_conv, mem_prompt, g_mix, w_in, ml_b_if, ml_norm, rw_mu, rw_w0, rw_w2, rw_a0, rw_a2, rw_g2, rw_k_k, rw_k_a, rw_r_k, rw_ln, gd_conv_w, gd_a_log, gd_dt_bias, gd_norm, w_branch, w_out, g_ca, g_mem, w_ca_q, w_ca_kv, w_ca_o, g_ffn, w_up, ffn_conv_w, w_down, g_final):
    bp, tp, d = x_prompt.shape
    bs, ts, _ = x_sample.shape
    depth = w_in.shape[0]
    n_mem = mem_prompt.shape[1]
    assert tp % CHUNK == 0 and ts % CHUNK == 0
    stacked = dict(g_mix=g_mix, w_in=w_in, ml_b_if=ml_b_if, ml_norm=ml_norm, rw_mu=rw_mu, rw_w0=rw_w0,
                   rw_w2=rw_w2, rw_a0=rw_a0, rw_a2=rw_a2, rw_g2=rw_g2, rw_k_k=rw_k_k, rw_k_a=rw_k_a,
                   rw_r_k=rw_r_k, rw_ln=rw_ln, gd_conv_w=gd_conv_w, gd_a_log=gd_a_log, gd_dt_bias=gd_dt_bias,
                   gd_norm=gd_norm, w_branch=w_branch, w_out=w_out, g_ca=g_ca, g_mem=g_mem, w_ca_q=w_ca_q,
                   w_ca_kv=w_ca_kv, w_ca_o=w_ca_o, g_ffn=g_ffn, w_up=w_up, ffn_conv_w=ffn_conv_w,
                   w_down=w_down)
    keys = ("ml_C", "ml_n", "ml_m", "rw_S", "rw_shift", "gd_S", "gd_conv", "ffn_conv")
    new_p = {k: [] for k in keys}
    new_s = {k: [] for k in keys}
    mem_k_list, mem_v_list = [], []
    hp = x_prompt.reshape(bp * tp, d)
    hs = x_sample.reshape(bs * ts, d)
    mem2d = mem_prompt.reshape(bp * n_mem, d)
    for l in range(depth):
        p = _prep_layer({k: v[l] for k, v in stacked.items()}, d)
        kv = _matmul(_rmsnorm(mem2d, p["g_mem"], BF16), p["w_ca_kv_b"])
        mk = kv[:, :CA_WIDTH].reshape(bp, n_mem, CA_WIDTH)
        mv = kv[:, CA_WIDTH:].reshape(bp, n_mem, CA_WIDTH)
        hp, stp = _layer(hp, mk.astype(BF16), mv.astype(BF16), _zero_state(bp, w_up.shape[2]), p, bp, tp)
        mem_k_list.append(mk.reshape(bp, n_mem, CA_HEADS, CA_HEAD_DIM))
        mem_v_list.append(mv.reshape(bp, n_mem, CA_HEADS, CA_HEAD_DIM))
        st_in = dict(ml_C=state_mlstm_C[l], ml_n=state_mlstm_n[l], ml_m=state_mlstm_m[l],
                     rw_S=state_rwkv_S[l], rw_shift=state_rwkv_shift[l], gd_S=state_gdn_S[l],
                     gd_conv=state_gdn_conv[l], ffn_conv=state_ffn_conv[l])
        ck = cache_mem_k[l].reshape(bs, n_mem, CA_WIDTH).astype(BF16)
        cv = cache_mem_v[l].reshape(bs, n_mem, CA_WIDTH).astype(BF16)
        hs, sts = _layer(hs, ck, cv, st_in, p, bs, ts)
        for k in keys:
            new_p[k].append(stp[k])
            new_s[k].append(sts[k])
    y_prompt = _rmsnorm(hp, g_final, F32).reshape(bp, tp, d)
    y_sample = _rmsnorm(hs, g_final, F32).reshape(bs, ts, d)
    outs = [y_prompt, y_sample, jnp.stack(mem_k_list), jnp.stack(mem_v_list)]
    outs += [jnp.stack(new_p[k]) for k in keys]
    outs += [jnp.stack(new_s[k]) for k in keys]
    return tuple(outs)
```

```python
import functools

import jax
import jax.numpy as jnp
from jax import lax
from jax.experimental import pallas as pl
from jax.experimental.pallas import tpu as pltpu

F32 = jnp.float32
BF16 = jnp.bfloat16

EPS = 1e-6
CHUNK = 64
SUB_CHUNKS = 4

ML_HEADS, ML_DQK, ML_DV = 4, 128, 256
ML_QK = ML_HEADS * ML_DQK
ML_WIDTH = ML_HEADS * ML_DV
ML_MAIN = 2 * ML_QK + 2 * ML_WIDTH
ML_PASSES = 3

RW_HEADS, RW_N = 16, 64
RW_WIDTH = RW_HEADS * RW_N
RW_W_RANK, RW_A_RANK, RW_G_RANK = 64, 64, 128
RW_COLS = 3 * RW_WIDTH + RW_W_RANK + RW_A_RANK + RW_G_RANK
RW_GN_EPS = 64e-5
RW_GROUP = 4
RW_PASSES = 1

GD_HEADS, GD_DK, GD_DV = 8, 128, 128
GD_QK = GD_HEADS * GD_DK
GD_WIDTH = GD_HEADS * GD_DV
GD_QKV = 2 * GD_QK + GD_WIDTH
GD_CONV = 4
GD_MAIN = GD_QKV + GD_WIDTH
GD_GROUP = 4
GD_PASSES = 1

BR_WIDTH = 1024
CA_HEADS, CA_HEAD_DIM = 4, 256
CA_WIDTH = CA_HEADS * CA_HEAD_DIM
FFN_CONV = 3

SMALL_W = 128
SM_ML_I, SM_ML_F, SM_GD_B, SM_GD_A = 0, ML_HEADS, 2 * ML_HEADS, 2 * ML_HEADS + GD_HEADS

V7X_VMEM_LIMIT = 56 * 1024 * 1024
TRI_BLOCK = 16


def _cparams(sem):
    return pltpu.CompilerParams(dimension_semantics=sem, vmem_limit_bytes=V7X_VMEM_LIMIT)


def _tile(dim, pref, quantum):
    if dim <= pref:
        return dim
    t = (pref // quantum) * quantum
    while t > quantum and dim % t:
        t -= quantum
    assert dim % t == 0, (dim, pref, quantum)
    return t


def _sub_chunks(t):
    n = SUB_CHUNKS
    while (t // CHUNK) % n:
        n -= 1
    return n


def _split2(a):
    hi = a.astype(BF16)
    lo = (a - hi.astype(F32)).astype(BF16)
    return hi, lo


def _dg(a, b, ca, cb):
    return lax.dot_general(a, b, (((ca,), (cb,)), ((), ())), preferred_element_type=F32)


def _dot3(a, b, ca=1, cb=0):
    ah, al = _split2(a)
    bh, bl = _split2(b)
    return _dg(ah, bh, ca, cb) + (_dg(al, bh, ca, cb) + _dg(ah, bl, ca, cb))


def _dotp(a, b, ca, cb, passes):
    if passes == 1:
        return _dg(a.astype(BF16), b.astype(BF16), ca, cb)
    return _dot3(a, b, ca, cb)


def _tri_masks(c):
    row = lax.broadcasted_iota(jnp.int32, (c, c), 0)
    col = lax.broadcasted_iota(jnp.int32, (c, c), 1)
    return row, col


def _cumsum_rows(x):
    c = x.shape[0]
    row, col = _tri_masks(c)
    tri = jnp.where(row >= col, 1.0, 0.0).astype(BF16)
    x0 = x.astype(BF16)
    r1 = x - x0.astype(F32)
    x1 = r1.astype(BF16)
    x2 = (r1 - x1.astype(F32)).astype(BF16)
    return _dg(tri, x0, 1, 0) + (_dg(tri, x1, 1, 0) + _dg(tri, x2, 1, 0))


def _tri_solve(low, rhs, c, passes):
    return _tri_solve_many([low], [rhs], c, passes)[0]


def _tri_solve_many(lows, rhss, c, passes):
    return _tri_apply_many(_tri_factor_many(lows, c, passes), rhss, passes)


def _tri_factor_many(lows, c, passes):
    mm = lambda a, b: _dotp(a, b, 1, 0, passes)
    row, col = _tri_masks(lows[0].shape[0])
    same = (row // TRI_BLOCK) == (col // TRI_BLOCK)
    eye = jnp.where(row == col, 1.0, 0.0).astype(F32)
    ps = [jnp.where(same, -low, 0.0) for low in lows]
    offs = [jnp.where(same, 0.0, low) for low in lows]
    xs = [eye + p for p in ps]
    steps = 1
    while steps * 2 < TRI_BLOCK:
        ps = [mm(p, p) for p in ps]
        xs = [x + mm(x, p) for x, p in zip(xs, ps)]
        steps *= 2
    ms = [mm(x, off) for x, off in zip(xs, offs)]
    nblk = c // TRI_BLOCK
    terms = []
    pws = ms
    k = 1
    while k < nblk:
        terms.append(pws)
        k *= 2
        if k < nblk:
            pws = [mm(pw, pw) for pw in pws]
    return xs, terms


def _tri_apply_many(factors, rhss, passes):
    mm = lambda a, b: _dotp(a, b, 1, 0, passes)
    xs, terms = factors
    us = [mm(x, rhs) for x, rhs in zip(xs, rhss)]
    for i in range(len(terms) - 1, 0, -1):
        us = [u + mm(t, u) for t, u in zip(terms[i], us)]
    if terms:
        us = [u - mm(t, u) for t, u in zip(terms[0], us)]
    return us


def _head_sums(x, hw):
    assert 2 * hw == 128
    lane = lax.broadcasted_iota(jnp.int32, (x.shape[0], 128), 1)
    lo = lane < hw
    out = []
    for s in range(x.shape[1] // 128):
        xs = x[:, s * 128:(s + 1) * 128]
        s_lo = jnp.sum(jnp.where(lo, xs, 0.0), axis=-1, keepdims=True)
        s_hi = jnp.sum(jnp.where(lo, 0.0, xs), axis=-1, keepdims=True)
        out.append(jnp.where(lo, s_lo, s_hi))
    return jnp.concatenate(out, axis=1)


def _softplus(x):
    return jnp.maximum(x, 0.0) + jnp.log1p(jnp.exp(-jnp.abs(x)))


def _sigmoid(x):
    return 1.0 / (1.0 + jnp.exp(-x))


def _silu(x):
    return x * _sigmoid(x)


def _shift_rows(x, prev8, k):
    xr = pltpu.roll(x, k, 0)
    pr = pltpu.roll(prev8, k, 0)
    row = lax.broadcasted_iota(jnp.int32, (8, x.shape[1]), 0)
    head = jnp.where(row < k, pr, xr[0:8])
    if x.shape[0] == 8:
        return head
    return jnp.concatenate([head, xr[8:]], axis=0)


def _rmsnorm_kernel(x_ref, g_ref, o_ref):
    x = x_ref[...]
    y = x * lax.rsqrt(jnp.mean(x * x, axis=-1, keepdims=True) + EPS)
    o_ref[...] = (y * g_ref[...]).astype(o_ref.dtype)


def _rmsnorm(x, g, out_dtype):
    m, d = x.shape
    tr = _tile(m, 256, 8)
    return pl.pallas_call(
        _rmsnorm_kernel,
        grid=(m // tr,),
        in_specs=[pl.BlockSpec((tr, d), lambda i: (i, 0)), pl.BlockSpec((1, d), lambda i: (0, 0))],
        out_specs=pl.BlockSpec((tr, d), lambda i: (i, 0)),
        out_shape=jax.ShapeDtypeStruct((m, d), out_dtype),
        compiler_params=_cparams(("parallel",)),
        name="rmsnorm",
    )(x, g.reshape(1, d))


def _mm_kernel(a_ref, w_ref, o_ref):
    o_ref[...] = jnp.dot(a_ref[...], w_ref[...], preferred_element_type=F32).astype(o_ref.dtype)


def _mm_res_kernel(a_ref, w_ref, r_ref, o_ref):
    o_ref[...] = r_ref[...] + jnp.dot(a_ref[...], w_ref[...], preferred_element_type=F32)


def _mm_tiles(m, k, n, has_residual):
    tm = _tile(m, 1024 if k <= 4096 else 512, 8)
    tn = _tile(n, 1024 if (k <= 4096 and not has_residual) else 512, 128)
    return tm, tn


def _matmul(a, w, residual=None, out_dtype=F32):
    m, k = a.shape
    n = w.shape[1]
    tm, tn = _mm_tiles(m, k, n, residual is not None)
    in_specs = [pl.BlockSpec((tm, k), lambda i, j: (i, 0)), pl.BlockSpec((k, tn), lambda i, j: (0, j))]
    args = [a, w]
    body = _mm_kernel
    if residual is not None:
        in_specs.append(pl.BlockSpec((tm, tn), lambda i, j: (i, j)))
        args.append(residual)
        body = _mm_res_kernel
    return pl.pallas_call(
        body,
        grid=(m // tm, n // tn),
        in_specs=in_specs,
        out_specs=pl.BlockSpec((tm, tn), lambda i, j: (i, j)),
        out_shape=jax.ShapeDtypeStruct((m, n), out_dtype),
        compiler_params=_cparams(("parallel", "arbitrary")),
        name="matmul",
    )(*args)


def _mlstm_kernel(q_ref, k_ref, v_ref, og_ref, sm_ref, bias_ref, norm_ref, c0_ref, n0_ref, m0_ref,
                  y_ref, c_ref, n_ref, m_ref):
    @pl.when(pl.program_id(1) == 0)
    def _():
        c_ref[...] = c0_ref[...]
        n_ref[...] = n0_ref[...]
        m_ref[...] = m0_ref[...]

    c = q_ref.shape[0]
    pre = sm_ref[...] + bias_ref[...]
    logf = -_softplus(-pre)
    cum = _cumsum_rows(logf)
    pre_t = pre.T
    cum_t = cum.T
    row, col = _tri_masks(c)
    causal = row >= col
    hs = range(ML_HEADS)
    q = [q_ref[:, h * ML_DQK:(h + 1) * ML_DQK] for h in hs]
    k = [k_ref[:, h * ML_DQK:(h + 1) * ML_DQK] * (ML_DQK ** -0.5) for h in hs]
    v = [v_ref[:, h * ML_DV:(h + 1) * ML_DV] for h in hs]
    b_col = [cum[:, SM_ML_F + h:SM_ML_F + h + 1] for h in hs]
    b_row = [cum_t[SM_ML_F + h:SM_ML_F + h + 1, :] for h in hs]
    i_col = [pre[:, SM_ML_I + h:SM_ML_I + h + 1] for h in hs]
    i_row = [pre_t[SM_ML_I + h:SM_ML_I + h + 1, :] for h in hs]
    m_old = [m_ref[0, :, h:h + 1] for h in hs]
    c_old = [c_ref[0, h] for h in hs]
    n_old = [n_ref[0, h:h + 1, :] for h in hs]

    qk = [_dotp(q[h], k[h], 1, 1, ML_PASSES) for h in hs]
    qc = [_dotp(q[h], c_old[h], 1, 0, ML_PASSES) for h in hs]
    dmat = [jnp.where(causal, b_col[h] - b_row[h] + i_row[h], -jnp.inf) for h in hs]
    mt = [jnp.maximum(b_col[h] + m_old[h], jnp.max(dmat[h], axis=-1, keepdims=True)) for h in hs]
    pmat = [jnp.exp(dmat[h] - mt[h]) * qk[h] for h in hs]
    inter = [jnp.exp(b_col[h] + m_old[h] - mt[h]) for h in hs]
    num = [inter[h] * qc[h] + _dotp(pmat[h], v[h], 1, 0, ML_PASSES) for h in hs]
    den = [inter[h] * jnp.sum(q[h] * n_old[h], axis=-1, keepdims=True)
           + jnp.sum(pmat[h], axis=-1, keepdims=True) for h in hs]
    hh = [num[h] / jnp.maximum(jnp.abs(den[h]), jnp.exp(-mt[h])) for h in hs]

    m_new = [mt[h][c - 1:c, :] for h in hs]
    b_last = [b_col[h][c - 1:c, :] for h in hs]
    kw = [k[h] * jnp.exp(b_last[h] - b_col[h] + i_col[h] - m_new[h]) for h in hs]
    dec = [jnp.exp(b_last[h] + m_old[h] - m_new[h]) for h in hs]
    c_new = [dec[h] * c_old[h] + _dotp(kw[h].T, v[h], 1, 0, ML_PASSES) for h in hs]
    for h in hs:
        c_ref[0, h] = c_new[h]
        n_ref[0, h:h + 1, :] = dec[h] * n_old[h] + jnp.sum(kw[h], axis=0, keepdims=True)
        m_ref[0, :, h:h + 1] = m_new[h]
        hn = hh[h] * lax.rsqrt(jnp.mean(hh[h] * hh[h], axis=-1, keepdims=True) + EPS)
        og = og_ref[:, h * ML_DV:(h + 1) * ML_DV]
        y = hn * norm_ref[:, h * ML_DV:(h + 1) * ML_DV] * _sigmoid(og)
        y_ref[:, h * ML_DV:(h + 1) * ML_DV] = y.astype(y_ref.dtype)


def _mlstm(p_main, p_small, bias_row, ml_norm, c0, n0, m0, bsz, t):
    nc = t // CHUNK
    rows = bsz * t
    rmap = lambda b, c: b * nc + c
    return pl.pallas_call(
        _mlstm_kernel,
        grid=(bsz, nc),
        in_specs=[
            pl.BlockSpec((CHUNK, ML_QK), lambda b, c: (rmap(b, c), 0)),
            pl.BlockSpec((CHUNK, ML_QK), lambda b, c: (rmap(b, c), 1)),
            pl.BlockSpec((CHUNK, ML_WIDTH), lambda b, c: (rmap(b, c), 1)),
            pl.BlockSpec((CHUNK, ML_WIDTH), lambda b, c: (rmap(b, c), 2)),
            pl.BlockSpec((CHUNK, SMALL_W), lambda b, c: (rmap(b, c), 0)),
            pl.BlockSpec((1, SMALL_W), lambda b, c: (0, 0)),
            pl.BlockSpec((1, ML_WIDTH), lambda b, c: (0, 0)),
            pl.BlockSpec((1, ML_HEADS, ML_DQK, ML_DV), lambda b, c: (b, 0, 0, 0)),
            pl.BlockSpec((1, ML_HEADS, ML_DQK), lambda b, c: (b, 0, 0)),
            pl.BlockSpec((1, 1, ML_HEADS), lambda b, c: (b, 0, 0)),
        ],
        out_specs=[
            pl.BlockSpec((CHUNK, ML_WIDTH), lambda b, c: (rmap(b, c), 0)),
            pl.BlockSpec((1, ML_HEADS, ML_DQK, ML_DV), lambda b, c: (b, 0, 0, 0)),
            pl.BlockSpec((1, ML_HEADS, ML_DQK), lambda b, c: (b, 0, 0)),
            pl.BlockSpec((1, 1, ML_HEADS), lambda b, c: (b, 0, 0)),
        ],
        out_shape=[
            jax.ShapeDtypeStruct((rows, ML_WIDTH), BF16),
            jax.ShapeDtypeStruct(c0.shape, F32),
            jax.ShapeDtypeStruct(n0.shape, F32),
            jax.ShapeDtypeStruct((bsz, 1, ML_HEADS), F32),
        ],
        compiler_params=_cparams(("parallel", "arbitrary")),
        name="mlstm",
    )(p_main, p_main, p_main, p_main, p_small, bias_row, ml_norm.reshape(1, ML_WIDTH),
      c0, n0, m0.reshape(bsz, 1, ML_HEADS))


def _rwkv_front(x, prev8, mu_ref, w0_ref, w2_ref, a0_ref, a2_ref, g2_ref, kk_ref, ka_ref, rk_ref):
    c = x.shape[0]
    w = RW_WIDTH
    xprev = _shift_rows(x, prev8, 1)
    xm = x + (xprev - x) * mu_ref[...]
    rr = xm[:, 0:w]
    rk = xm[:, w:2 * w]
    rv = xm[:, 2 * w:3 * w]
    xw = xm[:, 3 * w:3 * w + RW_W_RANK]
    xa = xm[:, 3 * w + RW_W_RANK:3 * w + RW_W_RANK + RW_A_RANK]
    xg = xm[:, 3 * w + RW_W_RANK + RW_A_RANK:]

    w_pre = w0_ref[...] + _dot3(jnp.tanh(xw), w2_ref[...])
    lw = -jnp.exp(-_softplus(-w_pre) - 0.5)
    a = _sigmoid(a0_ref[...] + _dot3(xa, a2_ref[...]))
    g_out = _dot3(_sigmoid(xg), g2_ref[...])
    kk_raw = rk * kk_ref[...]
    kmod = rk * (1.0 + (a - 1.0) * ka_ref[...])
    bon = rr * kmod * rk_ref[...]

    lc = _cumsum_rows(lw)
    l_end = lc[c - 1:c, :]
    l_mid = lc[c // 2 - 1:c // 2, :]
    lcc = lc - l_mid
    p_mid = jnp.exp(l_mid)
    p_in = jnp.exp(lcc)
    p_prev = jnp.exp(lcc - lw)
    p_inv = jnp.exp(-lcc)
    p_end = jnp.exp(l_end - lc)
    p_all = jnp.exp(l_end)

    kkn = kk_raw * lax.rsqrt(_head_sums(kk_raw * kk_raw, RW_N) + 1e-6)
    bb = kkn * a
    kkp = kkn * p_prev
    rp = rr * p_in
    kd = kmod * p_inv
    bd = bb * p_inv
    k_end = kmod * p_end
    b_end = bb * p_end
    return dict(kkp=kkp, rp=rp, kd=kd, bd=bd, k_end=k_end, b_end=b_end, rv=rv, p_mid=p_mid, p_all=p_all,
                bonus=_head_sums(bon, RW_N) * rv, g_out=g_out)


def _rw_groups():
    gw = RW_GROUP * RW_N
    return [slice(g * gw, (g + 1) * gw) for g in range(RW_HEADS // RW_GROUP)]


def _rw_masks(c):
    shape = (RW_GROUP * c, RW_GROUP * RW_N)
    head_eq = lax.broadcasted_iota(jnp.int32, shape, 0) // c == lax.broadcasted_iota(jnp.int32, shape, 1) // RW_N
    tn_ = lax.broadcasted_iota(jnp.int32, (c, RW_GROUP * RW_N), 0)
    sn_ = lax.broadcasted_iota(jnp.int32, (c, RW_GROUP * RW_N), 1) % RW_N
    return head_eq, tn_ > sn_, tn_ >= sn_


def _rw_blockdiag(z, head_eq):
    return jnp.where(head_eq, jnp.concatenate([z] * RW_GROUP, axis=0), 0.0)


def _rw_rowsum(z, c):
    out = z[0:c]
    for i in range(1, RW_GROUP):
        out = out + z[i * c:(i + 1) * c]
    return out


def _rwkv_mid(f, masks):
    kkp, rp, kd, bd, rv = f["kkp"], f["rp"], f["kd"], f["bd"], f["rv"]
    c = rv.shape[0]
    head_eq, strict, incl = masks
    groups = _rw_groups()
    nt = lambda x, y: _dotp(x, y, 1, 1, RW_PASSES)
    x2 = [jnp.concatenate([kkp[:, cs], rp[:, cs]], axis=0) for cs in groups]
    ab_b = [nt(x, _rw_blockdiag(bd[:, cs], head_eq)) for x, cs in zip(x2, groups)]
    ab_k = [nt(x, _rw_blockdiag(kd[:, cs], head_eq)) for x, cs in zip(x2, groups)]
    a_b = [_rw_blockdiag(jnp.where(strict, z[0:c], 0.0), head_eq) for z in ab_b]
    return dict(
        x2=x2, v_bd=[_rw_blockdiag(rv[:, cs], head_eq) for cs in groups],
        factors=_tri_factor_many(a_b, c, RW_PASSES),
        a_k=[jnp.where(strict, z[0:c], 0.0) for z in ab_k],
        r_k=[jnp.where(incl, z[c:], 0.0) for z in ab_k],
        r_b=[jnp.where(incl, z[c:], 0.0) for z in ab_b])


def _rwkv_chain(f, g, s_olds, ln_ref, masks):
    k_end, b_end, rv, p_mid, p_all = f["k_end"], f["b_end"], f["rv"], f["p_mid"], f["p_all"]
    c = rv.shape[0]
    groups = _rw_groups()
    head_eq = masks[0]
    nt = lambda x, y: _dotp(x, y, 1, 1, RW_PASSES)
    nn = lambda x, y: _dotp(x, y, 1, 0, RW_PASSES)
    v_bd = g["v_bd"]
    ab_s = [nt(x, _rw_blockdiag(s * p_mid[:, cs], head_eq))
            for x, s, cs in zip(g["x2"], s_olds, groups)]
    rhs = [_rw_blockdiag(z[0:c] + nn(ak, v), head_eq) for z, ak, v in zip(ab_s, g["a_k"], v_bd)]
    us = _tri_apply_many(g["factors"], rhs, RW_PASSES)
    ygs = [z[c:] + nn(rk_, v) - nn(rb_, u) for z, rk_, v, rb_, u in zip(ab_s, g["r_k"], v_bd, g["r_b"], us)]
    upds = [nn(jnp.concatenate([rv[:, cs], -_rw_rowsum(u, c)], axis=0).T,
               jnp.concatenate([k_end[:, cs], b_end[:, cs]], axis=0)) for cs, u in zip(groups, us)]
    s_news = [s * p_all[:, cs] + _rw_rowsum(jnp.where(head_eq, upd, 0.0), c)
              for cs, s, upd in zip(groups, s_olds, upds)]

    yh = jnp.concatenate(ygs, axis=1)
    mu_ = _head_sums(yh, RW_N) * (1.0 / RW_N)
    yc = yh - mu_
    var = _head_sums(yc * yc, RW_N) * (1.0 / RW_N)
    yn = yc * lax.rsqrt(var + RW_GN_EPS)
    return (yn * ln_ref[...] + f["bonus"]) * f["g_out"], s_news


def _rwkv_kernel(p_ref, shift_ref, mu_ref, w0_ref, w2_ref, a0_ref, a2_ref, g2_ref, kk_ref, ka_ref,
                 rk_ref, ln_ref, s0_ref, y_ref, s_ref, prev_ref):
    @pl.when(pl.program_id(1) == 0)
    def _():
        s_ref[...] = s0_ref[...]
        prev_ref[...] = jnp.broadcast_to(shift_ref[0], prev_ref.shape)

    c = CHUNK
    nsub = p_ref.shape[0] // c
    masks = _rw_masks(c)
    fronts = []
    for k in range(nsub):
        last8 = prev_ref[...] if k == 0 else p_ref[k * c - 8:k * c, :]
        fronts.append(_rwkv_front(p_ref[k * c:(k + 1) * c, :], last8, mu_ref, w0_ref, w2_ref, a0_ref, a2_ref,
                                  g2_ref, kk_ref, ka_ref, rk_ref))
    groups = _rw_groups()
    states = [s_ref[0, :, cs] for cs in groups]
    for k in range(nsub):
        out, states = _rwkv_chain(fronts[k], _rwkv_mid(fronts[k], masks), states, ln_ref, masks)
        y_ref[k * c:(k + 1) * c, :] = out.astype(y_ref.dtype)
    prev_ref[...] = p_ref[nsub * c - 8:nsub * c, :]
    for cs, s in zip(groups, states):
        s_ref[0, :, cs] = s


def _rwkv(p_rw, shift0, s0, p, bsz, t):
    step = CHUNK * _sub_chunks(t)
    nc = t // step
    rows = bsz * t
    rmap = lambda b, c: b * nc + c
    full = lambda shape: pl.BlockSpec(shape, lambda b, c: (0,) * len(shape))
    w = RW_WIDTH
    s_nat = s0.transpose(0, 2, 1, 3).reshape(bsz, RW_N, w)
    y, s = pl.pallas_call(
        _rwkv_kernel,
        grid=(bsz, nc),
        in_specs=[
            pl.BlockSpec((step, RW_COLS), lambda b, c: (rmap(b, c), 0)),
            pl.BlockSpec((1, 1, RW_COLS), lambda b, c: (b, 0, 0)),
            full((1, RW_COLS)), full((1, w)), full((RW_W_RANK, w)), full((1, w)), full((RW_A_RANK, w)),
            full((RW_G_RANK, w)), full((1, w)), full((1, w)), full((1, w)), full((1, w)),
            pl.BlockSpec((1, RW_N, w), lambda b, c: (b, 0, 0)),
        ],
        out_specs=[
            pl.BlockSpec((step, w), lambda b, c: (rmap(b, c), 0)),
            pl.BlockSpec((1, RW_N, w), lambda b, c: (b, 0, 0)),
        ],
        out_shape=[jax.ShapeDtypeStruct((rows, w), BF16), jax.ShapeDtypeStruct(s_nat.shape, F32)],
        scratch_shapes=[pltpu.VMEM((8, RW_COLS), F32)],
        compiler_params=_cparams(("parallel", "arbitrary")),
        name="rwkv7",
    )(p_rw, shift0.reshape(bsz, 1, RW_COLS), p["rw_mu"].reshape(1, RW_COLS), p["rw_w0"].reshape(1, w),
      p["rw_w2"], p["rw_a0"].reshape(1, w), p["rw_a2"], p["rw_g2"], p["rw_k_k"].reshape(1, w),
      p["rw_k_a"].reshape(1, w), p["rw_r_k"].reshape(1, w), p["rw_ln"].reshape(1, w), s_nat)
    return y, s.reshape(bsz, RW_N, RW_HEADS, RW_N).transpose(0, 2, 1, 3)


def _gdn_kernel(q_ref, k_ref, v_ref, z_ref, sm_ref, cw_ref, conv0_ref, alog_ref, dtb_ref, norm_ref, s0_ref,
                y_ref, s_ref, prev_ref):
    @pl.when(pl.program_id(1) == 0)
    def _():
        s_ref[...] = s0_ref[...]
        prev_ref[...] = jnp.zeros(prev_ref.shape, F32)
        prev_ref[8 - (GD_CONV - 1):8, :] = conv0_ref[0]

    c = q_ref.shape[0]
    acts = []
    for sec, ref in enumerate((q_ref, k_ref, v_ref)):
        cs = slice(sec * GD_QK, (sec + 1) * GD_QK)
        x = ref[...]
        prev8 = prev_ref[:, cs]
        y = _shift_rows(x, prev8, GD_CONV - 1) * cw_ref[0:1, cs]
        for i in range(1, GD_CONV - 1):
            y = y + _shift_rows(x, prev8, GD_CONV - 1 - i) * cw_ref[i:i + 1, cs]
        y = y + x * cw_ref[GD_CONV - 1:GD_CONV, cs]
        prev_ref[:, cs] = x[c - 8:c]
        acts.append(_silu(y))
    qa, ka, va = acts

    sm = sm_ref[...]
    beta = _sigmoid(sm)
    g = -jnp.exp(alog_ref[...]) * _softplus(sm + dtb_ref[...])
    gc = _cumsum_rows(g)
    gc_t = gc.T

    gr = GD_GROUP * c
    gk = GD_GROUP * GD_DK
    r4 = lax.broadcasted_iota(jnp.int32, (gr, gr), 0)
    c4 = lax.broadcasted_iota(jnp.int32, (gr, gr), 1)
    t_minus_s = jnp.where((r4 // c) == (c4 // c), (r4 % c) - (c4 % c), -1)
    m_strict = t_minus_s > 0
    m_incl = t_minus_s >= 0
    head_eq = (lax.broadcasted_iota(jnp.int32, (gr, gk), 0) // c
               == lax.broadcasted_iota(jnp.int32, (gr, gk), 1) // GD_DK)

    def rep(z):
        return jnp.concatenate([z] * GD_GROUP, axis=0)

    def stack(parts):
        return jnp.concatenate(parts, axis=0)

    kn = []
    qn = []
    for h in range(GD_HEADS):
        ks = slice(h * GD_DK, (h + 1) * GD_DK)
        q = qa[:, ks]
        qn.append(q * lax.rsqrt(jnp.sum(q * q, axis=-1, keepdims=True) + 1e-6) * (GD_DK ** -0.5))
        k = ka[:, ks]
        kn.append(k * lax.rsqrt(jnp.sum(k * k, axis=-1, keepdims=True) + 1e-6))
    groups = [range(g0, g0 + GD_GROUP) for g0 in range(0, GD_HEADS, GD_GROUP)]
    k_nat = [jnp.concatenate([kn[h] for h in hs], axis=1) for hs in groups]
    q_nat = [jnp.concatenate([qn[h] for h in hs], axis=1) for hs in groups]
    k_bd = [jnp.where(head_eq, rep(kk_), 0.0) for kk_ in k_nat]
    v_st = [stack([va[:, h * GD_DV:(h + 1) * GD_DV] for h in hs]) for hs in groups]
    b_col = [stack([beta[:, SM_GD_B + h:SM_GD_B + h + 1] for h in hs]) for hs in groups]
    g_col = [stack([gc[:, SM_GD_A + h:SM_GD_A + h + 1] for h in hs]) for hs in groups]
    g_row = [jnp.concatenate([gc_t[SM_GD_A + h:SM_GD_A + h + 1, :] for h in hs], axis=1) for hs in groups]
    s_old = [s_ref[0, hs[0]:hs[0] + GD_GROUP] for hs in groups]

    dec = [jnp.exp(jnp.where(m_incl, gc_ - gr_, -jnp.inf)) for gc_, gr_ in zip(g_col, g_row)]
    kq = [_dotp(stack([kk_, qq_]), kb, 1, 1, GD_PASSES) for kk_, qq_, kb in zip(k_nat, q_nat, k_bd)]
    kqs = [[_dotp(stack([kn[h], qn[h]]), s_old[gi][i], 1, 0, GD_PASSES) for i, h in enumerate(hs)]
           for gi, hs in enumerate(groups)]
    ks = [stack([z[0:c] for z in zs]) for zs in kqs]
    qs = [stack([z[c:] for z in zs]) for zs in kqs]
    low = [jnp.where(m_strict, b * rep(z[0:c]) * d, 0.0) for b, z, d in zip(b_col, kq, dec)]
    eg = [jnp.exp(gc_) for gc_ in g_col]
    rhs = [b * (v - e * z) for b, v, e, z in zip(b_col, v_st, eg, ks)]
    us = _tri_solve_many(low, rhs, c, GD_PASSES)
    os_ = [e * z + _dotp(rep(zq[c:]) * d, u, 1, 0, GD_PASSES)
           for e, z, zq, d, u in zip(eg, qs, kq, dec, us)]
    for gi, hs in enumerate(groups):
        for i, h in enumerate(hs):
            rs = slice(i * c, (i + 1) * c)
            gl = gc[c - 1:c, SM_GD_A + h:SM_GD_A + h + 1]
            kw_h = kn[h] * jnp.exp(gl - g_col[gi][rs])
            s_ref[0, h] = jnp.exp(gl) * s_old[gi][i] + _dotp(kw_h.T, us[gi][rs], 1, 0, GD_PASSES)
            o_h = os_[gi][rs]
            og = o_h * lax.rsqrt(jnp.mean(o_h * o_h, axis=-1, keepdims=True) + EPS) * norm_ref[...]
            vs = slice(h * GD_DV, (h + 1) * GD_DV)
            y_ref[:, vs] = (og * _silu(z_ref[:, vs])).astype(y_ref.dtype)


def _gdn(p_main, p_small, conv_w, conv0, alog_row, dtb_row, gd_norm, s0, bsz, t):
    nc = t // CHUNK
    rows = bsz * t
    rmap = lambda b, c: b * nc + c
    base = ML_MAIN // GD_QK
    assert ML_MAIN % GD_QK == 0
    full = lambda shape: pl.BlockSpec(shape, lambda b, c: (0,) * len(shape))
    return pl.pallas_call(
        _gdn_kernel,
        grid=(bsz, nc),
        in_specs=[
            pl.BlockSpec((CHUNK, GD_QK), lambda b, c: (rmap(b, c), base)),
            pl.BlockSpec((CHUNK, GD_QK), lambda b, c: (rmap(b, c), base + 1)),
            pl.BlockSpec((CHUNK, GD_WIDTH), lambda b, c: (rmap(b, c), base + 2)),
            pl.BlockSpec((CHUNK, GD_WIDTH), lambda b, c: (rmap(b, c), base + 3)),
            pl.BlockSpec((CHUNK, SMALL_W), lambda b, c: (rmap(b, c), 0)),
            full((GD_CONV, GD_QKV)),
            pl.BlockSpec((1, GD_CONV - 1, GD_QKV), lambda b, c: (b, 0, 0)),
            full((1, SMALL_W)), full((1, SMALL_W)), full((1, GD_DV)),
            pl.BlockSpec((1, GD_HEADS, GD_DK, GD_DV), lambda b, c: (b, 0, 0, 0)),
        ],
        out_specs=[
            pl.BlockSpec((CHUNK, GD_WIDTH), lambda b, c: (rmap(b, c), 0)),
            pl.BlockSpec((1, GD_HEADS, GD_DK, GD_DV), lambda b, c: (b, 0, 0, 0)),
        ],
        out_shape=[jax.ShapeDtypeStruct((rows, GD_WIDTH), BF16), jax.ShapeDtypeStruct(s0.shape, F32)],
        scratch_shapes=[pltpu.VMEM((8, GD_QKV), F32)],
        compiler_params=_cparams(("parallel", "arbitrary")),
        name="gdn",
    )(p_main, p_main, p_main, p_main, p_small, conv_w, conv0, alog_row, dtb_row, gd_norm.reshape(1, GD_DV), s0)


def _merge_kernel(y0_ref, y1_ref, y2_ref, w_ref, g0_ref, g1_ref, g2_ref, o_ref):
    acc = _sigmoid(g0_ref[...]) * jnp.dot(y0_ref[...], w_ref[0], preferred_element_type=F32)
    acc = acc + _sigmoid(g1_ref[...]) * jnp.dot(y1_ref[...], w_ref[1], preferred_element_type=F32)
    acc = acc + _sigmoid(g2_ref[...]) * jnp.dot(y2_ref[...], w_ref[2], preferred_element_type=F32)
    o_ref[...] = acc.astype(o_ref.dtype)


def _merge(ys, w_branch, p_main, d):
    m = ys[0].shape[0]
    tm = _tile(m, 1024, 8)
    tn = _tile(d, 512, 128)
    gate0 = ML_MAIN + GD_MAIN
    assert gate0 % tn == 0
    gb = gate0 // tn
    nb = d // tn
    yspec = pl.BlockSpec((tm, BR_WIDTH), lambda i, j: (i, 0))
    gspec = lambda b: pl.BlockSpec((tm, tn), lambda i, j: (i, gb + b * nb + j))
    return pl.pallas_call(
        _merge_kernel,
        grid=(m // tm, nb),
        in_specs=[yspec, yspec, yspec, pl.BlockSpec((3, BR_WIDTH, tn), lambda i, j: (0, 0, j)),
                  gspec(0), gspec(1), gspec(2)],
        out_specs=pl.BlockSpec((tm, tn), lambda i, j: (i, j)),
        out_shape=jax.ShapeDtypeStruct((m, d), BF16),
        compiler_params=_cparams(("parallel", "arbitrary")),
        name="merge",
    )(ys[0], ys[1], ys[2], w_branch, p_main, p_main, p_main)


def _attn_kernel(q_ref, k_ref, v_ref, o_ref):
    for h in range(CA_HEADS):
        hs = slice(h * CA_HEAD_DIM, (h + 1) * CA_HEAD_DIM)
        s = _dg(q_ref[:, hs], k_ref[0, :, hs], 1, 1) * (CA_HEAD_DIM ** -0.5)
        s = s - jnp.max(s, axis=-1, keepdims=True)
        e = jnp.exp(s)
        pr = e / jnp.sum(e, axis=-1, keepdims=True)
        o = jnp.dot(pr.astype(BF16), v_ref[0, :, hs], preferred_element_type=F32)
        o_ref[:, hs] = o.astype(o_ref.dtype)


def _attention(q, mem_k, mem_v, bsz, t):
    tq = _tile(t, 512, 8)
    nt = t // tq
    n_mem = mem_k.shape[1]
    return pl.pallas_call(
        _attn_kernel,
        grid=(bsz, nt),
        in_specs=[
            pl.BlockSpec((tq, CA_WIDTH), lambda b, i: (b * nt + i, 0)),
            pl.BlockSpec((1, n_mem, CA_WIDTH), lambda b, i: (b, 0, 0)),
            pl.BlockSpec((1, n_mem, CA_WIDTH), lambda b, i: (b, 0, 0)),
        ],
        out_specs=pl.BlockSpec((tq, CA_WIDTH), lambda b, i: (b * nt + i, 0)),
        out_shape=jax.ShapeDtypeStruct((bsz * t, CA_WIDTH), BF16),
        compiler_params=_cparams(("parallel", "arbitrary")),
        name="mem_attention",
    )(q, mem_k, mem_v)


def _ffn_conv(x, prev, w):
    row = lax.broadcasted_iota(jnp.int32, x.shape, 0)
    x1 = jnp.where(row == 0, prev[1:2], pltpu.roll(x, 1, 0))
    x2 = jnp.where(row == 0, prev[0:1], jnp.where(row == 1, prev[1:2], pltpu.roll(x, 2, 0)))
    return x2 * w[0:1] + x1 * w[1:2] + x * w[2:3]


def _ffn_act_kernel(a_ref, g_ref, pa_ref, pg_ref, wa_ref, wg_ref, o_ref):
    fa = _ffn_conv(a_ref[...], pa_ref[0, 0], wa_ref[...])
    fg = _ffn_conv(g_ref[...], pg_ref[0, 0], wg_ref[...])
    o_ref[...] = (_silu(fg) * fa).astype(o_ref.dtype)


def _ffn_act(up, conv0, conv_w, bsz, t):
    c2 = up.shape[1]
    f = c2 // 2
    tr = _tile(t, 512, 8)
    nt = t // tr
    tc = _tile(f, 1024, 128)
    nj = f // tc
    tails = up.reshape(bsz, nt, tr, c2)[:, :nt - 1, tr - (FFN_CONV - 1):, :]
    prev = jnp.concatenate([conv0[:, None], tails], axis=1)
    return pl.pallas_call(
        _ffn_act_kernel,
        grid=(bsz, nt, nj),
        in_specs=[
            pl.BlockSpec((tr, tc), lambda b, i, j: (b * nt + i, j)),
            pl.BlockSpec((tr, tc), lambda b, i, j: (b * nt + i, nj + j)),
            pl.BlockSpec((1, 1, FFN_CONV - 1, tc), lambda b, i, j: (b, i, 0, j)),
            pl.BlockSpec((1, 1, FFN_CONV - 1, tc), lambda b, i, j: (b, i, 0, nj + j)),
            pl.BlockSpec((FFN_CONV, tc), lambda b, i, j: (0, j)),
            pl.BlockSpec((FFN_CONV, tc), lambda b, i, j: (0, nj + j)),
        ],
        out_specs=pl.BlockSpec((tr, tc), lambda b, i, j: (b * nt + i, j)),
        out_shape=jax.ShapeDtypeStruct((bsz * t, f), BF16),
        compiler_params=_cparams(("parallel", "parallel", "arbitrary")),
        name="ffn_conv_act",
    )(up, up, prev, prev, conv_w, conv_w)


def _prep_layer(p, d):
    w_in = p["w_in"]
    o_ml = 0
    o_if = ML_MAIN
    o_rw = o_if + 2 * ML_HEADS
    o_gd = o_rw + RW_COLS
    o_ba = o_gd + GD_MAIN
    o_gate = o_ba + 2 * GD_HEADS
    w_main = jnp.concatenate(
        [w_in[:, o_ml:o_ml + ML_MAIN], w_in[:, o_gd:o_gd + GD_MAIN], w_in[:, o_gate:]], axis=1).astype(BF16)
    w_rw = w_in[:, o_rw:o_rw + RW_COLS].astype(BF16)
    n_small = 2 * ML_HEADS + 2 * GD_HEADS
    w_small = jnp.concatenate(
        [w_in[:, o_if:o_if + 2 * ML_HEADS], w_in[:, o_ba:o_ba + 2 * GD_HEADS],
         jnp.zeros((d, SMALL_W - n_small), F32)], axis=1).astype(BF16)
    zrow = jnp.zeros((SMALL_W,), F32)
    q = dict(p)
    q.update(
        w_main=w_main, w_rw=w_rw, w_small=w_small,
        ml_bias_row=zrow.at[SM_ML_I:SM_ML_I + 2 * ML_HEADS].set(p["ml_b_if"]).reshape(1, SMALL_W),
        gd_alog_row=zrow.at[SM_GD_A:SM_GD_A + GD_HEADS].set(p["gd_a_log"]).reshape(1, SMALL_W),
        gd_dtb_row=zrow.at[SM_GD_A:SM_GD_A + GD_HEADS].set(p["gd_dt_bias"]).reshape(1, SMALL_W),
        w_branch_b=p["w_branch"].astype(BF16), w_out_b=p["w_out"].astype(BF16),
        w_ca_q_b=p["w_ca_q"].astype(BF16), w_ca_kv_b=p["w_ca_kv"].astype(BF16),
        w_ca_o_b=p["w_ca_o"].astype(BF16), w_up_b=p["w_up"].astype(BF16), w_down_b=p["w_down"].astype(BF16))
    return q


def _layer(h, mem_k, mem_v, st, p, bsz, t):
    d = h.shape[1]
    u = _rmsnorm(h, p["g_mix"], BF16)
    p_main = _matmul(u, p["w_main"])
    p_rw = _matmul(u, p["w_rw"])
    p_small = _matmul(u, p["w_small"])

    y_ml, ml_c, ml_n, ml_m = _mlstm(p_main, p_small, p["ml_bias_row"], p["ml_norm"],
                                    st["ml_C"], st["ml_n"], st["ml_m"], bsz, t)
    y_rw, rw_s = _rwkv(p_rw, st["rw_shift"], st["rw_S"], p, bsz, t)
    y_gd, gd_s = _gdn(p_main, p_small, p["gd_conv_w"], st["gd_conv"], p["gd_alog_row"], p["gd_dtb_row"],
                      p["gd_norm"], st["gd_S"], bsz, t)
    merged = _merge((y_ml, y_rw, y_gd), p["w_branch_b"], p_main, d)
    h = _matmul(merged, p["w_out_b"], residual=h)

    u = _rmsnorm(h, p["g_ca"], BF16)
    q = _matmul(u, p["w_ca_q_b"], out_dtype=BF16)
    o = _attention(q, mem_k, mem_v, bsz, t)
    h = _matmul(o, p["w_ca_o_b"], residual=h)

    u = _rmsnorm(h, p["g_ffn"], BF16)
    up = _matmul(u, p["w_up_b"])
    act = _ffn_act(up, st["ffn_conv"], p["ffn_conv_w"], bsz, t)
    h = _matmul(act, p["w_down_b"], residual=h)

    gd0 = ML_MAIN
    new_st = dict(
        ml_C=ml_c, ml_n=ml_n, ml_m=ml_m.reshape(bsz, ML_HEADS), rw_S=rw_s,
        rw_shift=p_rw.reshape(bsz, t, RW_COLS)[:, t - 1],
        gd_S=gd_s,
        gd_conv=p_main.reshape(bsz, t, -1)[:, t - (GD_CONV - 1):, gd0:gd0 + GD_QKV],
        ffn_conv=up.reshape(bsz, t, -1)[:, t - (FFN_CONV - 1):])
    return h, new_st


def _zero_state(bsz, d_ff2):
    return dict(
        ml_C=jnp.zeros((bsz, ML_HEADS, ML_DQK, ML_DV), F32), ml_n=jnp.zeros((bsz, ML_HEADS, ML_DQK), F32),
        ml_m=jnp.zeros((bsz, ML_HEADS), F32), rw_S=jnp.zeros((bsz, RW_HEADS, RW_N, RW_N), F32),
        rw_shift=jnp.zeros((bsz, RW_COLS), F32), gd_S=jnp.zeros((bsz, GD_HEADS, GD_DK, GD_DV), F32),
        gd_conv=jnp.zeros((bsz, GD_CONV - 1, GD_QKV), F32), ffn_conv=jnp.zeros((bsz, FFN_CONV - 1, d_ff2), F32))


def kernel(x_prompt, x_sample, cache_mem_k, cache_mem_v, state_mlstm_C, state_mlstm_n, state_mlstm_m, state_rwkv_S, state_rwkv_shift, state_gdn_S, state_gdn_conv, state_ffn_conv, mem_prompt, g_mix, w_in, ml_b_if, ml_norm, rw_mu, rw_w0, rw_w2, rw_a0, rw_a2, rw_g2, rw_k_k, rw_k_a, rw_r_k, rw_ln, gd_conv_w, gd_a_log, gd_dt_bias, gd_norm, w_branch, w_out, g_ca, g_mem, w_ca_q, w_ca_kv, w_ca_o, g_ffn, w_up, ffn_conv_w, w_down, g_final):
    bp, tp, d = x_prompt.shape
    bs, ts, _ = x_sample.shape
    depth = w_in.shape[0]
    n_mem = mem_prompt.shape[1]
    assert tp % CHUNK == 0 and ts % CHUNK == 0
    stacked = dict(g_mix=g_mix, w_in=w_in, ml_b_if=ml_b_if, ml_norm=ml_norm, rw_mu=rw_mu, rw_w0=rw_w0,
                   rw_w2=rw_w2, rw_a0=rw_a0, rw_a2=rw_a2, rw_g2=rw_g2, rw_k_k=rw_k_k, rw_k_a=rw_k_a,
                   rw_r_k=rw_r_k, rw_ln=rw_ln, gd_conv_w=gd_conv_w, gd_a_log=gd_a_log, gd_dt_bias=gd_dt_bias,
                   gd_norm=gd_norm, w_branch=w_branch, w_out=w_out, g_ca=g_ca, g_mem=g_mem, w_ca_q=w_ca_q,
                   w_ca_kv=w_ca_kv, w_ca_o=w_ca_o, g_ffn=g_ffn, w_up=w_up, ffn_conv_w=ffn_conv_w,
                   w_down=w_down)
    keys = ("ml_C", "ml_n", "ml_m", "rw_S", "rw_shift", "gd_S", "gd_conv", "ffn_conv")
    new_p = {k: [] for k in keys}
    new_s = {k: [] for k in keys}
    mem_k_list, mem_v_list = [], []
    hp = x_prompt.reshape(bp * tp, d)
    hs = x_sample.reshape(bs * ts, d)
    mem2d = mem_prompt.reshape(bp * n_mem, d)
    for l in range(depth):
        p = _prep_layer({k: v[l] for k, v in stacked.items()}, d)
        kv = _matmul(_rmsnorm(mem2d, p["g_mem"], BF16), p["w_ca_kv_b"])
        mk = kv[:, :CA_WIDTH].reshape(bp, n_mem, CA_WIDTH)
        mv = kv[:, CA_WIDTH:].reshape(bp, n_mem, CA_WIDTH)
        hp, stp = _layer(hp, mk.astype(BF16), mv.astype(BF16), _zero_state(bp, w_up.shape[2]), p, bp, tp)
        mem_k_list.append(mk.reshape(bp, n_mem, CA_HEADS, CA_HEAD_DIM))
        mem_v_list.append(mv.reshape(bp, n_mem, CA_HEADS, CA_HEAD_DIM))
        st_in = dict(ml_C=state_mlstm_C[l], ml_n=state_mlstm_n[l], ml_m=state_mlstm_m[l],
                     rw_S=state_rwkv_S[l], rw_shift=state_rwkv_shift[l], gd_S=state_gdn_S[l],
                     gd_conv=state_gdn_conv[l], ffn_conv=state_ffn_conv[l])
        ck = cache_mem_k[l].reshape(bs, n_mem, CA_WIDTH).astype(BF16)
        cv = cache_mem_v[l].reshape(bs, n_mem, CA_WIDTH).astype(BF16)
        hs, sts = _layer(hs, ck, cv, st_in, p, bs, ts)
        for k in keys:
            new_p[k].append(stp[k])
            new_s[k].append(sts[k])
    y_prompt = _rmsnorm(hp, g_final, F32).reshape(bp, tp, d)
    y_sample = _rmsnorm(hs, g_final, F32).reshape(bs, ts, d)
    outs = [y_prompt, y_sample, jnp.stack(mem_k_list), jnp.stack(mem_v_list)]
    outs += [jnp.stack(new_p[k]) for k in keys]
    outs += [jnp.stack(new_s[k]) for k in keys]
    return tuple(outs)
```

```python
import functools

import jax
import jax.numpy as jnp
from jax import lax
from jax.experimental import pallas as pl
from jax.experimental.pallas import tpu as pltpu

F32 = jnp.float32
BF16 = jnp.bfloat16

EPS = 1e-6
CHUNK = 64
SUB_CHUNKS = 4

ML_HEADS, ML_DQK, ML_DV = 4, 128, 256
ML_QK = ML_HEADS * ML_DQK
ML_WIDTH = ML_HEADS * ML_DV
ML_MAIN = 2 * ML_QK + 2 * ML_WIDTH
ML_PASSES = 3

RW_HEADS, RW_N = 16, 64
RW_WIDTH = RW_HEADS * RW_N
RW_W_RANK, RW_A_RANK, RW_G_RANK = 64, 64, 128
RW_COLS = 3 * RW_WIDTH + RW_W_RANK + RW_A_RANK + RW_G_RANK
RW_GN_EPS = 64e-5
RW_GROUP = 4
RW_PASSES = 1

GD_HEADS, GD_DK, GD_DV = 8, 128, 128
GD_QK = GD_HEADS * GD_DK
GD_WIDTH = GD_HEADS * GD_DV
GD_QKV = 2 * GD_QK + GD_WIDTH
GD_CONV = 4
GD_MAIN = GD_QKV + GD_WIDTH
GD_GROUP = 4
GD_PASSES = 1

BR_WIDTH = 1024
CA_HEADS, CA_HEAD_DIM = 4, 256
CA_WIDTH = CA_HEADS * CA_HEAD_DIM
FFN_CONV = 3
FFN_FUSE_MIN_T = 512
FFN_ROW_TILE = 1024
FFN_SUB_ROWS = 256

SMALL_W = 128
SM_ML_I, SM_ML_F, SM_GD_B, SM_GD_A = 0, ML_HEADS, 2 * ML_HEADS, 2 * ML_HEADS + GD_HEADS

V7X_VMEM_LIMIT = 56 * 1024 * 1024
TRI_BLOCK = 16


def _cparams(sem):
    return pltpu.CompilerParams(dimension_semantics=sem, vmem_limit_bytes=V7X_VMEM_LIMIT)


def _tile(dim, pref, quantum):
    if dim <= pref:
        return dim
    t = (pref // quantum) * quantum
    while t > quantum and dim % t:
        t -= quantum
    assert dim % t == 0, (dim, pref, quantum)
    return t


def _sub_chunks(t):
    n = SUB_CHUNKS
    while (t // CHUNK) % n:
        n -= 1
    return n


def _split2(a):
    hi = a.astype(BF16)
    lo = (a - hi.astype(F32)).astype(BF16)
    return hi, lo


def _dg(a, b, ca, cb):
    return lax.dot_general(a, b, (((ca,), (cb,)), ((), ())), preferred_element_type=F32)


def _dot3(a, b, ca=1, cb=0):
    ah, al = _split2(a)
    bh, bl = _split2(b)
    return _dg(ah, bh, ca, cb) + (_dg(al, bh, ca, cb) + _dg(ah, bl, ca, cb))


def _dotp(a, b, ca, cb, passes):
    if passes == 1:
        return _dg(a.astype(BF16), b.astype(BF16), ca, cb)
    return _dot3(a, b, ca, cb)


def _tri_masks(c):
    row = lax.broadcasted_iota(jnp.int32, (c, c), 0)
    col = lax.broadcasted_iota(jnp.int32, (c, c), 1)
    return row, col


def _cumsum_rows(x):
    c = x.shape[0]
    row, col = _tri_masks(c)
    tri = jnp.where(row >= col, 1.0, 0.0).astype(BF16)
    x0 = x.astype(BF16)
    r1 = x - x0.astype(F32)
    x1 = r1.astype(BF16)
    x2 = (r1 - x1.astype(F32)).astype(BF16)
    return _dg(tri, x0, 1, 0) + (_dg(tri, x1, 1, 0) + _dg(tri, x2, 1, 0))


def _tri_solve(low, rhs, c, passes):
    return _tri_solve_many([low], [rhs], c, passes)[0]


def _tri_solve_many(lows, rhss, c, passes):
    return _tri_apply_many(_tri_factor_many(lows, c, passes), rhss, passes)


def _tri_factor_many(lows, c, passes):
    mm = lambda a, b: _dotp(a, b, 1, 0, passes)
    row, col = _tri_masks(lows[0].shape[0])
    same = (row // TRI_BLOCK) == (col // TRI_BLOCK)
    eye = jnp.where(row == col, 1.0, 0.0).astype(F32)
    ps = [jnp.where(same, -low, 0.0) for low in lows]
    offs = [jnp.where(same, 0.0, low) for low in lows]
    xs = [eye + p for p in ps]
    steps = 1
    while steps * 2 < TRI_BLOCK:
        ps = [mm(p, p) for p in ps]
        xs = [x + mm(x, p) for x, p in zip(xs, ps)]
        steps *= 2
    ms = [mm(x, off) for x, off in zip(xs, offs)]
    nblk = c // TRI_BLOCK
    terms = []
    pws = ms
    k = 1
    while k < nblk:
        terms.append(pws)
        k *= 2
        if k < nblk:
            pws = [mm(pw, pw) for pw in pws]
    return xs, terms


def _tri_apply_many(factors, rhss, passes):
    mm = lambda a, b: _dotp(a, b, 1, 0, passes)
    xs, terms = factors
    us = [mm(x, rhs) for x, rhs in zip(xs, rhss)]
    for i in range(len(terms) - 1, 0, -1):
        us = [u + mm(t, u) for t, u in zip(terms[i], us)]
    if terms:
        us = [u - mm(t, u) for t, u in zip(terms[0], us)]
    return us


def _head_sums(x, hw):
    assert 2 * hw == 128
    lane = lax.broadcasted_iota(jnp.int32, (x.shape[0], 128), 1)
    lo = lane < hw
    out = []
    for s in range(x.shape[1] // 128):
        xs = x[:, s * 128:(s + 1) * 128]
        s_lo = jnp.sum(jnp.where(lo, xs, 0.0), axis=-1, keepdims=True)
        s_hi = jnp.sum(jnp.where(lo, 0.0, xs), axis=-1, keepdims=True)
        out.append(jnp.where(lo, s_lo, s_hi))
    return jnp.concatenate(out, axis=1)


def _softplus(x):
    return jnp.maximum(x, 0.0) + jnp.log1p(jnp.exp(-jnp.abs(x)))


def _sigmoid(x):
    return 1.0 / (1.0 + jnp.exp(-x))


def _silu(x):
    return x * _sigmoid(x)


def _shift_rows(x, prev8, k):
    xr = pltpu.roll(x, k, 0)
    pr = pltpu.roll(prev8, k, 0)
    row = lax.broadcasted_iota(jnp.int32, (8, x.shape[1]), 0)
    head = jnp.where(row < k, pr, xr[0:8])
    if x.shape[0] == 8:
        return head
    return jnp.concatenate([head, xr[8:]], axis=0)


def _rmsnorm_kernel(x_ref, g_ref, o_ref):
    x = x_ref[...]
    y = x * lax.rsqrt(jnp.mean(x * x, axis=-1, keepdims=True) + EPS)
    o_ref[...] = (y * g_ref[...]).astype(o_ref.dtype)


def _rmsnorm(x, g, out_dtype):
    m, d = x.shape
    tr = _tile(m, 256, 8)
    return pl.pallas_call(
        _rmsnorm_kernel,
        grid=(m // tr,),
        in_specs=[pl.BlockSpec((tr, d), lambda i: (i, 0)), pl.BlockSpec((1, d), lambda i: (0, 0))],
        out_specs=pl.BlockSpec((tr, d), lambda i: (i, 0)),
        out_shape=jax.ShapeDtypeStruct((m, d), out_dtype),
        compiler_params=_cparams(("parallel",)),
        name="rmsnorm",
    )(x, g.reshape(1, d))


def _mm_kernel(a_ref, w_ref, o_ref):
    o_ref[...] = jnp.dot(a_ref[...], w_ref[...], preferred_element_type=F32).astype(o_ref.dtype)


def _mm_res_kernel(a_ref, w_ref, r_ref, o_ref):
    o_ref[...] = r_ref[...] + jnp.dot(a_ref[...], w_ref[...], preferred_element_type=F32)


def _mm_tiles(m, k, n, has_residual):
    tm = _tile(m, 1024 if k <= 4096 else 512, 8)
    tn = _tile(n, 1024 if (k <= 4096 and not has_residual) else 512, 128)
    return tm, tn


def _matmul(a, w, residual=None, out_dtype=F32):
    m, k = a.shape
    n = w.shape[1]
    tm, tn = _mm_tiles(m, k, n, residual is not None)
    in_specs = [pl.BlockSpec((tm, k), lambda i, j: (i, 0)), pl.BlockSpec((k, tn), lambda i, j: (0, j))]
    args = [a, w]
    body = _mm_kernel
    if residual is not None:
        in_specs.append(pl.BlockSpec((tm, tn), lambda i, j: (i, j)))
        args.append(residual)
        body = _mm_res_kernel
    return pl.pallas_call(
        body,
        grid=(m // tm, n // tn),
        in_specs=in_specs,
        out_specs=pl.BlockSpec((tm, tn), lambda i, j: (i, j)),
        out_shape=jax.ShapeDtypeStruct((m, n), out_dtype),
        compiler_params=_cparams(("parallel", "arbitrary")),
        name="matmul",
    )(*args)


def _mlstm_kernel(q_ref, k_ref, v_ref, og_ref, sm_ref, bias_ref, norm_ref, c0_ref, n0_ref, m0_ref,
                  y_ref, c_ref, n_ref, m_ref):
    @pl.when(pl.program_id(1) == 0)
    def _():
        c_ref[...] = c0_ref[...]
        n_ref[...] = n0_ref[...]
        m_ref[...] = m0_ref[...]

    c = q_ref.shape[0]
    pre = sm_ref[...] + bias_ref[...]
    logf = -_softplus(-pre)
    cum = _cumsum_rows(logf)
    pre_t = pre.T
    cum_t = cum.T
    row, col = _tri_masks(c)
    causal = row >= col
    hs = range(ML_HEADS)
    q = [q_ref[:, h * ML_DQK:(h + 1) * ML_DQK] for h in hs]
    k = [k_ref[:, h * ML_DQK:(h + 1) * ML_DQK] * (ML_DQK ** -0.5) for h in hs]
    v = [v_ref[:, h * ML_DV:(h + 1) * ML_DV] for h in hs]
    b_col = [cum[:, SM_ML_F + h:SM_ML_F + h + 1] for h in hs]
    b_row = [cum_t[SM_ML_F + h:SM_ML_F + h + 1, :] for h in hs]
    i_col = [pre[:, SM_ML_I + h:SM_ML_I + h + 1] for h in hs]
    i_row = [pre_t[SM_ML_I + h:SM_ML_I + h + 1, :] for h in hs]
    m_old = [m_ref[0, :, h:h + 1] for h in hs]
    c_old = [c_ref[0, h] for h in hs]
    n_old = [n_ref[0, h:h + 1, :] for h in hs]

    qk = [_dotp(q[h], k[h], 1, 1, ML_PASSES) for h in hs]
    qc = [_dotp(q[h], c_old[h], 1, 0, ML_PASSES) for h in hs]
    dmat = [jnp.where(causal, b_col[h] - b_row[h] + i_row[h], -jnp.inf) for h in hs]
    mt = [jnp.maximum(b_col[h] + m_old[h], jnp.max(dmat[h], axis=-1, keepdims=True)) for h in hs]
    pmat = [jnp.exp(dmat[h] - mt[h]) * qk[h] for h in hs]
    inter = [jnp.exp(b_col[h] + m_old[h] - mt[h]) for h in hs]
    num = [inter[h] * qc[h] + _dotp(pmat[h], v[h], 1, 0, ML_PASSES) for h in hs]
    den = [inter[h] * jnp.sum(q[h] * n_old[h], axis=-1, keepdims=True)
           + jnp.sum(pmat[h], axis=-1, keepdims=True) for h in hs]
    hh = [num[h] / jnp.maximum(jnp.abs(den[h]), jnp.exp(-mt[h])) for h in hs]

    m_new = [mt[h][c - 1:c, :] for h in hs]
    b_last = [b_col[h][c - 1:c, :] for h in hs]
    kw = [k[h] * jnp.exp(b_last[h] - b_col[h] + i_col[h] - m_new[h]) for h in hs]
    dec = [jnp.exp(b_last[h] + m_old[h] - m_new[h]) for h in hs]
    c_new = [dec[h] * c_old[h] + _dotp(kw[h].T, v[h], 1, 0, ML_PASSES) for h in hs]
    for h in hs:
        c_ref[0, h] = c_new[h]
        n_ref[0, h:h + 1, :] = dec[h] * n_old[h] + jnp.sum(kw[h], axis=0, keepdims=True)
        m_ref[0, :, h:h + 1] = m_new[h]
        hn = hh[h] * lax.rsqrt(jnp.mean(hh[h] * hh[h], axis=-1, keepdims=True) + EPS)
        og = og_ref[:, h * ML_DV:(h + 1) * ML_DV]
        y = hn * norm_ref[:, h * ML_DV:(h + 1) * ML_DV] * _sigmoid(og)
        y_ref[:, h * ML_DV:(h + 1) * ML_DV] = y.astype(y_ref.dtype)


def _mlstm(p_main, p_small, bias_row, ml_norm, c0, n0, m0, bsz, t):
    nc = t // CHUNK
    rows = bsz * t
    rmap = lambda b, c: b * nc + c
    return pl.pallas_call(
        _mlstm_kernel,
        grid=(bsz, nc),
        in_specs=[
            pl.BlockSpec((CHUNK, ML_QK), lambda b, c: (rmap(b, c), 0)),
            pl.BlockSpec((CHUNK, ML_QK), lambda b, c: (rmap(b, c), 1)),
            pl.BlockSpec((CHUNK, ML_WIDTH), lambda b, c: (rmap(b, c), 1)),
            pl.BlockSpec((CHUNK, ML_WIDTH), lambda b, c: (rmap(b, c), 2)),
            pl.BlockSpec((CHUNK, SMALL_W), lambda b, c: (rmap(b, c), 0)),
            pl.BlockSpec((1, SMALL_W), lambda b, c: (0, 0)),
            pl.BlockSpec((1, ML_WIDTH), lambda b, c: (0, 0)),
            pl.BlockSpec((1, ML_HEADS, ML_DQK, ML_DV), lambda b, c: (b, 0, 0, 0)),
            pl.BlockSpec((1, ML_HEADS, ML_DQK), lambda b, c: (b, 0, 0)),
            pl.BlockSpec((1, 1, ML_HEADS), lambda b, c: (b, 0, 0)),
        ],
        out_specs=[
            pl.BlockSpec((CHUNK, ML_WIDTH), lambda b, c: (rmap(b, c), 0)),
            pl.BlockSpec((1, ML_HEADS, ML_DQK, ML_DV), lambda b, c: (b, 0, 0, 0)),
            pl.BlockSpec((1, ML_HEADS, ML_DQK), lambda b, c: (b, 0, 0)),
            pl.BlockSpec((1, 1, ML_HEADS), lambda b, c: (b, 0, 0)),
        ],
        out_shape=[
            jax.ShapeDtypeStruct((rows, ML_WIDTH), BF16),
            jax.ShapeDtypeStruct(c0.shape, F32),
            jax.ShapeDtypeStruct(n0.shape, F32),
            jax.ShapeDtypeStruct((bsz, 1, ML_HEADS), F32),
        ],
        compiler_params=_cparams(("parallel", "arbitrary")),
        name="mlstm",
    )(p_main, p_main, p_main, p_main, p_small, bias_row, ml_norm.reshape(1, ML_WIDTH),
      c0, n0, m0.reshape(bsz, 1, ML_HEADS))


def _rwkv_front(x, prev8, mu_ref, w0_ref, w2_ref, a0_ref, a2_ref, g2_ref, kk_ref, ka_ref, rk_ref):
    c = x.shape[0]
    w = RW_WIDTH
    xprev = _shift_rows(x, prev8, 1)
    xm = x + (xprev - x) * mu_ref[...]
    rr = xm[:, 0:w]
    rk = xm[:, w:2 * w]
    rv = xm[:, 2 * w:3 * w]
    xw = xm[:, 3 * w:3 * w + RW_W_RANK]
    xa = xm[:, 3 * w + RW_W_RANK:3 * w + RW_W_RANK + RW_A_RANK]
    xg = xm[:, 3 * w + RW_W_RANK + RW_A_RANK:]

    w_pre = w0_ref[...] + _dot3(jnp.tanh(xw), w2_ref[...])
    lw = -jnp.exp(-_softplus(-w_pre) - 0.5)
    a = _sigmoid(a0_ref[...] + _dot3(xa, a2_ref[...]))
    g_out = _dot3(_sigmoid(xg), g2_ref[...])
    kk_raw = rk * kk_ref[...]
    kmod = rk * (1.0 + (a - 1.0) * ka_ref[...])
    bon = rr * kmod * rk_ref[...]

    lc = _cumsum_rows(lw)
    l_end = lc[c - 1:c, :]
    l_mid = lc[c // 2 - 1:c // 2, :]
    lcc = lc - l_mid
    p_mid = jnp.exp(l_mid)
    p_in = jnp.exp(lcc)
    p_prev = jnp.exp(lcc - lw)
    p_inv = jnp.exp(-lcc)
    p_end = jnp.exp(l_end - lc)
    p_all = jnp.exp(l_end)

    kkn = kk_raw * lax.rsqrt(_head_sums(kk_raw * kk_raw, RW_N) + 1e-6)
    bb = kkn * a
    kkp = kkn * p_prev
    rp = rr * p_in
    kd = kmod * p_inv
    bd = bb * p_inv
    k_end = kmod * p_end
    b_end = bb * p_end
    return dict(kkp=kkp, rp=rp, kd=kd, bd=bd, k_end=k_end, b_end=b_end, rv=rv, p_mid=p_mid, p_all=p_all,
                bonus=_head_sums(bon, RW_N) * rv, g_out=g_out)


def _rw_groups():
    gw = RW_GROUP * RW_N
    return [slice(g * gw, (g + 1) * gw) for g in range(RW_HEADS // RW_GROUP)]


def _rw_masks(c):
    shape = (RW_GROUP * c, RW_GROUP * RW_N)
    head_eq = lax.broadcasted_iota(jnp.int32, shape, 0) // c == lax.broadcasted_iota(jnp.int32, shape, 1) // RW_N
    tn_ = lax.broadcasted_iota(jnp.int32, (c, RW_GROUP * RW_N), 0)
    sn_ = lax.broadcasted_iota(jnp.int32, (c, RW_GROUP * RW_N), 1) % RW_N
    return head_eq, tn_ > sn_, tn_ >= sn_


def _rw_blockdiag(z, head_eq):
    return jnp.where(head_eq, jnp.concatenate([z] * RW_GROUP, axis=0), 0.0)


def _rw_rowsum(z, c):
    out = z[0:c]
    for i in range(1, RW_GROUP):
        out = out + z[i * c:(i + 1) * c]
    return out


def _rwkv_mid(f, masks):
    kkp, rp, kd, bd, rv = f["kkp"], f["rp"], f["kd"], f["bd"], f["rv"]
    c = rv.shape[0]
    head_eq, strict, incl = masks
    groups = _rw_groups()
    nt = lambda x, y: _dotp(x, y, 1, 1, RW_PASSES)
    x2 = [jnp.concatenate([kkp[:, cs], rp[:, cs]], axis=0) for cs in groups]
    ab_b = [nt(x, _rw_blockdiag(bd[:, cs], head_eq)) for x, cs in zip(x2, groups)]
    ab_k = [nt(x, _rw_blockdiag(kd[:, cs], head_eq)) for x, cs in zip(x2, groups)]
    a_b = [_rw_blockdiag(jnp.where(strict, z[0:c], 0.0), head_eq) for z in ab_b]
    return dict(
        x2=x2, v_bd=[_rw_blockdiag(rv[:, cs], head_eq) for cs in groups],
        factors=_tri_factor_many(a_b, c, RW_PASSES),
        a_k=[jnp.where(strict, z[0:c], 0.0) for z in ab_k],
        r_k=[jnp.where(incl, z[c:], 0.0) for z in ab_k],
        r_b=[jnp.where(incl, z[c:], 0.0) for z in ab_b])


def _rwkv_chain(f, g, s_olds, ln_ref, masks):
    k_end, b_end, rv, p_mid, p_all = f["k_end"], f["b_end"], f["rv"], f["p_mid"], f["p_all"]
    c = rv.shape[0]
    groups = _rw_groups()
    head_eq = masks[0]
    nt = lambda x, y: _dotp(x, y, 1, 1, RW_PASSES)
    nn = lambda x, y: _dotp(x, y, 1, 0, RW_PASSES)
    v_bd = g["v_bd"]
    ab_s = [nt(x, _rw_blockdiag(s * p_mid[:, cs], head_eq))
            for x, s, cs in zip(g["x2"], s_olds, groups)]
    rhs = [_rw_blockdiag(z[0:c] + nn(ak, v), head_eq) for z, ak, v in zip(ab_s, g["a_k"], v_bd)]
    us = _tri_apply_many(g["factors"], rhs, RW_PASSES)
    ygs = [z[c:] + nn(rk_, v) - nn(rb_, u) for z, rk_, v, rb_, u in zip(ab_s, g["r_k"], v_bd, g["r_b"], us)]
    upds = [nn(jnp.concatenate([rv[:, cs], -_rw_rowsum(u, c)], axis=0).T,
               jnp.concatenate([k_end[:, cs], b_end[:, cs]], axis=0)) for cs, u in zip(groups, us)]
    s_news = [s * p_all[:, cs] + _rw_rowsum(jnp.where(head_eq, upd, 0.0), c)
              for cs, s, upd in zip(groups, s_olds, upds)]

    yh = jnp.concatenate(ygs, axis=1)
    mu_ = _head_sums(yh, RW_N) * (1.0 / RW_N)
    yc = yh - mu_
    var = _head_sums(yc * yc, RW_N) * (1.0 / RW_N)
    yn = yc * lax.rsqrt(var + RW_GN_EPS)
    return (yn * ln_ref[...] + f["bonus"]) * f["g_out"], s_news


def _rwkv_kernel(p_ref, shift_ref, mu_ref, w0_ref, w2_ref, a0_ref, a2_ref, g2_ref, kk_ref, ka_ref,
                 rk_ref, ln_ref, s0_ref, y_ref, s_ref, prev_ref):
    @pl.when(pl.program_id(1) == 0)
    def _():
        s_ref[...] = s0_ref[...]
        prev_ref[...] = jnp.broadcast_to(shift_ref[0], prev_ref.shape)

    c = CHUNK
    nsub = p_ref.shape[0] // c
    masks = _rw_masks(c)
    fronts = []
    for k in range(nsub):
        last8 = prev_ref[...] if k == 0 else p_ref[k * c - 8:k * c, :]
        fronts.append(_rwkv_front(p_ref[k * c:(k + 1) * c, :], last8, mu_ref, w0_ref, w2_ref, a0_ref, a2_ref,
                                  g2_ref, kk_ref, ka_ref, rk_ref))
    groups = _rw_groups()
    states = [s_ref[0, :, cs] for cs in groups]
    for k in range(nsub):
        out, states = _rwkv_chain(fronts[k], _rwkv_mid(fronts[k], masks), states, ln_ref, masks)
        y_ref[k * c:(k + 1) * c, :] = out.astype(y_ref.dtype)
    prev_ref[...] = p_ref[nsub * c - 8:nsub * c, :]
    for cs, s in zip(groups, states):
        s_ref[0, :, cs] = s


def _rwkv(p_rw, shift0, s0, p, bsz, t):
    step = CHUNK * _sub_chunks(t)
    nc = t // step
    rows = bsz * t
    rmap = lambda b, c: b * nc + c
    full = lambda shape: pl.BlockSpec(shape, lambda b, c: (0,) * len(shape))
    w = RW_WIDTH
    s_nat = s0.transpose(0, 2, 1, 3).reshape(bsz, RW_N, w)
    y, s = pl.pallas_call(
        _rwkv_kernel,
        grid=(bsz, nc),
        in_specs=[
            pl.BlockSpec((step, RW_COLS), lambda b, c: (rmap(b, c), 0)),
            pl.BlockSpec((1, 1, RW_COLS), lambda b, c: (b, 0, 0)),
            full((1, RW_COLS)), full((1, w)), full((RW_W_RANK, w)), full((1, w)), full((RW_A_RANK, w)),
            full((RW_G_RANK, w)), full((1, w)), full((1, w)), full((1, w)), full((1, w)),
            pl.BlockSpec((1, RW_N, w), lambda b, c: (b, 0, 0)),
        ],
        out_specs=[
            pl.BlockSpec((step, w), lambda b, c: (rmap(b, c), 0)),
            pl.BlockSpec((1, RW_N, w), lambda b, c: (b, 0, 0)),
        ],
        out_shape=[jax.ShapeDtypeStruct((rows, w), BF16), jax.ShapeDtypeStruct(s_nat.shape, F32)],
        scratch_shapes=[pltpu.VMEM((8, RW_COLS), F32)],
        compiler_params=_cparams(("parallel", "arbitrary")),
        name="rwkv7",
    )(p_rw, shift0.reshape(bsz, 1, RW_COLS), p["rw_mu"].reshape(1, RW_COLS), p["rw_w0"].reshape(1, w),
      p["rw_w2"], p["rw_a0"].reshape(1, w), p["rw_a2"], p["rw_g2"], p["rw_k_k"].reshape(1, w),
      p["rw_k_a"].reshape(1, w), p["rw_r_k"].reshape(1, w), p["rw_ln"].reshape(1, w), s_nat)
    return y, s.reshape(bsz, RW_N, RW_HEADS, RW_N).transpose(0, 2, 1, 3)


def _gdn_kernel(q_ref, k_ref, v_ref, z_ref, sm_ref, cw_ref, conv0_ref, alog_ref, dtb_ref, norm_ref, s0_ref,
                y_ref, s_ref, prev_ref):
    @pl.when(pl.program_id(1) == 0)
    def _():
        s_ref[...] = s0_ref[...]
        prev_ref[...] = jnp.zeros(prev_ref.shape, F32)
        prev_ref[8 - (GD_CONV - 1):8, :] = conv0_ref[0]

    c = q_ref.shape[0]
    acts = []
    for sec, ref in enumerate((q_ref, k_ref, v_ref)):
        cs = slice(sec * GD_QK, (sec + 1) * GD_QK)
        x = ref[...]
        prev8 = prev_ref[:, cs]
        y = _shift_rows(x, prev8, GD_CONV - 1) * cw_ref[0:1, cs]
        for i in range(1, GD_CONV - 1):
            y = y + _shift_rows(x, prev8, GD_CONV - 1 - i) * cw_ref[i:i + 1, cs]
        y = y + x * cw_ref[GD_CONV - 1:GD_CONV, cs]
        prev_ref[:, cs] = x[c - 8:c]
        acts.append(_silu(y))
    qa, ka, va = acts

    sm = sm_ref[...]
    beta = _sigmoid(sm)
    g = -jnp.exp(alog_ref[...]) * _softplus(sm + dtb_ref[...])
    gc = _cumsum_rows(g)
    gc_t = gc.T

    gr = GD_GROUP * c
    gk = GD_GROUP * GD_DK
    r4 = lax.broadcasted_iota(jnp.int32, (gr, gr), 0)
    c4 = lax.broadcasted_iota(jnp.int32, (gr, gr), 1)
    t_minus_s = jnp.where((r4 // c) == (c4 // c), (r4 % c) - (c4 % c), -1)
    m_strict = t_minus_s > 0
    m_incl = t_minus_s >= 0
    head_eq = (lax.broadcasted_iota(jnp.int32, (gr, gk), 0) // c
               == lax.broadcasted_iota(jnp.int32, (gr, gk), 1) // GD_DK)

    def rep(z):
        return jnp.concatenate([z] * GD_GROUP, axis=0)

    def stack(parts):
        return jnp.concatenate(parts, axis=0)

    kn = []
    qn = []
    for h in range(GD_HEADS):
        ks = slice(h * GD_DK, (h + 1) * GD_DK)
        q = qa[:, ks]
        qn.append(q * lax.rsqrt(jnp.sum(q * q, axis=-1, keepdims=True) + 1e-6) * (GD_DK ** -0.5))
        k = ka[:, ks]
        kn.append(k * lax.rsqrt(jnp.sum(k * k, axis=-1, keepdims=True) + 1e-6))
    groups = [range(g0, g0 + GD_GROUP) for g0 in range(0, GD_HEADS, GD_GROUP)]
    k_nat = [jnp.concatenate([kn[h] for h in hs], axis=1) for hs in groups]
    q_nat = [jnp.concatenate([qn[h] for h in hs], axis=1) for hs in groups]
    k_bd = [jnp.where(head_eq, rep(kk_), 0.0) for kk_ in k_nat]
    v_st = [stack([va[:, h * GD_DV:(h + 1) * GD_DV] for h in hs]) for hs in groups]
    b_col = [stack([beta[:, SM_GD_B + h:SM_GD_B + h + 1] for h in hs]) for hs in groups]
    g_col = [stack([gc[:, SM_GD_A + h:SM_GD_A + h + 1] for h in hs]) for hs in groups]
    g_row = [jnp.concatenate([gc_t[SM_GD_A + h:SM_GD_A + h + 1, :] for h in hs], axis=1) for hs in groups]
    s_old = [s_ref[0, hs[0]:hs[0] + GD_GROUP] for hs in groups]

    dec = [jnp.exp(jnp.where(m_incl, gc_ - gr_, -jnp.inf)) for gc_, gr_ in zip(g_col, g_row)]
    kq = [_dotp(stack([kk_, qq_]), kb, 1, 1, GD_PASSES) for kk_, qq_, kb in zip(k_nat, q_nat, k_bd)]
    kqs = [[_dotp(stack([kn[h], qn[h]]), s_old[gi][i], 1, 0, GD_PASSES) for i, h in enumerate(hs)]
           for gi, hs in enumerate(groups)]
    ks = [stack([z[0:c] for z in zs]) for zs in kqs]
    qs = [stack([z[c:] for z in zs]) for zs in kqs]
    low = [jnp.where(m_strict, b * rep(z[0:c]) * d, 0.0) for b, z, d in zip(b_col, kq, dec)]
    eg = [jnp.exp(gc_) for gc_ in g_col]
    rhs = [b * (v - e * z) for b, v, e, z in zip(b_col, v_st, eg, ks)]
    us = _tri_solve_many(low, rhs, c, GD_PASSES)
    os_ = [e * z + _dotp(rep(zq[c:]) * d, u, 1, 0, GD_PASSES)
           for e, z, zq, d, u in zip(eg, qs, kq, dec, us)]
    for gi, hs in enumerate(groups):
        for i, h in enumerate(hs):
            rs = slice(i * c, (i + 1) * c)
            gl = gc[c - 1:c, SM_GD_A + h:SM_GD_A + h + 1]
            kw_h = kn[h] * jnp.exp(gl - g_col[gi][rs])
            s_ref[0, h] = jnp.exp(gl) * s_old[gi][i] + _dotp(kw_h.T, us[gi][rs], 1, 0, GD_PASSES)
            o_h = os_[gi][rs]
            og = o_h * lax.rsqrt(jnp.mean(o_h * o_h, axis=-1, keepdims=True) + EPS) * norm_ref[...]
            vs = slice(h * GD_DV, (h + 1) * GD_DV)
            y_ref[:, vs] = (og * _silu(z_ref[:, vs])).astype(y_ref.dtype)


def _gdn(p_main, p_small, conv_w, conv0, alog_row, dtb_row, gd_norm, s0, bsz, t):
    nc = t // CHUNK
    rows = bsz * t
    rmap = lambda b, c: b * nc + c
    base = ML_MAIN // GD_QK
    assert ML_MAIN % GD_QK == 0
    full = lambda shape: pl.BlockSpec(shape, lambda b, c: (0,) * len(shape))
    return pl.pallas_call(
        _gdn_kernel,
        grid=(bsz, nc),
        in_specs=[
            pl.BlockSpec((CHUNK, GD_QK), lambda b, c: (rmap(b, c), base)),
            pl.BlockSpec((CHUNK, GD_QK), lambda b, c: (rmap(b, c), base + 1)),
            pl.BlockSpec((CHUNK, GD_WIDTH), lambda b, c: (rmap(b, c), base + 2)),
            pl.BlockSpec((CHUNK, GD_WIDTH), lambda b, c: (rmap(b, c), base + 3)),
            pl.BlockSpec((CHUNK, SMALL_W), lambda b, c: (rmap(b, c), 0)),
            full((GD_CONV, GD_QKV)),
            pl.BlockSpec((1, GD_CONV - 1, GD_QKV), lambda b, c: (b, 0, 0)),
            full((1, SMALL_W)), full((1, SMALL_W)), full((1, GD_DV)),
            pl.BlockSpec((1, GD_HEADS, GD_DK, GD_DV), lambda b, c: (b, 0, 0, 0)),
        ],
        out_specs=[
            pl.BlockSpec((CHUNK, GD_WIDTH), lambda b, c: (rmap(b, c), 0)),
            pl.BlockSpec((1, GD_HEADS, GD_DK, GD_DV), lambda b, c: (b, 0, 0, 0)),
        ],
        out_shape=[jax.ShapeDtypeStruct((rows, GD_WIDTH), BF16), jax.ShapeDtypeStruct(s0.shape, F32)],
        scratch_shapes=[pltpu.VMEM((8, GD_QKV), F32)],
        compiler_params=_cparams(("parallel", "arbitrary")),
        name="gdn",
    )(p_main, p_main, p_main, p_main, p_small, conv_w, conv0, alog_row, dtb_row, gd_norm.reshape(1, GD_DV), s0)


def _merge_kernel(y0_ref, y1_ref, y2_ref, w_ref, g0_ref, g1_ref, g2_ref, o_ref):
    acc = _sigmoid(g0_ref[...]) * jnp.dot(y0_ref[...], w_ref[0], preferred_element_type=F32)
    acc = acc + _sigmoid(g1_ref[...]) * jnp.dot(y1_ref[...], w_ref[1], preferred_element_type=F32)
    acc = acc + _sigmoid(g2_ref[...]) * jnp.dot(y2_ref[...], w_ref[2], preferred_element_type=F32)
    o_ref[...] = acc.astype(o_ref.dtype)


def _merge(ys, w_branch, p_main, d):
    m = ys[0].shape[0]
    tm = _tile(m, 1024, 8)
    tn = _tile(d, 512, 128)
    gate0 = ML_MAIN + GD_MAIN
    assert gate0 % tn == 0
    gb = gate0 // tn
    nb = d // tn
    yspec = pl.BlockSpec((tm, BR_WIDTH), lambda i, j: (i, 0))
    gspec = lambda b: pl.BlockSpec((tm, tn), lambda i, j: (i, gb + b * nb + j))
    return pl.pallas_call(
        _merge_kernel,
        grid=(m // tm, nb),
        in_specs=[yspec, yspec, yspec, pl.BlockSpec((3, BR_WIDTH, tn), lambda i, j: (0, 0, j)),
                  gspec(0), gspec(1), gspec(2)],
        out_specs=pl.BlockSpec((tm, tn), lambda i, j: (i, j)),
        out_shape=jax.ShapeDtypeStruct((m, d), BF16),
        compiler_params=_cparams(("parallel", "arbitrary")),
        name="merge",
    )(ys[0], ys[1], ys[2], w_branch, p_main, p_main, p_main)


def _attn_kernel(q_ref, k_ref, v_ref, o_ref):
    for h in range(CA_HEADS):
        hs = slice(h * CA_HEAD_DIM, (h + 1) * CA_HEAD_DIM)
        s = _dg(q_ref[:, hs], k_ref[0, :, hs], 1, 1) * (CA_HEAD_DIM ** -0.5)
        s = s - jnp.max(s, axis=-1, keepdims=True)
        e = jnp.exp(s)
        pr = e / jnp.sum(e, axis=-1, keepdims=True)
        o = jnp.dot(pr.astype(BF16), v_ref[0, :, hs], preferred_element_type=F32)
        o_ref[:, hs] = o.astype(o_ref.dtype)


def _attention(q, mem_k, mem_v, bsz, t):
    tq = _tile(t, 512, 8)
    nt = t // tq
    n_mem = mem_k.shape[1]
    return pl.pallas_call(
        _attn_kernel,
        grid=(bsz, nt),
        in_specs=[
            pl.BlockSpec((tq, CA_WIDTH), lambda b, i: (b * nt + i, 0)),
            pl.BlockSpec((1, n_mem, CA_WIDTH), lambda b, i: (b, 0, 0)),
            pl.BlockSpec((1, n_mem, CA_WIDTH), lambda b, i: (b, 0, 0)),
        ],
        out_specs=pl.BlockSpec((tq, CA_WIDTH), lambda b, i: (b * nt + i, 0)),
        out_shape=jax.ShapeDtypeStruct((bsz * t, CA_WIDTH), BF16),
        compiler_params=_cparams(("parallel", "arbitrary")),
        name="mem_attention",
    )(q, mem_k, mem_v)


def _ffn_conv(x, prev, w):
    row = lax.broadcasted_iota(jnp.int32, x.shape, 0)
    x1 = jnp.where(row == 0, prev[1:2], pltpu.roll(x, 1, 0))
    x2 = jnp.where(row == 0, prev[0:1], jnp.where(row == 1, prev[1:2], pltpu.roll(x, 2, 0)))
    return x2 * w[0:1] + x1 * w[1:2] + x * w[2:3]


def _ffn_act_kernel(a_ref, g_ref, pa_ref, pg_ref, wa_ref, wg_ref, o_ref):
    fa = _ffn_conv(a_ref[...], pa_ref[0, 0], wa_ref[...])
    fg = _ffn_conv(g_ref[...], pg_ref[0, 0], wg_ref[...])
    o_ref[...] = (_silu(fg) * fa).astype(o_ref.dtype)


def _ffn_act(up, conv0, conv_w, bsz, t):
    c2 = up.shape[1]
    f = c2 // 2
    tr = _tile(t, 512, 8)
    nt = t // tr
    tc = _tile(f, 1024, 128)
    nj = f // tc
    tails = up.reshape(bsz, nt, tr, c2)[:, :nt - 1, tr - (FFN_CONV - 1):, :]
    prev = jnp.concatenate([conv0[:, None], tails], axis=1)
    return pl.pallas_call(
        _ffn_act_kernel,
        grid=(bsz, nt, nj),
        in_specs=[
            pl.BlockSpec((tr, tc), lambda b, i, j: (b * nt + i, j)),
            pl.BlockSpec((tr, tc), lambda b, i, j: (b * nt + i, nj + j)),
            pl.BlockSpec((1, 1, FFN_CONV - 1, tc), lambda b, i, j: (b, i, 0, j)),
            pl.BlockSpec((1, 1, FFN_CONV - 1, tc), lambda b, i, j: (b, i, 0, nj + j)),
            pl.BlockSpec((FFN_CONV, tc), lambda b, i, j: (0, j)),
            pl.BlockSpec((FFN_CONV, tc), lambda b, i, j: (0, nj + j)),
        ],
        out_specs=pl.BlockSpec((tr, tc), lambda b, i, j: (b * nt + i, j)),
        out_shape=jax.ShapeDtypeStruct((bsz * t, f), BF16),
        compiler_params=_cparams(("parallel", "parallel", "arbitrary")),
        name="ffn_conv_act",
    )(up, up, prev, prev, conv_w, conv_w)


def _ffn_up_kernel(u_ref, wa_ref, wg_ref, cwa_ref, cwg_ref, c0a_ref, c0g_ref, act_ref, ta_ref, tg_ref, carry_ref,
                   *, tiles_per_seq, sub):
    i = pl.program_id(0)
    j = pl.program_id(1)
    tm, tn = act_ref.shape

    @pl.when(i == 0)
    def _():
        carry_ref[j] = jnp.zeros(carry_ref.shape[1:], F32)

    first = (i % tiles_per_seq) == 0
    pad = jnp.zeros((8 - (FFN_CONV - 1), tn), F32)
    carried = carry_ref[j]
    prev_a = jnp.where(first, jnp.concatenate([pad, c0a_ref[0]], axis=0), carried[0:8])
    prev_g = jnp.where(first, jnp.concatenate([pad, c0g_ref[0]], axis=0), carried[8:16])
    for r in range(tm // sub):
        rows = slice(r * sub, (r + 1) * sub)
        ur = u_ref[rows, :]
        za = jnp.dot(ur, wa_ref[...], preferred_element_type=F32)
        zg = jnp.dot(ur, wg_ref[...], preferred_element_type=F32)
        fa = (_shift_rows(za, prev_a, 2) * cwa_ref[0:1] + _shift_rows(za, prev_a, 1) * cwa_ref[1:2]
              + za * cwa_ref[2:3])
        fg = (_shift_rows(zg, prev_g, 2) * cwg_ref[0:1] + _shift_rows(zg, prev_g, 1) * cwg_ref[1:2]
              + zg * cwg_ref[2:3])
        act_ref[rows, :] = (_silu(fg) * fa).astype(act_ref.dtype)
        prev_a = za[sub - 8:sub]
        prev_g = zg[sub - 8:sub]
    carry_ref[j] = jnp.concatenate([prev_a, prev_g], axis=0)
    ta_ref[0] = prev_a[8 - (FFN_CONV - 1):8]
    tg_ref[0] = prev_g[8 - (FFN_CONV - 1):8]


def _ffn_up_act(u, w_up, conv0, conv_w, bsz, t):
    m, d = u.shape
    f = w_up.shape[1] // 2
    tm = _tile(t, FFN_ROW_TILE, 8)
    sub = _tile(tm, FFN_SUB_ROWS, 8)
    tn = _tile(f, 512, 128)
    nj = f // tn
    tps = t // tm
    act, ta, tg = pl.pallas_call(
        functools.partial(_ffn_up_kernel, tiles_per_seq=tps, sub=sub),
        grid=(m // tm, nj),
        in_specs=[
            pl.BlockSpec((tm, d), lambda i, j: (i, 0)),
            pl.BlockSpec((d, tn), lambda i, j: (0, j)),
            pl.BlockSpec((d, tn), lambda i, j: (0, nj + j)),
            pl.BlockSpec((FFN_CONV, tn), lambda i, j: (0, j)),
            pl.BlockSpec((FFN_CONV, tn), lambda i, j: (0, nj + j)),
            pl.BlockSpec((1, FFN_CONV - 1, tn), lambda i, j: (i // tps, 0, j)),
            pl.BlockSpec((1, FFN_CONV - 1, tn), lambda i, j: (i // tps, 0, nj + j)),
        ],
        out_specs=[
            pl.BlockSpec((tm, tn), lambda i, j: (i, j)),
            pl.BlockSpec((1, FFN_CONV - 1, tn), lambda i, j: (i // tps, 0, j)),
            pl.BlockSpec((1, FFN_CONV - 1, tn), lambda i, j: (i // tps, 0, j)),
        ],
        out_shape=[jax.ShapeDtypeStruct((m, f), BF16),
                   jax.ShapeDtypeStruct((bsz, FFN_CONV - 1, f), F32),
                   jax.ShapeDtypeStruct((bsz, FFN_CONV - 1, f), F32)],
        scratch_shapes=[pltpu.VMEM((nj, 16, tn), F32)],
        compiler_params=_cparams(("arbitrary", "arbitrary")),
        name="ffn_up_conv_act",
    )(u, w_up, w_up, conv_w, conv_w, conv0, conv0)
    return act, jnp.concatenate([ta, tg], axis=-1)


def _prep_layer(p, d):
    w_in = p["w_in"]
    o_ml = 0
    o_if = ML_MAIN
    o_rw = o_if + 2 * ML_HEADS
    o_gd = o_rw + RW_COLS
    o_ba = o_gd + GD_MAIN
    o_gate = o_ba + 2 * GD_HEADS
    w_main = jnp.concatenate(
        [w_in[:, o_ml:o_ml + ML_MAIN], w_in[:, o_gd:o_gd + GD_MAIN], w_in[:, o_gate:]], axis=1).astype(BF16)
    w_rw = w_in[:, o_rw:o_rw + RW_COLS].astype(BF16)
    n_small = 2 * ML_HEADS + 2 * GD_HEADS
    w_small = jnp.concatenate(
        [w_in[:, o_if:o_if + 2 * ML_HEADS], w_in[:, o_ba:o_ba + 2 * GD_HEADS],
         jnp.zeros((d, SMALL_W - n_small), F32)], axis=1).astype(BF16)
    zrow = jnp.zeros((SMALL_W,), F32)
    q = dict(p)
    q.update(
        w_main=w_main, w_rw=w_rw, w_small=w_small,
        ml_bias_row=zrow.at[SM_ML_I:SM_ML_I + 2 * ML_HEADS].set(p["ml_b_if"]).reshape(1, SMALL_W),
        gd_alog_row=zrow.at[SM_GD_A:SM_GD_A + GD_HEADS].set(p["gd_a_log"]).reshape(1, SMALL_W),
        gd_dtb_row=zrow.at[SM_GD_A:SM_GD_A + GD_HEADS].set(p["gd_dt_bias"]).reshape(1, SMALL_W),
        w_branch_b=p["w_branch"].astype(BF16), w_out_b=p["w_out"].astype(BF16),
        w_ca_q_b=p["w_ca_q"].astype(BF16), w_ca_kv_b=p["w_ca_kv"].astype(BF16),
        w_ca_o_b=p["w_ca_o"].astype(BF16), w_up_b=p["w_up"].astype(BF16), w_down_b=p["w_down"].astype(BF16))
    return q


def _layer(h, mem_k, mem_v, st, p, bsz, t):
    d = h.shape[1]
    u = _rmsnorm(h, p["g_mix"], BF16)
    p_main = _matmul(u, p["w_main"])
    p_rw = _matmul(u, p["w_rw"])
    p_small = _matmul(u, p["w_small"])

    y_ml, ml_c, ml_n, ml_m = _mlstm(p_main, p_small, p["ml_bias_row"], p["ml_norm"],
                                    st["ml_C"], st["ml_n"], st["ml_m"], bsz, t)
    y_rw, rw_s = _rwkv(p_rw, st["rw_shift"], st["rw_S"], p, bsz, t)
    y_gd, gd_s = _gdn(p_main, p_small, p["gd_conv_w"], st["gd_conv"], p["gd_alog_row"], p["gd_dtb_row"],
                      p["gd_norm"], st["gd_S"], bsz, t)
    merged = _merge((y_ml, y_rw, y_gd), p["w_branch_b"], p_main, d)
    h = _matmul(merged, p["w_out_b"], residual=h)

    u = _rmsnorm(h, p["g_ca"], BF16)
    q = _matmul(u, p["w_ca_q_b"], out_dtype=BF16)
    o = _attention(q, mem_k, mem_v, bsz, t)
    h = _matmul(o, p["w_ca_o_b"], residual=h)

    u = _rmsnorm(h, p["g_ffn"], BF16)
    if t >= FFN_FUSE_MIN_T:
        act, ffn_conv = _ffn_up_act(u, p["w_up_b"], st["ffn_conv"], p["ffn_conv_w"], bsz, t)
    else:
        up = _matmul(u, p["w_up_b"])
        act = _ffn_act(up, st["ffn_conv"], p["ffn_conv_w"], bsz, t)
        ffn_conv = up.reshape(bsz, t, -1)[:, t - (FFN_CONV - 1):]
    h = _matmul(act, p["w_down_b"], residual=h)

    gd0 = ML_MAIN
    new_st = dict(
        ml_C=ml_c, ml_n=ml_n, ml_m=ml_m.reshape(bsz, ML_HEADS), rw_S=rw_s,
        rw_shift=p_rw.reshape(bsz, t, RW_COLS)[:, t - 1],
        gd_S=gd_s,
        gd_conv=p_main.reshape(bsz, t, -1)[:, t - (GD_CONV - 1):, gd0:gd0 + GD_QKV],
        ffn_conv=ffn_conv)
    return h, new_st


def _zero_state(bsz, d_ff2):
    return dict(
        ml_C=jnp.zeros((bsz, ML_HEADS, ML_DQK, ML_DV), F32), ml_n=jnp.zeros((bsz, ML_HEADS, ML_DQK), F32),
        ml_m=jnp.zeros((bsz, ML_HEADS), F32), rw_S=jnp.zeros((bsz, RW_HEADS, RW_N, RW_N), F32),
        rw_shift=jnp.zeros((bsz, RW_COLS), F32), gd_S=jnp.zeros((bsz, GD_HEADS, GD_DK, GD_DV), F32),
        gd_conv=jnp.zeros((bsz, GD_CONV - 1, GD_QKV), F32), ffn_conv=jnp.zeros((bsz, FFN_CONV - 1, d_ff2), F32))


def kernel(x_prompt, x_sample, cache_mem_k, cache_mem_v, state_mlstm_C, state_mlstm_n, state_mlstm_m, state_rwkv_S, state_rwkv_shift, state_gdn_S, state_gdn_conv, state_ffn_conv, mem_prompt, g_mix, w_in, ml_b_if, ml_norm, rw_mu, rw_w0, rw_w2, rw_a0, rw_a2, rw_g2, rw_k_k, rw_k_a, rw_r_k, rw_ln, gd_conv_w, gd_a_log, gd_dt_bias, gd_norm, w_branch, w_out, g_ca, g_mem, w_ca_q, w_ca_kv, w_ca_o, g_ffn, w_up, ffn_conv_w, w_down, g_final):
    bp, tp, d = x_prompt.shape
    bs, ts, _ = x_sample.shape
    depth = w_in.shape[0]
    n_mem = mem_prompt.shape[1]
    assert tp % CHUNK == 0 and ts % CHUNK == 0
    stacked = dict(g_mix=g_mix, w_in=w_in, ml_b_if=ml_b_if, ml_norm=ml_norm, rw_mu=rw_mu, rw_w0=rw_w0,
                   rw_w2=rw_w2, rw_a0=rw_a0, rw_a2=rw_a2, rw_g2=rw_g2, rw_k_k=rw_k_k, rw_k_a=rw_k_a,
                   rw_r_k=rw_r_k, rw_ln=rw_ln, gd_conv_w=gd_conv_w, gd_a_log=gd_a_log, gd_dt_bias=gd_dt_bias,
                   gd_norm=gd_norm, w_branch=w_branch, w_out=w_out, g_ca=g_ca, g_mem=g_mem, w_ca_q=w_ca_q,
                   w_ca_kv=w_ca_kv, w_ca_o=w_ca_o, g_ffn=g_ffn, w_up=w_up, ffn_conv_w=ffn_conv_w,
                   w_down=w_down)
    keys = ("ml_C", "ml_n", "ml_m", "rw_S", "rw_shift", "gd_S", "gd_conv", "ffn_conv")
    new_p = {k: [] for k in keys}
    new_s = {k: [] for k in keys}
    mem_k_list, mem_v_list = [], []
    hp = x_prompt.reshape(bp * tp, d)
    hs = x_sample.reshape(bs * ts, d)
    mem2d = mem_prompt.reshape(bp * n_mem, d)
    for l in range(depth):
        p = _prep_layer({k: v[l] for k, v in stacked.items()}, d)
        kv = _matmul(_rmsnorm(mem2d, p["g_mem"], BF16), p["w_ca_kv_b"])
        mk = kv[:, :CA_WIDTH].reshape(bp, n_mem, CA_WIDTH)
        mv = kv[:, CA_WIDTH:].reshape(bp, n_mem, CA_WIDTH)
        hp, stp = _layer(hp, mk.astype(BF16), mv.astype(BF16), _zero_state(bp, w_up.shape[2]), p, bp, tp)
        mem_k_list.append(mk.reshape(bp, n_mem, CA_HEADS, CA_HEAD_DIM))
        mem_v_list.append(mv.reshape(bp, n_mem, CA_HEADS, CA_HEAD_DIM))
        st_in = dict(ml_C=state_mlstm_C[l], ml_n=state_mlstm_n[l], ml_m=state_mlstm_m[l],
                     rw_S=state_rwkv_S[l], rw_shift=state_rwkv_shift[l], gd_S=state_gdn_S[l],
                     gd_conv=state_gdn_conv[l], ffn_conv=state_ffn_conv[l])
        ck = cache_mem_k[l].reshape(bs, n_mem, CA_WIDTH).astype(BF16)
        cv = cache_mem_v[l].reshape(bs, n_mem, CA_WIDTH).astype(BF16)
        hs, sts = _layer(hs, ck, cv, st_in, p, bs, ts)
        for k in keys:
            new_p[k].append(stp[k])
            new_s[k].append(sts[k])
    y_prompt = _rmsnorm(hp, g_final, F32).reshape(bp, tp, d)
    y_sample = _rmsnorm(hs, g_final, F32).reshape(bs, ts, d)
    outs = [y_prompt, y_sample, jnp.stack(mem_k_list), jnp.stack(mem_v_list)]
    outs += [jnp.stack(new_p[k]) for k in keys]
    outs += [jnp.stack(new_s[k]) for k in keys]
    return tuple(outs)
```

```python
import functools

import jax
import jax.numpy as jnp
from jax import lax
from jax.experimental import pallas as pl
from jax.experimental.pallas import tpu as pltpu

F32 = jnp.float32
BF16 = jnp.bfloat16

EPS = 1e-6
CHUNK = 64
SUB_CHUNKS = 4

ML_HEADS, ML_DQK, ML_DV = 4, 128, 256
ML_QK = ML_HEADS * ML_DQK
ML_WIDTH = ML_HEADS * ML_DV
ML_MAIN = 2 * ML_QK + 2 * ML_WIDTH
ML_PASSES = 3

RW_HEADS, RW_N = 16, 64
RW_WIDTH = RW_HEADS * RW_N
RW_W_RANK, RW_A_RANK, RW_G_RANK = 64, 64, 128
RW_COLS = 3 * RW_WIDTH + RW_W_RANK + RW_A_RANK + RW_G_RANK
RW_GN_EPS = 64e-5
RW_GROUP = 4
RW_PASSES = 1

GD_HEADS, GD_DK, GD_DV = 8, 128, 128
GD_QK = GD_HEADS * GD_DK
GD_WIDTH = GD_HEADS * GD_DV
GD_QKV = 2 * GD_QK + GD_WIDTH
GD_CONV = 4
GD_MAIN = GD_QKV + GD_WIDTH
GD_GROUP = 4
GD_PASSES = 1

BR_WIDTH = 1024
CA_HEADS, CA_HEAD_DIM = 4, 256
CA_WIDTH = CA_HEADS * CA_HEAD_DIM
FFN_CONV = 3
FFN_FUSE_MIN_T = 512
FFN_ROW_TILE = 1024
FFN_SUB_ROWS = 128

SMALL_W = 128
SM_ML_I, SM_ML_F, SM_GD_B, SM_GD_A = 0, ML_HEADS, 2 * ML_HEADS, 2 * ML_HEADS + GD_HEADS

V7X_VMEM_LIMIT = 56 * 1024 * 1024
TRI_BLOCK = 16


def _cparams(sem):
    return pltpu.CompilerParams(dimension_semantics=sem, vmem_limit_bytes=V7X_VMEM_LIMIT)


def _tile(dim, pref, quantum):
    if dim <= pref:
        return dim
    t = (pref // quantum) * quantum
    while t > quantum and dim % t:
        t -= quantum
    assert dim % t == 0, (dim, pref, quantum)
    return t


def _sub_chunks(t):
    n = SUB_CHUNKS
    while (t // CHUNK) % n:
        n -= 1
    return n


def _split2(a):
    hi = a.astype(BF16)
    lo = (a - hi.astype(F32)).astype(BF16)
    return hi, lo


def _dg(a, b, ca, cb):
    return lax.dot_general(a, b, (((ca,), (cb,)), ((), ())), preferred_element_type=F32)


def _dot3(a, b, ca=1, cb=0):
    ah, al = _split2(a)
    bh, bl = _split2(b)
    return _dg(ah, bh, ca, cb) + (_dg(al, bh, ca, cb) + _dg(ah, bl, ca, cb))


def _dotp(a, b, ca, cb, passes):
    if passes == 1:
        return _dg(a.astype(BF16), b.astype(BF16), ca, cb)
    return _dot3(a, b, ca, cb)


def _tri_masks(c):
    row = lax.broadcasted_iota(jnp.int32, (c, c), 0)
    col = lax.broadcasted_iota(jnp.int32, (c, c), 1)
    return row, col


def _cumsum_rows(x):
    c = x.shape[0]
    row, col = _tri_masks(c)
    tri = jnp.where(row >= col, 1.0, 0.0).astype(BF16)
    x0 = x.astype(BF16)
    r1 = x - x0.astype(F32)
    x1 = r1.astype(BF16)
    x2 = (r1 - x1.astype(F32)).astype(BF16)
    return _dg(tri, x0, 1, 0) + (_dg(tri, x1, 1, 0) + _dg(tri, x2, 1, 0))


def _tri_solve(low, rhs, c, passes):
    return _tri_solve_many([low], [rhs], c, passes)[0]


def _tri_solve_many(lows, rhss, c, passes):
    return _tri_apply_many(_tri_factor_many(lows, c, passes), rhss, passes)


def _tri_factor_many(lows, c, passes):
    mm = lambda a, b: _dotp(a, b, 1, 0, passes)
    row, col = _tri_masks(lows[0].shape[0])
    same = (row // TRI_BLOCK) == (col // TRI_BLOCK)
    eye = jnp.where(row == col, 1.0, 0.0).astype(F32)
    ps = [jnp.where(same, -low, 0.0) for low in lows]
    offs = [jnp.where(same, 0.0, low) for low in lows]
    xs = [eye + p for p in ps]
    steps = 1
    while steps * 2 < TRI_BLOCK:
        ps = [mm(p, p) for p in ps]
        xs = [x + mm(x, p) for x, p in zip(xs, ps)]
        steps *= 2
    ms = [mm(x, off) for x, off in zip(xs, offs)]
    nblk = c // TRI_BLOCK
    terms = []
    pws = ms
    k = 1
    while k < nblk:
        terms.append(pws)
        k *= 2
        if k < nblk:
            pws = [mm(pw, pw) for pw in pws]
    return xs, terms


def _tri_apply_many(factors, rhss, passes):
    mm = lambda a, b: _dotp(a, b, 1, 0, passes)
    xs, terms = factors
    us = xs if rhss is None else [mm(x, rhs) for x, rhs in zip(xs, rhss)]
    for i in range(len(terms) - 1, 0, -1):
        us = [u + mm(t, u) for t, u in zip(terms[i], us)]
    if terms:
        us = [u - mm(t, u) for t, u in zip(terms[0], us)]
    return us


def _head_sums(x, hw):
    assert 2 * hw == 128
    lane = lax.broadcasted_iota(jnp.int32, (x.shape[0], 128), 1)
    lo = lane < hw
    out = []
    for s in range(x.shape[1] // 128):
        xs = x[:, s * 128:(s + 1) * 128]
        s_lo = jnp.sum(jnp.where(lo, xs, 0.0), axis=-1, keepdims=True)
        s_hi = jnp.sum(jnp.where(lo, 0.0, xs), axis=-1, keepdims=True)
        out.append(jnp.where(lo, s_lo, s_hi))
    return jnp.concatenate(out, axis=1)


def _softplus(x):
    return jnp.maximum(x, 0.0) + jnp.log1p(jnp.exp(-jnp.abs(x)))


def _sigmoid(x):
    return 1.0 / (1.0 + jnp.exp(-x))


def _silu(x):
    return x * _sigmoid(x)


def _shift_rows(x, prev8, k):
    xr = pltpu.roll(x, k, 0)
    pr = pltpu.roll(prev8, k, 0)
    row = lax.broadcasted_iota(jnp.int32, (8, x.shape[1]), 0)
    head = jnp.where(row < k, pr, xr[0:8])
    if x.shape[0] == 8:
        return head
    return jnp.concatenate([head, xr[8:]], axis=0)


def _rmsnorm_kernel(x_ref, g_ref, o_ref):
    x = x_ref[...]
    y = x * lax.rsqrt(jnp.mean(x * x, axis=-1, keepdims=True) + EPS)
    o_ref[...] = (y * g_ref[...]).astype(o_ref.dtype)


def _rmsnorm(x, g, out_dtype):
    m, d = x.shape
    tr = _tile(m, 256, 8)
    return pl.pallas_call(
        _rmsnorm_kernel,
        grid=(m // tr,),
        in_specs=[pl.BlockSpec((tr, d), lambda i: (i, 0)), pl.BlockSpec((1, d), lambda i: (0, 0))],
        out_specs=pl.BlockSpec((tr, d), lambda i: (i, 0)),
        out_shape=jax.ShapeDtypeStruct((m, d), out_dtype),
        compiler_params=_cparams(("parallel",)),
        name="rmsnorm",
    )(x, g.reshape(1, d))


def _mm_kernel(a_ref, w_ref, o_ref):
    o_ref[...] = jnp.dot(a_ref[...], w_ref[...], preferred_element_type=F32).astype(o_ref.dtype)


def _mm_res_kernel(a_ref, w_ref, r_ref, o_ref):
    o_ref[...] = r_ref[...] + jnp.dot(a_ref[...], w_ref[...], preferred_element_type=F32)


def _mm_tiles(m, k, n, has_residual):
    tm = _tile(m, 1024 if k <= 4096 else 512, 8)
    tn = _tile(n, 1024 if (k <= 4096 and not has_residual) else 512, 128)
    return tm, tn


def _matmul(a, w, residual=None, out_dtype=F32):
    m, k = a.shape
    n = w.shape[1]
    tm, tn = _mm_tiles(m, k, n, residual is not None)
    in_specs = [pl.BlockSpec((tm, k), lambda i, j: (i, 0)), pl.BlockSpec((k, tn), lambda i, j: (0, j))]
    args = [a, w]
    body = _mm_kernel
    if residual is not None:
        in_specs.append(pl.BlockSpec((tm, tn), lambda i, j: (i, j)))
        args.append(residual)
        body = _mm_res_kernel
    return pl.pallas_call(
        body,
        grid=(m // tm, n // tn),
        in_specs=in_specs,
        out_specs=pl.BlockSpec((tm, tn), lambda i, j: (i, j)),
        out_shape=jax.ShapeDtypeStruct((m, n), out_dtype),
        compiler_params=_cparams(("parallel", "arbitrary")),
        name="matmul",
    )(*args)


def _mlstm_kernel(q_ref, k_ref, v_ref, og_ref, sm_ref, bias_ref, norm_ref, c0_ref, n0_ref, m0_ref,
                  y_ref, c_ref, n_ref, m_ref):
    @pl.when(pl.program_id(1) == 0)
    def _():
        c_ref[...] = c0_ref[...]
        n_ref[...] = n0_ref[...]
        m_ref[...] = m0_ref[...]

    c = q_ref.shape[0]
    pre = sm_ref[...] + bias_ref[...]
    logf = -_softplus(-pre)
    cum = _cumsum_rows(logf)
    pre_t = pre.T
    cum_t = cum.T
    row, col = _tri_masks(c)
    causal = row >= col
    hs = range(ML_HEADS)
    q = [q_ref[:, h * ML_DQK:(h + 1) * ML_DQK] for h in hs]
    k = [k_ref[:, h * ML_DQK:(h + 1) * ML_DQK] * (ML_DQK ** -0.5) for h in hs]
    v = [v_ref[:, h * ML_DV:(h + 1) * ML_DV] for h in hs]
    b_col = [cum[:, SM_ML_F + h:SM_ML_F + h + 1] for h in hs]
    b_row = [cum_t[SM_ML_F + h:SM_ML_F + h + 1, :] for h in hs]
    i_col = [pre[:, SM_ML_I + h:SM_ML_I + h + 1] for h in hs]
    i_row = [pre_t[SM_ML_I + h:SM_ML_I + h + 1, :] for h in hs]
    m_old = [m_ref[0, :, h:h + 1] for h in hs]
    c_old = [c_ref[0, h] for h in hs]
    n_old = [n_ref[0, h:h + 1, :] for h in hs]

    qk = [_dotp(q[h], k[h], 1, 1, ML_PASSES) for h in hs]
    qc = [_dotp(q[h], c_old[h], 1, 0, ML_PASSES) for h in hs]
    dmat = [jnp.where(causal, b_col[h] - b_row[h] + i_row[h], -jnp.inf) for h in hs]
    mt = [jnp.maximum(b_col[h] + m_old[h], jnp.max(dmat[h], axis=-1, keepdims=True)) for h in hs]
    pmat = [jnp.exp(dmat[h] - mt[h]) * qk[h] for h in hs]
    inter = [jnp.exp(b_col[h] + m_old[h] - mt[h]) for h in hs]
    num = [inter[h] * qc[h] + _dotp(pmat[h], v[h], 1, 0, ML_PASSES) for h in hs]
    den = [inter[h] * jnp.sum(q[h] * n_old[h], axis=-1, keepdims=True)
           + jnp.sum(pmat[h], axis=-1, keepdims=True) for h in hs]
    hh = [num[h] / jnp.maximum(jnp.abs(den[h]), jnp.exp(-mt[h])) for h in hs]

    m_new = [mt[h][c - 1:c, :] for h in hs]
    b_last = [b_col[h][c - 1:c, :] for h in hs]
    kw = [k[h] * jnp.exp(b_last[h] - b_col[h] + i_col[h] - m_new[h]) for h in hs]
    dec = [jnp.exp(b_last[h] + m_old[h] - m_new[h]) for h in hs]
    c_new = [dec[h] * c_old[h] + _dotp(kw[h].T, v[h], 1, 0, ML_PASSES) for h in hs]
    for h in hs:
        c_ref[0, h] = c_new[h]
        n_ref[0, h:h + 1, :] = dec[h] * n_old[h] + jnp.sum(kw[h], axis=0, keepdims=True)
        m_ref[0, :, h:h + 1] = m_new[h]
        hn = hh[h] * lax.rsqrt(jnp.mean(hh[h] * hh[h], axis=-1, keepdims=True) + EPS)
        og = og_ref[:, h * ML_DV:(h + 1) * ML_DV]
        y = hn * norm_ref[:, h * ML_DV:(h + 1) * ML_DV] * _sigmoid(og)
        y_ref[:, h * ML_DV:(h + 1) * ML_DV] = y.astype(y_ref.dtype)


def _mlstm(p_main, p_small, bias_row, ml_norm, c0, n0, m0, bsz, t):
    nc = t // CHUNK
    rows = bsz * t
    rmap = lambda b, c: b * nc + c
    return pl.pallas_call(
        _mlstm_kernel,
        grid=(bsz, nc),
        in_specs=[
            pl.BlockSpec((CHUNK, ML_QK), lambda b, c: (rmap(b, c), 0)),
            pl.BlockSpec((CHUNK, ML_QK), lambda b, c: (rmap(b, c), 1)),
            pl.BlockSpec((CHUNK, ML_WIDTH), lambda b, c: (rmap(b, c), 1)),
            pl.BlockSpec((CHUNK, ML_WIDTH), lambda b, c: (rmap(b, c), 2)),
            pl.BlockSpec((CHUNK, SMALL_W), lambda b, c: (rmap(b, c), 0)),
            pl.BlockSpec((1, SMALL_W), lambda b, c: (0, 0)),
            pl.BlockSpec((1, ML_WIDTH), lambda b, c: (0, 0)),
            pl.BlockSpec((1, ML_HEADS, ML_DQK, ML_DV), lambda b, c: (b, 0, 0, 0)),
            pl.BlockSpec((1, ML_HEADS, ML_DQK), lambda b, c: (b, 0, 0)),
            pl.BlockSpec((1, 1, ML_HEADS), lambda b, c: (b, 0, 0)),
        ],
        out_specs=[
            pl.BlockSpec((CHUNK, ML_WIDTH), lambda b, c: (rmap(b, c), 0)),
            pl.BlockSpec((1, ML_HEADS, ML_DQK, ML_DV), lambda b, c: (b, 0, 0, 0)),
            pl.BlockSpec((1, ML_HEADS, ML_DQK), lambda b, c: (b, 0, 0)),
            pl.BlockSpec((1, 1, ML_HEADS), lambda b, c: (b, 0, 0)),
        ],
        out_shape=[
            jax.ShapeDtypeStruct((rows, ML_WIDTH), BF16),
            jax.ShapeDtypeStruct(c0.shape, F32),
            jax.ShapeDtypeStruct(n0.shape, F32),
            jax.ShapeDtypeStruct((bsz, 1, ML_HEADS), F32),
        ],
        compiler_params=_cparams(("parallel", "arbitrary")),
        name="mlstm",
    )(p_main, p_main, p_main, p_main, p_small, bias_row, ml_norm.reshape(1, ML_WIDTH),
      c0, n0, m0.reshape(bsz, 1, ML_HEADS))


def _rwkv_front(x, prev8, mu_ref, w0_ref, w2_ref, a0_ref, a2_ref, g2_ref, kk_ref, ka_ref, rk_ref):
    c = x.shape[0]
    w = RW_WIDTH
    xprev = _shift_rows(x, prev8, 1)
    xm = x + (xprev - x) * mu_ref[...]
    rr = xm[:, 0:w]
    rk = xm[:, w:2 * w]
    rv = xm[:, 2 * w:3 * w]
    xw = xm[:, 3 * w:3 * w + RW_W_RANK]
    xa = xm[:, 3 * w + RW_W_RANK:3 * w + RW_W_RANK + RW_A_RANK]
    xg = xm[:, 3 * w + RW_W_RANK + RW_A_RANK:]

    w_pre = w0_ref[...] + _dot3(jnp.tanh(xw), w2_ref[...])
    lw = -jnp.exp(-_softplus(-w_pre) - 0.5)
    a = _sigmoid(a0_ref[...] + _dot3(xa, a2_ref[...]))
    g_out = _dot3(_sigmoid(xg), g2_ref[...])
    kk_raw = rk * kk_ref[...]
    kmod = rk * (1.0 + (a - 1.0) * ka_ref[...])
    bon = rr * kmod * rk_ref[...]

    lc = _cumsum_rows(lw)
    l_end = lc[c - 1:c, :]
    l_mid = lc[c // 2 - 1:c // 2, :]
    lcc = lc - l_mid
    p_mid = jnp.exp(l_mid)
    p_in = jnp.exp(lcc)
    p_prev = jnp.exp(lcc - lw)
    p_inv = jnp.exp(-lcc)
    p_end = jnp.exp(l_end - lc)
    p_all = jnp.exp(l_end)

    kkn = kk_raw * lax.rsqrt(_head_sums(kk_raw * kk_raw, RW_N) + 1e-6)
    bb = kkn * a
    kkp = kkn * p_prev
    rp = rr * p_in
    kd = kmod * p_inv
    bd = bb * p_inv
    k_end = kmod * p_end
    b_end = bb * p_end
    return dict(kkp=kkp, rp=rp, kd=kd, bd=bd, k_end=k_end, b_end=b_end, rv=rv, p_mid=p_mid, p_all=p_all,
                bonus=_head_sums(bon, RW_N) * rv, g_out=g_out)


def _rw_groups():
    gw = RW_GROUP * RW_N
    return [slice(g * gw, (g + 1) * gw) for g in range(RW_HEADS // RW_GROUP)]


def _rw_masks(c):
    shape = (RW_GROUP * c, RW_GROUP * RW_N)
    head_eq = lax.broadcasted_iota(jnp.int32, shape, 0) // c == lax.broadcasted_iota(jnp.int32, shape, 1) // RW_N
    tn_ = lax.broadcasted_iota(jnp.int32, (c, RW_GROUP * RW_N), 0)
    sn_ = lax.broadcasted_iota(jnp.int32, (c, RW_GROUP * RW_N), 1) % RW_N
    return head_eq, tn_ > sn_, tn_ >= sn_


def _rw_blockdiag(z, head_eq):
    return jnp.where(head_eq, jnp.concatenate([z] * RW_GROUP, axis=0), 0.0)


def _rw_rowsum(z, c):
    out = z[0:c]
    for i in range(1, RW_GROUP):
        out = out + z[i * c:(i + 1) * c]
    return out


def _rwkv_mid(f, masks):
    kkp, rp, kd, bd, rv = f["kkp"], f["rp"], f["kd"], f["bd"], f["rv"]
    c = rv.shape[0]
    head_eq, strict, incl = masks
    groups = _rw_groups()
    nt = lambda x, y: _dotp(x, y, 1, 1, RW_PASSES)
    x2 = [jnp.concatenate([kkp[:, cs], rp[:, cs]], axis=0) for cs in groups]
    ab_b = [nt(x, _rw_blockdiag(bd[:, cs], head_eq)) for x, cs in zip(x2, groups)]
    ab_k = [nt(x, _rw_blockdiag(kd[:, cs], head_eq)) for x, cs in zip(x2, groups)]
    a_b = [_rw_blockdiag(jnp.where(strict, z[0:c], 0.0), head_eq) for z in ab_b]
    return dict(
        x2=x2, v_bd=[_rw_blockdiag(rv[:, cs], head_eq) for cs in groups],
        t_inv=_tri_apply_many(_tri_factor_many(a_b, c, RW_PASSES), None, RW_PASSES),
        a_k=[jnp.where(strict, z[0:c], 0.0) for z in ab_k],
        r_k=[jnp.where(incl, z[c:], 0.0) for z in ab_k],
        r_b=[jnp.where(incl, z[c:], 0.0) for z in ab_b])


def _rwkv_chain(f, g, s_olds, ln_ref, masks):
    k_end, b_end, rv, p_mid, p_all = f["k_end"], f["b_end"], f["rv"], f["p_mid"], f["p_all"]
    c = rv.shape[0]
    groups = _rw_groups()
    head_eq = masks[0]
    nt = lambda x, y: _dotp(x, y, 1, 1, RW_PASSES)
    nn = lambda x, y: _dotp(x, y, 1, 0, RW_PASSES)
    v_bd = g["v_bd"]
    ab_s = [nt(x, _rw_blockdiag(s * p_mid[:, cs], head_eq))
            for x, s, cs in zip(g["x2"], s_olds, groups)]
    rhs = [_rw_blockdiag(z[0:c] + nn(ak, v), head_eq) for z, ak, v in zip(ab_s, g["a_k"], v_bd)]
    us = [nn(t_, r_) for t_, r_ in zip(g["t_inv"], rhs)]
    ygs = [z[c:] + nn(rk_, v) - nn(rb_, u) for z, rk_, v, rb_, u in zip(ab_s, g["r_k"], v_bd, g["r_b"], us)]
    upds = [nn(jnp.concatenate([rv[:, cs], -_rw_rowsum(u, c)], axis=0).T,
               jnp.concatenate([k_end[:, cs], b_end[:, cs]], axis=0)) for cs, u in zip(groups, us)]
    s_news = [s * p_all[:, cs] + _rw_rowsum(jnp.where(head_eq, upd, 0.0), c)
              for cs, s, upd in zip(groups, s_olds, upds)]

    yh = jnp.concatenate(ygs, axis=1)
    mu_ = _head_sums(yh, RW_N) * (1.0 / RW_N)
    yc = yh - mu_
    var = _head_sums(yc * yc, RW_N) * (1.0 / RW_N)
    yn = yc * lax.rsqrt(var + RW_GN_EPS)
    return (yn * ln_ref[...] + f["bonus"]) * f["g_out"], s_news


def _rwkv_kernel(p_ref, shift_ref, mu_ref, w0_ref, w2_ref, a0_ref, a2_ref, g2_ref, kk_ref, ka_ref,
                 rk_ref, ln_ref, s0_ref, y_ref, s_ref, prev_ref):
    @pl.when(pl.program_id(1) == 0)
    def _():
        s_ref[...] = s0_ref[...]
        prev_ref[...] = jnp.broadcast_to(shift_ref[0], prev_ref.shape)

    c = CHUNK
    nsub = p_ref.shape[0] // c
    masks = _rw_masks(c)
    fronts = []
    for k in range(nsub):
        last8 = prev_ref[...] if k == 0 else p_ref[k * c - 8:k * c, :]
        fronts.append(_rwkv_front(p_ref[k * c:(k + 1) * c, :], last8, mu_ref, w0_ref, w2_ref, a0_ref, a2_ref,
                                  g2_ref, kk_ref, ka_ref, rk_ref))
    groups = _rw_groups()
    states = [s_ref[0, :, cs] for cs in groups]
    for k in range(nsub):
        out, states = _rwkv_chain(fronts[k], _rwkv_mid(fronts[k], masks), states, ln_ref, masks)
        y_ref[k * c:(k + 1) * c, :] = out.astype(y_ref.dtype)
    prev_ref[...] = p_ref[nsub * c - 8:nsub * c, :]
    for cs, s in zip(groups, states):
        s_ref[0, :, cs] = s


def _rwkv(p_rw, shift0, s0, p, bsz, t):
    step = CHUNK * _sub_chunks(t)
    nc = t // step
    rows = bsz * t
    rmap = lambda b, c: b * nc + c
    full = lambda shape: pl.BlockSpec(shape, lambda b, c: (0,) * len(shape))
    w = RW_WIDTH
    s_nat = s0.transpose(0, 2, 1, 3).reshape(bsz, RW_N, w)
    y, s = pl.pallas_call(
        _rwkv_kernel,
        grid=(bsz, nc),
        in_specs=[
            pl.BlockSpec((step, RW_COLS), lambda b, c: (rmap(b, c), 0)),
            pl.BlockSpec((1, 1, RW_COLS), lambda b, c: (b, 0, 0)),
            full((1, RW_COLS)), full((1, w)), full((RW_W_RANK, w)), full((1, w)), full((RW_A_RANK, w)),
            full((RW_G_RANK, w)), full((1, w)), full((1, w)), full((1, w)), full((1, w)),
            pl.BlockSpec((1, RW_N, w), lambda b, c: (b, 0, 0)),
        ],
        out_specs=[
            pl.BlockSpec((step, w), lambda b, c: (rmap(b, c), 0)),
            pl.BlockSpec((1, RW_N, w), lambda b, c: (b, 0, 0)),
        ],
        out_shape=[jax.ShapeDtypeStruct((rows, w), BF16), jax.ShapeDtypeStruct(s_nat.shape, F32)],
        scratch_shapes=[pltpu.VMEM((8, RW_COLS), F32)],
        compiler_params=_cparams(("parallel", "arbitrary")),
        name="rwkv7",
    )(p_rw, shift0.reshape(bsz, 1, RW_COLS), p["rw_mu"].reshape(1, RW_COLS), p["rw_w0"].reshape(1, w),
      p["rw_w2"], p["rw_a0"].reshape(1, w), p["rw_a2"], p["rw_g2"], p["rw_k_k"].reshape(1, w),
      p["rw_k_a"].reshape(1, w), p["rw_r_k"].reshape(1, w), p["rw_ln"].reshape(1, w), s_nat)
    return y, s.reshape(bsz, RW_N, RW_HEADS, RW_N).transpose(0, 2, 1, 3)


def _gdn_front(xs, prevs, sm, cw_ref, alog_ref, dtb_ref):
    c = xs[0].shape[0]
    acts = []
    for sec, (x, prev8) in enumerate(zip(xs, prevs)):
        cs = slice(sec * GD_QK, (sec + 1) * GD_QK)
        y = _shift_rows(x, prev8, GD_CONV - 1) * cw_ref[0:1, cs]
        for i in range(1, GD_CONV - 1):
            y = y + _shift_rows(x, prev8, GD_CONV - 1 - i) * cw_ref[i:i + 1, cs]
        y = y + x * cw_ref[GD_CONV - 1:GD_CONV, cs]
        acts.append(_silu(y))
    qa, ka, va = acts

    beta = _sigmoid(sm)
    g = -jnp.exp(alog_ref[...]) * _softplus(sm + dtb_ref[...])
    gc = _cumsum_rows(g)
    gc_t = gc.T

    gr = GD_GROUP * c
    gk = GD_GROUP * GD_DK
    r4 = lax.broadcasted_iota(jnp.int32, (gr, gr), 0)
    c4 = lax.broadcasted_iota(jnp.int32, (gr, gr), 1)
    t_minus_s = jnp.where((r4 // c) == (c4 // c), (r4 % c) - (c4 % c), -1)
    m_strict = t_minus_s > 0
    m_incl = t_minus_s >= 0
    head_eq = (lax.broadcasted_iota(jnp.int32, (gr, gk), 0) // c
               == lax.broadcasted_iota(jnp.int32, (gr, gk), 1) // GD_DK)

    def rep(z):
        return jnp.concatenate([z] * GD_GROUP, axis=0)

    def stack(parts):
        return jnp.concatenate(parts, axis=0)

    kn = []
    qn = []
    for h in range(GD_HEADS):
        ks = slice(h * GD_DK, (h + 1) * GD_DK)
        q = qa[:, ks]
        qn.append(q * lax.rsqrt(jnp.sum(q * q, axis=-1, keepdims=True) + 1e-6) * (GD_DK ** -0.5))
        k = ka[:, ks]
        kn.append(k * lax.rsqrt(jnp.sum(k * k, axis=-1, keepdims=True) + 1e-6))
    groups = [range(g0, g0 + GD_GROUP) for g0 in range(0, GD_HEADS, GD_GROUP)]
    k_nat = [jnp.concatenate([kn[h] for h in hs], axis=1) for hs in groups]
    q_nat = [jnp.concatenate([qn[h] for h in hs], axis=1) for hs in groups]
    k_bd = [jnp.where(head_eq, rep(kk_), 0.0) for kk_ in k_nat]
    v_st = [stack([va[:, h * GD_DV:(h + 1) * GD_DV] for h in hs]) for hs in groups]
    b_col = [stack([beta[:, SM_GD_B + h:SM_GD_B + h + 1] for h in hs]) for hs in groups]
    g_col = [stack([gc[:, SM_GD_A + h:SM_GD_A + h + 1] for h in hs]) for hs in groups]
    g_row = [jnp.concatenate([gc_t[SM_GD_A + h:SM_GD_A + h + 1, :] for h in hs], axis=1) for hs in groups]

    dec = [jnp.exp(jnp.where(m_incl, gc_ - gr_, -jnp.inf)) for gc_, gr_ in zip(g_col, g_row)]
    kq = [_dotp(stack([kk_, qq_]), kb, 1, 1, GD_PASSES) for kk_, qq_, kb in zip(k_nat, q_nat, k_bd)]
    low = [jnp.where(m_strict, b * rep(z[0:c]) * d, 0.0) for b, z, d in zip(b_col, kq, dec)]
    gl = [gc[c - 1:c, SM_GD_A + h:SM_GD_A + h + 1] for h in range(GD_HEADS)]
    return dict(
        kq_lhs=[stack([kn[h], qn[h]]) for h in range(GD_HEADS)], v_st=v_st, b_col=b_col,
        eg=[jnp.exp(gc_) for gc_ in g_col],
        t_inv=_tri_apply_many(_tri_factor_many(low, c, GD_PASSES), None, GD_PASSES),
        qkt=[rep(z[c:]) * d for z, d in zip(kq, dec)],
        kw_t=[(kn[h] * jnp.exp(gl[h] - gc[:, SM_GD_A + h:SM_GD_A + h + 1])).T for h in range(GD_HEADS)],
        s_decay=[jnp.exp(gl[h]) for h in range(GD_HEADS)])


def _gdn_chain(f, s_old):
    c = f["kq_lhs"][0].shape[0] // 2
    stack = lambda parts: jnp.concatenate(parts, axis=0)
    groups = [range(g0, g0 + GD_GROUP) for g0 in range(0, GD_HEADS, GD_GROUP)]
    kqs = [_dotp(f["kq_lhs"][h], s_old[h], 1, 0, GD_PASSES) for h in range(GD_HEADS)]
    ks = [stack([kqs[h][0:c] for h in hs]) for hs in groups]
    qs = [stack([kqs[h][c:] for h in hs]) for hs in groups]
    rhs = [b * (v - e * z) for b, v, e, z in zip(f["b_col"], f["v_st"], f["eg"], ks)]
    us = [_dotp(t_, r_, 1, 0, GD_PASSES) for t_, r_ in zip(f["t_inv"], rhs)]
    os_ = [e * z + _dotp(qk, u, 1, 0, GD_PASSES) for e, z, qk, u in zip(f["eg"], qs, f["qkt"], us)]
    outs = []
    s_new = []
    for gi, hs in enumerate(groups):
        for i, h in enumerate(hs):
            rs = slice(i * c, (i + 1) * c)
            s_new.append(f["s_decay"][h] * s_old[h] + _dotp(f["kw_t"][h], us[gi][rs], 1, 0, GD_PASSES))
            outs.append(os_[gi][rs])
    return outs, s_new


def _gdn_kernel(q_ref, k_ref, v_ref, z_ref, sm_ref, cw_ref, conv0_ref, alog_ref, dtb_ref, norm_ref, s0_ref,
                y_ref, s_ref, prev_ref):
    @pl.when(pl.program_id(1) == 0)
    def _():
        s_ref[...] = s0_ref[...]
        prev_ref[...] = jnp.zeros(prev_ref.shape, F32)
        prev_ref[8 - (GD_CONV - 1):8, :] = conv0_ref[0]

    c = CHUNK
    nsub = q_ref.shape[0] // c
    refs = (q_ref, k_ref, v_ref)
    fronts = []
    for k in range(nsub):
        xs = [r[k * c:(k + 1) * c, :] for r in refs]
        if k == 0:
            prevs = [prev_ref[:, sec * GD_QK:(sec + 1) * GD_QK] for sec in range(3)]
        else:
            prevs = [r[k * c - 8:k * c, :] for r in refs]
        fronts.append(_gdn_front(xs, prevs, sm_ref[k * c:(k + 1) * c, :], cw_ref, alog_ref, dtb_ref))
    states = [s_ref[0, h] for h in range(GD_HEADS)]
    for k in range(nsub):
        outs, states = _gdn_chain(fronts[k], states)
        for h in range(GD_HEADS):
            o_h = outs[h]
            og = o_h * lax.rsqrt(jnp.mean(o_h * o_h, axis=-1, keepdims=True) + EPS) * norm_ref[...]
            vs = slice(h * GD_DV, (h + 1) * GD_DV)
            y_ref[k * c:(k + 1) * c, vs] = (og * _silu(z_ref[k * c:(k + 1) * c, vs])).astype(y_ref.dtype)
    for h in range(GD_HEADS):
        s_ref[0, h] = states[h]
    for sec, r in enumerate(refs):
        prev_ref[:, sec * GD_QK:(sec + 1) * GD_QK] = r[nsub * c - 8:nsub * c, :]


def _gdn(p_main, p_small, conv_w, conv0, alog_row, dtb_row, gd_norm, s0, bsz, t):
    step = CHUNK * _sub_chunks(t)
    nc = t // step
    rows = bsz * t
    rmap = lambda b, c: b * nc + c
    base = ML_MAIN // GD_QK
    assert ML_MAIN % GD_QK == 0
    full = lambda shape: pl.BlockSpec(shape, lambda b, c: (0,) * len(shape))
    return pl.pallas_call(
        _gdn_kernel,
        grid=(bsz, nc),
        in_specs=[
            pl.BlockSpec((step, GD_QK), lambda b, c: (rmap(b, c), base)),
            pl.BlockSpec((step, GD_QK), lambda b, c: (rmap(b, c), base + 1)),
            pl.BlockSpec((step, GD_WIDTH), lambda b, c: (rmap(b, c), base + 2)),
            pl.BlockSpec((step, GD_WIDTH), lambda b, c: (rmap(b, c), base + 3)),
            pl.BlockSpec((step, SMALL_W), lambda b, c: (rmap(b, c), 0)),
            full((GD_CONV, GD_QKV)),
            pl.BlockSpec((1, GD_CONV - 1, GD_QKV), lambda b, c: (b, 0, 0)),
            full((1, SMALL_W)), full((1, SMALL_W)), full((1, GD_DV)),
            pl.BlockSpec((1, GD_HEADS, GD_DK, GD_DV), lambda b, c: (b, 0, 0, 0)),
        ],
        out_specs=[
            pl.BlockSpec((step, GD_WIDTH), lambda b, c: (rmap(b, c), 0)),
            pl.BlockSpec((1, GD_HEADS, GD_DK, GD_DV), lambda b, c: (b, 0, 0, 0)),
        ],
        out_shape=[jax.ShapeDtypeStruct((rows, GD_WIDTH), BF16), jax.ShapeDtypeStruct(s0.shape, F32)],
        scratch_shapes=[pltpu.VMEM((8, GD_QKV), F32)],
        compiler_params=_cparams(("parallel", "arbitrary")),
        name="gdn",
    )(p_main, p_main, p_main, p_main, p_small, conv_w, conv0, alog_row, dtb_row, gd_norm.reshape(1, GD_DV), s0)


def _merge_kernel(y0_ref, y1_ref, y2_ref, w_ref, g0_ref, g1_ref, g2_ref, o_ref):
    acc = _sigmoid(g0_ref[...]) * jnp.dot(y0_ref[...], w_ref[0], preferred_element_type=F32)
    acc = acc + _sigmoid(g1_ref[...]) * jnp.dot(y1_ref[...], w_ref[1], preferred_element_type=F32)
    acc = acc + _sigmoid(g2_ref[...]) * jnp.dot(y2_ref[...], w_ref[2], preferred_element_type=F32)
    o_ref[...] = acc.astype(o_ref.dtype)


def _merge(ys, w_branch, p_main, d):
    m = ys[0].shape[0]
    tm = _tile(m, 1024, 8)
    tn = _tile(d, 512, 128)
    gate0 = ML_MAIN + GD_MAIN
    assert gate0 % tn == 0
    gb = gate0 // tn
    nb = d // tn
    yspec = pl.BlockSpec((tm, BR_WIDTH), lambda i, j: (i, 0))
    gspec = lambda b: pl.BlockSpec((tm, tn), lambda i, j: (i, gb + b * nb + j))
    return pl.pallas_call(
        _merge_kernel,
        grid=(m // tm, nb),
        in_specs=[yspec, yspec, yspec, pl.BlockSpec((3, BR_WIDTH, tn), lambda i, j: (0, 0, j)),
                  gspec(0), gspec(1), gspec(2)],
        out_specs=pl.BlockSpec((tm, tn), lambda i, j: (i, j)),
        out_shape=jax.ShapeDtypeStruct((m, d), BF16),
        compiler_params=_cparams(("parallel", "arbitrary")),
        name="merge",
    )(ys[0], ys[1], ys[2], w_branch, p_main, p_main, p_main)


def _attn_kernel(q_ref, k_ref, v_ref, o_ref):
    for h in range(CA_HEADS):
        hs = slice(h * CA_HEAD_DIM, (h + 1) * CA_HEAD_DIM)
        s = _dg(q_ref[:, hs], k_ref[0, :, hs], 1, 1) * (CA_HEAD_DIM ** -0.5)
        s = s - jnp.max(s, axis=-1, keepdims=True)
        e = jnp.exp(s)
        pr = e / jnp.sum(e, axis=-1, keepdims=True)
        o = jnp.dot(pr.astype(BF16), v_ref[0, :, hs], preferred_element_type=F32)
        o_ref[:, hs] = o.astype(o_ref.dtype)


def _attention(q, mem_k, mem_v, bsz, t):
    tq = _tile(t, 512, 8)
    nt = t // tq
    n_mem = mem_k.shape[1]
    return pl.pallas_call(
        _attn_kernel,
        grid=(bsz, nt),
        in_specs=[
            pl.BlockSpec((tq, CA_WIDTH), lambda b, i: (b * nt + i, 0)),
            pl.BlockSpec((1, n_mem, CA_WIDTH), lambda b, i: (b, 0, 0)),
            pl.BlockSpec((1, n_mem, CA_WIDTH), lambda b, i: (b, 0, 0)),
        ],
        out_specs=pl.BlockSpec((tq, CA_WIDTH), lambda b, i: (b * nt + i, 0)),
        out_shape=jax.ShapeDtypeStruct((bsz * t, CA_WIDTH), BF16),
        compiler_params=_cparams(("parallel", "arbitrary")),
        name="mem_attention",
    )(q, mem_k, mem_v)


def _ffn_conv(x, prev, w):
    row = lax.broadcasted_iota(jnp.int32, x.shape, 0)
    x1 = jnp.where(row == 0, prev[1:2], pltpu.roll(x, 1, 0))
    x2 = jnp.where(row == 0, prev[0:1], jnp.where(row == 1, prev[1:2], pltpu.roll(x, 2, 0)))
    return x2 * w[0:1] + x1 * w[1:2] + x * w[2:3]


def _ffn_act_kernel(a_ref, g_ref, pa_ref, pg_ref, wa_ref, wg_ref, o_ref):
    fa = _ffn_conv(a_ref[...], pa_ref[0, 0], wa_ref[...])
    fg = _ffn_conv(g_ref[...], pg_ref[0, 0], wg_ref[...])
    o_ref[...] = (_silu(fg) * fa).astype(o_ref.dtype)


def _ffn_act(up, conv0, conv_w, bsz, t):
    c2 = up.shape[1]
    f = c2 // 2
    tr = _tile(t, 512, 8)
    nt = t // tr
    tc = _tile(f, 1024, 128)
    nj = f // tc
    tails = up.reshape(bsz, nt, tr, c2)[:, :nt - 1, tr - (FFN_CONV - 1):, :]
    prev = jnp.concatenate([conv0[:, None], tails], axis=1)
    return pl.pallas_call(
        _ffn_act_kernel,
        grid=(bsz, nt, nj),
        in_specs=[
            pl.BlockSpec((tr, tc), lambda b, i, j: (b * nt + i, j)),
            pl.BlockSpec((tr, tc), lambda b, i, j: (b * nt + i, nj + j)),
            pl.BlockSpec((1, 1, FFN_CONV - 1, tc), lambda b, i, j: (b, i, 0, j)),
            pl.BlockSpec((1, 1, FFN_CONV - 1, tc), lambda b, i, j: (b, i, 0, nj + j)),
            pl.BlockSpec((FFN_CONV, tc), lambda b, i, j: (0, j)),
            pl.BlockSpec((FFN_CONV, tc), lambda b, i, j: (0, nj + j)),
        ],
        out_specs=pl.BlockSpec((tr, tc), lambda b, i, j: (b * nt + i, j)),
        out_shape=jax.ShapeDtypeStruct((bsz * t, f), BF16),
        compiler_params=_cparams(("parallel", "parallel", "arbitrary")),
        name="ffn_conv_act",
    )(up, up, prev, prev, conv_w, conv_w)


def _ffn_up_kernel(u_ref, wa_ref, wg_ref, cwa_ref, cwg_ref, c0a_ref, c0g_ref, act_ref, ta_ref, tg_ref, carry_ref,
                   *, tiles_per_seq, sub):
    i = pl.program_id(0)
    j = pl.program_id(1)
    tm, tn = act_ref.shape

    @pl.when(i == 0)
    def _():
        carry_ref[j] = jnp.zeros(carry_ref.shape[1:], F32)

    first = (i % tiles_per_seq) == 0
    pad = jnp.zeros((8 - (FFN_CONV - 1), tn), F32)
    carried = carry_ref[j]
    prev_a = jnp.where(first, jnp.concatenate([pad, c0a_ref[0]], axis=0), carried[0:8])
    prev_g = jnp.where(first, jnp.concatenate([pad, c0g_ref[0]], axis=0), carried[8:16])
    def up(r):
        ur = u_ref[r * sub:(r + 1) * sub, :]
        return (jnp.dot(ur, wa_ref[...], preferred_element_type=F32),
                jnp.dot(ur, wg_ref[...], preferred_element_type=F32))

    nxt = up(0)
    for r in range(tm // sub):
        rows = slice(r * sub, (r + 1) * sub)
        za, zg = nxt
        if r + 1 < tm // sub:
            nxt = up(r + 1)
        fa =(_shift_rows(za, prev_a, 2) * cwa_ref[0:1] + _shift_rows(za, prev_a, 1) * cwa_ref[1:2]
              + za * cwa_ref[2:3])
        fg = (_shift_rows(zg, prev_g, 2) * cwg_ref[0:1] + _shift_rows(zg, prev_g, 1) * cwg_ref[1:2]
              + zg * cwg_ref[2:3])
        act_ref[rows, :] = (_silu(fg) * fa).astype(act_ref.dtype)
        prev_a = za[sub - 8:sub]
        prev_g = zg[sub - 8:sub]
    carry_ref[j] = jnp.concatenate([prev_a, prev_g], axis=0)
    ta_ref[0] = prev_a[8 - (FFN_CONV - 1):8]
    tg_ref[0] = prev_g[8 - (FFN_CONV - 1):8]


def _ffn_up_act(u, w_up, conv0, conv_w, bsz, t):
    m, d = u.shape
    f = w_up.shape[1] // 2
    tm = _tile(t, FFN_ROW_TILE, 8)
    sub = _tile(tm, FFN_SUB_ROWS, 8)
    tn = _tile(f, 512, 128)
    nj = f // tn
    tps = t // tm
    act, ta, tg = pl.pallas_call(
        functools.partial(_ffn_up_kernel, tiles_per_seq=tps, sub=sub),
        grid=(m // tm, nj),
        in_specs=[
            pl.BlockSpec((tm, d), lambda i, j: (i, 0)),
            pl.BlockSpec((d, tn), lambda i, j: (0, j)),
            pl.BlockSpec((d, tn), lambda i, j: (0, nj + j)),
            pl.BlockSpec((FFN_CONV, tn), lambda i, j: (0, j)),
            pl.BlockSpec((FFN_CONV, tn), lambda i, j: (0, nj + j)),
            pl.BlockSpec((1, FFN_CONV - 1, tn), lambda i, j: (i // tps, 0, j)),
            pl.BlockSpec((1, FFN_CONV - 1, tn), lambda i, j: (i // tps, 0, nj + j)),
        ],
        out_specs=[
            pl.BlockSpec((tm, tn), lambda i, j: (i, j)),
            pl.BlockSpec((1, FFN_CONV - 1, tn), lambda i, j: (i // tps, 0, j)),
            pl.BlockSpec((1, FFN_CONV - 1, tn), lambda i, j: (i // tps, 0, j)),
        ],
        out_shape=[jax.ShapeDtypeStruct((m, f), BF16),
                   jax.ShapeDtypeStruct((bsz, FFN_CONV - 1, f), F32),
                   jax.ShapeDtypeStruct((bsz, FFN_CONV - 1, f), F32)],
        scratch_shapes=[pltpu.VMEM((nj, 16, tn), F32)],
        compiler_params=_cparams(("arbitrary", "arbitrary")),
        name="ffn_up_conv_act",
    )(u, w_up, w_up, conv_w, conv_w, conv0, conv0)
    return act, jnp.concatenate([ta, tg], axis=-1)


def _prep_layer(p, d):
    w_in = p["w_in"]
    o_ml = 0
    o_if = ML_MAIN
    o_rw = o_if + 2 * ML_HEADS
    o_gd = o_rw + RW_COLS
    o_ba = o_gd + GD_MAIN
    o_gate = o_ba + 2 * GD_HEADS
    w_main = jnp.concatenate(
        [w_in[:, o_ml:o_ml + ML_MAIN], w_in[:, o_gd:o_gd + GD_MAIN], w_in[:, o_gate:]], axis=1).astype(BF16)
    w_rw = w_in[:, o_rw:o_rw + RW_COLS].astype(BF16)
    n_small = 2 * ML_HEADS + 2 * GD_HEADS
    w_small = jnp.concatenate(
        [w_in[:, o_if:o_if + 2 * ML_HEADS], w_in[:, o_ba:o_ba + 2 * GD_HEADS],
         jnp.zeros((d, SMALL_W - n_small), F32)], axis=1).astype(BF16)
    zrow = jnp.zeros((SMALL_W,), F32)
    q = dict(p)
    q.update(
        w_main=w_main, w_rw=w_rw, w_small=w_small,
        ml_bias_row=zrow.at[SM_ML_I:SM_ML_I + 2 * ML_HEADS].set(p["ml_b_if"]).reshape(1, SMALL_W),
        gd_alog_row=zrow.at[SM_GD_A:SM_GD_A + GD_HEADS].set(p["gd_a_log"]).reshape(1, SMALL_W),
        gd_dtb_row=zrow.at[SM_GD_A:SM_GD_A + GD_HEADS].set(p["gd_dt_bias"]).reshape(1, SMALL_W),
        w_branch_b=p["w_branch"].astype(BF16), w_out_b=p["w_out"].astype(BF16),
        w_ca_q_b=p["w_ca_q"].astype(BF16), w_ca_kv_b=p["w_ca_kv"].astype(BF16),
        w_ca_o_b=p["w_ca_o"].astype(BF16), w_up_b=p["w_up"].astype(BF16), w_down_b=p["w_down"].astype(BF16))
    return q


def _layer(h, mem_k, mem_v, st, p, bsz, t):
    d = h.shape[1]
    u = _rmsnorm(h, p["g_mix"], BF16)
    p_main = _matmul(u, p["w_main"])
    p_rw = _matmul(u, p["w_rw"])
    p_small = _matmul(u, p["w_small"])

    y_ml, ml_c, ml_n, ml_m = _mlstm(p_main, p_small, p["ml_bias_row"], p["ml_norm"],
                                    st["ml_C"], st["ml_n"], st["ml_m"], bsz, t)
    y_rw, rw_s = _rwkv(p_rw, st["rw_shift"], st["rw_S"], p, bsz, t)
    y_gd, gd_s = _gdn(p_main, p_small, p["gd_conv_w"], st["gd_conv"], p["gd_alog_row"], p["gd_dtb_row"],
                      p["gd_norm"], st["gd_S"], bsz, t)
    merged = _merge((y_ml, y_rw, y_gd), p["w_branch_b"], p_main, d)
    h = _matmul(merged, p["w_out_b"], residual=h)

    u = _rmsnorm(h, p["g_ca"], BF16)
    q = _matmul(u, p["w_ca_q_b"], out_dtype=BF16)
    o = _attention(q, mem_k, mem_v, bsz, t)
    h = _matmul(o, p["w_ca_o_b"], residual=h)

    u = _rmsnorm(h, p["g_ffn"], BF16)
    if t >= FFN_FUSE_MIN_T:
        act, ffn_conv = _ffn_up_act(u, p["w_up_b"], st["ffn_conv"], p["ffn_conv_w"], bsz, t)
    else:
        up = _matmul(u, p["w_up_b"])
        act = _ffn_act(up, st["ffn_conv"], p["ffn_conv_w"], bsz, t)
        ffn_conv = up.reshape(bsz, t, -1)[:, t - (FFN_CONV - 1):]
    h = _matmul(act, p["w_down_b"], residual=h)

    gd0 = ML_MAIN
    new_st = dict(
        ml_C=ml_c, ml_n=ml_n, ml_m=ml_m.reshape(bsz, ML_HEADS), rw_S=rw_s,
        rw_shift=p_rw.reshape(bsz, t, RW_COLS)[:, t - 1],
        gd_S=gd_s,
        gd_conv=p_main.reshape(bsz, t, -1)[:, t - (GD_CONV - 1):, gd0:gd0 + GD_QKV],
        ffn_conv=ffn_conv)
    return h, new_st


def _zero_state(bsz, d_ff2):
    return dict(
        ml_C=jnp.zeros((bsz, ML_HEADS, ML_DQK, ML_DV), F32), ml_n=jnp.zeros((bsz, ML_HEADS, ML_DQK), F32),
        ml_m=jnp.zeros((bsz, ML_HEADS), F32), rw_S=jnp.zeros((bsz, RW_HEADS, RW_N, RW_N), F32),
        rw_shift=jnp.zeros((bsz, RW_COLS), F32), gd_S=jnp.zeros((bsz, GD_HEADS, GD_DK, GD_DV), F32),
        gd_conv=jnp.zeros((bsz, GD_CONV - 1, GD_QKV), F32), ffn_conv=jnp.zeros((bsz, FFN_CONV - 1, d_ff2), F32))


def kernel(x_prompt, x_sample, cache_mem_k, cache_mem_v, state_mlstm_C, state_mlstm_n, state_mlstm_m, state_rwkv_S, state_rwkv_shift, state_gdn_S, state_gdn_conv, state_ffn_conv, mem_prompt, g_mix, w_in, ml_b_if, ml_norm, rw_mu, rw_w0, rw_w2, rw_a0, rw_a2, rw_g2, rw_k_k, rw_k_a, rw_r_k, rw_ln, gd_conv_w, gd_a_log, gd_dt_bias, gd_norm, w_branch, w_out, g_ca, g_mem, w_ca_q, w_ca_kv, w_ca_o, g_ffn, w_up, ffn_conv_w, w_down, g_final):
    bp, tp, d = x_prompt.shape
    bs, ts, _ = x_sample.shape
    depth = w_in.shape[0]
    n_mem = mem_prompt.shape[1]
    assert tp % CHUNK == 0 and ts % CHUNK == 0
    stacked = dict(g_mix=g_mix, w_in=w_in, ml_b_if=ml_b_if, ml_norm=ml_norm, rw_mu=rw_mu, rw_w0=rw_w0,
                   rw_w2=rw_w2, rw_a0=rw_a0, rw_a2=rw_a2, rw_g2=rw_g2, rw_k_k=rw_k_k, rw_k_a=rw_k_a,
                   rw_r_k=rw_r_k, rw_ln=rw_ln, gd_conv_w=gd_conv_w, gd_a_log=gd_a_log, gd_dt_bias=gd_dt_bias,
                   gd_norm=gd_norm, w_branch=w_branch, w_out=w_out, g_ca=g_ca, g_mem=g_mem, w_ca_q=w_ca_q,
                   w_ca_kv=w_ca_kv, w_ca_o=w_ca_o, g_ffn=g_ffn, w_up=w_up, ffn_conv_w=ffn_conv_w,
                   w_down=w_down)
    keys = ("ml_C", "ml_n", "ml_m", "rw_S", "rw_shift", "gd_S", "gd_conv", "ffn_conv")
    new_p = {k: [] for k in keys}
    new_s = {k: [] for k in keys}
    mem_k_list, mem_v_list = [], []
    hp = x_prompt.reshape(bp * tp, d)
    hs = x_sample.reshape(bs * ts, d)
    mem2d = mem_prompt.reshape(bp * n_mem, d)
    for l in range(depth):
        p = _prep_layer({k: v[l] for k, v in stacked.items()}, d)
        kv = _matmul(_rmsnorm(mem2d, p["g_mem"], BF16), p["w_ca_kv_b"])
        mk = kv[:, :CA_WIDTH].reshape(bp, n_mem, CA_WIDTH)
        mv = kv[:, CA_WIDTH:].reshape(bp, n_mem, CA_WIDTH)
        hp, stp = _layer(hp, mk.astype(BF16), mv.astype(BF16), _zero_state(bp, w_up.shape[2]), p, bp, tp)
        mem_k_list.append(mk.reshape(bp, n_mem, CA_HEADS, CA_HEAD_DIM))
        mem_v_list.append(mv.reshape(bp, n_mem, CA_HEADS, CA_HEAD_DIM))
        st_in = dict(ml_C=state_mlstm_C[l], ml_n=state_mlstm_n[l], ml_m=state_mlstm_m[l],
                     rw_S=state_rwkv_S[l], rw_shift=state_rwkv_shift[l], gd_S=state_gdn_S[l],
                     gd_conv=state_gdn_conv[l], ffn_conv=state_ffn_conv[l])
        ck = cache_mem_k[l].reshape(bs, n_mem, CA_WIDTH).astype(BF16)
        cv = cache_mem_v[l].reshape(bs, n_mem, CA_WIDTH).astype(BF16)
        hs, sts = _layer(hs, ck, cv, st_in, p, bs, ts)
        for k in keys:
            new_p[k].append(stp[k])
            new_s[k].append(sts[k])
    y_prompt = _rmsnorm(hp, g_final, F32).reshape(bp, tp, d)
    y_sample = _rmsnorm(hs, g_final, F32).reshape(bs, ts, d)
    outs = [y_prompt, y_sample, jnp.stack(mem_k_list), jnp.stack(mem_v_list)]
    outs += [jnp.stack(new_p[k]) for k in keys]
    outs += [jnp.stack(new_s[k]) for k in keys]
    return tuple(outs)
```

```python
import functools

import jax
import jax.numpy as jnp
from jax import lax
from jax.experimental import pallas as pl
from jax.experimental.pallas import tpu as pltpu

F32 = jnp.float32
BF16 = jnp.bfloat16

EPS = 1e-6
CHUNK = 64
SUB_CHUNKS = 4

ML_HEADS, ML_DQK, ML_DV = 4, 128, 256
ML_QK = ML_HEADS * ML_DQK
ML_WIDTH = ML_HEADS * ML_DV
ML_MAIN = 2 * ML_QK + 2 * ML_WIDTH
ML_PASSES = 3

RW_HEADS, RW_N = 16, 64
RW_WIDTH = RW_HEADS * RW_N
RW_W_RANK, RW_A_RANK, RW_G_RANK = 64, 64, 128
RW_COLS = 3 * RW_WIDTH + RW_W_RANK + RW_A_RANK + RW_G_RANK
RW_GN_EPS = 64e-5
RW_GROUP = 4
RW_PASSES = 1

GD_HEADS, GD_DK, GD_DV = 8, 128, 128
GD_QK = GD_HEADS * GD_DK
GD_WIDTH = GD_HEADS * GD_DV
GD_QKV = 2 * GD_QK + GD_WIDTH
GD_CONV = 4
GD_MAIN = GD_QKV + GD_WIDTH
GD_GROUP = 4
GD_PASSES = 1

BR_WIDTH = 1024
CA_HEADS, CA_HEAD_DIM = 4, 256
CA_WIDTH = CA_HEADS * CA_HEAD_DIM
FFN_CONV = 3
FFN_FUSE_MIN_T = 512
FFN_ROW_TILE = 1024
FFN_SUB_ROWS = 128

SMALL_W = 128
SM_ML_I, SM_ML_F, SM_GD_B, SM_GD_A = 0, ML_HEADS, 2 * ML_HEADS, 2 * ML_HEADS + GD_HEADS

V7X_VMEM_LIMIT = 56 * 1024 * 1024
TRI_BLOCK = 16


def _cparams(sem):
    return pltpu.CompilerParams(dimension_semantics=sem, vmem_limit_bytes=V7X_VMEM_LIMIT)


def _tile(dim, pref, quantum):
    if dim <= pref:
        return dim
    t = (pref // quantum) * quantum
    while t > quantum and dim % t:
        t -= quantum
    assert dim % t == 0, (dim, pref, quantum)
    return t


def _sub_chunks(t):
    n = SUB_CHUNKS
    while (t // CHUNK) % n:
        n -= 1
    return n


def _split2(a):
    hi = a.astype(BF16)
    lo = (a - hi.astype(F32)).astype(BF16)
    return hi, lo


def _dg(a, b, ca, cb):
    return lax.dot_general(a, b, (((ca,), (cb,)), ((), ())), preferred_element_type=F32)


def _dot3(a, b, ca=1, cb=0):
    ah, al = _split2(a)
    bh, bl = _split2(b)
    return _dg(ah, bh, ca, cb) + (_dg(al, bh, ca, cb) + _dg(ah, bl, ca, cb))


def _dotp(a, b, ca, cb, passes):
    if passes == 1:
        return _dg(a.astype(BF16), b.astype(BF16), ca, cb)
    return _dot3(a, b, ca, cb)


def _tri_masks(c):
    row = lax.broadcasted_iota(jnp.int32, (c, c), 0)
    col = lax.broadcasted_iota(jnp.int32, (c, c), 1)
    return row, col


def _cumsum_rows(x):
    c = x.shape[0]
    row, col = _tri_masks(c)
    tri = jnp.where(row >= col, 1.0, 0.0).astype(BF16)
    x0 = x.astype(BF16)
    r1 = x - x0.astype(F32)
    x1 = r1.astype(BF16)
    x2 = (r1 - x1.astype(F32)).astype(BF16)
    return _dg(tri, x0, 1, 0) + (_dg(tri, x1, 1, 0) + _dg(tri, x2, 1, 0))


def _tri_solve(low, rhs, c, passes):
    return _tri_solve_many([low], [rhs], c, passes)[0]


def _tri_solve_many(lows, rhss, c, passes):
    return _tri_apply_many(_tri_factor_many(lows, c, passes), rhss, passes)


def _tri_factor_many(lows, c, passes):
    mm = lambda a, b: _dotp(a, b, 1, 0, passes)
    row, col = _tri_masks(lows[0].shape[0])
    same = (row // TRI_BLOCK) == (col // TRI_BLOCK)
    eye = jnp.where(row == col, 1.0, 0.0).astype(F32)
    ps = [jnp.where(same, -low, 0.0) for low in lows]
    offs = [jnp.where(same, 0.0, low) for low in lows]
    xs = [eye + p for p in ps]
    steps = 1
    while steps * 2 < TRI_BLOCK:
        ps = [mm(p, p) for p in ps]
        xs = [x + mm(x, p) for x, p in zip(xs, ps)]
        steps *= 2
    ms = [mm(x, off) for x, off in zip(xs, offs)]
    nblk = c // TRI_BLOCK
    terms = []
    pws = ms
    k = 1
    while k < nblk:
        terms.append(pws)
        k *= 2
        if k < nblk:
            pws = [mm(pw, pw) for pw in pws]
    return xs, terms


def _tri_apply_many(factors, rhss, passes):
    mm = lambda a, b: _dotp(a, b, 1, 0, passes)
    xs, terms = factors
    us = xs if rhss is None else [mm(x, rhs) for x, rhs in zip(xs, rhss)]
    for i in range(len(terms) - 1, 0, -1):
        us = [u + mm(t, u) for t, u in zip(terms[i], us)]
    if terms:
        us = [u - mm(t, u) for t, u in zip(terms[0], us)]
    return us


def _head_sums(x, hw):
    assert 2 * hw == 128
    lane = lax.broadcasted_iota(jnp.int32, (x.shape[0], 128), 1)
    lo = lane < hw
    out = []
    for s in range(x.shape[1] // 128):
        xs = x[:, s * 128:(s + 1) * 128]
        s_lo = jnp.sum(jnp.where(lo, xs, 0.0), axis=-1, keepdims=True)
        s_hi = jnp.sum(jnp.where(lo, 0.0, xs), axis=-1, keepdims=True)
        out.append(jnp.where(lo, s_lo, s_hi))
    return jnp.concatenate(out, axis=1)


def _softplus(x):
    return jnp.maximum(x, 0.0) + jnp.log1p(jnp.exp(-jnp.abs(x)))


def _sigmoid(x):
    return 1.0 / (1.0 + jnp.exp(-x))


def _silu(x):
    return x * _sigmoid(x)


def _shift_rows(x, prev8, k):
    xr = pltpu.roll(x, k, 0)
    pr = pltpu.roll(prev8, k, 0)
    row = lax.broadcasted_iota(jnp.int32, (8, x.shape[1]), 0)
    head = jnp.where(row < k, pr, xr[0:8])
    if x.shape[0] == 8:
        return head
    return jnp.concatenate([head, xr[8:]], axis=0)


def _rmsnorm_kernel(x_ref, g_ref, o_ref):
    x = x_ref[...]
    y = x * lax.rsqrt(jnp.mean(x * x, axis=-1, keepdims=True) + EPS)
    o_ref[...] = (y * g_ref[...]).astype(o_ref.dtype)


def _rmsnorm(x, g, out_dtype):
    m, d = x.shape
    tr = _tile(m, 256, 8)
    return pl.pallas_call(
        _rmsnorm_kernel,
        grid=(m // tr,),
        in_specs=[pl.BlockSpec((tr, d), lambda i: (i, 0)), pl.BlockSpec((1, d), lambda i: (0, 0))],
        out_specs=pl.BlockSpec((tr, d), lambda i: (i, 0)),
        out_shape=jax.ShapeDtypeStruct((m, d), out_dtype),
        compiler_params=_cparams(("parallel",)),
        name="rmsnorm",
    )(x, g.reshape(1, d))


def _mm_kernel(a_ref, w_ref, o_ref):
    o_ref[...] = jnp.dot(a_ref[...], w_ref[...], preferred_element_type=F32).astype(o_ref.dtype)


def _mm_res_kernel(a_ref, w_ref, r_ref, o_ref):
    o_ref[...] = r_ref[...] + jnp.dot(a_ref[...], w_ref[...], preferred_element_type=F32)


def _row_rstd(ss_ref, d):
    return lax.rsqrt(jnp.sum(ss_ref[...], axis=-1, keepdims=True) * (1.0 / d) + EPS)


def _mm_scaled_kernel(a_ref, w_ref, ss_ref, o_ref, *, d):
    acc = jnp.dot(a_ref[...], w_ref[...], preferred_element_type=F32)
    o_ref[...] = (acc * _row_rstd(ss_ref, d)).astype(o_ref.dtype)


def _mm_res_norm_kernel(a_ref, w_ref, r_ref, g_ref, o_ref, hg_ref, ss_ref):
    h = r_ref[...] + jnp.dot(a_ref[...], w_ref[...], preferred_element_type=F32)
    o_ref[...] = h
    hg_ref[...] = (h * g_ref[...]).astype(hg_ref.dtype)
    h2 = h * h
    part = h2[:, 0:128]
    for s in range(1, h.shape[1] // 128):
        part = part + h2[:, s * 128:(s + 1) * 128]

    @pl.when(pl.program_id(1) == 0)
    def _():
        ss_ref[...] = part

    @pl.when(pl.program_id(1) != 0)
    def _():
        ss_ref[...] = ss_ref[...] + part


def _mm_tiles(m, k, n, has_residual):
    tm = _tile(m, 1024 if k <= 4096 else 512, 8)
    tn = _tile(n, 1024 if (k <= 4096 and not has_residual) else 512, 128)
    return tm, tn


def _matmul(a, w, residual=None, out_dtype=F32, row_ss=None, next_gain=None):
    m, k = a.shape
    n = w.shape[1]
    tm, tn = _mm_tiles(m, k, n, residual is not None)
    in_specs = [pl.BlockSpec((tm, k), lambda i, j: (i, 0)), pl.BlockSpec((k, tn), lambda i, j: (0, j))]
    args = [a, w]
    body = _mm_kernel
    out_specs = pl.BlockSpec((tm, tn), lambda i, j: (i, j))
    out_shape = jax.ShapeDtypeStruct((m, n), out_dtype)
    if residual is not None:
        in_specs.append(pl.BlockSpec((tm, tn), lambda i, j: (i, j)))
        args.append(residual)
        body = _mm_res_kernel
        if next_gain is not None:
            in_specs.append(pl.BlockSpec((1, tn), lambda i, j: (0, j)))
            args.append(next_gain.reshape(1, n))
            body = _mm_res_norm_kernel
            out_specs = [out_specs, pl.BlockSpec((tm, tn), lambda i, j: (i, j)),
                         pl.BlockSpec((tm, 128), lambda i, j: (i, 0))]
            out_shape = [out_shape, jax.ShapeDtypeStruct((m, n), BF16), jax.ShapeDtypeStruct((m, 128), F32)]
    elif row_ss is not None:
        in_specs.append(pl.BlockSpec((tm, 128), lambda i, j: (i, 0)))
        args.append(row_ss)
        body = functools.partial(_mm_scaled_kernel, d=k)
    return pl.pallas_call(
        body,
        grid=(m // tm, n // tn),
        in_specs=in_specs,
        out_specs=out_specs,
        out_shape=out_shape,
        compiler_params=_cparams(("parallel", "arbitrary")),
        name="matmul",
    )(*args)


def _mlstm_kernel(q_ref, k_ref, v_ref, og_ref, sm_ref, bias_ref, norm_ref, c0_ref, n0_ref, m0_ref,
                  y_ref, c_ref, n_ref, m_ref):
    @pl.when(pl.program_id(1) == 0)
    def _():
        c_ref[...] = c0_ref[...]
        n_ref[...] = n0_ref[...]
        m_ref[...] = m0_ref[...]

    c = q_ref.shape[0]
    pre = sm_ref[...] + bias_ref[...]
    logf = -_softplus(-pre)
    cum = _cumsum_rows(logf)
    pre_t = pre.T
    cum_t = cum.T
    row, col = _tri_masks(c)
    causal = row >= col
    hs = range(ML_HEADS)
    q = [q_ref[:, h * ML_DQK:(h + 1) * ML_DQK] for h in hs]
    k = [k_ref[:, h * ML_DQK:(h + 1) * ML_DQK] * (ML_DQK ** -0.5) for h in hs]
    v = [v_ref[:, h * ML_DV:(h + 1) * ML_DV] for h in hs]
    b_col = [cum[:, SM_ML_F + h:SM_ML_F + h + 1] for h in hs]
    b_row = [cum_t[SM_ML_F + h:SM_ML_F + h + 1, :] for h in hs]
    i_col = [pre[:, SM_ML_I + h:SM_ML_I + h + 1] for h in hs]
    i_row = [pre_t[SM_ML_I + h:SM_ML_I + h + 1, :] for h in hs]
    m_old = [m_ref[0, :, h:h + 1] for h in hs]
    c_old = [c_ref[0, h] for h in hs]
    n_old = [n_ref[0, h:h + 1, :] for h in hs]

    qk = [_dotp(q[h], k[h], 1, 1, ML_PASSES) for h in hs]
    qc = [_dotp(q[h], c_old[h], 1, 0, ML_PASSES) for h in hs]
    dmat = [jnp.where(causal, b_col[h] - b_row[h] + i_row[h], -jnp.inf) for h in hs]
    mt = [jnp.maximum(b_col[h] + m_old[h], jnp.max(dmat[h], axis=-1, keepdims=True)) for h in hs]
    pmat = [jnp.exp(dmat[h] - mt[h]) * qk[h] for h in hs]
    inter = [jnp.exp(b_col[h] + m_old[h] - mt[h]) for h in hs]
    num = [inter[h] * qc[h] + _dotp(pmat[h], v[h], 1, 0, ML_PASSES) for h in hs]
    den = [inter[h] * jnp.sum(q[h] * n_old[h], axis=-1, keepdims=True)
           + jnp.sum(pmat[h], axis=-1, keepdims=True) for h in hs]
    hh = [num[h] / jnp.maximum(jnp.abs(den[h]), jnp.exp(-mt[h])) for h in hs]

    m_new = [mt[h][c - 1:c, :] for h in hs]
    b_last = [b_col[h][c - 1:c, :] for h in hs]
    kw = [k[h] * jnp.exp(b_last[h] - b_col[h] + i_col[h] - m_new[h]) for h in hs]
    dec = [jnp.exp(b_last[h] + m_old[h] - m_new[h]) for h in hs]
    c_new = [dec[h] * c_old[h] + _dotp(kw[h].T, v[h], 1, 0, ML_PASSES) for h in hs]
    for h in hs:
        c_ref[0, h] = c_new[h]
        n_ref[0, h:h + 1, :] = dec[h] * n_old[h] + jnp.sum(kw[h], axis=0, keepdims=True)
        m_ref[0, :, h:h + 1] = m_new[h]
        hn = hh[h] * lax.rsqrt(jnp.mean(hh[h] * hh[h], axis=-1, keepdims=True) + EPS)
        og = og_ref[:, h * ML_DV:(h + 1) * ML_DV]
        y = hn * norm_ref[:, h * ML_DV:(h + 1) * ML_DV] * _sigmoid(og)
        y_ref[:, h * ML_DV:(h + 1) * ML_DV] = y.astype(y_ref.dtype)


def _mlstm(p_main, p_small, bias_row, ml_norm, c0, n0, m0, bsz, t):
    nc = t // CHUNK
    rows = bsz * t
    rmap = lambda b, c: b * nc + c
    return pl.pallas_call(
        _mlstm_kernel,
        grid=(bsz, nc),
        in_specs=[
            pl.BlockSpec((CHUNK, ML_QK), lambda b, c: (rmap(b, c), 0)),
            pl.BlockSpec((CHUNK, ML_QK), lambda b, c: (rmap(b, c), 1)),
            pl.BlockSpec((CHUNK, ML_WIDTH), lambda b, c: (rmap(b, c), 1)),
            pl.BlockSpec((CHUNK, ML_WIDTH), lambda b, c: (rmap(b, c), 2)),
            pl.BlockSpec((CHUNK, SMALL_W), lambda b, c: (rmap(b, c), 0)),
            pl.BlockSpec((1, SMALL_W), lambda b, c: (0, 0)),
            pl.BlockSpec((1, ML_WIDTH), lambda b, c: (0, 0)),
            pl.BlockSpec((1, ML_HEADS, ML_DQK, ML_DV), lambda b, c: (b, 0, 0, 0)),
            pl.BlockSpec((1, ML_HEADS, ML_DQK), lambda b, c: (b, 0, 0)),
            pl.BlockSpec((1, 1, ML_HEADS), lambda b, c: (b, 0, 0)),
        ],
        out_specs=[
            pl.BlockSpec((CHUNK, ML_WIDTH), lambda b, c: (rmap(b, c), 0)),
            pl.BlockSpec((1, ML_HEADS, ML_DQK, ML_DV), lambda b, c: (b, 0, 0, 0)),
            pl.BlockSpec((1, ML_HEADS, ML_DQK), lambda b, c: (b, 0, 0)),
            pl.BlockSpec((1, 1, ML_HEADS), lambda b, c: (b, 0, 0)),
        ],
        out_shape=[
            jax.ShapeDtypeStruct((rows, ML_WIDTH), BF16),
            jax.ShapeDtypeStruct(c0.shape, F32),
            jax.ShapeDtypeStruct(n0.shape, F32),
            jax.ShapeDtypeStruct((bsz, 1, ML_HEADS), F32),
        ],
        compiler_params=_cparams(("parallel", "arbitrary")),
        name="mlstm",
    )(p_main, p_main, p_main, p_main, p_small, bias_row, ml_norm.reshape(1, ML_WIDTH),
      c0, n0, m0.reshape(bsz, 1, ML_HEADS))


def _rwkv_front(x, prev8, mu_ref, w0_ref, w2_ref, a0_ref, a2_ref, g2_ref, kk_ref, ka_ref, rk_ref):
    c = x.shape[0]
    w = RW_WIDTH
    xprev = _shift_rows(x, prev8, 1)
    xm = x + (xprev - x) * mu_ref[...]
    rr = xm[:, 0:w]
    rk = xm[:, w:2 * w]
    rv = xm[:, 2 * w:3 * w]
    xw = xm[:, 3 * w:3 * w + RW_W_RANK]
    xa = xm[:, 3 * w + RW_W_RANK:3 * w + RW_W_RANK + RW_A_RANK]
    xg = xm[:, 3 * w + RW_W_RANK + RW_A_RANK:]

    w_pre = w0_ref[...] + _dot3(jnp.tanh(xw), w2_ref[...])
    lw = -jnp.exp(-_softplus(-w_pre) - 0.5)
    a = _sigmoid(a0_ref[...] + _dot3(xa, a2_ref[...]))
    g_out = _dot3(_sigmoid(xg), g2_ref[...])
    kk_raw = rk * kk_ref[...]
    kmod = rk * (1.0 + (a - 1.0) * ka_ref[...])
    bon = rr * kmod * rk_ref[...]

    lc = _cumsum_rows(lw)
    l_end = lc[c - 1:c, :]
    l_mid = lc[c // 2 - 1:c // 2, :]
    lcc = lc - l_mid
    p_mid = jnp.exp(l_mid)
    p_in = jnp.exp(lcc)
    p_prev = jnp.exp(lcc - lw)
    p_inv = jnp.exp(-lcc)
    p_end = jnp.exp(l_end - lc)
    p_all = jnp.exp(l_end)

    kkn = kk_raw * lax.rsqrt(_head_sums(kk_raw * kk_raw, RW_N) + 1e-6)
    bb = kkn * a
    kkp = kkn * p_prev
    rp = rr * p_in
    kd = kmod * p_inv
    bd = bb * p_inv
    k_end = kmod * p_end
    b_end = bb * p_end
    return dict(kkp=kkp, rp=rp, kd=kd, bd=bd, k_end=k_end, b_end=b_end, rv=rv, p_mid=p_mid, p_all=p_all,
                bonus=_head_sums(bon, RW_N) * rv, g_out=g_out)


def _rw_groups():
    gw = RW_GROUP * RW_N
    return [slice(g * gw, (g + 1) * gw) for g in range(RW_HEADS // RW_GROUP)]


def _rw_masks(c):
    shape = (RW_GROUP * c, RW_GROUP * RW_N)
    head_eq = lax.broadcasted_iota(jnp.int32, shape, 0) // c == lax.broadcasted_iota(jnp.int32, shape, 1) // RW_N
    tn_ = lax.broadcasted_iota(jnp.int32, (c, RW_GROUP * RW_N), 0)
    sn_ = lax.broadcasted_iota(jnp.int32, (c, RW_GROUP * RW_N), 1) % RW_N
    return head_eq, tn_ > sn_, tn_ >= sn_


def _rw_blockdiag(z, head_eq):
    return jnp.where(head_eq, jnp.concatenate([z] * RW_GROUP, axis=0), 0.0)


def _rw_rowsum(z, c):
    out = z[0:c]
    for i in range(1, RW_GROUP):
        out = out + z[i * c:(i + 1) * c]
    return out


def _rwkv_mid(f, masks):
    kkp, rp, kd, bd, rv = f["kkp"], f["rp"], f["kd"], f["bd"], f["rv"]
    c = rv.shape[0]
    head_eq, strict, incl = masks
    groups = _rw_groups()
    nt = lambda x, y: _dotp(x, y, 1, 1, RW_PASSES)
    x2 = [jnp.concatenate([kkp[:, cs], rp[:, cs]], axis=0) for cs in groups]
    ab_b = [nt(x, _rw_blockdiag(bd[:, cs], head_eq)) for x, cs in zip(x2, groups)]
    ab_k = [nt(x, _rw_blockdiag(kd[:, cs], head_eq)) for x, cs in zip(x2, groups)]
    a_b = [_rw_blockdiag(jnp.where(strict, z[0:c], 0.0), head_eq) for z in ab_b]
    return dict(
        x2=x2, v_bd=[_rw_blockdiag(rv[:, cs], head_eq) for cs in groups],
        t_inv=_tri_apply_many(_tri_factor_many(a_b, c, RW_PASSES), None, RW_PASSES),
        a_k=[jnp.where(strict, z[0:c], 0.0) for z in ab_k],
        r_k=[jnp.where(incl, z[c:], 0.0) for z in ab_k],
        r_b=[jnp.where(incl, z[c:], 0.0) for z in ab_b])


def _rwkv_chain(f, g, s_olds, ln_ref, masks):
    k_end, b_end, rv, p_mid, p_all = f["k_end"], f["b_end"], f["rv"], f["p_mid"], f["p_all"]
    c = rv.shape[0]
    groups = _rw_groups()
    head_eq = masks[0]
    nt = lambda x, y: _dotp(x, y, 1, 1, RW_PASSES)
    nn = lambda x, y: _dotp(x, y, 1, 0, RW_PASSES)
    v_bd = g["v_bd"]
    ab_s = [nt(x, _rw_blockdiag(s * p_mid[:, cs], head_eq))
            for x, s, cs in zip(g["x2"], s_olds, groups)]
    rhs = [_rw_blockdiag(z[0:c] + nn(ak, v), head_eq) for z, ak, v in zip(ab_s, g["a_k"], v_bd)]
    us = [nn(t_, r_) for t_, r_ in zip(g["t_inv"], rhs)]
    ygs = [z[c:] + nn(rk_, v) - nn(rb_, u) for z, rk_, v, rb_, u in zip(ab_s, g["r_k"], v_bd, g["r_b"], us)]
    upds = [nn(jnp.concatenate([rv[:, cs], -_rw_rowsum(u, c)], axis=0).T,
               jnp.concatenate([k_end[:, cs], b_end[:, cs]], axis=0)) for cs, u in zip(groups, us)]
    s_news = [s * p_all[:, cs] + _rw_rowsum(jnp.where(head_eq, upd, 0.0), c)
              for cs, s, upd in zip(groups, s_olds, upds)]

    yh = jnp.concatenate(ygs, axis=1)
    mu_ = _head_sums(yh, RW_N) * (1.0 / RW_N)
    yc = yh - mu_
    var = _head_sums(yc * yc, RW_N) * (1.0 / RW_N)
    yn = yc * lax.rsqrt(var + RW_GN_EPS)
    return (yn * ln_ref[...] + f["bonus"]) * f["g_out"], s_news


def _rwkv_kernel(p_ref, shift_ref, mu_ref, w0_ref, w2_ref, a0_ref, a2_ref, g2_ref, kk_ref, ka_ref,
                 rk_ref, ln_ref, s0_ref, y_ref, s_ref, prev_ref):
    @pl.when(pl.program_id(1) == 0)
    def _():
        s_ref[...] = s0_ref[...]
        prev_ref[...] = jnp.broadcast_to(shift_ref[0], prev_ref.shape)

    c = CHUNK
    nsub = p_ref.shape[0] // c
    masks = _rw_masks(c)
    fronts = []
    for k in range(nsub):
        last8 = prev_ref[...] if k == 0 else p_ref[k * c - 8:k * c, :]
        fronts.append(_rwkv_front(p_ref[k * c:(k + 1) * c, :], last8, mu_ref, w0_ref, w2_ref, a0_ref, a2_ref,
                                  g2_ref, kk_ref, ka_ref, rk_ref))
    groups = _rw_groups()
    states = [s_ref[0, :, cs] for cs in groups]
    for k in range(nsub):
        out, states = _rwkv_chain(fronts[k], _rwkv_mid(fronts[k], masks), states, ln_ref, masks)
        y_ref[k * c:(k + 1) * c, :] = out.astype(y_ref.dtype)
    prev_ref[...] = p_ref[nsub * c - 8:nsub * c, :]
    for cs, s in zip(groups, states):
        s_ref[0, :, cs] = s


def _rwkv(p_rw, shift0, s0, p, bsz, t):
    step = CHUNK * _sub_chunks(t)
    nc = t // step
    rows = bsz * t
    rmap = lambda b, c: b * nc + c
    full = lambda shape: pl.BlockSpec(shape, lambda b, c: (0,) * len(shape))
    w = RW_WIDTH
    s_nat = s0.transpose(0, 2, 1, 3).reshape(bsz, RW_N, w)
    y, s = pl.pallas_call(
        _rwkv_kernel,
        grid=(bsz, nc),
        in_specs=[
            pl.BlockSpec((step, RW_COLS), lambda b, c: (rmap(b, c), 0)),
            pl.BlockSpec((1, 1, RW_COLS), lambda b, c: (b, 0, 0)),
            full((1, RW_COLS)), full((1, w)), full((RW_W_RANK, w)), full((1, w)), full((RW_A_RANK, w)),
            full((RW_G_RANK, w)), full((1, w)), full((1, w)), full((1, w)), full((1, w)),
            pl.BlockSpec((1, RW_N, w), lambda b, c: (b, 0, 0)),
        ],
        out_specs=[
            pl.BlockSpec((step, w), lambda b, c: (rmap(b, c), 0)),
            pl.BlockSpec((1, RW_N, w), lambda b, c: (b, 0, 0)),
        ],
        out_shape=[jax.ShapeDtypeStruct((rows, w), BF16), jax.ShapeDtypeStruct(s_nat.shape, F32)],
        scratch_shapes=[pltpu.VMEM((8, RW_COLS), F32)],
        compiler_params=_cparams(("parallel", "arbitrary")),
        name="rwkv7",
    )(p_rw, shift0.reshape(bsz, 1, RW_COLS), p["rw_mu"].reshape(1, RW_COLS), p["rw_w0"].reshape(1, w),
      p["rw_w2"], p["rw_a0"].reshape(1, w), p["rw_a2"], p["rw_g2"], p["rw_k_k"].reshape(1, w),
      p["rw_k_a"].reshape(1, w), p["rw_r_k"].reshape(1, w), p["rw_ln"].reshape(1, w), s_nat)
    return y, s.reshape(bsz, RW_N, RW_HEADS, RW_N).transpose(0, 2, 1, 3)


def _gdn_front(xs, prevs, sm, cw_ref, alog_ref, dtb_ref):
    c = xs[0].shape[0]
    acts = []
    for sec, (x, prev8) in enumerate(zip(xs, prevs)):
        cs = slice(sec * GD_QK, (sec + 1) * GD_QK)
        y = _shift_rows(x, prev8, GD_CONV - 1) * cw_ref[0:1, cs]
        for i in range(1, GD_CONV - 1):
            y = y + _shift_rows(x, prev8, GD_CONV - 1 - i) * cw_ref[i:i + 1, cs]
        y = y + x * cw_ref[GD_CONV - 1:GD_CONV, cs]
        acts.append(_silu(y))
    qa, ka, va = acts

    beta = _sigmoid(sm)
    g = -jnp.exp(alog_ref[...]) * _softplus(sm + dtb_ref[...])
    gc = _cumsum_rows(g)
    gc_t = gc.T

    gr = GD_GROUP * c
    gk = GD_GROUP * GD_DK
    r4 = lax.broadcasted_iota(jnp.int32, (gr, gr), 0)
    c4 = lax.broadcasted_iota(jnp.int32, (gr, gr), 1)
    t_minus_s = jnp.where((r4 // c) == (c4 // c), (r4 % c) - (c4 % c), -1)
    m_strict = t_minus_s > 0
    m_incl = t_minus_s >= 0
    head_eq = (lax.broadcasted_iota(jnp.int32, (gr, gk), 0) // c
               == lax.broadcasted_iota(jnp.int32, (gr, gk), 1) // GD_DK)

    def rep(z):
        return jnp.concatenate([z] * GD_GROUP, axis=0)

    def stack(parts):
        return jnp.concatenate(parts, axis=0)

    kn = []
    qn = []
    for h in range(GD_HEADS):
        ks = slice(h * GD_DK, (h + 1) * GD_DK)
        q = qa[:, ks]
        qn.append(q * lax.rsqrt(jnp.sum(q * q, axis=-1, keepdims=True) + 1e-6) * (GD_DK ** -0.5))
        k = ka[:, ks]
        kn.append(k * lax.rsqrt(jnp.sum(k * k, axis=-1, keepdims=True) + 1e-6))
    groups = [range(g0, g0 + GD_GROUP) for g0 in range(0, GD_HEADS, GD_GROUP)]
    k_nat = [jnp.concatenate([kn[h] for h in hs], axis=1) for hs in groups]
    q_nat = [jnp.concatenate([qn[h] for h in hs], axis=1) for hs in groups]
    k_bd = [jnp.where(head_eq, rep(kk_), 0.0) for kk_ in k_nat]
    v_st = [stack([va[:, h * GD_DV:(h + 1) * GD_DV] for h in hs]) for hs in groups]
    b_col = [stack([beta[:, SM_GD_B + h:SM_GD_B + h + 1] for h in hs]) for hs in groups]
    g_col = [stack([gc[:, SM_GD_A + h:SM_GD_A + h + 1] for h in hs]) for hs in groups]
    g_row = [jnp.concatenate([gc_t[SM_GD_A + h:SM_GD_A + h + 1, :] for h in hs], axis=1) for hs in groups]

    dec = [jnp.exp(jnp.where(m_incl, gc_ - gr_, -jnp.inf)) for gc_, gr_ in zip(g_col, g_row)]
    kq = [_dotp(stack([kk_, qq_]), kb, 1, 1, GD_PASSES) for kk_, qq_, kb in zip(k_nat, q_nat, k_bd)]
    low = [jnp.where(m_strict, b * rep(z[0:c]) * d, 0.0) for b, z, d in zip(b_col, kq, dec)]
    gl = [gc[c - 1:c, SM_GD_A + h:SM_GD_A + h + 1] for h in range(GD_HEADS)]
    return dict(
        kq_lhs=[stack([kn[h], qn[h]]) for h in range(GD_HEADS)], v_st=v_st, b_col=b_col,
        eg=[jnp.exp(gc_) for gc_ in g_col],
        t_inv=_tri_apply_many(_tri_factor_many(low, c, GD_PASSES), None, GD_PASSES),
        qkt=[rep(z[c:]) * d for z, d in zip(kq, dec)],
        kw_t=[(kn[h] * jnp.exp(gl[h] - gc[:, SM_GD_A + h:SM_GD_A + h + 1])).T for h in range(GD_HEADS)],
        s_decay=[jnp.exp(gl[h]) for h in range(GD_HEADS)])


def _gdn_chain(f, s_old):
    c = f["kq_lhs"][0].shape[0] // 2
    stack = lambda parts: jnp.concatenate(parts, axis=0)
    groups = [range(g0, g0 + GD_GROUP) for g0 in range(0, GD_HEADS, GD_GROUP)]
    kqs = [_dotp(f["kq_lhs"][h], s_old[h], 1, 0, GD_PASSES) for h in range(GD_HEADS)]
    ks = [stack([kqs[h][0:c] for h in hs]) for hs in groups]
    qs = [stack([kqs[h][c:] for h in hs]) for hs in groups]
    rhs = [b * (v - e * z) for b, v, e, z in zip(f["b_col"], f["v_st"], f["eg"], ks)]
    us = [_dotp(t_, r_, 1, 0, GD_PASSES) for t_, r_ in zip(f["t_inv"], rhs)]
    os_ = [e * z + _dotp(qk, u, 1, 0, GD_PASSES) for e, z, qk, u in zip(f["eg"], qs, f["qkt"], us)]
    outs = []
    s_new = []
    for gi, hs in enumerate(groups):
        for i, h in enumerate(hs):
            rs = slice(i * c, (i + 1) * c)
            s_new.append(f["s_decay"][h] * s_old[h] + _dotp(f["kw_t"][h], us[gi][rs], 1, 0, GD_PASSES))
            outs.append(os_[gi][rs])
    return outs, s_new


def _gdn_kernel(q_ref, k_ref, v_ref, z_ref, sm_ref, cw_ref, conv0_ref, alog_ref, dtb_ref, norm_ref, s0_ref,
                y_ref, s_ref, prev_ref):
    @pl.when(pl.program_id(1) == 0)
    def _():
        s_ref[...] = s0_ref[...]
        prev_ref[...] = jnp.zeros(prev_ref.shape, F32)
        prev_ref[8 - (GD_CONV - 1):8, :] = conv0_ref[0]

    c = CHUNK
    nsub = q_ref.shape[0] // c
    refs = (q_ref, k_ref, v_ref)
    fronts = []
    for k in range(nsub):
        xs = [r[k * c:(k + 1) * c, :] for r in refs]
        if k == 0:
            prevs = [prev_ref[:, sec * GD_QK:(sec + 1) * GD_QK] for sec in range(3)]
        else:
            prevs = [r[k * c - 8:k * c, :] for r in refs]
        fronts.append(_gdn_front(xs, prevs, sm_ref[k * c:(k + 1) * c, :], cw_ref, alog_ref, dtb_ref))
    states = [s_ref[0, h] for h in range(GD_HEADS)]
    for k in range(nsub):
        outs, states = _gdn_chain(fronts[k], states)
        for h in range(GD_HEADS):
            o_h = outs[h]
            og = o_h * lax.rsqrt(jnp.mean(o_h * o_h, axis=-1, keepdims=True) + EPS) * norm_ref[...]
            vs = slice(h * GD_DV, (h + 1) * GD_DV)
            y_ref[k * c:(k + 1) * c, vs] = (og * _silu(z_ref[k * c:(k + 1) * c, vs])).astype(y_ref.dtype)
    for h in range(GD_HEADS):
        s_ref[0, h] = states[h]
    for sec, r in enumerate(refs):
        prev_ref[:, sec * GD_QK:(sec + 1) * GD_QK] = r[nsub * c - 8:nsub * c, :]


def _gdn(p_main, p_small, conv_w, conv0, alog_row, dtb_row, gd_norm, s0, bsz, t):
    step = CHUNK * _sub_chunks(t)
    nc = t // step
    rows = bsz * t
    rmap = lambda b, c: b * nc + c
    base = ML_MAIN // GD_QK
    assert ML_MAIN % GD_QK == 0
    full = lambda shape: pl.BlockSpec(shape, lambda b, c: (0,) * len(shape))
    return pl.pallas_call(
        _gdn_kernel,
        grid=(bsz, nc),
        in_specs=[
            pl.BlockSpec((step, GD_QK), lambda b, c: (rmap(b, c), base)),
            pl.BlockSpec((step, GD_QK), lambda b, c: (rmap(b, c), base + 1)),
            pl.BlockSpec((step, GD_WIDTH), lambda b, c: (rmap(b, c), base + 2)),
            pl.BlockSpec((step, GD_WIDTH), lambda b, c: (rmap(b, c), base + 3)),
            pl.BlockSpec((step, SMALL_W), lambda b, c: (rmap(b, c), 0)),
            full((GD_CONV, GD_QKV)),
            pl.BlockSpec((1, GD_CONV - 1, GD_QKV), lambda b, c: (b, 0, 0)),
            full((1, SMALL_W)), full((1, SMALL_W)), full((1, GD_DV)),
            pl.BlockSpec((1, GD_HEADS, GD_DK, GD_DV), lambda b, c: (b, 0, 0, 0)),
        ],
        out_specs=[
            pl.BlockSpec((step, GD_WIDTH), lambda b, c: (rmap(b, c), 0)),
            pl.BlockSpec((1, GD_HEADS, GD_DK, GD_DV), lambda b, c: (b, 0, 0, 0)),
        ],
        out_shape=[jax.ShapeDtypeStruct((rows, GD_WIDTH), BF16), jax.ShapeDtypeStruct(s0.shape, F32)],
        scratch_shapes=[pltpu.VMEM((8, GD_QKV), F32)],
        compiler_params=_cparams(("parallel", "arbitrary")),
        name="gdn",
    )(p_main, p_main, p_main, p_main, p_small, conv_w, conv0, alog_row, dtb_row, gd_norm.reshape(1, GD_DV), s0)


def _merge_kernel(y0_ref, y1_ref, y2_ref, w_ref, g0_ref, g1_ref, g2_ref, o_ref):
    acc = _sigmoid(g0_ref[...]) * jnp.dot(y0_ref[...], w_ref[0], preferred_element_type=F32)
    acc = acc + _sigmoid(g1_ref[...]) * jnp.dot(y1_ref[...], w_ref[1], preferred_element_type=F32)
    acc = acc + _sigmoid(g2_ref[...]) * jnp.dot(y2_ref[...], w_ref[2], preferred_element_type=F32)
    o_ref[...] = acc.astype(o_ref.dtype)


def _merge(ys, w_branch, p_main, d):
    m = ys[0].shape[0]
    tm = _tile(m, 1024, 8)
    tn = _tile(d, 512, 128)
    gate0 = ML_MAIN + GD_MAIN
    assert gate0 % tn == 0
    gb = gate0 // tn
    nb = d // tn
    yspec = pl.BlockSpec((tm, BR_WIDTH), lambda i, j: (i, 0))
    gspec = lambda b: pl.BlockSpec((tm, tn), lambda i, j: (i, gb + b * nb + j))
    return pl.pallas_call(
        _merge_kernel,
        grid=(m // tm, nb),
        in_specs=[yspec, yspec, yspec, pl.BlockSpec((3, BR_WIDTH, tn), lambda i, j: (0, 0, j)),
                  gspec(0), gspec(1), gspec(2)],
        out_specs=pl.BlockSpec((tm, tn), lambda i, j: (i, j)),
        out_shape=jax.ShapeDtypeStruct((m, d), BF16),
        compiler_params=_cparams(("parallel", "arbitrary")),
        name="merge",
    )(ys[0], ys[1], ys[2], w_branch, p_main, p_main, p_main)


def _attn_kernel(q_ref, k_ref, v_ref, o_ref):
    for h in range(CA_HEADS):
        hs = slice(h * CA_HEAD_DIM, (h + 1) * CA_HEAD_DIM)
        s = _dg(q_ref[:, hs], k_ref[0, :, hs], 1, 1) * (CA_HEAD_DIM ** -0.5)
        s = s - jnp.max(s, axis=-1, keepdims=True)
        e = jnp.exp(s)
        pr = e / jnp.sum(e, axis=-1, keepdims=True)
        o = jnp.dot(pr.astype(BF16), v_ref[0, :, hs], preferred_element_type=F32)
        o_ref[:, hs] = o.astype(o_ref.dtype)


def _attention(q, mem_k, mem_v, bsz, t):
    tq = _tile(t, 512, 8)
    nt = t // tq
    n_mem = mem_k.shape[1]
    return pl.pallas_call(
        _attn_kernel,
        grid=(bsz, nt),
        in_specs=[
            pl.BlockSpec((tq, CA_WIDTH), lambda b, i: (b * nt + i, 0)),
            pl.BlockSpec((1, n_mem, CA_WIDTH), lambda b, i: (b, 0, 0)),
            pl.BlockSpec((1, n_mem, CA_WIDTH), lambda b, i: (b, 0, 0)),
        ],
        out_specs=pl.BlockSpec((tq, CA_WIDTH), lambda b, i: (b * nt + i, 0)),
        out_shape=jax.ShapeDtypeStruct((bsz * t, CA_WIDTH), BF16),
        compiler_params=_cparams(("parallel", "arbitrary")),
        name="mem_attention",
    )(q, mem_k, mem_v)


def _ffn_conv(x, prev, w):
    row = lax.broadcasted_iota(jnp.int32, x.shape, 0)
    x1 = jnp.where(row == 0, prev[1:2], pltpu.roll(x, 1, 0))
    x2 = jnp.where(row == 0, prev[0:1], jnp.where(row == 1, prev[1:2], pltpu.roll(x, 2, 0)))
    return x2 * w[0:1] + x1 * w[1:2] + x * w[2:3]


def _ffn_act_kernel(a_ref, g_ref, pa_ref, pg_ref, wa_ref, wg_ref, o_ref):
    fa = _ffn_conv(a_ref[...], pa_ref[0, 0], wa_ref[...])
    fg = _ffn_conv(g_ref[...], pg_ref[0, 0], wg_ref[...])
    o_ref[...] = (_silu(fg) * fa).astype(o_ref.dtype)


def _ffn_act(up, conv0, conv_w, bsz, t):
    c2 = up.shape[1]
    f = c2 // 2
    tr = _tile(t, 512, 8)
    nt = t // tr
    tc = _tile(f, 1024, 128)
    nj = f // tc
    tails = up.reshape(bsz, nt, tr, c2)[:, :nt - 1, tr - (FFN_CONV - 1):, :]
    prev = jnp.concatenate([conv0[:, None], tails], axis=1)
    return pl.pallas_call(
        _ffn_act_kernel,
        grid=(bsz, nt, nj),
        in_specs=[
            pl.BlockSpec((tr, tc), lambda b, i, j: (b * nt + i, j)),
            pl.BlockSpec((tr, tc), lambda b, i, j: (b * nt + i, nj + j)),
            pl.BlockSpec((1, 1, FFN_CONV - 1, tc), lambda b, i, j: (b, i, 0, j)),
            pl.BlockSpec((1, 1, FFN_CONV - 1, tc), lambda b, i, j: (b, i, 0, nj + j)),
            pl.BlockSpec((FFN_CONV, tc), lambda b, i, j: (0, j)),
            pl.BlockSpec((FFN_CONV, tc), lambda b, i, j: (0, nj + j)),
        ],
        out_specs=pl.BlockSpec((tr, tc), lambda b, i, j: (b * nt + i, j)),
        out_shape=jax.ShapeDtypeStruct((bsz * t, f), BF16),
        compiler_params=_cparams(("parallel", "parallel", "arbitrary")),
        name="ffn_conv_act",
    )(up, up, prev, prev, conv_w, conv_w)


def _ffn_up_kernel(u_ref, ss_ref, wa_ref, wg_ref, cwa_ref, cwg_ref, c0a_ref, c0g_ref, act_ref, ta_ref, tg_ref,
                   carry_ref, *, tiles_per_seq, sub):
    i = pl.program_id(0)
    j = pl.program_id(1)
    tm, tn = act_ref.shape

    @pl.when(i == 0)
    def _():
        carry_ref[j] = jnp.zeros(carry_ref.shape[1:], F32)

    first = (i % tiles_per_seq) == 0
    pad = jnp.zeros((8 - (FFN_CONV - 1), tn), F32)
    carried = carry_ref[j]
    prev_a = jnp.where(first, jnp.concatenate([pad, c0a_ref[0]], axis=0), carried[0:8])
    prev_g = jnp.where(first, jnp.concatenate([pad, c0g_ref[0]], axis=0), carried[8:16])
    rstd = _row_rstd(ss_ref, u_ref.shape[1])

    def up(r):
        ur = u_ref[r * sub:(r + 1) * sub, :]
        sc = rstd[r * sub:(r + 1) * sub]
        return (jnp.dot(ur, wa_ref[...], preferred_element_type=F32) * sc,
                jnp.dot(ur, wg_ref[...], preferred_element_type=F32) * sc)

    nxt = up(0)
    for r in range(tm // sub):
        rows = slice(r * sub, (r + 1) * sub)
        za, zg = nxt
        if r + 1 < tm // sub:
            nxt = up(r + 1)
        fa =(_shift_rows(za, prev_a, 2) * cwa_ref[0:1] + _shift_rows(za, prev_a, 1) * cwa_ref[1:2]
              + za * cwa_ref[2:3])
        fg = (_shift_rows(zg, prev_g, 2) * cwg_ref[0:1] + _shift_rows(zg, prev_g, 1) * cwg_ref[1:2]
              + zg * cwg_ref[2:3])
        act_ref[rows, :] = (_silu(fg) * fa).astype(act_ref.dtype)
        prev_a = za[sub - 8:sub]
        prev_g = zg[sub - 8:sub]
    carry_ref[j] = jnp.concatenate([prev_a, prev_g], axis=0)
    ta_ref[0] = prev_a[8 - (FFN_CONV - 1):8]
    tg_ref[0] = prev_g[8 - (FFN_CONV - 1):8]


def _ffn_up_act(u, row_ss, w_up, conv0, conv_w, bsz, t):
    m, d = u.shape
    f = w_up.shape[1] // 2
    tm = _tile(t, FFN_ROW_TILE, 8)
    sub = _tile(tm, FFN_SUB_ROWS, 8)
    tn = _tile(f, 512, 128)
    nj = f // tn
    tps = t // tm
    act, ta, tg = pl.pallas_call(
        functools.partial(_ffn_up_kernel, tiles_per_seq=tps, sub=sub),
        grid=(m // tm, nj),
        in_specs=[
            pl.BlockSpec((tm, d), lambda i, j: (i, 0)),
            pl.BlockSpec((tm, 128), lambda i, j: (i, 0)),
            pl.BlockSpec((d, tn), lambda i, j: (0, j)),
            pl.BlockSpec((d, tn), lambda i, j: (0, nj + j)),
            pl.BlockSpec((FFN_CONV, tn), lambda i, j: (0, j)),
            pl.BlockSpec((FFN_CONV, tn), lambda i, j: (0, nj + j)),
            pl.BlockSpec((1, FFN_CONV - 1, tn), lambda i, j: (i // tps, 0, j)),
            pl.BlockSpec((1, FFN_CONV - 1, tn), lambda i, j: (i // tps, 0, nj + j)),
        ],
        out_specs=[
            pl.BlockSpec((tm, tn), lambda i, j: (i, j)),
            pl.BlockSpec((1, FFN_CONV - 1, tn), lambda i, j: (i // tps, 0, j)),
            pl.BlockSpec((1, FFN_CONV - 1, tn), lambda i, j: (i // tps, 0, j)),
        ],
        out_shape=[jax.ShapeDtypeStruct((m, f), BF16),
                   jax.ShapeDtypeStruct((bsz, FFN_CONV - 1, f), F32),
                   jax.ShapeDtypeStruct((bsz, FFN_CONV - 1, f), F32)],
        scratch_shapes=[pltpu.VMEM((nj, 16, tn), F32)],
        compiler_params=_cparams(("arbitrary", "arbitrary")),
        name="ffn_up_conv_act",
    )(u, row_ss, w_up, w_up, conv_w, conv_w, conv0, conv0)
    return act, jnp.concatenate([ta, tg], axis=-1)


def _prep_layer(p, d):
    w_in = p["w_in"]
    o_ml = 0
    o_if = ML_MAIN
    o_rw = o_if + 2 * ML_HEADS
    o_gd = o_rw + RW_COLS
    o_ba = o_gd + GD_MAIN
    o_gate = o_ba + 2 * GD_HEADS
    w_main = jnp.concatenate(
        [w_in[:, o_ml:o_ml + ML_MAIN], w_in[:, o_gd:o_gd + GD_MAIN], w_in[:, o_gate:]], axis=1).astype(BF16)
    w_rw = w_in[:, o_rw:o_rw + RW_COLS].astype(BF16)
    n_small = 2 * ML_HEADS + 2 * GD_HEADS
    w_small = jnp.concatenate(
        [w_in[:, o_if:o_if + 2 * ML_HEADS], w_in[:, o_ba:o_ba + 2 * GD_HEADS],
         jnp.zeros((d, SMALL_W - n_small), F32)], axis=1).astype(BF16)
    zrow = jnp.zeros((SMALL_W,), F32)
    q = dict(p)
    q.update(
        w_main=w_main, w_rw=w_rw, w_small=w_small,
        ml_bias_row=zrow.at[SM_ML_I:SM_ML_I + 2 * ML_HEADS].set(p["ml_b_if"]).reshape(1, SMALL_W),
        gd_alog_row=zrow.at[SM_GD_A:SM_GD_A + GD_HEADS].set(p["gd_a_log"]).reshape(1, SMALL_W),
        gd_dtb_row=zrow.at[SM_GD_A:SM_GD_A + GD_HEADS].set(p["gd_dt_bias"]).reshape(1, SMALL_W),
        w_branch_b=p["w_branch"].astype(BF16), w_out_b=p["w_out"].astype(BF16),
        w_ca_q_b=p["w_ca_q"].astype(BF16), w_ca_kv_b=p["w_ca_kv"].astype(BF16),
        w_ca_o_b=p["w_ca_o"].astype(BF16), w_up_b=p["w_up"].astype(BF16), w_down_b=p["w_down"].astype(BF16))
    return q


def _layer(h, mixed, mem_k, mem_v, st, p, next_gain, bsz, t):
    d = h.shape[1]
    if mixed is None:
        u, ss = _rmsnorm(h, p["g_mix"], BF16), None
    else:
        u, ss = mixed
    p_main = _matmul(u, p["w_main"], row_ss=ss)
    p_rw = _matmul(u, p["w_rw"], row_ss=ss)
    p_small = _matmul(u, p["w_small"], row_ss=ss)

    y_ml, ml_c, ml_n, ml_m = _mlstm(p_main, p_small, p["ml_bias_row"], p["ml_norm"],
                                    st["ml_C"], st["ml_n"], st["ml_m"], bsz, t)
    y_rw, rw_s = _rwkv(p_rw, st["rw_shift"], st["rw_S"], p, bsz, t)
    y_gd, gd_s = _gdn(p_main, p_small, p["gd_conv_w"], st["gd_conv"], p["gd_alog_row"], p["gd_dtb_row"],
                      p["gd_norm"], st["gd_S"], bsz, t)
    merged = _merge((y_ml, y_rw, y_gd), p["w_branch_b"], p_main, d)
    h, u, ss = _matmul(merged, p["w_out_b"], residual=h, next_gain=p["g_ca"])

    q = _matmul(u, p["w_ca_q_b"], out_dtype=BF16, row_ss=ss)
    o = _attention(q, mem_k, mem_v, bsz, t)
    h, u, ss = _matmul(o, p["w_ca_o_b"], residual=h, next_gain=p["g_ffn"])

    if t >= FFN_FUSE_MIN_T:
        act, ffn_conv = _ffn_up_act(u, ss, p["w_up_b"], st["ffn_conv"], p["ffn_conv_w"], bsz, t)
    else:
        up = _matmul(u, p["w_up_b"], row_ss=ss)
        act = _ffn_act(up, st["ffn_conv"], p["ffn_conv_w"], bsz, t)
        ffn_conv = up.reshape(bsz, t, -1)[:, t - (FFN_CONV - 1):]
    if next_gain is None:
        h, mixed_next = _matmul(act, p["w_down_b"], residual=h), None
    else:
        h, u, ss = _matmul(act, p["w_down_b"], residual=h, next_gain=next_gain)
        mixed_next = (u, ss)

    gd0 = ML_MAIN
    new_st = dict(
        ml_C=ml_c, ml_n=ml_n, ml_m=ml_m.reshape(bsz, ML_HEADS), rw_S=rw_s,
        rw_shift=p_rw.reshape(bsz, t, RW_COLS)[:, t - 1],
        gd_S=gd_s,
        gd_conv=p_main.reshape(bsz, t, -1)[:, t - (GD_CONV - 1):, gd0:gd0 + GD_QKV],
        ffn_conv=ffn_conv)
    return h, mixed_next, new_st


def _zero_state(bsz, d_ff2):
    return dict(
        ml_C=jnp.zeros((bsz, ML_HEADS, ML_DQK, ML_DV), F32), ml_n=jnp.zeros((bsz, ML_HEADS, ML_DQK), F32),
        ml_m=jnp.zeros((bsz, ML_HEADS), F32), rw_S=jnp.zeros((bsz, RW_HEADS, RW_N, RW_N), F32),
        rw_shift=jnp.zeros((bsz, RW_COLS), F32), gd_S=jnp.zeros((bsz, GD_HEADS, GD_DK, GD_DV), F32),
        gd_conv=jnp.zeros((bsz, GD_CONV - 1, GD_QKV), F32), ffn_conv=jnp.zeros((bsz, FFN_CONV - 1, d_ff2), F32))


def kernel(x_prompt, x_sample, cache_mem_k, cache_mem_v, state_mlstm_C, state_mlstm_n, state_mlstm_m, state_rwkv_S, state_rwkv_shift, state_gdn_S, state_gdn_conv, state_ffn_conv, mem_prompt, g_mix, w_in, ml_b_if, ml_norm, rw_mu, rw_w0, rw_w2, rw_a0, rw_a2, rw_g2, rw_k_k, rw_k_a, rw_r_k, rw_ln, gd_conv_w, gd_a_log, gd_dt_bias, gd_norm, w_branch, w_out, g_ca, g_mem, w_ca_q, w_ca_kv, w_ca_o, g_ffn, w_up, ffn_conv_w, w_down, g_final):
    bp, tp, d = x_prompt.shape
    bs, ts, _ = x_sample.shape
    depth = w_in.shape[0]
    n_mem = mem_prompt.shape[1]
    assert tp % CHUNK == 0 and ts % CHUNK == 0
    stacked = dict(g_mix=g_mix, w_in=w_in, ml_b_if=ml_b_if, ml_norm=ml_norm, rw_mu=rw_mu, rw_w0=rw_w0,
                   rw_w2=rw_w2, rw_a0=rw_a0, rw_a2=rw_a2, rw_g2=rw_g2, rw_k_k=rw_k_k, rw_k_a=rw_k_a,
                   rw_r_k=rw_r_k, rw_ln=rw_ln, gd_conv_w=gd_conv_w, gd_a_log=gd_a_log, gd_dt_bias=gd_dt_bias,
                   gd_norm=gd_norm, w_branch=w_branch, w_out=w_out, g_ca=g_ca, g_mem=g_mem, w_ca_q=w_ca_q,
                   w_ca_kv=w_ca_kv, w_ca_o=w_ca_o, g_ffn=g_ffn, w_up=w_up, ffn_conv_w=ffn_conv_w,
                   w_down=w_down)
    keys = ("ml_C", "ml_n", "ml_m", "rw_S", "rw_shift", "gd_S", "gd_conv", "ffn_conv")
    new_p = {k: [] for k in keys}
    new_s = {k: [] for k in keys}
    mem_k_list, mem_v_list = [], []
    hp = x_prompt.reshape(bp * tp, d)
    hs = x_sample.reshape(bs * ts, d)
    mem2d = mem_prompt.reshape(bp * n_mem, d)
    mixed_p = mixed_s = None
    for l in range(depth):
        p = _prep_layer({k: v[l] for k, v in stacked.items()}, d)
        next_gain = g_mix[l + 1] if l + 1 < depth else None
        kv = _matmul(_rmsnorm(mem2d, p["g_mem"], BF16), p["w_ca_kv_b"])
        mk = kv[:, :CA_WIDTH].reshape(bp, n_mem, CA_WIDTH)
        mv = kv[:, CA_WIDTH:].reshape(bp, n_mem, CA_WIDTH)
        hp, mixed_p, stp = _layer(hp, mixed_p, mk.astype(BF16), mv.astype(BF16), _zero_state(bp, w_up.shape[2]),
                                  p, next_gain, bp, tp)
        mem_k_list.append(mk.reshape(bp, n_mem, CA_HEADS, CA_HEAD_DIM))
        mem_v_list.append(mv.reshape(bp, n_mem, CA_HEADS, CA_HEAD_DIM))
        st_in = dict(ml_C=state_mlstm_C[l], ml_n=state_mlstm_n[l], ml_m=state_mlstm_m[l],
                     rw_S=state_rwkv_S[l], rw_shift=state_rwkv_shift[l], gd_S=state_gdn_S[l],
                     gd_conv=state_gdn_conv[l], ffn_conv=state_ffn_conv[l])
        ck = cache_mem_k[l].reshape(bs, n_mem, CA_WIDTH).astype(BF16)
        cv = cache_mem_v[l].reshape(bs, n_mem, CA_WIDTH).astype(BF16)
        hs, mixed_s, sts = _layer(hs, mixed_s, ck, cv, st_in, p, next_gain, bs, ts)
        for k in keys:
            new_p[k].append(stp[k])
            new_s[k].append(sts[k])
    y_prompt = _rmsnorm(hp, g_final, F32).reshape(bp, tp, d)
    y_sample = _rmsnorm(hs, g_final, F32).reshape(bs, ts, d)
    outs = [y_prompt, y_sample, jnp.stack(mem_k_list), jnp.stack(mem_v_list)]
    outs += [jnp.stack(new_p[k]) for k in keys]
    outs += [jnp.stack(new_s[k]) for k in keys]
    return tuple(outs)
```

```python
import functools

import jax
import jax.numpy as jnp
from jax import lax
from jax.experimental import pallas as pl
from jax.experimental.pallas import tpu as pltpu

F32 = jnp.float32
BF16 = jnp.bfloat16

EPS = 1e-6
CHUNK = 64
SUB_CHUNKS = 4

ML_HEADS, ML_DQK, ML_DV = 4, 128, 256
ML_QK = ML_HEADS * ML_DQK
ML_WIDTH = ML_HEADS * ML_DV
ML_MAIN = 2 * ML_QK + 2 * ML_WIDTH
ML_PASSES = 3

RW_HEADS, RW_N = 16, 64
RW_WIDTH = RW_HEADS * RW_N
RW_W_RANK, RW_A_RANK, RW_G_RANK = 64, 64, 128
RW_COLS = 3 * RW_WIDTH + RW_W_RANK + RW_A_RANK + RW_G_RANK
RW_GN_EPS = 64e-5
RW_GROUP = 4
RW_PASSES = 1

GD_HEADS, GD_DK, GD_DV = 8, 128, 128
GD_QK = GD_HEADS * GD_DK
GD_WIDTH = GD_HEADS * GD_DV
GD_QKV = 2 * GD_QK + GD_WIDTH
GD_CONV = 4
GD_MAIN = GD_QKV + GD_WIDTH
GD_GROUP = 4
GD_PASSES = 1

BR_WIDTH = 1024
CA_HEADS, CA_HEAD_DIM = 4, 256
CA_WIDTH = CA_HEADS * CA_HEAD_DIM
FFN_CONV = 3
FFN_FUSE_MIN_T = 512
FFN_ROW_TILE = 1024
FFN_SHORT_SUB_ROWS = 256
FFN_SUB_ROWS = 128

SMALL_W = 128
SM_ML_I, SM_ML_F, SM_GD_B, SM_GD_A = 0, ML_HEADS, 2 * ML_HEADS, 2 * ML_HEADS + GD_HEADS
RW_SLAB = 3584
SMALL_BLK = RW_COLS // SMALL_W
assert RW_COLS % SMALL_W == 0 and RW_SLAB >= RW_COLS + SMALL_W and RW_SLAB % 512 == 0

V7X_VMEM_LIMIT = 56 * 1024 * 1024
V7X_MXU_COLS = 256
TRI_BLOCK = 16


def _cparams(sem):
    return pltpu.CompilerParams(dimension_semantics=sem, vmem_limit_bytes=V7X_VMEM_LIMIT)


def _tile(dim, pref, quantum):
    if dim <= pref:
        return dim
    t = (pref // quantum) * quantum
    while t > quantum and dim % t:
        t -= quantum
    assert dim % t == 0, (dim, pref, quantum)
    return t


def _sub_chunks(t):
    n = SUB_CHUNKS
    while (t // CHUNK) % n:
        n -= 1
    return n


def _split2(a):
    hi = a.astype(BF16)
    lo = (a - hi.astype(F32)).astype(BF16)
    return hi, lo


def _dg(a, b, ca, cb):
    return lax.dot_general(a, b, (((ca,), (cb,)), ((), ())), preferred_element_type=F32)


def _dot3(a, b, ca=1, cb=0):
    ah, al = _split2(a)
    bh, bl = _split2(b)
    return _dg(ah, bh, ca, cb) + (_dg(al, bh, ca, cb) + _dg(ah, bl, ca, cb))


def _dotp(a, b, ca, cb, passes):
    if passes == 1:
        return _dg(a.astype(BF16), b.astype(BF16), ca, cb)
    return _dot3(a, b, ca, cb)


def _tri_masks(c):
    row = lax.broadcasted_iota(jnp.int32, (c, c), 0)
    col = lax.broadcasted_iota(jnp.int32, (c, c), 1)
    return row, col


def _cumsum_rows(x):
    c = x.shape[0]
    row, col = _tri_masks(c)
    tri = jnp.where(row >= col, 1.0, 0.0).astype(BF16)
    x0 = x.astype(BF16)
    r1 = x - x0.astype(F32)
    x1 = r1.astype(BF16)
    x2 = (r1 - x1.astype(F32)).astype(BF16)
    return _dg(tri, x0, 1, 0) + (_dg(tri, x1, 1, 0) + _dg(tri, x2, 1, 0))


def _tri_solve(low, rhs, c, passes):
    return _tri_solve_many([low], [rhs], c, passes)[0]


def _tri_solve_many(lows, rhss, c, passes):
    return _tri_apply_many(_tri_factor_many(lows, c, passes), rhss, passes)


def _tri_factor_many(lows, c, passes):
    mm = lambda a, b: _dotp(a, b, 1, 0, passes)
    row, col = _tri_masks(lows[0].shape[0])
    same = (row // TRI_BLOCK) == (col // TRI_BLOCK)
    eye = jnp.where(row == col, 1.0, 0.0).astype(F32)
    ps = [jnp.where(same, -low, 0.0) for low in lows]
    offs = [jnp.where(same, 0.0, low) for low in lows]
    xs = [eye + p for p in ps]
    steps = 1
    while steps * 2 < TRI_BLOCK:
        ps = [mm(p, p) for p in ps]
        xs = [x + mm(x, p) for x, p in zip(xs, ps)]
        steps *= 2
    ms = [mm(x, off) for x, off in zip(xs, offs)]
    nblk = c // TRI_BLOCK
    terms = []
    pws = ms
    k = 1
    while k < nblk:
        terms.append(pws)
        k *= 2
        if k < nblk:
            pws = [mm(pw, pw) for pw in pws]
    return xs, terms


def _tri_apply_many(factors, rhss, passes):
    mm = lambda a, b: _dotp(a, b, 1, 0, passes)
    xs, terms = factors
    us = xs if rhss is None else [mm(x, rhs) for x, rhs in zip(xs, rhss)]
    for i in range(len(terms) - 1, 0, -1):
        us = [u + mm(t, u) for t, u in zip(terms[i], us)]
    if terms:
        us = [u - mm(t, u) for t, u in zip(terms[0], us)]
    return us


def _head_sums(x, hw):
    assert 2 * hw == 128
    lane = lax.broadcasted_iota(jnp.int32, (x.shape[0], 128), 1)
    lo = lane < hw
    out = []
    for s in range(x.shape[1] // 128):
        xs = x[:, s * 128:(s + 1) * 128]
        s_lo = jnp.sum(jnp.where(lo, xs, 0.0), axis=-1, keepdims=True)
        s_hi = jnp.sum(jnp.where(lo, 0.0, xs), axis=-1, keepdims=True)
        out.append(jnp.where(lo, s_lo, s_hi))
    return jnp.concatenate(out, axis=1)


def _softplus(x):
    return jnp.maximum(x, 0.0) + jnp.log1p(jnp.exp(-jnp.abs(x)))


def _sigmoid(x):
    return 1.0 / (1.0 + jnp.exp(-x))


def _silu(x):
    return x * _sigmoid(x)


def _shift_rows(x, prev8, k):
    xr = pltpu.roll(x, k, 0)
    pr = pltpu.roll(prev8, k, 0)
    row = lax.broadcasted_iota(jnp.int32, (8, x.shape[1]), 0)
    head = jnp.where(row < k, pr, xr[0:8])
    if x.shape[0] == 8:
        return head
    return jnp.concatenate([head, xr[8:]], axis=0)


def _rmsnorm_kernel(x_ref, g_ref, o_ref):
    x = x_ref[...]
    y = x * lax.rsqrt(jnp.mean(x * x, axis=-1, keepdims=True) + EPS)
    o_ref[...] = (y * g_ref[...]).astype(o_ref.dtype)


def _rmsnorm(x, g, out_dtype):
    m, d = x.shape
    tr = _tile(m, 256, 8)
    return pl.pallas_call(
        _rmsnorm_kernel,
        grid=(m // tr,),
        in_specs=[pl.BlockSpec((tr, d), lambda i: (i, 0)), pl.BlockSpec((1, d), lambda i: (0, 0))],
        out_specs=pl.BlockSpec((tr, d), lambda i: (i, 0)),
        out_shape=jax.ShapeDtypeStruct((m, d), out_dtype),
        compiler_params=_cparams(("parallel",)),
        name="rmsnorm",
    )(x, g.reshape(1, d))


def _mm_kernel(a_ref, w_ref, o_ref):
    o_ref[...] = jnp.dot(a_ref[...], w_ref[...], preferred_element_type=F32).astype(o_ref.dtype)


def _mm_res_kernel(a_ref, w_ref, r_ref, o_ref):
    o_ref[...] = r_ref[...] + jnp.dot(a_ref[...], w_ref[...], preferred_element_type=F32)


def _row_rstd(ss_ref, d):
    return lax.rsqrt(jnp.sum(ss_ref[...], axis=-1, keepdims=True) * (1.0 / d) + EPS)


def _mm_scaled_kernel(a_ref, w_ref, ss_ref, o_ref, *, d):
    acc = jnp.dot(a_ref[...], w_ref[...], preferred_element_type=F32)
    o_ref[...] = (acc * _row_rstd(ss_ref, d)).astype(o_ref.dtype)


def _mm_res_norm_kernel(a_ref, w_ref, r_ref, g_ref, o_ref, hg_ref, ss_ref):
    h = r_ref[...] + jnp.dot(a_ref[...], w_ref[...], preferred_element_type=F32)
    o_ref[...] = h
    hg_ref[...] = (h * g_ref[...]).astype(hg_ref.dtype)
    h2 = h * h
    part = h2[:, 0:128]
    for s in range(1, h.shape[1] // 128):
        part = part + h2[:, s * 128:(s + 1) * 128]

    @pl.when(pl.program_id(1) == 0)
    def _():
        ss_ref[...] = part

    @pl.when(pl.program_id(1) != 0)
    def _():
        ss_ref[...] = ss_ref[...] + part


def _mm_tiles(m, k, n, has_residual):
    tm = _tile(m, 1024 if k <= 4096 else 512, 8)
    tn = _tile(n, 1024 if (k <= 4096 and not has_residual) else 512, V7X_MXU_COLS if n % V7X_MXU_COLS == 0 else 128)
    return tm, tn


def _matmul(a, w, residual=None, out_dtype=F32, row_ss=None, next_gain=None):
    m, k = a.shape
    n = w.shape[1]
    tm, tn = _mm_tiles(m, k, n, residual is not None)
    in_specs = [pl.BlockSpec((tm, k), lambda i, j: (i, 0)), pl.BlockSpec((k, tn), lambda i, j: (0, j))]
    args = [a, w]
    body = _mm_kernel
    out_specs = pl.BlockSpec((tm, tn), lambda i, j: (i, j))
    out_shape = jax.ShapeDtypeStruct((m, n), out_dtype)
    if residual is not None:
        in_specs.append(pl.BlockSpec((tm, tn), lambda i, j: (i, j)))
        args.append(residual)
        body = _mm_res_kernel
        if next_gain is not None:
            in_specs.append(pl.BlockSpec((1, tn), lambda i, j: (0, j)))
            args.append(next_gain.reshape(1, n))
            body = _mm_res_norm_kernel
            out_specs = [out_specs, pl.BlockSpec((tm, tn), lambda i, j: (i, j)),
                         pl.BlockSpec((tm, 128), lambda i, j: (i, 0))]
            out_shape = [out_shape, jax.ShapeDtypeStruct((m, n), BF16), jax.ShapeDtypeStruct((m, 128), F32)]
    elif row_ss is not None:
        in_specs.append(pl.BlockSpec((tm, 128), lambda i, j: (i, 0)))
        args.append(row_ss)
        body = functools.partial(_mm_scaled_kernel, d=k)
    return pl.pallas_call(
        body,
        grid=(m // tm, n // tn),
        in_specs=in_specs,
        out_specs=out_specs,
        out_shape=out_shape,
        compiler_params=_cparams(("parallel", "arbitrary")),
        name="matmul",
    )(*args)


def _mlstm_kernel(q_ref, k_ref, v_ref, og_ref, sm_ref, bias_ref, norm_ref, c0_ref, n0_ref, m0_ref,
                  y_ref, c_ref, n_ref, m_ref):
    @pl.when(pl.program_id(1) == 0)
    def _():
        c_ref[...] = c0_ref[...]
        n_ref[...] = n0_ref[...]
        m_ref[...] = m0_ref[...]

    c = q_ref.shape[0]
    pre = sm_ref[...] + bias_ref[...]
    logf = -_softplus(-pre)
    cum = _cumsum_rows(logf)
    pre_t = pre.T
    cum_t = cum.T
    row, col = _tri_masks(c)
    causal = row >= col
    hs = range(ML_HEADS)
    q = [q_ref[:, h * ML_DQK:(h + 1) * ML_DQK] for h in hs]
    k = [k_ref[:, h * ML_DQK:(h + 1) * ML_DQK] * (ML_DQK ** -0.5) for h in hs]
    v = [v_ref[:, h * ML_DV:(h + 1) * ML_DV] for h in hs]
    b_col = [cum[:, SM_ML_F + h:SM_ML_F + h + 1] for h in hs]
    b_row = [cum_t[SM_ML_F + h:SM_ML_F + h + 1, :] for h in hs]
    i_col = [pre[:, SM_ML_I + h:SM_ML_I + h + 1] for h in hs]
    i_row = [pre_t[SM_ML_I + h:SM_ML_I + h + 1, :] for h in hs]
    m_old = [m_ref[0, :, h:h + 1] for h in hs]
    c_old = [c_ref[0, h] for h in hs]
    n_old = [n_ref[0, h:h + 1, :] for h in hs]

    qk = [_dotp(q[h], k[h], 1, 1, ML_PASSES) for h in hs]
    qc = [_dotp(q[h], c_old[h], 1, 0, ML_PASSES) for h in hs]
    dmat = [jnp.where(causal, b_col[h] - b_row[h] + i_row[h], -jnp.inf) for h in hs]
    mt = [jnp.maximum(b_col[h] + m_old[h], jnp.max(dmat[h], axis=-1, keepdims=True)) for h in hs]
    pmat = [jnp.exp(dmat[h] - mt[h]) * qk[h] for h in hs]
    inter = [jnp.exp(b_col[h] + m_old[h] - mt[h]) for h in hs]
    num = [inter[h] * qc[h] + _dotp(pmat[h], v[h], 1, 0, ML_PASSES) for h in hs]
    den = [inter[h] * jnp.sum(q[h] * n_old[h], axis=-1, keepdims=True)
           + jnp.sum(pmat[h], axis=-1, keepdims=True) for h in hs]
    hh = [num[h] / jnp.maximum(jnp.abs(den[h]), jnp.exp(-mt[h])) for h in hs]

    m_new = [mt[h][c - 1:c, :] for h in hs]
    b_last = [b_col[h][c - 1:c, :] for h in hs]
    kw = [k[h] * jnp.exp(b_last[h] - b_col[h] + i_col[h] - m_new[h]) for h in hs]
    dec = [jnp.exp(b_last[h] + m_old[h] - m_new[h]) for h in hs]
    c_new = [dec[h] * c_old[h] + _dotp(kw[h].T, v[h], 1, 0, ML_PASSES) for h in hs]
    for h in hs:
        c_ref[0, h] = c_new[h]
        n_ref[0, h:h + 1, :] = dec[h] * n_old[h] + jnp.sum(kw[h], axis=0, keepdims=True)
        m_ref[0, :, h:h + 1] = m_new[h]
        hn = hh[h] * lax.rsqrt(jnp.mean(hh[h] * hh[h], axis=-1, keepdims=True) + EPS)
        og = og_ref[:, h * ML_DV:(h + 1) * ML_DV]
        y = hn * norm_ref[:, h * ML_DV:(h + 1) * ML_DV] * _sigmoid(og)
        y_ref[:, h * ML_DV:(h + 1) * ML_DV] = y.astype(y_ref.dtype)


def _mlstm(p_main, p_small, bias_row, ml_norm, c0, n0, m0, bsz, t):
    nc = t // CHUNK
    rows = bsz * t
    rmap = lambda b, c: b * nc + c
    return pl.pallas_call(
        _mlstm_kernel,
        grid=(bsz, nc),
        in_specs=[
            pl.BlockSpec((CHUNK, ML_QK), lambda b, c: (rmap(b, c), 0)),
            pl.BlockSpec((CHUNK, ML_QK), lambda b, c: (rmap(b, c), 1)),
            pl.BlockSpec((CHUNK, ML_WIDTH), lambda b, c: (rmap(b, c), 1)),
            pl.BlockSpec((CHUNK, ML_WIDTH), lambda b, c: (rmap(b, c), 2)),
            pl.BlockSpec((CHUNK, SMALL_W), lambda b, c: (rmap(b, c), SMALL_BLK)),
            pl.BlockSpec((1, SMALL_W), lambda b, c: (0, 0)),
            pl.BlockSpec((1, ML_WIDTH), lambda b, c: (0, 0)),
            pl.BlockSpec((1, ML_HEADS, ML_DQK, ML_DV), lambda b, c: (b, 0, 0, 0)),
            pl.BlockSpec((1, ML_HEADS, ML_DQK), lambda b, c: (b, 0, 0)),
            pl.BlockSpec((1, 1, ML_HEADS), lambda b, c: (b, 0, 0)),
        ],
        out_specs=[
            pl.BlockSpec((CHUNK, ML_WIDTH), lambda b, c: (rmap(b, c), 0)),
            pl.BlockSpec((1, ML_HEADS, ML_DQK, ML_DV), lambda b, c: (b, 0, 0, 0)),
            pl.BlockSpec((1, ML_HEADS, ML_DQK), lambda b, c: (b, 0, 0)),
            pl.BlockSpec((1, 1, ML_HEADS), lambda b, c: (b, 0, 0)),
        ],
        out_shape=[
            jax.ShapeDtypeStruct((rows, ML_WIDTH), BF16),
            jax.ShapeDtypeStruct(c0.shape, F32),
            jax.ShapeDtypeStruct(n0.shape, F32),
            jax.ShapeDtypeStruct((bsz, 1, ML_HEADS), F32),
        ],
        compiler_params=_cparams(("parallel", "arbitrary")),
        name="mlstm",
    )(p_main, p_main, p_main, p_main, p_small, bias_row, ml_norm.reshape(1, ML_WIDTH),
      c0, n0, m0.reshape(bsz, 1, ML_HEADS))


def _rwkv_front(x, prev8, mu_ref, w0_ref, w2_ref, a0_ref, a2_ref, g2_ref, kk_ref, ka_ref, rk_ref):
    c = x.shape[0]
    w = RW_WIDTH
    xprev = _shift_rows(x, prev8, 1)
    xm = x + (xprev - x) * mu_ref[...]
    rr = xm[:, 0:w]
    rk = xm[:, w:2 * w]
    rv = xm[:, 2 * w:3 * w]
    xw = xm[:, 3 * w:3 * w + RW_W_RANK]
    xa = xm[:, 3 * w + RW_W_RANK:3 * w + RW_W_RANK + RW_A_RANK]
    xg = xm[:, 3 * w + RW_W_RANK + RW_A_RANK:]

    w_pre = w0_ref[...] + _dot3(jnp.tanh(xw), w2_ref[...])
    lw = -jnp.exp(-_softplus(-w_pre) - 0.5)
    a = _sigmoid(a0_ref[...] + _dot3(xa, a2_ref[...]))
    g_out = _dot3(_sigmoid(xg), g2_ref[...])
    kk_raw = rk * kk_ref[...]
    kmod = rk * (1.0 + (a - 1.0) * ka_ref[...])
    bon = rr * kmod * rk_ref[...]

    lc = _cumsum_rows(lw)
    l_end = lc[c - 1:c, :]
    l_mid = lc[c // 2 - 1:c // 2, :]
    lcc = lc - l_mid
    p_mid = jnp.exp(l_mid)
    p_in = jnp.exp(lcc)
    p_prev = jnp.exp(lcc - lw)
    p_inv = jnp.exp(-lcc)
    p_end = jnp.exp(l_end - lc)
    p_all = jnp.exp(l_end)

    kkn = kk_raw * lax.rsqrt(_head_sums(kk_raw * kk_raw, RW_N) + 1e-6)
    bb = kkn * a
    kkp = kkn * p_prev
    rp = rr * p_in
    kd = kmod * p_inv
    bd = bb * p_inv
    k_end = kmod * p_end
    b_end = bb * p_end
    return dict(kkp=kkp, rp=rp, kd=kd, bd=bd, k_end=k_end, b_end=b_end, rv=rv, p_mid=p_mid, p_all=p_all,
                bonus=_head_sums(bon, RW_N) * rv, g_out=g_out)


def _rw_groups():
    gw = RW_GROUP * RW_N
    return [slice(g * gw, (g + 1) * gw) for g in range(RW_HEADS // RW_GROUP)]


def _rw_masks(c):
    shape = (RW_GROUP * c, RW_GROUP * RW_N)
    head_eq = lax.broadcasted_iota(jnp.int32, shape, 0) // c == lax.broadcasted_iota(jnp.int32, shape, 1) // RW_N
    tn_ = lax.broadcasted_iota(jnp.int32, (c, RW_GROUP * RW_N), 0)
    sn_ = lax.broadcasted_iota(jnp.int32, (c, RW_GROUP * RW_N), 1) % RW_N
    return head_eq, tn_ > sn_, tn_ >= sn_


def _rw_blockdiag(z, head_eq):
    return jnp.where(head_eq, jnp.concatenate([z] * RW_GROUP, axis=0), 0.0)


def _rw_rowsum(z, c):
    out = z[0:c]
    for i in range(1, RW_GROUP):
        out = out + z[i * c:(i + 1) * c]
    return out


def _rwkv_mid(f, masks):
    kkp, rp, kd, bd, rv = f["kkp"], f["rp"], f["kd"], f["bd"], f["rv"]
    c = rv.shape[0]
    head_eq, strict, incl = masks
    groups = _rw_groups()
    nt = lambda x, y: _dotp(x, y, 1, 1, RW_PASSES)
    x2 = [jnp.concatenate([kkp[:, cs], rp[:, cs]], axis=0) for cs in groups]
    ab_b = [nt(x, _rw_blockdiag(bd[:, cs], head_eq)) for x, cs in zip(x2, groups)]
    ab_k = [nt(x, _rw_blockdiag(kd[:, cs], head_eq)) for x, cs in zip(x2, groups)]
    a_b = [_rw_blockdiag(jnp.where(strict, z[0:c], 0.0), head_eq) for z in ab_b]
    return dict(
        x2=x2, v_bd=[_rw_blockdiag(rv[:, cs], head_eq) for cs in groups],
        t_inv=_tri_apply_many(_tri_factor_many(a_b, c, RW_PASSES), None, RW_PASSES),
        a_k=[jnp.where(strict, z[0:c], 0.0) for z in ab_k],
        r_k=[jnp.where(incl, z[c:], 0.0) for z in ab_k],
        r_b=[jnp.where(incl, z[c:], 0.0) for z in ab_b])


def _rwkv_chain(f, g, s_olds, ln_ref, masks):
    k_end, b_end, rv, p_mid, p_all = f["k_end"], f["b_end"], f["rv"], f["p_mid"], f["p_all"]
    c = rv.shape[0]
    groups = _rw_groups()
    head_eq = masks[0]
    nt = lambda x, y: _dotp(x, y, 1, 1, RW_PASSES)
    nn = lambda x, y: _dotp(x, y, 1, 0, RW_PASSES)
    v_bd = g["v_bd"]
    ab_s = [nt(x, _rw_blockdiag(s * p_mid[:, cs], head_eq))
            for x, s, cs in zip(g["x2"], s_olds, groups)]
    rhs = [_rw_blockdiag(z[0:c] + nn(ak, v), head_eq) for z, ak, v in zip(ab_s, g["a_k"], v_bd)]
    us = [nn(t_, r_) for t_, r_ in zip(g["t_inv"], rhs)]
    ygs = [z[c:] + nn(rk_, v) - nn(rb_, u) for z, rk_, v, rb_, u in zip(ab_s, g["r_k"], v_bd, g["r_b"], us)]
    upds = [nn(jnp.concatenate([rv[:, cs], -_rw_rowsum(u, c)], axis=0).T,
               jnp.concatenate([k_end[:, cs], b_end[:, cs]], axis=0)) for cs, u in zip(groups, us)]
    s_news = [s * p_all[:, cs] + _rw_rowsum(jnp.where(head_eq, upd, 0.0), c)
              for cs, s, upd in zip(groups, s_olds, upds)]

    yh = jnp.concatenate(ygs, axis=1)
    mu_ = _head_sums(yh, RW_N) * (1.0 / RW_N)
    yc = yh - mu_
    var = _head_sums(yc * yc, RW_N) * (1.0 / RW_N)
    yn = yc * lax.rsqrt(var + RW_GN_EPS)
    return (yn * ln_ref[...] + f["bonus"]) * f["g_out"], s_news


def _rwkv_kernel(p_ref, shift_ref, mu_ref, w0_ref, w2_ref, a0_ref, a2_ref, g2_ref, kk_ref, ka_ref,
                 rk_ref, ln_ref, s0_ref, y_ref, s_ref, prev_ref):
    @pl.when(pl.program_id(1) == 0)
    def _():
        s_ref[...] = s0_ref[...]
        prev_ref[...] = jnp.broadcast_to(shift_ref[0], prev_ref.shape)

    c = CHUNK
    nsub = p_ref.shape[0] // c
    masks = _rw_masks(c)
    fronts = []
    for k in range(nsub):
        last8 = prev_ref[...] if k == 0 else p_ref[k * c - 8:k * c, :]
        fronts.append(_rwkv_front(p_ref[k * c:(k + 1) * c, :], last8, mu_ref, w0_ref, w2_ref, a0_ref, a2_ref,
                                  g2_ref, kk_ref, ka_ref, rk_ref))
    groups = _rw_groups()
    states = [s_ref[0, :, cs] for cs in groups]
    for k in range(nsub):
        out, states = _rwkv_chain(fronts[k], _rwkv_mid(fronts[k], masks), states, ln_ref, masks)
        y_ref[k * c:(k + 1) * c, :] = out.astype(y_ref.dtype)
    prev_ref[...] = p_ref[nsub * c - 8:nsub * c, :]
    for cs, s in zip(groups, states):
        s_ref[0, :, cs] = s


def _rwkv(p_rw, shift0, s0, p, bsz, t):
    step = CHUNK * _sub_chunks(t)
    nc = t // step
    rows = bsz * t
    rmap = lambda b, c: b * nc + c
    full = lambda shape: pl.BlockSpec(shape, lambda b, c: (0,) * len(shape))
    w = RW_WIDTH
    s_nat = s0.transpose(0, 2, 1, 3).reshape(bsz, RW_N, w)
    y, s = pl.pallas_call(
        _rwkv_kernel,
        grid=(bsz, nc),
        in_specs=[
            pl.BlockSpec((step, RW_COLS), lambda b, c: (rmap(b, c), 0)),
            pl.BlockSpec((1, 1, RW_COLS), lambda b, c: (b, 0, 0)),
            full((1, RW_COLS)), full((1, w)), full((RW_W_RANK, w)), full((1, w)), full((RW_A_RANK, w)),
            full((RW_G_RANK, w)), full((1, w)), full((1, w)), full((1, w)), full((1, w)),
            pl.BlockSpec((1, RW_N, w), lambda b, c: (b, 0, 0)),
        ],
        out_specs=[
            pl.BlockSpec((step, w), lambda b, c: (rmap(b, c), 0)),
            pl.BlockSpec((1, RW_N, w), lambda b, c: (b, 0, 0)),
        ],
        out_shape=[jax.ShapeDtypeStruct((rows, w), BF16), jax.ShapeDtypeStruct(s_nat.shape, F32)],
        scratch_shapes=[pltpu.VMEM((8, RW_COLS), F32)],
        compiler_params=_cparams(("parallel", "arbitrary")),
        name="rwkv7",
    )(p_rw, shift0.reshape(bsz, 1, RW_COLS), p["rw_mu"].reshape(1, RW_COLS), p["rw_w0"].reshape(1, w),
      p["rw_w2"], p["rw_a0"].reshape(1, w), p["rw_a2"], p["rw_g2"], p["rw_k_k"].reshape(1, w),
      p["rw_k_a"].reshape(1, w), p["rw_r_k"].reshape(1, w), p["rw_ln"].reshape(1, w), s_nat)
    return y, s.reshape(bsz, RW_N, RW_HEADS, RW_N).transpose(0, 2, 1, 3)


def _gdn_front(xs, prevs, sm, cw_ref, alog_ref, dtb_ref):
    c = xs[0].shape[0]
    acts = []
    for sec, (x, prev8) in enumerate(zip(xs, prevs)):
        cs = slice(sec * GD_QK, (sec + 1) * GD_QK)
        y = _shift_rows(x, prev8, GD_CONV - 1) * cw_ref[0:1, cs]
        for i in range(1, GD_CONV - 1):
            y = y + _shift_rows(x, prev8, GD_CONV - 1 - i) * cw_ref[i:i + 1, cs]
        y = y + x * cw_ref[GD_CONV - 1:GD_CONV, cs]
        acts.append(_silu(y))
    qa, ka, va = acts

    beta = _sigmoid(sm)
    g = -jnp.exp(alog_ref[...]) * _softplus(sm + dtb_ref[...])
    gc = _cumsum_rows(g)
    gc_t = gc.T

    gr = GD_GROUP * c
    gk = GD_GROUP * GD_DK
    r4 = lax.broadcasted_iota(jnp.int32, (gr, gr), 0)
    c4 = lax.broadcasted_iota(jnp.int32, (gr, gr), 1)
    t_minus_s = jnp.where((r4 // c) == (c4 // c), (r4 % c) - (c4 % c), -1)
    m_strict = t_minus_s > 0
    m_incl = t_minus_s >= 0
    head_eq = (lax.broadcasted_iota(jnp.int32, (gr, gk), 0) // c
               == lax.broadcasted_iota(jnp.int32, (gr, gk), 1) // GD_DK)

    def rep(z):
        return jnp.concatenate([z] * GD_GROUP, axis=0)

    def stack(parts):
        return jnp.concatenate(parts, axis=0)

    kn = []
    qn = []
    for h in range(GD_HEADS):
        ks = slice(h * GD_DK, (h + 1) * GD_DK)
        q = qa[:, ks]
        qn.append(q * lax.rsqrt(jnp.sum(q * q, axis=-1, keepdims=True) + 1e-6) * (GD_DK ** -0.5))
        k = ka[:, ks]
        kn.append(k * lax.rsqrt(jnp.sum(k * k, axis=-1, keepdims=True) + 1e-6))
    groups = [range(g0, g0 + GD_GROUP) for g0 in range(0, GD_HEADS, GD_GROUP)]
    k_nat = [jnp.concatenate([kn[h] for h in hs], axis=1) for hs in groups]
    q_nat = [jnp.concatenate([qn[h] for h in hs], axis=1) for hs in groups]
    k_bd = [jnp.where(head_eq, rep(kk_), 0.0) for kk_ in k_nat]
    v_st = [stack([va[:, h * GD_DV:(h + 1) * GD_DV] for h in hs]) for hs in groups]
    b_col = [stack([beta[:, SM_GD_B + h:SM_GD_B + h + 1] for h in hs]) for hs in groups]
    g_col = [stack([gc[:, SM_GD_A + h:SM_GD_A + h + 1] for h in hs]) for hs in groups]
    g_row = [jnp.concatenate([gc_t[SM_GD_A + h:SM_GD_A + h + 1, :] for h in hs], axis=1) for hs in groups]

    dec = [jnp.exp(jnp.where(m_incl, gc_ - gr_, -jnp.inf)) for gc_, gr_ in zip(g_col, g_row)]
    kq = [_dotp(stack([kk_, qq_]), kb, 1, 1, GD_PASSES) for kk_, qq_, kb in zip(k_nat, q_nat, k_bd)]
    low = [jnp.where(m_strict, b * rep(z[0:c]) * d, 0.0) for b, z, d in zip(b_col, kq, dec)]
    gl = [gc[c - 1:c, SM_GD_A + h:SM_GD_A + h + 1] for h in range(GD_HEADS)]
    return dict(
        kq_lhs=[stack([kn[h], qn[h]]) for h in range(GD_HEADS)], v_st=v_st, b_col=b_col,
        eg=[jnp.exp(gc_) for gc_ in g_col],
        t_inv=_tri_apply_many(_tri_factor_many(low, c, GD_PASSES), None, GD_PASSES),
        qkt=[rep(z[c:]) * d for z, d in zip(kq, dec)],
        kw_t=[(kn[h] * jnp.exp(gl[h] - gc[:, SM_GD_A + h:SM_GD_A + h + 1])).T for h in range(GD_HEADS)],
        s_decay=[jnp.exp(gl[h]) for h in range(GD_HEADS)])


def _gdn_chain(f, s_old):
    c = f["kq_lhs"][0].shape[0] // 2
    stack = lambda parts: jnp.concatenate(parts, axis=0)
    groups = [range(g0, g0 + GD_GROUP) for g0 in range(0, GD_HEADS, GD_GROUP)]
    kqs = [_dotp(f["kq_lhs"][h], s_old[h], 1, 0, GD_PASSES) for h in range(GD_HEADS)]
    ks = [stack([kqs[h][0:c] for h in hs]) for hs in groups]
    qs = [stack([kqs[h][c:] for h in hs]) for hs in groups]
    rhs = [b * (v - e * z) for b, v, e, z in zip(f["b_col"], f["v_st"], f["eg"], ks)]
    us = [_dotp(t_, r_, 1, 0, GD_PASSES) for t_, r_ in zip(f["t_inv"], rhs)]
    os_ = [e * z + _dotp(qk, u, 1, 0, GD_PASSES) for e, z, qk, u in zip(f["eg"], qs, f["qkt"], us)]
    outs = []
    s_new = []
    for gi, hs in enumerate(groups):
        for i, h in enumerate(hs):
            rs = slice(i * c, (i + 1) * c)
            s_new.append(f["s_decay"][h] * s_old[h] + _dotp(f["kw_t"][h], us[gi][rs], 1, 0, GD_PASSES))
            outs.append(os_[gi][rs])
    return outs, s_new


def _gdn_kernel(q_ref, k_ref, v_ref, z_ref, sm_ref, cw_ref, conv0_ref, alog_ref, dtb_ref, norm_ref, s0_ref,
                y_ref, s_ref, prev_ref):
    @pl.when(pl.program_id(1) == 0)
    def _():
        s_ref[...] = s0_ref[...]
        prev_ref[...] = jnp.zeros(prev_ref.shape, F32)
        prev_ref[8 - (GD_CONV - 1):8, :] = conv0_ref[0]

    c = CHUNK
    nsub = q_ref.shape[0] // c
    refs = (q_ref, k_ref, v_ref)
    fronts = []
    for k in range(nsub):
        xs = [r[k * c:(k + 1) * c, :] for r in refs]
        if k == 0:
            prevs = [prev_ref[:, sec * GD_QK:(sec + 1) * GD_QK] for sec in range(3)]
        else:
            prevs = [r[k * c - 8:k * c, :] for r in refs]
        fronts.append(_gdn_front(xs, prevs, sm_ref[k * c:(k + 1) * c, :], cw_ref, alog_ref, dtb_ref))
    states = [s_ref[0, h] for h in range(GD_HEADS)]
    for k in range(nsub):
        outs, states = _gdn_chain(fronts[k], states)
        for h in range(GD_HEADS):
            o_h = outs[h]
            og = o_h * lax.rsqrt(jnp.mean(o_h * o_h, axis=-1, keepdims=True) + EPS) * norm_ref[...]
            vs = slice(h * GD_DV, (h + 1) * GD_DV)
            y_ref[k * c:(k + 1) * c, vs] = (og * _silu(z_ref[k * c:(k + 1) * c, vs])).astype(y_ref.dtype)
    for h in range(GD_HEADS):
        s_ref[0, h] = states[h]
    for sec, r in enumerate(refs):
        prev_ref[:, sec * GD_QK:(sec + 1) * GD_QK] = r[nsub * c - 8:nsub * c, :]


def _gdn(p_main, p_small, conv_w, conv0, alog_row, dtb_row, gd_norm, s0, bsz, t):
    step = CHUNK * _sub_chunks(t)
    nc = t // step
    rows = bsz * t
    rmap = lambda b, c: b * nc + c
    base = ML_MAIN // GD_QK
    assert ML_MAIN % GD_QK == 0
    full = lambda shape: pl.BlockSpec(shape, lambda b, c: (0,) * len(shape))
    return pl.pallas_call(
        _gdn_kernel,
        grid=(bsz, nc),
        in_specs=[
            pl.BlockSpec((step, GD_QK), lambda b, c: (rmap(b, c), base)),
            pl.BlockSpec((step, GD_QK), lambda b, c: (rmap(b, c), base + 1)),
            pl.BlockSpec((step, GD_WIDTH), lambda b, c: (rmap(b, c), base + 2)),
            pl.BlockSpec((step, GD_WIDTH), lambda b, c: (rmap(b, c), base + 3)),
            pl.BlockSpec((step, SMALL_W), lambda b, c: (rmap(b, c), SMALL_BLK)),
            full((GD_CONV, GD_QKV)),
            pl.BlockSpec((1, GD_CONV - 1, GD_QKV), lambda b, c: (b, 0, 0)),
            full((1, SMALL_W)), full((1, SMALL_W)), full((1, GD_DV)),
            pl.BlockSpec((1, GD_HEADS, GD_DK, GD_DV), lambda b, c: (b, 0, 0, 0)),
        ],
        out_specs=[
            pl.BlockSpec((step, GD_WIDTH), lambda b, c: (rmap(b, c), 0)),
            pl.BlockSpec((1, GD_HEADS, GD_DK, GD_DV), lambda b, c: (b, 0, 0, 0)),
        ],
        out_shape=[jax.ShapeDtypeStruct((rows, GD_WIDTH), BF16), jax.ShapeDtypeStruct(s0.shape, F32)],
        scratch_shapes=[pltpu.VMEM((8, GD_QKV), F32)],
        compiler_params=_cparams(("parallel", "arbitrary")),
        name="gdn",
    )(p_main, p_main, p_main, p_main, p_small, conv_w, conv0, alog_row, dtb_row, gd_norm.reshape(1, GD_DV), s0)


def _merge_kernel(y0_ref, y1_ref, y2_ref, w_ref, g0_ref, g1_ref, g2_ref, o_ref):
    acc = _sigmoid(g0_ref[...]) * jnp.dot(y0_ref[...], w_ref[0], preferred_element_type=F32)
    acc = acc + _sigmoid(g1_ref[...]) * jnp.dot(y1_ref[...], w_ref[1], preferred_element_type=F32)
    acc = acc + _sigmoid(g2_ref[...]) * jnp.dot(y2_ref[...], w_ref[2], preferred_element_type=F32)
    o_ref[...] = acc.astype(o_ref.dtype)


def _merge(ys, w_branch, p_main, d):
    m = ys[0].shape[0]
    tm = _tile(m, 1024, 8)
    tn = _tile(d, 512, 128)
    gate0 = ML_MAIN + GD_MAIN
    assert gate0 % tn == 0
    gb = gate0 // tn
    nb = d // tn
    yspec = pl.BlockSpec((tm, BR_WIDTH), lambda i, j: (i, 0))
    gspec = lambda b: pl.BlockSpec((tm, tn), lambda i, j: (i, gb + b * nb + j))
    return pl.pallas_call(
        _merge_kernel,
        grid=(m // tm, nb),
        in_specs=[yspec, yspec, yspec, pl.BlockSpec((3, BR_WIDTH, tn), lambda i, j: (0, 0, j)),
                  gspec(0), gspec(1), gspec(2)],
        out_specs=pl.BlockSpec((tm, tn), lambda i, j: (i, j)),
        out_shape=jax.ShapeDtypeStruct((m, d), BF16),
        compiler_params=_cparams(("parallel", "arbitrary")),
        name="merge",
    )(ys[0], ys[1], ys[2], w_branch, p_main, p_main, p_main)


def _attn_kernel(q_ref, k_ref, v_ref, o_ref):
    for h in range(CA_HEADS):
        hs = slice(h * CA_HEAD_DIM, (h + 1) * CA_HEAD_DIM)
        s = _dg(q_ref[:, hs], k_ref[0, :, hs], 1, 1) * (CA_HEAD_DIM ** -0.5)
        s = s - jnp.max(s, axis=-1, keepdims=True)
        e = jnp.exp(s)
        pr = e / jnp.sum(e, axis=-1, keepdims=True)
        o = jnp.dot(pr.astype(BF16), v_ref[0, :, hs], preferred_element_type=F32)
        o_ref[:, hs] = o.astype(o_ref.dtype)


def _attention(q, mem_k, mem_v, bsz, t):
    tq = _tile(t, 512, 8)
    nt = t // tq
    n_mem = mem_k.shape[1]
    return pl.pallas_call(
        _attn_kernel,
        grid=(bsz, nt),
        in_specs=[
            pl.BlockSpec((tq, CA_WIDTH), lambda b, i: (b * nt + i, 0)),
            pl.BlockSpec((1, n_mem, CA_WIDTH), lambda b, i: (b, 0, 0)),
            pl.BlockSpec((1, n_mem, CA_WIDTH), lambda b, i: (b, 0, 0)),
        ],
        out_specs=pl.BlockSpec((tq, CA_WIDTH), lambda b, i: (b * nt + i, 0)),
        out_shape=jax.ShapeDtypeStruct((bsz * t, CA_WIDTH), BF16),
        compiler_params=_cparams(("parallel", "arbitrary")),
        name="mem_attention",
    )(q, mem_k, mem_v)


def _ffn_up_kernel(u_ref, ss_ref, wa_ref, wg_ref, cwa_ref, cwg_ref, c0a_ref, c0g_ref, act_ref, ta_ref, tg_ref,
                   carry_ref, *, tiles_per_seq, sub):
    i = pl.program_id(0)
    j = pl.program_id(1)
    tm, tn = act_ref.shape

    @pl.when(i == 0)
    def _():
        carry_ref[j] = jnp.zeros(carry_ref.shape[1:], F32)

    first = (i % tiles_per_seq) == 0
    pad = jnp.zeros((8 - (FFN_CONV - 1), tn), F32)
    carried = carry_ref[j]
    prev_a = jnp.where(first, jnp.concatenate([pad, c0a_ref[0]], axis=0), carried[0:8])
    prev_g = jnp.where(first, jnp.concatenate([pad, c0g_ref[0]], axis=0), carried[8:16])
    rstd = _row_rstd(ss_ref, u_ref.shape[1])

    def up(r):
        ur = u_ref[r * sub:(r + 1) * sub, :]
        sc = rstd[r * sub:(r + 1) * sub]
        return (jnp.dot(ur, wa_ref[...], preferred_element_type=F32) * sc,
                jnp.dot(ur, wg_ref[...], preferred_element_type=F32) * sc)

    nxt = up(0)
    for r in range(tm // sub):
        rows = slice(r * sub, (r + 1) * sub)
        za, zg = nxt
        if r + 1 < tm // sub:
            nxt = up(r + 1)
        fa =(_shift_rows(za, prev_a, 2) * cwa_ref[0:1] + _shift_rows(za, prev_a, 1) * cwa_ref[1:2]
              + za * cwa_ref[2:3])
        fg = (_shift_rows(zg, prev_g, 2) * cwg_ref[0:1] + _shift_rows(zg, prev_g, 1) * cwg_ref[1:2]
              + zg * cwg_ref[2:3])
        act_ref[rows, :] = (_silu(fg) * fa).astype(act_ref.dtype)
        prev_a = za[sub - 8:sub]
        prev_g = zg[sub - 8:sub]
    carry_ref[j] = jnp.concatenate([prev_a, prev_g], axis=0)
    ta_ref[0] = prev_a[8 - (FFN_CONV - 1):8]
    tg_ref[0] = prev_g[8 - (FFN_CONV - 1):8]


def _ffn_up_act(u, row_ss, w_up, conv0, conv_w, bsz, t):
    m, d = u.shape
    f = w_up.shape[1] // 2
    tm = _tile(t, FFN_ROW_TILE, 8)
    sub = _tile(tm, FFN_SUB_ROWS, 8)
    tn = _tile(f, 512, 128)
    nj = f // tn
    tps = t // tm
    act, ta, tg = pl.pallas_call(
        functools.partial(_ffn_up_kernel, tiles_per_seq=tps, sub=sub),
        grid=(m // tm, nj),
        in_specs=[
            pl.BlockSpec((tm, d), lambda i, j: (i, 0)),
            pl.BlockSpec((tm, 128), lambda i, j: (i, 0)),
            pl.BlockSpec((d, tn), lambda i, j: (0, j)),
            pl.BlockSpec((d, tn), lambda i, j: (0, nj + j)),
            pl.BlockSpec((FFN_CONV, tn), lambda i, j: (0, j)),
            pl.BlockSpec((FFN_CONV, tn), lambda i, j: (0, nj + j)),
            pl.BlockSpec((1, FFN_CONV - 1, tn), lambda i, j: (i // tps, 0, j)),
            pl.BlockSpec((1, FFN_CONV - 1, tn), lambda i, j: (i // tps, 0, nj + j)),
        ],
        out_specs=[
            pl.BlockSpec((tm, tn), lambda i, j: (i, j)),
            pl.BlockSpec((1, FFN_CONV - 1, tn), lambda i, j: (i // tps, 0, j)),
            pl.BlockSpec((1, FFN_CONV - 1, tn), lambda i, j: (i // tps, 0, j)),
        ],
        out_shape=[jax.ShapeDtypeStruct((m, f), BF16),
                   jax.ShapeDtypeStruct((bsz, FFN_CONV - 1, f), F32),
                   jax.ShapeDtypeStruct((bsz, FFN_CONV - 1, f), F32)],
        scratch_shapes=[pltpu.VMEM((nj, 16, tn), F32)],
        compiler_params=_cparams(("arbitrary", "arbitrary")),
        name="ffn_up_conv_act",
    )(u, row_ss, w_up, w_up, conv_w, conv_w, conv0, conv0)
    return act, jnp.concatenate([ta, tg], axis=-1)


def _ffn_up_short_kernel(u_ref, ss_ref, wa_ref, wg_ref, cwa_ref, cwg_ref, c0a_ref, c0g_ref, act_ref, ta_ref, tg_ref,
                         *, t, sub):
    tm, tn = act_ref.shape
    pad = jnp.zeros((8 - (FFN_CONV - 1), tn), F32)
    rstd = _row_rstd(ss_ref, u_ref.shape[1])

    def up(r):
        ur = u_ref[r * sub:(r + 1) * sub, :]
        sc = rstd[r * sub:(r + 1) * sub]
        return (jnp.dot(ur, wa_ref[...], preferred_element_type=F32) * sc,
                jnp.dot(ur, wg_ref[...], preferred_element_type=F32) * sc)

    nxt = up(0)
    for r in range(tm // sub):
        za_all, zg_all = nxt
        if r + 1 < tm // sub:
            nxt = up(r + 1)
        for q in range(sub // t):
            s = r * (sub // t) + q
            za = za_all[q * t:(q + 1) * t]
            zg = zg_all[q * t:(q + 1) * t]
            prev_a = jnp.concatenate([pad, c0a_ref[s]], axis=0)
            prev_g = jnp.concatenate([pad, c0g_ref[s]], axis=0)
            fa = (_shift_rows(za, prev_a, 2) * cwa_ref[0:1] + _shift_rows(za, prev_a, 1) * cwa_ref[1:2]
                  + za * cwa_ref[2:3])
            fg = (_shift_rows(zg, prev_g, 2) * cwg_ref[0:1] + _shift_rows(zg, prev_g, 1) * cwg_ref[1:2]
                  + zg * cwg_ref[2:3])
            act_ref[s * t:(s + 1) * t, :] = (_silu(fg) * fa).astype(act_ref.dtype)
            ta_ref[s] = za[t - (FFN_CONV - 1):t]
            tg_ref[s] = zg[t - (FFN_CONV - 1):t]


def _ffn_up_act_short(u, row_ss, w_up, conv0, conv_w, bsz, t):
    m, d = u.shape
    f = w_up.shape[1] // 2
    spt = _tile(bsz, max(1, FFN_ROW_TILE // t), 1)
    tm = spt * t
    sub = t * _tile(spt, max(1, FFN_SHORT_SUB_ROWS // t), 1)
    tn = _tile(f, 512, 128)
    nj = f // tn
    act, ta, tg = pl.pallas_call(
        functools.partial(_ffn_up_short_kernel, t=t, sub=sub),
        grid=(m // tm, nj),
        in_specs=[
            pl.BlockSpec((tm, d), lambda i, j: (i, 0)),
            pl.BlockSpec((tm, 128), lambda i, j: (i, 0)),
            pl.BlockSpec((d, tn), lambda i, j: (0, j)),
            pl.BlockSpec((d, tn), lambda i, j: (0, nj + j)),
            pl.BlockSpec((FFN_CONV, tn), lambda i, j: (0, j)),
            pl.BlockSpec((FFN_CONV, tn), lambda i, j: (0, nj + j)),
            pl.BlockSpec((spt, FFN_CONV - 1, tn), lambda i, j: (i, 0, j)),
            pl.BlockSpec((spt, FFN_CONV - 1, tn), lambda i, j: (i, 0, nj + j)),
        ],
        out_specs=[
            pl.BlockSpec((tm, tn), lambda i, j: (i, j)),
            pl.BlockSpec((spt, FFN_CONV - 1, tn), lambda i, j: (i, 0, j)),
            pl.BlockSpec((spt, FFN_CONV - 1, tn), lambda i, j: (i, 0, j)),
        ],
        out_shape=[jax.ShapeDtypeStruct((m, f), BF16),
                   jax.ShapeDtypeStruct((bsz, FFN_CONV - 1, f), F32),
                   jax.ShapeDtypeStruct((bsz, FFN_CONV - 1, f), F32)],
        compiler_params=_cparams(("parallel", "arbitrary")),
        name="ffn_up_conv_act_short",
    )(u, row_ss, w_up, w_up, conv_w, conv_w, conv0, conv0)
    return act, jnp.concatenate([ta, tg], axis=-1)


def _prep_layer(p, d):
    w_in = p["w_in"]
    o_ml = 0
    o_if = ML_MAIN
    o_rw = o_if + 2 * ML_HEADS
    o_gd = o_rw + RW_COLS
    o_ba = o_gd + GD_MAIN
    o_gate = o_ba + 2 * GD_HEADS
    w_main = jnp.concatenate(
        [w_in[:, o_ml:o_ml + ML_MAIN], w_in[:, o_gd:o_gd + GD_MAIN], w_in[:, o_gate:]], axis=1).astype(BF16)
    n_small = 2 * ML_HEADS + 2 * GD_HEADS
    w_rw = jnp.concatenate(
        [w_in[:, o_rw:o_rw + RW_COLS], w_in[:, o_if:o_if + 2 * ML_HEADS], w_in[:, o_ba:o_ba + 2 * GD_HEADS],
         jnp.zeros((d, RW_SLAB - RW_COLS - n_small), F32)], axis=1).astype(BF16)
    zrow = jnp.zeros((SMALL_W,), F32)
    q = dict(p)
    q.update(
        w_main=w_main, w_rw=w_rw,
        ml_bias_row=zrow.at[SM_ML_I:SM_ML_I + 2 * ML_HEADS].set(p["ml_b_if"]).reshape(1, SMALL_W),
        gd_alog_row=zrow.at[SM_GD_A:SM_GD_A + GD_HEADS].set(p["gd_a_log"]).reshape(1, SMALL_W),
        gd_dtb_row=zrow.at[SM_GD_A:SM_GD_A + GD_HEADS].set(p["gd_dt_bias"]).reshape(1, SMALL_W),
        w_branch_b=p["w_branch"].astype(BF16), w_out_b=p["w_out"].astype(BF16),
        w_ca_q_b=p["w_ca_q"].astype(BF16), w_ca_kv_b=p["w_ca_kv"].astype(BF16),
        w_ca_o_b=p["w_ca_o"].astype(BF16), w_up_b=p["w_up"].astype(BF16), w_down_b=p["w_down"].astype(BF16))
    return q


def _layer(h, mixed, mem_k, mem_v, st, p, next_gain, bsz, t):
    d = h.shape[1]
    if mixed is None:
        u, ss = _rmsnorm(h, p["g_mix"], BF16), None
    else:
        u, ss = mixed
    p_main = _matmul(u, p["w_main"], row_ss=ss)
    p_rw = _matmul(u, p["w_rw"], row_ss=ss)
    p_small = p_rw

    y_ml, ml_c, ml_n, ml_m = _mlstm(p_main, p_small, p["ml_bias_row"], p["ml_norm"],
                                    st["ml_C"], st["ml_n"], st["ml_m"], bsz, t)
    y_rw, rw_s = _rwkv(p_rw, st["rw_shift"], st["rw_S"], p, bsz, t)
    y_gd, gd_s = _gdn(p_main, p_small, p["gd_conv_w"], st["gd_conv"], p["gd_alog_row"], p["gd_dtb_row"],
                      p["gd_norm"], st["gd_S"], bsz, t)
    merged = _merge((y_ml, y_rw, y_gd), p["w_branch_b"], p_main, d)
    h, u, ss = _matmul(merged, p["w_out_b"], residual=h, next_gain=p["g_ca"])

    q = _matmul(u, p["w_ca_q_b"], out_dtype=BF16, row_ss=ss)
    o = _attention(q, mem_k, mem_v, bsz, t)
    h, u, ss = _matmul(o, p["w_ca_o_b"], residual=h, next_gain=p["g_ffn"])

    if t >= FFN_FUSE_MIN_T:
        act, ffn_conv = _ffn_up_act(u, ss, p["w_up_b"], st["ffn_conv"], p["ffn_conv_w"], bsz, t)
    else:
        act, ffn_conv = _ffn_up_act_short(u, ss, p["w_up_b"], st["ffn_conv"], p["ffn_conv_w"], bsz, t)
    if next_gain is None:
        h, mixed_next = _matmul(act, p["w_down_b"], residual=h), None
    else:
        h, u, ss = _matmul(act, p["w_down_b"], residual=h, next_gain=next_gain)
        mixed_next = (u, ss)

    gd0 = ML_MAIN
    new_st = dict(
        ml_C=ml_c, ml_n=ml_n, ml_m=ml_m.reshape(bsz, ML_HEADS), rw_S=rw_s,
        rw_shift=p_rw.reshape(bsz, t, RW_SLAB)[:, t - 1, :RW_COLS],
        gd_S=gd_s,
        gd_conv=p_main.reshape(bsz, t, -1)[:, t - (GD_CONV - 1):, gd0:gd0 + GD_QKV],
        ffn_conv=ffn_conv)
    return h, mixed_next, new_st


def _zero_state(bsz, d_ff2):
    return dict(
        ml_C=jnp.zeros((bsz, ML_HEADS, ML_DQK, ML_DV), F32), ml_n=jnp.zeros((bsz, ML_HEADS, ML_DQK), F32),
        ml_m=jnp.zeros((bsz, ML_HEADS), F32), rw_S=jnp.zeros((bsz, RW_HEADS, RW_N, RW_N), F32),
        rw_shift=jnp.zeros((bsz, RW_COLS), F32), gd_S=jnp.zeros((bsz, GD_HEADS, GD_DK, GD_DV), F32),
        gd_conv=jnp.zeros((bsz, GD_CONV - 1, GD_QKV), F32), ffn_conv=jnp.zeros((bsz, FFN_CONV - 1, d_ff2), F32))


def kernel(x_prompt, x_sample, cache_mem_k, cache_mem_v, state_mlstm_C, state_mlstm_n, state_mlstm_m, state_rwkv_S, state_rwkv_shift, state_gdn_S, state_gdn_conv, state_ffn_conv, mem_prompt, g_mix, w_in, ml_b_if, ml_norm, rw_mu, rw_w0, rw_w2, rw_a0, rw_a2, rw_g2, rw_k_k, rw_k_a, rw_r_k, rw_ln, gd_conv_w, gd_a_log, gd_dt_bias, gd_norm, w_branch, w_out, g_ca, g_mem, w_ca_q, w_ca_kv, w_ca_o, g_ffn, w_up, ffn_conv_w, w_down, g_final):
    bp, tp, d = x_prompt.shape
    bs, ts, _ = x_sample.shape
    depth = w_in.shape[0]
    n_mem = mem_prompt.shape[1]
    assert tp % CHUNK == 0 and ts % CHUNK == 0
    stacked = dict(g_mix=g_mix, w_in=w_in, ml_b_if=ml_b_if, ml_norm=ml_norm, rw_mu=rw_mu, rw_w0=rw_w0,
                   rw_w2=rw_w2, rw_a0=rw_a0, rw_a2=rw_a2, rw_g2=rw_g2, rw_k_k=rw_k_k, rw_k_a=rw_k_a,
                   rw_r_k=rw_r_k, rw_ln=rw_ln, gd_conv_w=gd_conv_w, gd_a_log=gd_a_log, gd_dt_bias=gd_dt_bias,
                   gd_norm=gd_norm, w_branch=w_branch, w_out=w_out, g_ca=g_ca, g_mem=g_mem, w_ca_q=w_ca_q,
                   w_ca_kv=w_ca_kv, w_ca_o=w_ca_o, g_ffn=g_ffn, w_up=w_up, ffn_conv_w=ffn_conv_w,
                   w_down=w_down)
    keys = ("ml_C", "ml_n", "ml_m", "rw_S", "rw_shift", "gd_S", "gd_conv", "ffn_conv")
    new_p = {k: [] for k in keys}
    new_s = {k: [] for k in keys}
    mem_k_list, mem_v_list = [], []
    hp = x_prompt.reshape(bp * tp, d)
    hs = x_sample.reshape(bs * ts, d)
    mem2d = mem_prompt.reshape(bp * n_mem, d)
    mixed_p = mixed_s = None
    for l in range(depth):
        p = _prep_layer({k: v[l] for k, v in stacked.items()}, d)
        next_gain = g_mix[l + 1] if l + 1 < depth else None
        kv = _matmul(_rmsnorm(mem2d, p["g_mem"], BF16), p["w_ca_kv_b"])
        mk = kv[:, :CA_WIDTH].reshape(bp, n_mem, CA_WIDTH)
        mv = kv[:, CA_WIDTH:].reshape(bp, n_mem, CA_WIDTH)
        hp, mixed_p, stp = _layer(hp, mixed_p, mk.astype(BF16), mv.astype(BF16), _zero_state(bp, w_up.shape[2]),
                                  p, next_gain, bp, tp)
        mem_k_list.append(mk.reshape(bp, n_mem, CA_HEADS, CA_HEAD_DIM))
        mem_v_list.append(mv.reshape(bp, n_mem, CA_HEADS, CA_HEAD_DIM))
        st_in = dict(ml_C=state_mlstm_C[l], ml_n=state_mlstm_n[l], ml_m=state_mlstm_m[l],
                     rw_S=state_rwkv_S[l], rw_shift=state_rwkv_shift[l], gd_S=state_gdn_S[l],
                     gd_conv=state_gdn_conv[l], ffn_conv=state_ffn_conv[l])
        ck = cache_mem_k[l].reshape(bs, n_mem, CA_WIDTH).astype(BF16)
        cv = cache_mem_v[l].reshape(bs, n_mem, CA_WIDTH).astype(BF16)
        hs, mixed_s, sts = _layer(hs, mixed_s, ck, cv, st_in, p, next_gain, bs, ts)
        for k in keys:
            new_p[k].append(stp[k])
            new_s[k].append(sts[k])
    y_prompt = _rmsnorm(hp, g_final, F32).reshape(bp, tp, d)
    y_sample = _rmsnorm(hs, g_final, F32).reshape(bs, ts, d)
    outs = [y_prompt, y_sample, jnp.stack(mem_k_list), jnp.stack(mem_v_list)]
    outs += [jnp.stack(new_p[k]) for k in keys]
    outs += [jnp.stack(new_s[k]) for k in keys]
    return tuple(outs)
```

```python
import functools

import jax
import jax.numpy as jnp
from jax import lax
from jax.experimental import pallas as pl
from jax.experimental.pallas import tpu as pltpu

F32 = jnp.float32
BF16 = jnp.bfloat16

EPS = 1e-6
CHUNK = 64
SUB_CHUNKS = 4

ML_HEADS, ML_DQK, ML_DV = 4, 128, 256
ML_QK = ML_HEADS * ML_DQK
ML_WIDTH = ML_HEADS * ML_DV
ML_MAIN = 2 * ML_QK + 2 * ML_WIDTH
ML_PASSES = 3

RW_HEADS, RW_N = 16, 64
RW_WIDTH = RW_HEADS * RW_N
RW_W_RANK, RW_A_RANK, RW_G_RANK = 64, 64, 128
RW_COLS = 3 * RW_WIDTH + RW_W_RANK + RW_A_RANK + RW_G_RANK
RW_GN_EPS = 64e-5
RW_GROUP = 4
RW_PASSES = 1

GD_HEADS, GD_DK, GD_DV = 8, 128, 128
GD_QK = GD_HEADS * GD_DK
GD_WIDTH = GD_HEADS * GD_DV
GD_QKV = 2 * GD_QK + GD_WIDTH
GD_CONV = 4
GD_MAIN = GD_QKV + GD_WIDTH
GD_GROUP = 4
GD_PASSES = 1

BR_WIDTH = 1024
CA_HEADS, CA_HEAD_DIM = 4, 256
CA_WIDTH = CA_HEADS * CA_HEAD_DIM
FFN_CONV = 3
FFN_FUSE_MIN_T = 512
FFN_ROW_TILE = 1024
FFN_SHORT_SUB_ROWS = 256
FFN_SUB_ROWS = 128

SMALL_W = 128
SM_ML_I, SM_ML_F, SM_GD_B, SM_GD_A = 0, ML_HEADS, 2 * ML_HEADS, 2 * ML_HEADS + GD_HEADS
RW_SLAB = 4096
SMALL_BLK = RW_COLS // SMALL_W
OFF_ML = RW_SLAB
OFF_GD = OFF_ML + ML_MAIN
OFF_GATE = OFF_GD + GD_MAIN
assert RW_COLS % SMALL_W == 0 and RW_SLAB >= RW_COLS + SMALL_W
assert OFF_ML % ML_WIDTH == 0 and OFF_GD % GD_QK == 0

V7X_VMEM_LIMIT = 56 * 1024 * 1024
V7X_MXU_COLS = 256
TRI_BLOCK = 16


def _cparams(sem):
    return pltpu.CompilerParams(dimension_semantics=sem, vmem_limit_bytes=V7X_VMEM_LIMIT)


def _tile(dim, pref, quantum):
    if dim <= pref:
        return dim
    t = (pref // quantum) * quantum
    while t > quantum and dim % t:
        t -= quantum
    assert dim % t == 0, (dim, pref, quantum)
    return t


def _sub_chunks(t):
    n = SUB_CHUNKS
    while (t // CHUNK) % n:
        n -= 1
    return n


def _split2(a):
    hi = a.astype(BF16)
    lo = (a - hi.astype(F32)).astype(BF16)
    return hi, lo


def _dg(a, b, ca, cb):
    return lax.dot_general(a, b, (((ca,), (cb,)), ((), ())), preferred_element_type=F32)


def _dot3(a, b, ca=1, cb=0):
    ah, al = _split2(a)
    bh, bl = _split2(b)
    return _dg(ah, bh, ca, cb) + (_dg(al, bh, ca, cb) + _dg(ah, bl, ca, cb))


def _dotp(a, b, ca, cb, passes):
    if passes == 1:
        return _dg(a.astype(BF16), b.astype(BF16), ca, cb)
    return _dot3(a, b, ca, cb)


def _tri_masks(c):
    row = lax.broadcasted_iota(jnp.int32, (c, c), 0)
    col = lax.broadcasted_iota(jnp.int32, (c, c), 1)
    return row, col


def _cumsum_rows(x):
    c = x.shape[0]
    row, col = _tri_masks(c)
    tri = jnp.where(row >= col, 1.0, 0.0).astype(BF16)
    x0 = x.astype(BF16)
    r1 = x - x0.astype(F32)
    x1 = r1.astype(BF16)
    x2 = (r1 - x1.astype(F32)).astype(BF16)
    return _dg(tri, x0, 1, 0) + (_dg(tri, x1, 1, 0) + _dg(tri, x2, 1, 0))


def _tri_solve(low, rhs, c, passes):
    return _tri_solve_many([low], [rhs], c, passes)[0]


def _tri_solve_many(lows, rhss, c, passes):
    return _tri_apply_many(_tri_factor_many(lows, c, passes), rhss, passes)


def _tri_factor_many(lows, c, passes):
    mm = lambda a, b: _dotp(a, b, 1, 0, passes)
    row, col = _tri_masks(lows[0].shape[0])
    same = (row // TRI_BLOCK) == (col // TRI_BLOCK)
    eye = jnp.where(row == col, 1.0, 0.0).astype(F32)
    ps = [jnp.where(same, -low, 0.0) for low in lows]
    offs = [jnp.where(same, 0.0, low) for low in lows]
    xs = [eye + p for p in ps]
    steps = 1
    while steps * 2 < TRI_BLOCK:
        ps = [mm(p, p) for p in ps]
        xs = [x + mm(x, p) for x, p in zip(xs, ps)]
        steps *= 2
    ms = [mm(x, off) for x, off in zip(xs, offs)]
    nblk = c // TRI_BLOCK
    terms = []
    pws = ms
    k = 1
    while k < nblk:
        terms.append(pws)
        k *= 2
        if k < nblk:
            pws = [mm(pw, pw) for pw in pws]
    return xs, terms


def _tri_apply_many(factors, rhss, passes):
    mm = lambda a, b: _dotp(a, b, 1, 0, passes)
    xs, terms = factors
    us = xs if rhss is None else [mm(x, rhs) for x, rhs in zip(xs, rhss)]
    for i in range(len(terms) - 1, 0, -1):
        us = [u + mm(t, u) for t, u in zip(terms[i], us)]
    if terms:
        us = [u - mm(t, u) for t, u in zip(terms[0], us)]
    return us


def _head_sums(x, hw):
    assert 2 * hw == 128
    lane = lax.broadcasted_iota(jnp.int32, (x.shape[0], 128), 1)
    lo = lane < hw
    out = []
    for s in range(x.shape[1] // 128):
        xs = x[:, s * 128:(s + 1) * 128]
        s_lo = jnp.sum(jnp.where(lo, xs, 0.0), axis=-1, keepdims=True)
        s_hi = jnp.sum(jnp.where(lo, 0.0, xs), axis=-1, keepdims=True)
        out.append(jnp.where(lo, s_lo, s_hi))
    return jnp.concatenate(out, axis=1)


def _softplus(x):
    return jnp.maximum(x, 0.0) + jnp.log1p(jnp.exp(-jnp.abs(x)))


def _sigmoid(x):
    return 1.0 / (1.0 + jnp.exp(-x))


def _silu(x):
    return x * _sigmoid(x)


def _shift_rows(x, prev8, k):
    xr = pltpu.roll(x, k, 0)
    pr = pltpu.roll(prev8, k, 0)
    row = lax.broadcasted_iota(jnp.int32, (8, x.shape[1]), 0)
    head = jnp.where(row < k, pr, xr[0:8])
    if x.shape[0] == 8:
        return head
    return jnp.concatenate([head, xr[8:]], axis=0)


def _rmsnorm_kernel(x_ref, g_ref, o_ref):
    x = x_ref[...]
    y = x * lax.rsqrt(jnp.mean(x * x, axis=-1, keepdims=True) + EPS)
    o_ref[...] = (y * g_ref[...]).astype(o_ref.dtype)


def _rmsnorm(x, g, out_dtype):
    m, d = x.shape
    tr = _tile(m, 256, 8)
    return pl.pallas_call(
        _rmsnorm_kernel,
        grid=(m // tr,),
        in_specs=[pl.BlockSpec((tr, d), lambda i: (i, 0)), pl.BlockSpec((1, d), lambda i: (0, 0))],
        out_specs=pl.BlockSpec((tr, d), lambda i: (i, 0)),
        out_shape=jax.ShapeDtypeStruct((m, d), out_dtype),
        compiler_params=_cparams(("parallel",)),
        name="rmsnorm",
    )(x, g.reshape(1, d))


def _mm_kernel(a_ref, w_ref, o_ref):
    o_ref[...] = jnp.dot(a_ref[...], w_ref[...], preferred_element_type=F32).astype(o_ref.dtype)


def _mm_res_kernel(a_ref, w_ref, r_ref, o_ref):
    o_ref[...] = r_ref[...] + jnp.dot(a_ref[...], w_ref[...], preferred_element_type=F32)


def _row_rstd(ss_ref, d):
    return lax.rsqrt(jnp.sum(ss_ref[...], axis=-1, keepdims=True) * (1.0 / d) + EPS)


def _mm_scaled_kernel(a_ref, w_ref, ss_ref, o_ref, *, d):
    acc = jnp.dot(a_ref[...], w_ref[...], preferred_element_type=F32)
    o_ref[...] = (acc * _row_rstd(ss_ref, d)).astype(o_ref.dtype)


def _mm_res_norm_kernel(a_ref, w_ref, r_ref, g_ref, o_ref, hg_ref, ss_ref):
    h = r_ref[...] + jnp.dot(a_ref[...], w_ref[...], preferred_element_type=F32)
    o_ref[...] = h
    hg_ref[...] = (h * g_ref[...]).astype(hg_ref.dtype)
    h2 = h * h
    part = h2[:, 0:128]
    for s in range(1, h.shape[1] // 128):
        part = part + h2[:, s * 128:(s + 1) * 128]

    @pl.when(pl.program_id(1) == 0)
    def _():
        ss_ref[...] = part

    @pl.when(pl.program_id(1) != 0)
    def _():
        ss_ref[...] = ss_ref[...] + part


def _mm_tiles(m, k, n, has_residual):
    tm = _tile(m, 1024 if k <= 4096 else 512, 8)
    tn = _tile(n, 1024 if (k <= 4096 and not has_residual) else 512, V7X_MXU_COLS if n % V7X_MXU_COLS == 0 else 128)
    return tm, tn


def _matmul(a, w, residual=None, out_dtype=F32, row_ss=None, next_gain=None):
    m, k = a.shape
    n = w.shape[1]
    tm, tn = _mm_tiles(m, k, n, residual is not None)
    in_specs = [pl.BlockSpec((tm, k), lambda i, j: (i, 0)), pl.BlockSpec((k, tn), lambda i, j: (0, j))]
    args = [a, w]
    body = _mm_kernel
    out_specs = pl.BlockSpec((tm, tn), lambda i, j: (i, j))
    out_shape = jax.ShapeDtypeStruct((m, n), out_dtype)
    if residual is not None:
        in_specs.append(pl.BlockSpec((tm, tn), lambda i, j: (i, j)))
        args.append(residual)
        body = _mm_res_kernel
        if next_gain is not None:
            in_specs.append(pl.BlockSpec((1, tn), lambda i, j: (0, j)))
            args.append(next_gain.reshape(1, n))
            body = _mm_res_norm_kernel
            out_specs = [out_specs, pl.BlockSpec((tm, tn), lambda i, j: (i, j)),
                         pl.BlockSpec((tm, 128), lambda i, j: (i, 0))]
            out_shape = [out_shape, jax.ShapeDtypeStruct((m, n), BF16), jax.ShapeDtypeStruct((m, 128), F32)]
    elif row_ss is not None:
        in_specs.append(pl.BlockSpec((tm, 128), lambda i, j: (i, 0)))
        args.append(row_ss)
        body = functools.partial(_mm_scaled_kernel, d=k)
    return pl.pallas_call(
        body,
        grid=(m // tm, n // tn),
        in_specs=in_specs,
        out_specs=out_specs,
        out_shape=out_shape,
        compiler_params=_cparams(("parallel", "arbitrary")),
        name="matmul",
    )(*args)


def _mlstm_front(q_all, k_all, v_all, pre):
    n = len(q_all)
    c = q_all[0].shape[0]
    cum = [_cumsum_rows(-_softplus(-p_)) for p_ in pre]
    pre_t = [p_.T for p_ in pre]
    cum_t = [z.T for z in cum]
    row, col = _tri_masks(c)
    causal = row >= col
    it = [(j, h) for j in range(n) for h in range(ML_HEADS)]
    q = [q_all[j][:, h * ML_DQK:(h + 1) * ML_DQK] for j, h in it]
    k = [k_all[j][:, h * ML_DQK:(h + 1) * ML_DQK] * (ML_DQK ** -0.5) for j, h in it]
    v = [v_all[j][:, h * ML_DV:(h + 1) * ML_DV] for j, h in it]
    b_col = [cum[j][:, SM_ML_F + h:SM_ML_F + h + 1] for j, h in it]
    b_row = [cum_t[j][SM_ML_F + h:SM_ML_F + h + 1, :] for j, h in it]
    i_col = [pre[j][:, SM_ML_I + h:SM_ML_I + h + 1] for j, h in it]
    i_row = [pre_t[j][SM_ML_I + h:SM_ML_I + h + 1, :] for j, h in it]
    ix = range(len(it))
    qk = [_dotp(q[x], k[x], 1, 1, ML_PASSES) for x in ix]
    dmat = [jnp.where(causal, b_col[x] - b_row[x] + i_row[x], -jnp.inf) for x in ix]
    m_loc = [jnp.max(dmat[x], axis=-1, keepdims=True) for x in ix]
    pmat = [jnp.exp(dmat[x] - m_loc[x]) * qk[x] for x in ix]
    m_end = [m_loc[x][c - 1:c, :] for x in ix]
    kw = [k[x] * jnp.exp(b_col[x][c - 1:c, :] - b_col[x] + i_col[x] - m_end[x]) for x in ix]
    pv = [_dotp(pmat[x], v[x], 1, 0, ML_PASSES) for x in ix]
    kv = [_dotp(kw[x].T, v[x], 1, 0, ML_PASSES) for x in ix]
    ps = [jnp.sum(pmat[x], axis=-1, keepdims=True) for x in ix]
    ks = [jnp.sum(kw[x], axis=0, keepdims=True) for x in ix]
    per_chunk = lambda z: [z[j * ML_HEADS:(j + 1) * ML_HEADS] for j in range(n)]
    names = ("q", "b_col", "m_loc", "m_end", "pv", "ps", "kv", "ks")
    cols = [per_chunk(z) for z in (q, b_col, m_loc, m_end, pv, ps, kv, ks)]
    return [dict(zip(names, [col_[j] for col_ in cols])) for j in range(n)]


def _mlstm_kernel(q_ref, k_ref, v_ref, og_ref, sm_ref, bias_ref, norm_ref, c0_ref, n0_ref, m0_ref,
                  y_ref, c_ref, n_ref, m_ref):
    @pl.when(pl.program_id(1) == 0)
    def _():
        c_ref[...] = c0_ref[...]
        n_ref[...] = n0_ref[...]
        m_ref[...] = m0_ref[...]

    c = CHUNK
    nsub = q_ref.shape[0] // c
    hs = range(ML_HEADS)
    chunk_rows = [slice(k * c, (k + 1) * c) for k in range(nsub)]
    fronts = _mlstm_front([q_ref[r, :] for r in chunk_rows], [k_ref[r, :] for r in chunk_rows],
                          [v_ref[r, :] for r in chunk_rows], [sm_ref[r, :] + bias_ref[...] for r in chunk_rows])
    m_old = [m_ref[0, :, h:h + 1] for h in hs]
    c_old = [c_ref[0, h] for h in hs]
    n_old = [n_ref[0, h:h + 1, :] for h in hs]
    for k in range(nsub):
        f = fronts[k]
        rows = slice(k * c, (k + 1) * c)
        qc = [_dotp(f["q"][h], c_old[h], 1, 0, ML_PASSES) for h in hs]
        mt = [jnp.maximum(f["b_col"][h] + m_old[h], f["m_loc"][h]) for h in hs]
        e_loc = [jnp.exp(f["m_loc"][h] - mt[h]) for h in hs]
        inter = [jnp.exp(f["b_col"][h] + m_old[h] - mt[h]) for h in hs]
        num = [inter[h] * qc[h] + e_loc[h] * f["pv"][h] for h in hs]
        den = [inter[h] * jnp.sum(f["q"][h] * n_old[h], axis=-1, keepdims=True) + e_loc[h] * f["ps"][h] for h in hs]
        hh = [num[h] / jnp.maximum(jnp.abs(den[h]), jnp.exp(-mt[h])) for h in hs]
        m_new = [mt[h][c - 1:c, :] for h in hs]
        scale = [jnp.exp(f["m_end"][h] - m_new[h]) for h in hs]
        dec = [jnp.exp(f["b_col"][h][c - 1:c, :] + m_old[h] - m_new[h]) for h in hs]
        c_old = [dec[h] * c_old[h] + scale[h] * f["kv"][h] for h in hs]
        n_old = [dec[h] * n_old[h] + scale[h] * f["ks"][h] for h in hs]
        m_old = m_new
        for h in hs:
            hn = hh[h] * lax.rsqrt(jnp.mean(hh[h] * hh[h], axis=-1, keepdims=True) + EPS)
            og = og_ref[rows, h * ML_DV:(h + 1) * ML_DV]
            y = hn * norm_ref[:, h * ML_DV:(h + 1) * ML_DV] * _sigmoid(og)
            y_ref[rows, h * ML_DV:(h + 1) * ML_DV] = y.astype(y_ref.dtype)
    for h in hs:
        c_ref[0, h] = c_old[h]
        n_ref[0, h:h + 1, :] = n_old[h]
        m_ref[0, :, h:h + 1] = m_old[h]


def _mlstm(p_main, p_small, bias_row, ml_norm, c0, n0, m0, bsz, t):
    step = CHUNK * _sub_chunks(t)
    nc = t // step
    rows = bsz * t
    rmap = lambda b, c: b * nc + c
    return pl.pallas_call(
        _mlstm_kernel,
        grid=(bsz, nc),
        in_specs=[
            pl.BlockSpec((step, ML_QK), lambda b, c: (rmap(b, c), OFF_ML // ML_QK)),
            pl.BlockSpec((step, ML_QK), lambda b, c: (rmap(b, c), OFF_ML // ML_QK + 1)),
            pl.BlockSpec((step, ML_WIDTH), lambda b, c: (rmap(b, c), OFF_ML // ML_WIDTH + 1)),
            pl.BlockSpec((step, ML_WIDTH), lambda b, c: (rmap(b, c), OFF_ML // ML_WIDTH + 2)),
            pl.BlockSpec((step, SMALL_W), lambda b, c: (rmap(b, c), SMALL_BLK)),
            pl.BlockSpec((1, SMALL_W), lambda b, c: (0, 0)),
            pl.BlockSpec((1, ML_WIDTH), lambda b, c: (0, 0)),
            pl.BlockSpec((1, ML_HEADS, ML_DQK, ML_DV), lambda b, c: (b, 0, 0, 0)),
            pl.BlockSpec((1, ML_HEADS, ML_DQK), lambda b, c: (b, 0, 0)),
            pl.BlockSpec((1, 1, ML_HEADS), lambda b, c: (b, 0, 0)),
        ],
        out_specs=[
            pl.BlockSpec((step, ML_WIDTH), lambda b, c: (rmap(b, c), 0)),
            pl.BlockSpec((1, ML_HEADS, ML_DQK, ML_DV), lambda b, c: (b, 0, 0, 0)),
            pl.BlockSpec((1, ML_HEADS, ML_DQK), lambda b, c: (b, 0, 0)),
            pl.BlockSpec((1, 1, ML_HEADS), lambda b, c: (b, 0, 0)),
        ],
        out_shape=[
            jax.ShapeDtypeStruct((rows, ML_WIDTH), BF16),
            jax.ShapeDtypeStruct(c0.shape, F32),
            jax.ShapeDtypeStruct(n0.shape, F32),
            jax.ShapeDtypeStruct((bsz, 1, ML_HEADS), F32),
        ],
        compiler_params=_cparams(("parallel", "arbitrary")),
        name="mlstm",
    )(p_main, p_main, p_main, p_main, p_small, bias_row, ml_norm.reshape(1, ML_WIDTH),
      c0, n0, m0.reshape(bsz, 1, ML_HEADS))


def _rwkv_front(x, prev8, mu_ref, w0_ref, w2_ref, a0_ref, a2_ref, g2_ref, kk_ref, ka_ref, rk_ref):
    c = x.shape[0]
    w = RW_WIDTH
    xprev = _shift_rows(x, prev8, 1)
    xm = x + (xprev - x) * mu_ref[...]
    rr = xm[:, 0:w]
    rk = xm[:, w:2 * w]
    rv = xm[:, 2 * w:3 * w]
    xw = xm[:, 3 * w:3 * w + RW_W_RANK]
    xa = xm[:, 3 * w + RW_W_RANK:3 * w + RW_W_RANK + RW_A_RANK]
    xg = xm[:, 3 * w + RW_W_RANK + RW_A_RANK:]

    w_pre = w0_ref[...] + _dot3(jnp.tanh(xw), w2_ref[...])
    lw = -jnp.exp(-_softplus(-w_pre) - 0.5)
    a = _sigmoid(a0_ref[...] + _dot3(xa, a2_ref[...]))
    g_out = _dot3(_sigmoid(xg), g2_ref[...])
    kk_raw = rk * kk_ref[...]
    kmod = rk * (1.0 + (a - 1.0) * ka_ref[...])
    bon = rr * kmod * rk_ref[...]

    lc = _cumsum_rows(lw)
    l_end = lc[c - 1:c, :]
    l_mid = lc[c // 2 - 1:c // 2, :]
    lcc = lc - l_mid
    p_mid = jnp.exp(l_mid)
    p_in = jnp.exp(lcc)
    p_prev = jnp.exp(lcc - lw)
    p_inv = jnp.exp(-lcc)
    p_end = jnp.exp(l_end - lc)
    p_all = jnp.exp(l_end)

    kkn = kk_raw * lax.rsqrt(_head_sums(kk_raw * kk_raw, RW_N) + 1e-6)
    bb = kkn * a
    kkp = kkn * p_prev
    rp = rr * p_in
    kd = kmod * p_inv
    bd = bb * p_inv
    k_end = kmod * p_end
    b_end = bb * p_end
    return dict(kkp=kkp, rp=rp, kd=kd, bd=bd, k_end=k_end, b_end=b_end, rv=rv, p_mid=p_mid, p_all=p_all,
                bonus=_head_sums(bon, RW_N) * rv, g_out=g_out)


def _rw_groups():
    gw = RW_GROUP * RW_N
    return [slice(g * gw, (g + 1) * gw) for g in range(RW_HEADS // RW_GROUP)]


def _rw_masks(c):
    shape = (RW_GROUP * c, RW_GROUP * RW_N)
    head_eq = lax.broadcasted_iota(jnp.int32, shape, 0) // c == lax.broadcasted_iota(jnp.int32, shape, 1) // RW_N
    tn_ = lax.broadcasted_iota(jnp.int32, (c, RW_GROUP * RW_N), 0)
    sn_ = lax.broadcasted_iota(jnp.int32, (c, RW_GROUP * RW_N), 1) % RW_N
    return head_eq, tn_ > sn_, tn_ >= sn_


def _rw_blockdiag(z, head_eq):
    return jnp.where(head_eq, jnp.concatenate([z] * RW_GROUP, axis=0), 0.0)


def _rw_rowsum(z, c):
    out = z[0:c]
    for i in range(1, RW_GROUP):
        out = out + z[i * c:(i + 1) * c]
    return out


def _rwkv_mid(f, masks):
    kkp, rp, kd, bd, rv = f["kkp"], f["rp"], f["kd"], f["bd"], f["rv"]
    c = rv.shape[0]
    head_eq, strict, incl = masks
    groups = _rw_groups()
    nt = lambda x, y: _dotp(x, y, 1, 1, RW_PASSES)
    x2 = [jnp.concatenate([kkp[:, cs], rp[:, cs]], axis=0) for cs in groups]
    ab_b = [nt(x, _rw_blockdiag(bd[:, cs], head_eq)) for x, cs in zip(x2, groups)]
    ab_k = [nt(x, _rw_blockdiag(kd[:, cs], head_eq)) for x, cs in zip(x2, groups)]
    a_b = [_rw_blockdiag(jnp.where(strict, z[0:c], 0.0), head_eq) for z in ab_b]
    return dict(
        x2=x2, v_bd=[_rw_blockdiag(rv[:, cs], head_eq) for cs in groups],
        t_inv=_tri_apply_many(_tri_factor_many(a_b, c, RW_PASSES), None, RW_PASSES),
        a_k=[jnp.where(strict, z[0:c], 0.0) for z in ab_k],
        r_k=[jnp.where(incl, z[c:], 0.0) for z in ab_k],
        r_b=[jnp.where(incl, z[c:], 0.0) for z in ab_b])


def _rwkv_chain(f, g, s_olds, ln_ref, masks):
    k_end, b_end, rv, p_mid, p_all = f["k_end"], f["b_end"], f["rv"], f["p_mid"], f["p_all"]
    c = rv.shape[0]
    groups = _rw_groups()
    head_eq = masks[0]
    nt = lambda x, y: _dotp(x, y, 1, 1, RW_PASSES)
    nn = lambda x, y: _dotp(x, y, 1, 0, RW_PASSES)
    v_bd = g["v_bd"]
    ab_s = [nt(x, _rw_blockdiag(s * p_mid[:, cs], head_eq))
            for x, s, cs in zip(g["x2"], s_olds, groups)]
    rhs = [_rw_blockdiag(z[0:c] + nn(ak, v), head_eq) for z, ak, v in zip(ab_s, g["a_k"], v_bd)]
    us = [nn(t_, r_) for t_, r_ in zip(g["t_inv"], rhs)]
    ygs = [z[c:] + nn(rk_, v) - nn(rb_, u) for z, rk_, v, rb_, u in zip(ab_s, g["r_k"], v_bd, g["r_b"], us)]
    upds = [nn(jnp.concatenate([rv[:, cs], -_rw_rowsum(u, c)], axis=0).T,
               jnp.concatenate([k_end[:, cs], b_end[:, cs]], axis=0)) for cs, u in zip(groups, us)]
    s_news = [s * p_all[:, cs] + _rw_rowsum(jnp.where(head_eq, upd, 0.0), c)
              for cs, s, upd in zip(groups, s_olds, upds)]

    yh = jnp.concatenate(ygs, axis=1)
    mu_ = _head_sums(yh, RW_N) * (1.0 / RW_N)
    yc = yh - mu_
    var = _head_sums(yc * yc, RW_N) * (1.0 / RW_N)
    yn = yc * lax.rsqrt(var + RW_GN_EPS)
    return (yn * ln_ref[...] + f["bonus"]) * f["g_out"], s_news


def _rwkv_kernel(p_ref, shift_ref, mu_ref, w0_ref, w2_ref, a0_ref, a2_ref, g2_ref, kk_ref, ka_ref,
                 rk_ref, ln_ref, s0_ref, y_ref, s_ref, prev_ref):
    @pl.when(pl.program_id(1) == 0)
    def _():
        s_ref[...] = s0_ref[...]
        prev_ref[...] = jnp.broadcast_to(shift_ref[0], prev_ref.shape)

    c = CHUNK
    nsub = p_ref.shape[0] // c
    masks = _rw_masks(c)
    fronts = []
    for k in range(nsub):
        last8 = prev_ref[...] if k == 0 else p_ref[k * c - 8:k * c, :]
        fronts.append(_rwkv_front(p_ref[k * c:(k + 1) * c, :], last8, mu_ref, w0_ref, w2_ref, a0_ref, a2_ref,
                                  g2_ref, kk_ref, ka_ref, rk_ref))
    groups = _rw_groups()
    states = [s_ref[0, :, cs] for cs in groups]
    for k in range(nsub):
        out, states = _rwkv_chain(fronts[k], _rwkv_mid(fronts[k], masks), states, ln_ref, masks)
        y_ref[k * c:(k + 1) * c, :] = out.astype(y_ref.dtype)
    prev_ref[...] = p_ref[nsub * c - 8:nsub * c, :]
    for cs, s in zip(groups, states):
        s_ref[0, :, cs] = s


def _rwkv(p_rw, shift0, s0, p, bsz, t):
    step = CHUNK * _sub_chunks(t)
    nc = t // step
    rows = bsz * t
    rmap = lambda b, c: b * nc + c
    full = lambda shape: pl.BlockSpec(shape, lambda b, c: (0,) * len(shape))
    w = RW_WIDTH
    s_nat = s0.transpose(0, 2, 1, 3).reshape(bsz, RW_N, w)
    y, s = pl.pallas_call(
        _rwkv_kernel,
        grid=(bsz, nc),
        in_specs=[
            pl.BlockSpec((step, RW_COLS), lambda b, c: (rmap(b, c), 0)),
            pl.BlockSpec((1, 1, RW_COLS), lambda b, c: (b, 0, 0)),
            full((1, RW_COLS)), full((1, w)), full((RW_W_RANK, w)), full((1, w)), full((RW_A_RANK, w)),
            full((RW_G_RANK, w)), full((1, w)), full((1, w)), full((1, w)), full((1, w)),
            pl.BlockSpec((1, RW_N, w), lambda b, c: (b, 0, 0)),
        ],
        out_specs=[
            pl.BlockSpec((step, w), lambda b, c: (rmap(b, c), 0)),
            pl.BlockSpec((1, RW_N, w), lambda b, c: (b, 0, 0)),
        ],
        out_shape=[jax.ShapeDtypeStruct((rows, w), BF16), jax.ShapeDtypeStruct(s_nat.shape, F32)],
        scratch_shapes=[pltpu.VMEM((8, RW_COLS), F32)],
        compiler_params=_cparams(("parallel", "arbitrary")),
        name="rwkv7",
    )(p_rw, shift0.reshape(bsz, 1, RW_COLS), p["rw_mu"].reshape(1, RW_COLS), p["rw_w0"].reshape(1, w),
      p["rw_w2"], p["rw_a0"].reshape(1, w), p["rw_a2"], p["rw_g2"], p["rw_k_k"].reshape(1, w),
      p["rw_k_a"].reshape(1, w), p["rw_r_k"].reshape(1, w), p["rw_ln"].reshape(1, w), s_nat)
    return y, s.reshape(bsz, RW_N, RW_HEADS, RW_N).transpose(0, 2, 1, 3)


def _gdn_front(xs, prevs, sm, cw_ref, alog_ref, dtb_ref):
    c = xs[0].shape[0]
    acts = []
    for sec, (x, prev8) in enumerate(zip(xs, prevs)):
        cs = slice(sec * GD_QK, (sec + 1) * GD_QK)
        y = _shift_rows(x, prev8, GD_CONV - 1) * cw_ref[0:1, cs]
        for i in range(1, GD_CONV - 1):
            y = y + _shift_rows(x, prev8, GD_CONV - 1 - i) * cw_ref[i:i + 1, cs]
        y = y + x * cw_ref[GD_CONV - 1:GD_CONV, cs]
        acts.append(_silu(y))
    qa, ka, va = acts

    beta = _sigmoid(sm)
    g = -jnp.exp(alog_ref[...]) * _softplus(sm + dtb_ref[...])
    gc = _cumsum_rows(g)
    gc_t = gc.T

    gr = GD_GROUP * c
    gk = GD_GROUP * GD_DK
    r4 = lax.broadcasted_iota(jnp.int32, (gr, gr), 0)
    c4 = lax.broadcasted_iota(jnp.int32, (gr, gr), 1)
    t_minus_s = jnp.where((r4 // c) == (c4 // c), (r4 % c) - (c4 % c), -1)
    m_strict = t_minus_s > 0
    m_incl = t_minus_s >= 0
    head_eq = (lax.broadcasted_iota(jnp.int32, (gr, gk), 0) // c
               == lax.broadcasted_iota(jnp.int32, (gr, gk), 1) // GD_DK)

    def rep(z):
        return jnp.concatenate([z] * GD_GROUP, axis=0)

    def stack(parts):
        return jnp.concatenate(parts, axis=0)

    kn = []
    qn = []
    for h in range(GD_HEADS):
        ks = slice(h * GD_DK, (h + 1) * GD_DK)
        q = qa[:, ks]
        qn.append(q * lax.rsqrt(jnp.sum(q * q, axis=-1, keepdims=True) + 1e-6) * (GD_DK ** -0.5))
        k = ka[:, ks]
        kn.append(k * lax.rsqrt(jnp.sum(k * k, axis=-1, keepdims=True) + 1e-6))
    groups = [range(g0, g0 + GD_GROUP) for g0 in range(0, GD_HEADS, GD_GROUP)]
    k_nat = [jnp.concatenate([kn[h] for h in hs], axis=1) for hs in groups]
    q_nat = [jnp.concatenate([qn[h] for h in hs], axis=1) for hs in groups]
    k_bd = [jnp.where(head_eq, rep(kk_), 0.0) for kk_ in k_nat]
    v_st = [stack([va[:, h * GD_DV:(h + 1) * GD_DV] for h in hs]) for hs in groups]
    b_col = [stack([beta[:, SM_GD_B + h:SM_GD_B + h + 1] for h in hs]) for hs in groups]
    g_col = [stack([gc[:, SM_GD_A + h:SM_GD_A + h + 1] for h in hs]) for hs in groups]
    g_row = [jnp.concatenate([gc_t[SM_GD_A + h:SM_GD_A + h + 1, :] for h in hs], axis=1) for hs in groups]

    dec = [jnp.exp(jnp.where(m_incl, gc_ - gr_, -jnp.inf)) for gc_, gr_ in zip(g_col, g_row)]
    kq = [_dotp(stack([kk_, qq_]), kb, 1, 1, GD_PASSES) for kk_, qq_, kb in zip(k_nat, q_nat, k_bd)]
    low = [jnp.where(m_strict, b * rep(z[0:c]) * d, 0.0) for b, z, d in zip(b_col, kq, dec)]
    gl = [gc[c - 1:c, SM_GD_A + h:SM_GD_A + h + 1] for h in range(GD_HEADS)]
    return dict(
        kq_lhs=[stack([kn[h], qn[h]]) for h in range(GD_HEADS)], v_st=v_st, b_col=b_col,
        eg=[jnp.exp(gc_) for gc_ in g_col],
        t_inv=_tri_apply_many(_tri_factor_many(low, c, GD_PASSES), None, GD_PASSES),
        qkt=[rep(z[c:]) * d for z, d in zip(kq, dec)],
        kw_t=[(kn[h] * jnp.exp(gl[h] - gc[:, SM_GD_A + h:SM_GD_A + h + 1])).T for h in range(GD_HEADS)],
        s_decay=[jnp.exp(gl[h]) for h in range(GD_HEADS)])


def _gdn_chain(f, s_old):
    c = f["kq_lhs"][0].shape[0] // 2
    stack = lambda parts: jnp.concatenate(parts, axis=0)
    groups = [range(g0, g0 + GD_GROUP) for g0 in range(0, GD_HEADS, GD_GROUP)]
    kqs = [_dotp(f["kq_lhs"][h], s_old[h], 1, 0, GD_PASSES) for h in range(GD_HEADS)]
    ks = [stack([kqs[h][0:c] for h in hs]) for hs in groups]
    qs = [stack([kqs[h][c:] for h in hs]) for hs in groups]
    rhs = [b * (v - e * z) for b, v, e, z in zip(f["b_col"], f["v_st"], f["eg"], ks)]
    us = [_dotp(t_, r_, 1, 0, GD_PASSES) for t_, r_ in zip(f["t_inv"], rhs)]
    os_ = [e * z + _dotp(qk, u, 1, 0, GD_PASSES) for e, z, qk, u in zip(f["eg"], qs, f["qkt"], us)]
    outs = []
    s_new = []
    for gi, hs in enumerate(groups):
        for i, h in enumerate(hs):
            rs = slice(i * c, (i + 1) * c)
            s_new.append(f["s_decay"][h] * s_old[h] + _dotp(f["kw_t"][h], us[gi][rs], 1, 0, GD_PASSES))
            outs.append(os_[gi][rs])
    return outs, s_new


def _gdn_kernel(q_ref, k_ref, v_ref, z_ref, sm_ref, cw_ref, conv0_ref, alog_ref, dtb_ref, norm_ref, s0_ref,
                y_ref, s_ref, prev_ref):
    @pl.when(pl.program_id(1) == 0)
    def _():
        s_ref[...] = s0_ref[...]
        prev_ref[...] = jnp.zeros(prev_ref.shape, F32)
        prev_ref[8 - (GD_CONV - 1):8, :] = conv0_ref[0]

    c = CHUNK
    nsub = q_ref.shape[0] // c
    refs = (q_ref, k_ref, v_ref)
    fronts = []
    for k in range(nsub):
        xs = [r[k * c:(k + 1) * c, :] for r in refs]
        if k == 0:
            prevs = [prev_ref[:, sec * GD_QK:(sec + 1) * GD_QK] for sec in range(3)]
        else:
            prevs = [r[k * c - 8:k * c, :] for r in refs]
        fronts.append(_gdn_front(xs, prevs, sm_ref[k * c:(k + 1) * c, :], cw_ref, alog_ref, dtb_ref))
    states = [s_ref[0, h] for h in range(GD_HEADS)]
    for k in range(nsub):
        outs, states = _gdn_chain(fronts[k], states)
        for h in range(GD_HEADS):
            o_h = outs[h]
            og = o_h * lax.rsqrt(jnp.mean(o_h * o_h, axis=-1, keepdims=True) + EPS) * norm_ref[...]
            vs = slice(h * GD_DV, (h + 1) * GD_DV)
            y_ref[k * c:(k + 1) * c, vs] = (og * _silu(z_ref[k * c:(k + 1) * c, vs])).astype(y_ref.dtype)
    for h in range(GD_HEADS):
        s_ref[0, h] = states[h]
    for sec, r in enumerate(refs):
        prev_ref[:, sec * GD_QK:(sec + 1) * GD_QK] = r[nsub * c - 8:nsub * c, :]


def _gdn(p_main, p_small, conv_w, conv0, alog_row, dtb_row, gd_norm, s0, bsz, t):
    step = CHUNK * _sub_chunks(t)
    nc = t // step
    rows = bsz * t
    rmap = lambda b, c: b * nc + c
    base = OFF_GD // GD_QK
    full = lambda shape: pl.BlockSpec(shape, lambda b, c: (0,) * len(shape))
    return pl.pallas_call(
        _gdn_kernel,
        grid=(bsz, nc),
        in_specs=[
            pl.BlockSpec((step, GD_QK), lambda b, c: (rmap(b, c), base)),
            pl.BlockSpec((step, GD_QK), lambda b, c: (rmap(b, c), base + 1)),
            pl.BlockSpec((step, GD_WIDTH), lambda b, c: (rmap(b, c), base + 2)),
            pl.BlockSpec((step, GD_WIDTH), lambda b, c: (rmap(b, c), base + 3)),
            pl.BlockSpec((step, SMALL_W), lambda b, c: (rmap(b, c), SMALL_BLK)),
            full((GD_CONV, GD_QKV)),
            pl.BlockSpec((1, GD_CONV - 1, GD_QKV), lambda b, c: (b, 0, 0)),
            full((1, SMALL_W)), full((1, SMALL_W)), full((1, GD_DV)),
            pl.BlockSpec((1, GD_HEADS, GD_DK, GD_DV), lambda b, c: (b, 0, 0, 0)),
        ],
        out_specs=[
            pl.BlockSpec((step, GD_WIDTH), lambda b, c: (rmap(b, c), 0)),
            pl.BlockSpec((1, GD_HEADS, GD_DK, GD_DV), lambda b, c: (b, 0, 0, 0)),
        ],
        out_shape=[jax.ShapeDtypeStruct((rows, GD_WIDTH), BF16), jax.ShapeDtypeStruct(s0.shape, F32)],
        scratch_shapes=[pltpu.VMEM((8, GD_QKV), F32)],
        compiler_params=_cparams(("parallel", "arbitrary")),
        name="gdn",
    )(p_main, p_main, p_main, p_main, p_small, conv_w, conv0, alog_row, dtb_row, gd_norm.reshape(1, GD_DV), s0)


def _merge_kernel(y0_ref, y1_ref, y2_ref, w_ref, g0_ref, g1_ref, g2_ref, o_ref):
    acc = _sigmoid(g0_ref[...]) * jnp.dot(y0_ref[...], w_ref[0], preferred_element_type=F32)
    acc = acc + _sigmoid(g1_ref[...]) * jnp.dot(y1_ref[...], w_ref[1], preferred_element_type=F32)
    acc = acc + _sigmoid(g2_ref[...]) * jnp.dot(y2_ref[...], w_ref[2], preferred_element_type=F32)
    o_ref[...] = acc.astype(o_ref.dtype)


def _merge(ys, w_branch, p_main, d):
    m = ys[0].shape[0]
    tm = _tile(m, 1024, 8)
    tn = _tile(d, 512, 128)
    gate0 = OFF_GATE
    assert gate0 % tn == 0
    gb = gate0 // tn
    nb = d // tn
    yspec = pl.BlockSpec((tm, BR_WIDTH), lambda i, j: (i, 0))
    gspec = lambda b: pl.BlockSpec((tm, tn), lambda i, j: (i, gb + b * nb + j))
    return pl.pallas_call(
        _merge_kernel,
        grid=(m // tm, nb),
        in_specs=[yspec, yspec, yspec, pl.BlockSpec((3, BR_WIDTH, tn), lambda i, j: (0, 0, j)),
                  gspec(0), gspec(1), gspec(2)],
        out_specs=pl.BlockSpec((tm, tn), lambda i, j: (i, j)),
        out_shape=jax.ShapeDtypeStruct((m, d), BF16),
        compiler_params=_cparams(("parallel", "arbitrary")),
        name="merge",
    )(ys[0], ys[1], ys[2], w_branch, p_main, p_main, p_main)


def _attn_kernel(q_ref, k_ref, v_ref, o_ref):
    for h in range(CA_HEADS):
        hs = slice(h * CA_HEAD_DIM, (h + 1) * CA_HEAD_DIM)
        s = _dg(q_ref[:, hs], k_ref[0, :, hs], 1, 1) * (CA_HEAD_DIM ** -0.5)
        s = s - jnp.max(s, axis=-1, keepdims=True)
        e = jnp.exp(s)
        pr = e / jnp.sum(e, axis=-1, keepdims=True)
        o = jnp.dot(pr.astype(BF16), v_ref[0, :, hs], preferred_element_type=F32)
        o_ref[:, hs] = o.astype(o_ref.dtype)


def _attention(q, mem_k, mem_v, bsz, t):
    tq = _tile(t, 512, 8)
    nt = t // tq
    n_mem = mem_k.shape[1]
    return pl.pallas_call(
        _attn_kernel,
        grid=(bsz, nt),
        in_specs=[
            pl.BlockSpec((tq, CA_WIDTH), lambda b, i: (b * nt + i, 0)),
            pl.BlockSpec((1, n_mem, CA_WIDTH), lambda b, i: (b, 0, 0)),
            pl.BlockSpec((1, n_mem, CA_WIDTH), lambda b, i: (b, 0, 0)),
        ],
        out_specs=pl.BlockSpec((tq, CA_WIDTH), lambda b, i: (b * nt + i, 0)),
        out_shape=jax.ShapeDtypeStruct((bsz * t, CA_WIDTH), BF16),
        compiler_params=_cparams(("parallel", "arbitrary")),
        name="mem_attention",
    )(q, mem_k, mem_v)


def _ffn_up_kernel(u_ref, ss_ref, wa_ref, wg_ref, cwa_ref, cwg_ref, c0a_ref, c0g_ref, act_ref, ta_ref, tg_ref,
                   carry_ref, *, tiles_per_seq, sub):
    i = pl.program_id(0)
    j = pl.program_id(1)
    tm, tn = act_ref.shape

    @pl.when(i == 0)
    def _():
        carry_ref[j] = jnp.zeros(carry_ref.shape[1:], F32)

    first = (i % tiles_per_seq) == 0
    pad = jnp.zeros((8 - (FFN_CONV - 1), tn), F32)
    carried = carry_ref[j]
    prev_a = jnp.where(first, jnp.concatenate([pad, c0a_ref[0]], axis=0), carried[0:8])
    prev_g = jnp.where(first, jnp.concatenate([pad, c0g_ref[0]], axis=0), carried[8:16])
    rstd = _row_rstd(ss_ref, u_ref.shape[1])

    def up(r):
        ur = u_ref[r * sub:(r + 1) * sub, :]
        sc = rstd[r * sub:(r + 1) * sub]
        return (jnp.dot(ur, wa_ref[...], preferred_element_type=F32) * sc,
                jnp.dot(ur, wg_ref[...], preferred_element_type=F32) * sc)

    nxt = up(0)
    for r in range(tm // sub):
        rows = slice(r * sub, (r + 1) * sub)
        za, zg = nxt
        if r + 1 < tm // sub:
            nxt = up(r + 1)
        fa =(_shift_rows(za, prev_a, 2) * cwa_ref[0:1] + _shift_rows(za, prev_a, 1) * cwa_ref[1:2]
              + za * cwa_ref[2:3])
        fg = (_shift_rows(zg, prev_g, 2) * cwg_ref[0:1] + _shift_rows(zg, prev_g, 1) * cwg_ref[1:2]
              + zg * cwg_ref[2:3])
        act_ref[rows, :] = (_silu(fg) * fa).astype(act_ref.dtype)
        prev_a = za[sub - 8:sub]
        prev_g = zg[sub - 8:sub]
    carry_ref[j] = jnp.concatenate([prev_a, prev_g], axis=0)
    ta_ref[0] = prev_a[8 - (FFN_CONV - 1):8]
    tg_ref[0] = prev_g[8 - (FFN_CONV - 1):8]


def _ffn_up_act(u, row_ss, w_up, conv0, conv_w, bsz, t):
    m, d = u.shape
    f = w_up.shape[1] // 2
    tm = _tile(t, FFN_ROW_TILE, 8)
    sub = _tile(tm, FFN_SUB_ROWS, 8)
    tn = _tile(f, 512, 128)
    nj = f // tn
    tps = t // tm
    act, ta, tg = pl.pallas_call(
        functools.partial(_ffn_up_kernel, tiles_per_seq=tps, sub=sub),
        grid=(m // tm, nj),
        in_specs=[
            pl.BlockSpec((tm, d), lambda i, j: (i, 0)),
            pl.BlockSpec((tm, 128), lambda i, j: (i, 0)),
            pl.BlockSpec((d, tn), lambda i, j: (0, j)),
            pl.BlockSpec((d, tn), lambda i, j: (0, nj + j)),
            pl.BlockSpec((FFN_CONV, tn), lambda i, j: (0, j)),
            pl.BlockSpec((FFN_CONV, tn), lambda i, j: (0, nj + j)),
            pl.BlockSpec((1, FFN_CONV - 1, tn), lambda i, j: (i // tps, 0, j)),
            pl.BlockSpec((1, FFN_CONV - 1, tn), lambda i, j: (i // tps, 0, nj + j)),
        ],
        out_specs=[
            pl.BlockSpec((tm, tn), lambda i, j: (i, j)),
            pl.BlockSpec((1, FFN_CONV - 1, tn), lambda i, j: (i // tps, 0, j)),
            pl.BlockSpec((1, FFN_CONV - 1, tn), lambda i, j: (i // tps, 0, j)),
        ],
        out_shape=[jax.ShapeDtypeStruct((m, f), BF16),
                   jax.ShapeDtypeStruct((bsz, FFN_CONV - 1, f), F32),
                   jax.ShapeDtypeStruct((bsz, FFN_CONV - 1, f), F32)],
        scratch_shapes=[pltpu.VMEM((nj, 16, tn), F32)],
        compiler_params=_cparams(("arbitrary", "arbitrary")),
        name="ffn_up_conv_act",
    )(u, row_ss, w_up, w_up, conv_w, conv_w, conv0, conv0)
    return act, jnp.concatenate([ta, tg], axis=-1)


def _ffn_up_short_kernel(u_ref, ss_ref, wa_ref, wg_ref, cwa_ref, cwg_ref, c0a_ref, c0g_ref, act_ref, ta_ref, tg_ref,
                         *, t, sub):
    tm, tn = act_ref.shape
    pad = jnp.zeros((8 - (FFN_CONV - 1), tn), F32)
    rstd = _row_rstd(ss_ref, u_ref.shape[1])

    def up(r):
        ur = u_ref[r * sub:(r + 1) * sub, :]
        sc = rstd[r * sub:(r + 1) * sub]
        return (jnp.dot(ur, wa_ref[...], preferred_element_type=F32) * sc,
                jnp.dot(ur, wg_ref[...], preferred_element_type=F32) * sc)

    nxt = up(0)
    for r in range(tm // sub):
        za_all, zg_all = nxt
        if r + 1 < tm // sub:
            nxt = up(r + 1)
        for q in range(sub // t):
            s = r * (sub // t) + q
            za = za_all[q * t:(q + 1) * t]
            zg = zg_all[q * t:(q + 1) * t]
            prev_a = jnp.concatenate([pad, c0a_ref[s]], axis=0)
            prev_g = jnp.concatenate([pad, c0g_ref[s]], axis=0)
            fa = (_shift_rows(za, prev_a, 2) * cwa_ref[0:1] + _shift_rows(za, prev_a, 1) * cwa_ref[1:2]
                  + za * cwa_ref[2:3])
            fg = (_shift_rows(zg, prev_g, 2) * cwg_ref[0:1] + _shift_rows(zg, prev_g, 1) * cwg_ref[1:2]
                  + zg * cwg_ref[2:3])
            act_ref[s * t:(s + 1) * t, :] = (_silu(fg) * fa).astype(act_ref.dtype)
            ta_ref[s] = za[t - (FFN_CONV - 1):t]
            tg_ref[s] = zg[t - (FFN_CONV - 1):t]


def _ffn_up_act_short(u, row_ss, w_up, conv0, conv_w, bsz, t):
    m, d = u.shape
    f = w_up.shape[1] // 2
    spt = _tile(bsz, max(1, FFN_ROW_TILE // t), 1)
    tm = spt * t
    sub = t * _tile(spt, max(1, FFN_SHORT_SUB_ROWS // t), 1)
    tn = _tile(f, 512, 128)
    nj = f // tn
    act, ta, tg = pl.pallas_call(
        functools.partial(_ffn_up_short_kernel, t=t, sub=sub),
        grid=(m // tm, nj),
        in_specs=[
            pl.BlockSpec((tm, d), lambda i, j: (i, 0)),
            pl.BlockSpec((tm, 128), lambda i, j: (i, 0)),
            pl.BlockSpec((d, tn), lambda i, j: (0, j)),
            pl.BlockSpec((d, tn), lambda i, j: (0, nj + j)),
            pl.BlockSpec((FFN_CONV, tn), lambda i, j: (0, j)),
            pl.BlockSpec((FFN_CONV, tn), lambda i, j: (0, nj + j)),
            pl.BlockSpec((spt, FFN_CONV - 1, tn), lambda i, j: (i, 0, j)),
            pl.BlockSpec((spt, FFN_CONV - 1, tn), lambda i, j: (i, 0, nj + j)),
        ],
        out_specs=[
            pl.BlockSpec((tm, tn), lambda i, j: (i, j)),
            pl.BlockSpec((spt, FFN_CONV - 1, tn), lambda i, j: (i, 0, j)),
            pl.BlockSpec((spt, FFN_CONV - 1, tn), lambda i, j: (i, 0, j)),
        ],
        out_shape=[jax.ShapeDtypeStruct((m, f), BF16),
                   jax.ShapeDtypeStruct((bsz, FFN_CONV - 1, f), F32),
                   jax.ShapeDtypeStruct((bsz, FFN_CONV - 1, f), F32)],
        compiler_params=_cparams(("parallel", "arbitrary")),
        name="ffn_up_conv_act_short",
    )(u, row_ss, w_up, w_up, conv_w, conv_w, conv0, conv0)
    return act, jnp.concatenate([ta, tg], axis=-1)


def _prep_layer(p, d):
    w_in = p["w_in"]
    o_ml = 0
    o_if = ML_MAIN
    o_rw = o_if + 2 * ML_HEADS
    o_gd = o_rw + RW_COLS
    o_ba = o_gd + GD_MAIN
    o_gate = o_ba + 2 * GD_HEADS
    n_small = 2 * ML_HEADS + 2 * GD_HEADS
    w_main = jnp.concatenate(
        [w_in[:, o_rw:o_rw + RW_COLS], w_in[:, o_if:o_if + 2 * ML_HEADS], w_in[:, o_ba:o_ba + 2 * GD_HEADS],
         jnp.zeros((d, RW_SLAB - RW_COLS - n_small), F32),
         w_in[:, o_ml:o_ml + ML_MAIN], w_in[:, o_gd:o_gd + GD_MAIN], w_in[:, o_gate:]], axis=1).astype(BF16)
    zrow = jnp.zeros((SMALL_W,), F32)
    q = dict(p)
    q.update(
        w_main=w_main,
        ml_bias_row=zrow.at[SM_ML_I:SM_ML_I + 2 * ML_HEADS].set(p["ml_b_if"]).reshape(1, SMALL_W),
        gd_alog_row=zrow.at[SM_GD_A:SM_GD_A + GD_HEADS].set(p["gd_a_log"]).reshape(1, SMALL_W),
        gd_dtb_row=zrow.at[SM_GD_A:SM_GD_A + GD_HEADS].set(p["gd_dt_bias"]).reshape(1, SMALL_W),
        w_branch_b=p["w_branch"].astype(BF16), w_out_b=p["w_out"].astype(BF16),
        w_ca_q_b=p["w_ca_q"].astype(BF16), w_ca_kv_b=p["w_ca_kv"].astype(BF16),
        w_ca_o_b=p["w_ca_o"].astype(BF16), w_up_b=p["w_up"].astype(BF16), w_down_b=p["w_down"].astype(BF16))
    return q


def _layer(h, mixed, mem_k, mem_v, st, p, next_gain, bsz, t):
    d = h.shape[1]
    if mixed is None:
        u, ss = _rmsnorm(h, p["g_mix"], BF16), None
    else:
        u, ss = mixed
    p_main = _matmul(u, p["w_main"], row_ss=ss)
    p_rw = p_small = p_main

    y_ml, ml_c, ml_n, ml_m = _mlstm(p_main, p_small, p["ml_bias_row"], p["ml_norm"],
                                    st["ml_C"], st["ml_n"], st["ml_m"], bsz, t)
    y_rw, rw_s = _rwkv(p_rw, st["rw_shift"], st["rw_S"], p, bsz, t)
    y_gd, gd_s = _gdn(p_main, p_small, p["gd_conv_w"], st["gd_conv"], p["gd_alog_row"], p["gd_dtb_row"],
                      p["gd_norm"], st["gd_S"], bsz, t)
    merged = _merge((y_ml, y_rw, y_gd), p["w_branch_b"], p_main, d)
    h, u, ss = _matmul(merged, p["w_out_b"], residual=h, next_gain=p["g_ca"])

    q = _matmul(u, p["w_ca_q_b"], out_dtype=BF16, row_ss=ss)
    o = _attention(q, mem_k, mem_v, bsz, t)
    h, u, ss = _matmul(o, p["w_ca_o_b"], residual=h, next_gain=p["g_ffn"])

    if t >= FFN_FUSE_MIN_T:
        act, ffn_conv = _ffn_up_act(u, ss, p["w_up_b"], st["ffn_conv"], p["ffn_conv_w"], bsz, t)
    else:
        act, ffn_conv = _ffn_up_act_short(u, ss, p["w_up_b"], st["ffn_conv"], p["ffn_conv_w"], bsz, t)
    if next_gain is None:
        h, mixed_next = _matmul(act, p["w_down_b"], residual=h), None
    else:
        h, u, ss = _matmul(act, p["w_down_b"], residual=h, next_gain=next_gain)
        mixed_next = (u, ss)

    gd0 = OFF_GD
    new_st = dict(
        ml_C=ml_c, ml_n=ml_n, ml_m=ml_m.reshape(bsz, ML_HEADS), rw_S=rw_s,
        rw_shift=p_rw.reshape(bsz, t, -1)[:, t - 1, :RW_COLS],
        gd_S=gd_s,
        gd_conv=p_main.reshape(bsz, t, -1)[:, t - (GD_CONV - 1):, gd0:gd0 + GD_QKV],
        ffn_conv=ffn_conv)
    return h, mixed_next, new_st


def _zero_state(bsz, d_ff2):
    return dict(
        ml_C=jnp.zeros((bsz, ML_HEADS, ML_DQK, ML_DV), F32), ml_n=jnp.zeros((bsz, ML_HEADS, ML_DQK), F32),
        ml_m=jnp.zeros((bsz, ML_HEADS), F32), rw_S=jnp.zeros((bsz, RW_HEADS, RW_N, RW_N), F32),
        rw_shift=jnp.zeros((bsz, RW_COLS), F32), gd_S=jnp.zeros((bsz, GD_HEADS, GD_DK, GD_DV), F32),
        gd_conv=jnp.zeros((bsz, GD_CONV - 1, GD_QKV), F32), ffn_conv=jnp.zeros((bsz, FFN_CONV - 1, d_ff2), F32))


def kernel(x_prompt, x_sample, cache_mem_k, cache_mem_v, state_mlstm_C, state_mlstm_n, state_mlstm_m, state_rwkv_S, state_rwkv_shift, state_gdn_S, state_gdn_conv, state_ffn_conv, mem_prompt, g_mix, w_in, ml_b_if, ml_norm, rw_mu, rw_w0, rw_w2, rw_a0, rw_a2, rw_g2, rw_k_k, rw_k_a, rw_r_k, rw_ln, gd_conv_w, gd_a_log, gd_dt_bias, gd_norm, w_branch, w_out, g_ca, g_mem, w_ca_q, w_ca_kv, w_ca_o, g_ffn, w_up, ffn_conv_w, w_down, g_final):
    bp, tp, d = x_prompt.shape
    bs, ts, _ = x_sample.shape
    depth = w_in.shape[0]
    n_mem = mem_prompt.shape[1]
    assert tp % CHUNK == 0 and ts % CHUNK == 0
    stacked = dict(g_mix=g_mix, w_in=w_in, ml_b_if=ml_b_if, ml_norm=ml_norm, rw_mu=rw_mu, rw_w0=rw_w0,
                   rw_w2=rw_w2, rw_a0=rw_a0, rw_a2=rw_a2, rw_g2=rw_g2, rw_k_k=rw_k_k, rw_k_a=rw_k_a,
                   rw_r_k=rw_r_k, rw_ln=rw_ln, gd_conv_w=gd_conv_w, gd_a_log=gd_a_log, gd_dt_bias=gd_dt_bias,
                   gd_norm=gd_norm, w_branch=w_branch, w_out=w_out, g_ca=g_ca, g_mem=g_mem, w_ca_q=w_ca_q,
                   w_ca_kv=w_ca_kv, w_ca_o=w_ca_o, g_ffn=g_ffn, w_up=w_up, ffn_conv_w=ffn_conv_w,
                   w_down=w_down)
    keys = ("ml_C", "ml_n", "ml_m", "rw_S", "rw_shift", "gd_S", "gd_conv", "ffn_conv")
    new_p = {k: [] for k in keys}
    new_s = {k: [] for k in keys}
    mem_k_list, mem_v_list = [], []
    hp = x_prompt.reshape(bp * tp, d)
    hs = x_sample.reshape(bs * ts, d)
    mem2d = mem_prompt.reshape(bp * n_mem, d)
    mixed_p = mixed_s = None
    for l in range(depth):
        p = _prep_layer({k: v[l] for k, v in stacked.items()}, d)
        next_gain = g_mix[l + 1] if l + 1 < depth else None
        kv = _matmul(_rmsnorm(mem2d, p["g_mem"], BF16), p["w_ca_kv_b"])
        mk = kv[:, :CA_WIDTH].reshape(bp, n_mem, CA_WIDTH)
        mv = kv[:, CA_WIDTH:].reshape(bp, n_mem, CA_WIDTH)
        hp, mixed_p, stp = _layer(hp, mixed_p, mk.astype(BF16), mv.astype(BF16), _zero_state(bp, w_up.shape[2]),
                                  p, next_gain, bp, tp)
        mem_k_list.append(mk.reshape(bp, n_mem, CA_HEADS, CA_HEAD_DIM))
        mem_v_list.append(mv.reshape(bp, n_mem, CA_HEADS, CA_HEAD_DIM))
        st_in = dict(ml_C=state_mlstm_C[l], ml_n=state_mlstm_n[l], ml_m=state_mlstm_m[l],
                     rw_S=state_rwkv_S[l], rw_shift=state_rwkv_shift[l], gd_S=state_gdn_S[l],
                     gd_conv=state_gdn_conv[l], ffn_conv=state_ffn_conv[l])
        ck = cache_mem_k[l].reshape(bs, n_mem, CA_WIDTH).astype(BF16)
        cv = cache_mem_v[l].reshape(bs, n_mem, CA_WIDTH).astype(BF16)
        hs, mixed_s, sts = _layer(hs, mixed_s, ck, cv, st_in, p, next_gain, bs, ts)
        for k in keys:
            new_p[k].append(stp[k])
            new_s[k].append(sts[k])
    y_prompt = _rmsnorm(hp, g_final, F32).reshape(bp, tp, d)
    y_sample = _rmsnorm(hs, g_final, F32).reshape(bs, ts, d)
    outs = [y_prompt, y_sample, jnp.stack(mem_k_list), jnp.stack(mem_v_list)]
    outs += [jnp.stack(new_p[k]) for k in keys]
    outs += [jnp.stack(new_s[k]) for k in keys]
    return tuple(outs)
```

```python
import functools

import jax
import jax.numpy as jnp
from jax import lax
from jax.experimental import pallas as pl
from jax.experimental.pallas import tpu as pltpu

F32 = jnp.float32
BF16 = jnp.bfloat16

EPS = 1e-6
CHUNK = 64
SUB_CHUNKS = 4

ML_HEADS, ML_DQK, ML_DV = 4, 128, 256
ML_QK = ML_HEADS * ML_DQK
ML_WIDTH = ML_HEADS * ML_DV
ML_MAIN = 2 * ML_QK + 2 * ML_WIDTH
ML_PASSES = 3

RW_HEADS, RW_N = 16, 64
RW_WIDTH = RW_HEADS * RW_N
RW_W_RANK, RW_A_RANK, RW_G_RANK = 64, 64, 128
RW_COLS = 3 * RW_WIDTH + RW_W_RANK + RW_A_RANK + RW_G_RANK
RW_GN_EPS = 64e-5
RW_GROUP = 4
RW_PASSES = 1

GD_HEADS, GD_DK, GD_DV = 8, 128, 128
GD_QK = GD_HEADS * GD_DK
GD_WIDTH = GD_HEADS * GD_DV
GD_QKV = 2 * GD_QK + GD_WIDTH
GD_CONV = 4
GD_MAIN = GD_QKV + GD_WIDTH
GD_GROUP = 4
GD_PASSES = 1

BR_WIDTH = 1024
CA_HEADS, CA_HEAD_DIM = 4, 256
CA_WIDTH = CA_HEADS * CA_HEAD_DIM
FFN_CONV = 3
FFN_FUSE_MIN_T = 512
FFN_ROW_TILE = 1024
FFN_SHORT_SUB_ROWS = 256
FFN_SUB_ROWS = 128

SMALL_W = 128
SM_ML_I, SM_ML_F, SM_GD_B, SM_GD_A = 0, ML_HEADS, 2 * ML_HEADS, 2 * ML_HEADS + GD_HEADS
RW_SLAB = 3584
SMALL_BLK = RW_COLS // SMALL_W
OFF_ML = 0
OFF_GD = OFF_ML + ML_MAIN
OFF_GATE = OFF_GD + GD_MAIN
assert RW_COLS % SMALL_W == 0 and RW_SLAB >= RW_COLS + SMALL_W and RW_SLAB % 512 == 0
assert OFF_ML % ML_WIDTH == 0 and OFF_GD % GD_QK == 0

V7X_VMEM_LIMIT = 56 * 1024 * 1024
V7X_MXU_COLS = 256
TRI_BLOCK = 16


def _cparams(sem):
    return pltpu.CompilerParams(dimension_semantics=sem, vmem_limit_bytes=V7X_VMEM_LIMIT)


def _tile(dim, pref, quantum):
    if dim <= pref:
        return dim
    t = (pref // quantum) * quantum
    while t > quantum and dim % t:
        t -= quantum
    assert dim % t == 0, (dim, pref, quantum)
    return t


def _sub_chunks(t):
    n = SUB_CHUNKS
    while (t // CHUNK) % n:
        n -= 1
    return n


def _split2(a):
    hi = a.astype(BF16)
    lo = (a - hi.astype(F32)).astype(BF16)
    return hi, lo


def _dg(a, b, ca, cb):
    return lax.dot_general(a, b, (((ca,), (cb,)), ((), ())), preferred_element_type=F32)


def _dot3(a, b, ca=1, cb=0):
    ah, al = _split2(a)
    bh, bl = _split2(b)
    return _dg(ah, bh, ca, cb) + (_dg(al, bh, ca, cb) + _dg(ah, bl, ca, cb))


def _dotp(a, b, ca, cb, passes):
    if passes == 1:
        return _dg(a.astype(BF16), b.astype(BF16), ca, cb)
    return _dot3(a, b, ca, cb)


def _tri_masks(c):
    row = lax.broadcasted_iota(jnp.int32, (c, c), 0)
    col = lax.broadcasted_iota(jnp.int32, (c, c), 1)
    return row, col


def _cumsum_rows(x):
    c = x.shape[0]
    row, col = _tri_masks(c)
    tri = jnp.where(row >= col, 1.0, 0.0).astype(BF16)
    x0 = x.astype(BF16)
    r1 = x - x0.astype(F32)
    x1 = r1.astype(BF16)
    x2 = (r1 - x1.astype(F32)).astype(BF16)
    return _dg(tri, x0, 1, 0) + (_dg(tri, x1, 1, 0) + _dg(tri, x2, 1, 0))


def _tri_solve(low, rhs, c, passes):
    return _tri_solve_many([low], [rhs], c, passes)[0]


def _tri_solve_many(lows, rhss, c, passes):
    return _tri_apply_many(_tri_factor_many(lows, c, passes), rhss, passes)


def _tri_factor_many(lows, c, passes):
    mm = lambda a, b: _dotp(a, b, 1, 0, passes)
    row, col = _tri_masks(lows[0].shape[0])
    same = (row // TRI_BLOCK) == (col // TRI_BLOCK)
    eye = jnp.where(row == col, 1.0, 0.0).astype(F32)
    ps = [jnp.where(same, -low, 0.0) for low in lows]
    offs = [jnp.where(same, 0.0, low) for low in lows]
    xs = [eye + p for p in ps]
    steps = 1
    while steps * 2 < TRI_BLOCK:
        ps = [mm(p, p) for p in ps]
        xs = [x + mm(x, p) for x, p in zip(xs, ps)]
        steps *= 2
    ms = [mm(x, off) for x, off in zip(xs, offs)]
    nblk = c // TRI_BLOCK
    terms = []
    pws = ms
    k = 1
    while k < nblk:
        terms.append(pws)
        k *= 2
        if k < nblk:
            pws = [mm(pw, pw) for pw in pws]
    return xs, terms


def _tri_apply_many(factors, rhss, passes):
    mm = lambda a, b: _dotp(a, b, 1, 0, passes)
    xs, terms = factors
    us = xs if rhss is None else [mm(x, rhs) for x, rhs in zip(xs, rhss)]
    for i in range(len(terms) - 1, 0, -1):
        us = [u + mm(t, u) for t, u in zip(terms[i], us)]
    if terms:
        us = [u - mm(t, u) for t, u in zip(terms[0], us)]
    return us


def _head_sums(x, hw):
    assert 2 * hw == 128
    lane = lax.broadcasted_iota(jnp.int32, (x.shape[0], 128), 1)
    lo = lane < hw
    out = []
    for s in range(x.shape[1] // 128):
        xs = x[:, s * 128:(s + 1) * 128]
        s_lo = jnp.sum(jnp.where(lo, xs, 0.0), axis=-1, keepdims=True)
        s_hi = jnp.sum(jnp.where(lo, 0.0, xs), axis=-1, keepdims=True)
        out.append(jnp.where(lo, s_lo, s_hi))
    return jnp.concatenate(out, axis=1)


def _softplus(x):
    return jnp.maximum(x, 0.0) + jnp.log1p(jnp.exp(-jnp.abs(x)))


def _sigmoid(x):
    return 1.0 / (1.0 + jnp.exp(-x))


def _silu(x):
    return x * _sigmoid(x)


def _shift_rows(x, prev8, k):
    xr = pltpu.roll(x, k, 0)
    pr = pltpu.roll(prev8, k, 0)
    row = lax.broadcasted_iota(jnp.int32, (8, x.shape[1]), 0)
    head = jnp.where(row < k, pr, xr[0:8])
    if x.shape[0] == 8:
        return head
    return jnp.concatenate([head, xr[8:]], axis=0)


def _rmsnorm_kernel(x_ref, g_ref, o_ref):
    x = x_ref[...]
    y = x * lax.rsqrt(jnp.mean(x * x, axis=-1, keepdims=True) + EPS)
    o_ref[...] = (y * g_ref[...]).astype(o_ref.dtype)


def _rmsnorm(x, g, out_dtype):
    m, d = x.shape
    tr = _tile(m, 256, 8)
    return pl.pallas_call(
        _rmsnorm_kernel,
        grid=(m // tr,),
        in_specs=[pl.BlockSpec((tr, d), lambda i: (i, 0)), pl.BlockSpec((1, d), lambda i: (0, 0))],
        out_specs=pl.BlockSpec((tr, d), lambda i: (i, 0)),
        out_shape=jax.ShapeDtypeStruct((m, d), out_dtype),
        compiler_params=_cparams(("parallel",)),
        name="rmsnorm",
    )(x, g.reshape(1, d))


def _mm_kernel(a_ref, w_ref, o_ref):
    o_ref[...] = jnp.dot(a_ref[...], w_ref[...], preferred_element_type=F32).astype(o_ref.dtype)


def _mm_res_kernel(a_ref, w_ref, r_ref, o_ref):
    o_ref[...] = r_ref[...] + jnp.dot(a_ref[...], w_ref[...], preferred_element_type=F32)


def _row_rstd(ss_ref, d):
    return lax.rsqrt(jnp.sum(ss_ref[...], axis=-1, keepdims=True) * (1.0 / d) + EPS)


def _mm_scaled_kernel(a_ref, w_ref, ss_ref, o_ref, *, d):
    acc = jnp.dot(a_ref[...], w_ref[...], preferred_element_type=F32)
    o_ref[...] = (acc * _row_rstd(ss_ref, d)).astype(o_ref.dtype)


def _mm_res_norm_kernel(a_ref, w_ref, r_ref, g_ref, o_ref, hg_ref, ss_ref):
    h = r_ref[...] + jnp.dot(a_ref[...], w_ref[...], preferred_element_type=F32)
    o_ref[...] = h
    hg_ref[...] = (h * g_ref[...]).astype(hg_ref.dtype)
    h2 = h * h
    part = h2[:, 0:128]
    for s in range(1, h.shape[1] // 128):
        part = part + h2[:, s * 128:(s + 1) * 128]

    @pl.when(pl.program_id(1) == 0)
    def _():
        ss_ref[...] = part

    @pl.when(pl.program_id(1) != 0)
    def _():
        ss_ref[...] = ss_ref[...] + part


def _mm_tiles(m, k, n, has_residual):
    tm = _tile(m, 1024 if k <= 4096 else 512, 8)
    tn = _tile(n, 1024 if (k <= 4096 and not has_residual) else 512, V7X_MXU_COLS if n % V7X_MXU_COLS == 0 else 128)
    return tm, tn


def _matmul(a, w, residual=None, out_dtype=F32, row_ss=None, next_gain=None):
    m, k = a.shape
    n = w.shape[1]
    tm, tn = _mm_tiles(m, k, n, residual is not None)
    in_specs = [pl.BlockSpec((tm, k), lambda i, j: (i, 0)), pl.BlockSpec((k, tn), lambda i, j: (0, j))]
    args = [a, w]
    body = _mm_kernel
    out_specs = pl.BlockSpec((tm, tn), lambda i, j: (i, j))
    out_shape = jax.ShapeDtypeStruct((m, n), out_dtype)
    if residual is not None:
        in_specs.append(pl.BlockSpec((tm, tn), lambda i, j: (i, j)))
        args.append(residual)
        body = _mm_res_kernel
        if next_gain is not None:
            in_specs.append(pl.BlockSpec((1, tn), lambda i, j: (0, j)))
            args.append(next_gain.reshape(1, n))
            body = _mm_res_norm_kernel
            out_specs = [out_specs, pl.BlockSpec((tm, tn), lambda i, j: (i, j)),
                         pl.BlockSpec((tm, 128), lambda i, j: (i, 0))]
            out_shape = [out_shape, jax.ShapeDtypeStruct((m, n), BF16), jax.ShapeDtypeStruct((m, 128), F32)]
    elif row_ss is not None:
        in_specs.append(pl.BlockSpec((tm, 128), lambda i, j: (i, 0)))
        args.append(row_ss)
        body = functools.partial(_mm_scaled_kernel, d=k)
    return pl.pallas_call(
        body,
        grid=(m // tm, n // tn),
        in_specs=in_specs,
        out_specs=out_specs,
        out_shape=out_shape,
        compiler_params=_cparams(("parallel", "arbitrary")),
        name="matmul",
    )(*args)


def _mlstm_front(q_all, k_all, v_all, pre):
    n = len(q_all)
    c = q_all[0].shape[0]
    cum = [_cumsum_rows(-_softplus(-p_)) for p_ in pre]
    pre_t = [p_.T for p_ in pre]
    cum_t = [z.T for z in cum]
    row, col = _tri_masks(c)
    causal = row >= col
    it = [(j, h) for j in range(n) for h in range(ML_HEADS)]
    q = [q_all[j][:, h * ML_DQK:(h + 1) * ML_DQK] for j, h in it]
    k = [k_all[j][:, h * ML_DQK:(h + 1) * ML_DQK] * (ML_DQK ** -0.5) for j, h in it]
    v = [v_all[j][:, h * ML_DV:(h + 1) * ML_DV] for j, h in it]
    b_col = [cum[j][:, SM_ML_F + h:SM_ML_F + h + 1] for j, h in it]
    b_row = [cum_t[j][SM_ML_F + h:SM_ML_F + h + 1, :] for j, h in it]
    i_col = [pre[j][:, SM_ML_I + h:SM_ML_I + h + 1] for j, h in it]
    i_row = [pre_t[j][SM_ML_I + h:SM_ML_I + h + 1, :] for j, h in it]
    ix = range(len(it))
    qk = [_dotp(q[x], k[x], 1, 1, ML_PASSES) for x in ix]
    dmat = [jnp.where(causal, b_col[x] - b_row[x] + i_row[x], -jnp.inf) for x in ix]
    m_loc = [jnp.max(dmat[x], axis=-1, keepdims=True) for x in ix]
    pmat = [jnp.exp(dmat[x] - m_loc[x]) * qk[x] for x in ix]
    m_end = [m_loc[x][c - 1:c, :] for x in ix]
    kw = [k[x] * jnp.exp(b_col[x][c - 1:c, :] - b_col[x] + i_col[x] - m_end[x]) for x in ix]
    pv = [_dotp(pmat[x], v[x], 1, 0, ML_PASSES) for x in ix]
    kv = [_dotp(kw[x].T, v[x], 1, 0, ML_PASSES) for x in ix]
    ps = [jnp.sum(pmat[x], axis=-1, keepdims=True) for x in ix]
    ks = [jnp.sum(kw[x], axis=0, keepdims=True) for x in ix]
    per_chunk = lambda z: [z[j * ML_HEADS:(j + 1) * ML_HEADS] for j in range(n)]
    names = ("q", "b_col", "m_loc", "m_end", "pv", "ps", "kv", "ks")
    cols = [per_chunk(z) for z in (q, b_col, m_loc, m_end, pv, ps, kv, ks)]
    return [dict(zip(names, [col_[j] for col_ in cols])) for j in range(n)]


def _mlstm_kernel(q_ref, k_ref, v_ref, og_ref, sm_ref, bias_ref, norm_ref, c0_ref, n0_ref, m0_ref,
                  y_ref, c_ref, n_ref, m_ref):
    @pl.when(pl.program_id(1) == 0)
    def _():
        c_ref[...] = c0_ref[...]
        n_ref[...] = n0_ref[...]
        m_ref[...] = m0_ref[...]

    c = CHUNK
    nsub = q_ref.shape[0] // c
    hs = range(ML_HEADS)
    chunk_rows = [slice(k * c, (k + 1) * c) for k in range(nsub)]
    fronts = _mlstm_front([q_ref[r, :] for r in chunk_rows], [k_ref[r, :] for r in chunk_rows],
                          [v_ref[r, :] for r in chunk_rows], [sm_ref[r, :] + bias_ref[...] for r in chunk_rows])
    m_old = [m_ref[0, :, h:h + 1] for h in hs]
    c_old = [c_ref[0, h] for h in hs]
    n_old = [n_ref[0, h:h + 1, :] for h in hs]
    for k in range(nsub):
        f = fronts[k]
        rows = slice(k * c, (k + 1) * c)
        qc = [_dotp(f["q"][h], c_old[h], 1, 0, ML_PASSES) for h in hs]
        mt = [jnp.maximum(f["b_col"][h] + m_old[h], f["m_loc"][h]) for h in hs]
        e_loc = [jnp.exp(f["m_loc"][h] - mt[h]) for h in hs]
        inter = [jnp.exp(f["b_col"][h] + m_old[h] - mt[h]) for h in hs]
        num = [inter[h] * qc[h] + e_loc[h] * f["pv"][h] for h in hs]
        den = [inter[h] * jnp.sum(f["q"][h] * n_old[h], axis=-1, keepdims=True) + e_loc[h] * f["ps"][h] for h in hs]
        hh = [num[h] / jnp.maximum(jnp.abs(den[h]), jnp.exp(-mt[h])) for h in hs]
        m_new = [mt[h][c - 1:c, :] for h in hs]
        scale = [jnp.exp(f["m_end"][h] - m_new[h]) for h in hs]
        dec = [jnp.exp(f["b_col"][h][c - 1:c, :] + m_old[h] - m_new[h]) for h in hs]
        c_old = [dec[h] * c_old[h] + scale[h] * f["kv"][h] for h in hs]
        n_old = [dec[h] * n_old[h] + scale[h] * f["ks"][h] for h in hs]
        m_old = m_new
        for h in hs:
            hn = hh[h] * lax.rsqrt(jnp.mean(hh[h] * hh[h], axis=-1, keepdims=True) + EPS)
            og = og_ref[rows, h * ML_DV:(h + 1) * ML_DV]
            y = hn * norm_ref[:, h * ML_DV:(h + 1) * ML_DV] * _sigmoid(og)
            y_ref[rows, h * ML_DV:(h + 1) * ML_DV] = y.astype(y_ref.dtype)
    for h in hs:
        c_ref[0, h] = c_old[h]
        n_ref[0, h:h + 1, :] = n_old[h]
        m_ref[0, :, h:h + 1] = m_old[h]


def _mlstm(p_main, p_small, bias_row, ml_norm, c0, n0, m0, bsz, t):
    step = CHUNK * _sub_chunks(t)
    nc = t // step
    rows = bsz * t
    rmap = lambda b, c: b * nc + c
    return pl.pallas_call(
        _mlstm_kernel,
        grid=(bsz, nc),
        in_specs=[
            pl.BlockSpec((step, ML_QK), lambda b, c: (rmap(b, c), OFF_ML // ML_QK)),
            pl.BlockSpec((step, ML_QK), lambda b, c: (rmap(b, c), OFF_ML // ML_QK + 1)),
            pl.BlockSpec((step, ML_WIDTH), lambda b, c: (rmap(b, c), OFF_ML // ML_WIDTH + 1)),
            pl.BlockSpec((step, ML_WIDTH), lambda b, c: (rmap(b, c), OFF_ML // ML_WIDTH + 2)),
            pl.BlockSpec((step, SMALL_W), lambda b, c: (rmap(b, c), SMALL_BLK)),
            pl.BlockSpec((1, SMALL_W), lambda b, c: (0, 0)),
            pl.BlockSpec((1, ML_WIDTH), lambda b, c: (0, 0)),
            pl.BlockSpec((1, ML_HEADS, ML_DQK, ML_DV), lambda b, c: (b, 0, 0, 0)),
            pl.BlockSpec((1, ML_HEADS, ML_DQK), lambda b, c: (b, 0, 0)),
            pl.BlockSpec((1, 1, ML_HEADS), lambda b, c: (b, 0, 0)),
        ],
        out_specs=[
            pl.BlockSpec((step, ML_WIDTH), lambda b, c: (rmap(b, c), 0)),
            pl.BlockSpec((1, ML_HEADS, ML_DQK, ML_DV), lambda b, c: (b, 0, 0, 0)),
            pl.BlockSpec((1, ML_HEADS, ML_DQK), lambda b, c: (b, 0, 0)),
            pl.BlockSpec((1, 1, ML_HEADS), lambda b, c: (b, 0, 0)),
        ],
        out_shape=[
            jax.ShapeDtypeStruct((rows, ML_WIDTH), BF16),
            jax.ShapeDtypeStruct(c0.shape, F32),
            jax.ShapeDtypeStruct(n0.shape, F32),
            jax.ShapeDtypeStruct((bsz, 1, ML_HEADS), F32),
        ],
        compiler_params=_cparams(("parallel", "arbitrary")),
        name="mlstm",
    )(p_main, p_main, p_main, p_main, p_small, bias_row, ml_norm.reshape(1, ML_WIDTH),
      c0, n0, m0.reshape(bsz, 1, ML_HEADS))


def _rwkv_front(x, prev8, mu_ref, w0_ref, w2_ref, a0_ref, a2_ref, g2_ref, kk_ref, ka_ref, rk_ref):
    c = x.shape[0]
    w = RW_WIDTH
    xprev = _shift_rows(x, prev8, 1)
    xm = x + (xprev - x) * mu_ref[...]
    rr = xm[:, 0:w]
    rk = xm[:, w:2 * w]
    rv = xm[:, 2 * w:3 * w]
    xw = xm[:, 3 * w:3 * w + RW_W_RANK]
    xa = xm[:, 3 * w + RW_W_RANK:3 * w + RW_W_RANK + RW_A_RANK]
    xg = xm[:, 3 * w + RW_W_RANK + RW_A_RANK:]

    w_pre = w0_ref[...] + _dot3(jnp.tanh(xw), w2_ref[...])
    lw = -jnp.exp(-_softplus(-w_pre) - 0.5)
    a = _sigmoid(a0_ref[...] + _dot3(xa, a2_ref[...]))
    g_out = _dot3(_sigmoid(xg), g2_ref[...])
    kk_raw = rk * kk_ref[...]
    kmod = rk * (1.0 + (a - 1.0) * ka_ref[...])
    bon = rr * kmod * rk_ref[...]

    lc = _cumsum_rows(lw)
    l_end = lc[c - 1:c, :]
    l_mid = lc[c // 2 - 1:c // 2, :]
    lcc = lc - l_mid
    p_mid = jnp.exp(l_mid)
    p_in = jnp.exp(lcc)
    p_prev = jnp.exp(lcc - lw)
    p_inv = jnp.exp(-lcc)
    p_end = jnp.exp(l_end - lc)
    p_all = jnp.exp(l_end)

    kkn = kk_raw * lax.rsqrt(_head_sums(kk_raw * kk_raw, RW_N) + 1e-6)
    bb = kkn * a
    kkp = kkn * p_prev
    rp = rr * p_in
    kd = kmod * p_inv
    bd = bb * p_inv
    k_end = kmod * p_end
    b_end = bb * p_end
    return dict(kkp=kkp, rp=rp, kd=kd, bd=bd, k_end=k_end, b_end=b_end, rv=rv, p_mid=p_mid, p_all=p_all,
                bonus=_head_sums(bon, RW_N) * rv, g_out=g_out)


def _rw_groups():
    gw = RW_GROUP * RW_N
    return [slice(g * gw, (g + 1) * gw) for g in range(RW_HEADS // RW_GROUP)]


def _rw_masks(c):
    shape = (RW_GROUP * c, RW_GROUP * RW_N)
    head_eq = lax.broadcasted_iota(jnp.int32, shape, 0) // c == lax.broadcasted_iota(jnp.int32, shape, 1) // RW_N
    tn_ = lax.broadcasted_iota(jnp.int32, (c, RW_GROUP * RW_N), 0)
    sn_ = lax.broadcasted_iota(jnp.int32, (c, RW_GROUP * RW_N), 1) % RW_N
    return head_eq, tn_ > sn_, tn_ >= sn_


def _rw_blockdiag(z, head_eq):
    return jnp.where(head_eq, jnp.concatenate([z] * RW_GROUP, axis=0), 0.0)


def _rw_rowsum(z, c):
    out = z[0:c]
    for i in range(1, RW_GROUP):
        out = out + z[i * c:(i + 1) * c]
    return out


def _rwkv_mid(f, masks):
    kkp, rp, kd, bd, rv = f["kkp"], f["rp"], f["kd"], f["bd"], f["rv"]
    c = rv.shape[0]
    head_eq, strict, incl = masks
    groups = _rw_groups()
    nt = lambda x, y: _dotp(x, y, 1, 1, RW_PASSES)
    x2 = [jnp.concatenate([kkp[:, cs], rp[:, cs]], axis=0) for cs in groups]
    ab_b = [nt(x, _rw_blockdiag(bd[:, cs], head_eq)) for x, cs in zip(x2, groups)]
    ab_k = [nt(x, _rw_blockdiag(kd[:, cs], head_eq)) for x, cs in zip(x2, groups)]
    a_b = [_rw_blockdiag(jnp.where(strict, z[0:c], 0.0), head_eq) for z in ab_b]
    return dict(
        x2=x2, v_bd=[_rw_blockdiag(rv[:, cs], head_eq) for cs in groups],
        a_b=a_b,
        a_k=[jnp.where(strict, z[0:c], 0.0) for z in ab_k],
        r_k=[jnp.where(incl, z[c:], 0.0) for z in ab_k],
        r_b=[jnp.where(incl, z[c:], 0.0) for z in ab_b])


def _rwkv_chain(f, g, s_olds, ln_ref, masks):
    k_end, b_end, rv, p_mid, p_all = f["k_end"], f["b_end"], f["rv"], f["p_mid"], f["p_all"]
    c = rv.shape[0]
    groups = _rw_groups()
    head_eq = masks[0]
    nt = lambda x, y: _dotp(x, y, 1, 1, RW_PASSES)
    nn = lambda x, y: _dotp(x, y, 1, 0, RW_PASSES)
    v_bd = g["v_bd"]
    ab_s = [nt(x, _rw_blockdiag(s * p_mid[:, cs], head_eq))
            for x, s, cs in zip(g["x2"], s_olds, groups)]
    rhs = [_rw_blockdiag(z[0:c] + nn(ak, v), head_eq) for z, ak, v in zip(ab_s, g["a_k"], v_bd)]
    us = [nn(t_, r_) for t_, r_ in zip(g["t_inv"], rhs)]
    ygs = [z[c:] + nn(rk_, v) - nn(rb_, u) for z, rk_, v, rb_, u in zip(ab_s, g["r_k"], v_bd, g["r_b"], us)]
    upds = [nn(jnp.concatenate([rv[:, cs], -_rw_rowsum(u, c)], axis=0).T,
               jnp.concatenate([k_end[:, cs], b_end[:, cs]], axis=0)) for cs, u in zip(groups, us)]
    s_news = [s * p_all[:, cs] + _rw_rowsum(jnp.where(head_eq, upd, 0.0), c)
              for cs, s, upd in zip(groups, s_olds, upds)]

    yh = jnp.concatenate(ygs, axis=1)
    mu_ = _head_sums(yh, RW_N) * (1.0 / RW_N)
    yc = yh - mu_
    var = _head_sums(yc * yc, RW_N) * (1.0 / RW_N)
    yn = yc * lax.rsqrt(var + RW_GN_EPS)
    return (yn * ln_ref[...] + f["bonus"]) * f["g_out"], s_news


def _rwkv_kernel(p_ref, shift_ref, mu_ref, w0_ref, w2_ref, a0_ref, a2_ref, g2_ref, kk_ref, ka_ref,
                 rk_ref, ln_ref, s0_ref, y_ref, s_ref, prev_ref):
    @pl.when(pl.program_id(1) == 0)
    def _():
        s_ref[...] = s0_ref[...]
        prev_ref[...] = jnp.broadcast_to(shift_ref[0], prev_ref.shape)

    c = CHUNK
    nsub = p_ref.shape[0] // c
    masks = _rw_masks(c)
    fronts = []
    for k in range(nsub):
        last8 = prev_ref[...] if k == 0 else p_ref[k * c - 8:k * c, :]
        fronts.append(_rwkv_front(p_ref[k * c:(k + 1) * c, :], last8, mu_ref, w0_ref, w2_ref, a0_ref, a2_ref,
                                  g2_ref, kk_ref, ka_ref, rk_ref))
    mids = [_rwkv_mid(f, masks) for f in fronts]
    lows = [low for g in mids for low in g["a_b"]]
    t_inv = _tri_apply_many(_tri_factor_many(lows, c, RW_PASSES), None, RW_PASSES)
    groups = _rw_groups()
    for k, g in enumerate(mids):
        g["t_inv"] = t_inv[k * len(groups):(k + 1) * len(groups)]
    states = [s_ref[0, :, cs] for cs in groups]
    for k in range(nsub):
        out, states = _rwkv_chain(fronts[k], mids[k], states, ln_ref, masks)
        y_ref[k * c:(k + 1) * c, :] = out.astype(y_ref.dtype)
    prev_ref[...] = p_ref[nsub * c - 8:nsub * c, :]
    for cs, s in zip(groups, states):
        s_ref[0, :, cs] = s


def _rwkv(p_rw, shift0, s0, p, bsz, t):
    step = CHUNK * _sub_chunks(t)
    nc = t // step
    rows = bsz * t
    rmap = lambda b, c: b * nc + c
    full = lambda shape: pl.BlockSpec(shape, lambda b, c: (0,) * len(shape))
    w = RW_WIDTH
    s_nat = s0.transpose(0, 2, 1, 3).reshape(bsz, RW_N, w)
    y, s = pl.pallas_call(
        _rwkv_kernel,
        grid=(bsz, nc),
        in_specs=[
            pl.BlockSpec((step, RW_COLS), lambda b, c: (rmap(b, c), 0)),
            pl.BlockSpec((1, 1, RW_COLS), lambda b, c: (b, 0, 0)),
            full((1, RW_COLS)), full((1, w)), full((RW_W_RANK, w)), full((1, w)), full((RW_A_RANK, w)),
            full((RW_G_RANK, w)), full((1, w)), full((1, w)), full((1, w)), full((1, w)),
            pl.BlockSpec((1, RW_N, w), lambda b, c: (b, 0, 0)),
        ],
        out_specs=[
            pl.BlockSpec((step, w), lambda b, c: (rmap(b, c), 0)),
            pl.BlockSpec((1, RW_N, w), lambda b, c: (b, 0, 0)),
        ],
        out_shape=[jax.ShapeDtypeStruct((rows, w), BF16), jax.ShapeDtypeStruct(s_nat.shape, F32)],
        scratch_shapes=[pltpu.VMEM((8, RW_COLS), F32)],
        compiler_params=_cparams(("parallel", "arbitrary")),
        name="rwkv7",
    )(p_rw, shift0.reshape(bsz, 1, RW_COLS), p["rw_mu"].reshape(1, RW_COLS), p["rw_w0"].reshape(1, w),
      p["rw_w2"], p["rw_a0"].reshape(1, w), p["rw_a2"], p["rw_g2"], p["rw_k_k"].reshape(1, w),
      p["rw_k_a"].reshape(1, w), p["rw_r_k"].reshape(1, w), p["rw_ln"].reshape(1, w), s_nat)
    return y, s.reshape(bsz, RW_N, RW_HEADS, RW_N).transpose(0, 2, 1, 3)


def _gdn_front(xs, prevs, sm, cw_ref, alog_ref, dtb_ref):
    c = xs[0].shape[0]
    acts = []
    for sec, (x, prev8) in enumerate(zip(xs, prevs)):
        cs = slice(sec * GD_QK, (sec + 1) * GD_QK)
        y = _shift_rows(x, prev8, GD_CONV - 1) * cw_ref[0:1, cs]
        for i in range(1, GD_CONV - 1):
            y = y + _shift_rows(x, prev8, GD_CONV - 1 - i) * cw_ref[i:i + 1, cs]
        y = y + x * cw_ref[GD_CONV - 1:GD_CONV, cs]
        acts.append(_silu(y))
    qa, ka, va = acts

    beta = _sigmoid(sm)
    g = -jnp.exp(alog_ref[...]) * _softplus(sm + dtb_ref[...])
    gc = _cumsum_rows(g)
    gc_t = gc.T

    gr = GD_GROUP * c
    gk = GD_GROUP * GD_DK
    r4 = lax.broadcasted_iota(jnp.int32, (gr, gr), 0)
    c4 = lax.broadcasted_iota(jnp.int32, (gr, gr), 1)
    t_minus_s = jnp.where((r4 // c) == (c4 // c), (r4 % c) - (c4 % c), -1)
    m_strict = t_minus_s > 0
    m_incl = t_minus_s >= 0
    head_eq = (lax.broadcasted_iota(jnp.int32, (gr, gk), 0) // c
               == lax.broadcasted_iota(jnp.int32, (gr, gk), 1) // GD_DK)

    def rep(z):
        return jnp.concatenate([z] * GD_GROUP, axis=0)

    def stack(parts):
        return jnp.concatenate(parts, axis=0)

    kn = []
    qn = []
    for h in range(GD_HEADS):
        ks = slice(h * GD_DK, (h + 1) * GD_DK)
        q = qa[:, ks]
        qn.append(q * lax.rsqrt(jnp.sum(q * q, axis=-1, keepdims=True) + 1e-6) * (GD_DK ** -0.5))
        k = ka[:, ks]
        kn.append(k * lax.rsqrt(jnp.sum(k * k, axis=-1, keepdims=True) + 1e-6))
    groups = [range(g0, g0 + GD_GROUP) for g0 in range(0, GD_HEADS, GD_GROUP)]
    k_nat = [jnp.concatenate([kn[h] for h in hs], axis=1) for hs in groups]
    q_nat = [jnp.concatenate([qn[h] for h in hs], axis=1) for hs in groups]
    k_bd = [jnp.where(head_eq, rep(kk_), 0.0) for kk_ in k_nat]
    v_st = [stack([va[:, h * GD_DV:(h + 1) * GD_DV] for h in hs]) for hs in groups]
    b_col = [stack([beta[:, SM_GD_B + h:SM_GD_B + h + 1] for h in hs]) for hs in groups]
    g_col = [stack([gc[:, SM_GD_A + h:SM_GD_A + h + 1] for h in hs]) for hs in groups]
    g_row = [jnp.concatenate([gc_t[SM_GD_A + h:SM_GD_A + h + 1, :] for h in hs], axis=1) for hs in groups]

    dec = [jnp.exp(jnp.where(m_incl, gc_ - gr_, -jnp.inf)) for gc_, gr_ in zip(g_col, g_row)]
    kq = [_dotp(stack([kk_, qq_]), kb, 1, 1, GD_PASSES) for kk_, qq_, kb in zip(k_nat, q_nat, k_bd)]
    low = [jnp.where(m_strict, b * rep(z[0:c]) * d, 0.0) for b, z, d in zip(b_col, kq, dec)]
    gl = [gc[c - 1:c, SM_GD_A + h:SM_GD_A + h + 1] for h in range(GD_HEADS)]
    return dict(
        kq_lhs=[stack([kn[h], qn[h]]) for h in range(GD_HEADS)], v_st=v_st, b_col=b_col,
        eg=[jnp.exp(gc_) for gc_ in g_col],
        low=low,
        qkt=[rep(z[c:]) * d for z, d in zip(kq, dec)],
        kw_t=[(kn[h] * jnp.exp(gl[h] - gc[:, SM_GD_A + h:SM_GD_A + h + 1])).T for h in range(GD_HEADS)],
        s_decay=[jnp.exp(gl[h]) for h in range(GD_HEADS)])


def _gdn_chain(f, s_old):
    c = f["kq_lhs"][0].shape[0] // 2
    stack = lambda parts: jnp.concatenate(parts, axis=0)
    groups = [range(g0, g0 + GD_GROUP) for g0 in range(0, GD_HEADS, GD_GROUP)]
    kqs = [_dotp(f["kq_lhs"][h], s_old[h], 1, 0, GD_PASSES) for h in range(GD_HEADS)]
    ks = [stack([kqs[h][0:c] for h in hs]) for hs in groups]
    qs = [stack([kqs[h][c:] for h in hs]) for hs in groups]
    rhs = [b * (v - e * z) for b, v, e, z in zip(f["b_col"], f["v_st"], f["eg"], ks)]
    us = [_dotp(t_, r_, 1, 0, GD_PASSES) for t_, r_ in zip(f["t_inv"], rhs)]
    os_ = [e * z + _dotp(qk, u, 1, 0, GD_PASSES) for e, z, qk, u in zip(f["eg"], qs, f["qkt"], us)]
    outs = []
    s_new = []
    for gi, hs in enumerate(groups):
        for i, h in enumerate(hs):
            rs = slice(i * c, (i + 1) * c)
            s_new.append(f["s_decay"][h] * s_old[h] + _dotp(f["kw_t"][h], us[gi][rs], 1, 0, GD_PASSES))
            outs.append(os_[gi][rs])
    return outs, s_new


def _gdn_kernel(q_ref, k_ref, v_ref, z_ref, sm_ref, cw_ref, conv0_ref, alog_ref, dtb_ref, norm_ref, s0_ref,
                y_ref, s_ref, prev_ref):
    @pl.when(pl.program_id(1) == 0)
    def _():
        s_ref[...] = s0_ref[...]
        prev_ref[...] = jnp.zeros(prev_ref.shape, F32)
        prev_ref[8 - (GD_CONV - 1):8, :] = conv0_ref[0]

    c = CHUNK
    nsub = q_ref.shape[0] // c
    refs = (q_ref, k_ref, v_ref)
    fronts = []
    for k in range(nsub):
        xs = [r[k * c:(k + 1) * c, :] for r in refs]
        if k == 0:
            prevs = [prev_ref[:, sec * GD_QK:(sec + 1) * GD_QK] for sec in range(3)]
        else:
            prevs = [r[k * c - 8:k * c, :] for r in refs]
        fronts.append(_gdn_front(xs, prevs, sm_ref[k * c:(k + 1) * c, :], cw_ref, alog_ref, dtb_ref))
    lows = [low for f in fronts for low in f["low"]]
    t_inv = _tri_apply_many(_tri_factor_many(lows, c, GD_PASSES), None, GD_PASSES)
    per = len(fronts[0]["low"])
    for k, f in enumerate(fronts):
        f["t_inv"] = t_inv[k * per:(k + 1) * per]
    states = [s_ref[0, h] for h in range(GD_HEADS)]
    for k in range(nsub):
        outs, states = _gdn_chain(fronts[k], states)
        for h in range(GD_HEADS):
            o_h = outs[h]
            og = o_h * lax.rsqrt(jnp.mean(o_h * o_h, axis=-1, keepdims=True) + EPS) * norm_ref[...]
            vs = slice(h * GD_DV, (h + 1) * GD_DV)
            y_ref[k * c:(k + 1) * c, vs] = (og * _silu(z_ref[k * c:(k + 1) * c, vs])).astype(y_ref.dtype)
    for h in range(GD_HEADS):
        s_ref[0, h] = states[h]
    for sec, r in enumerate(refs):
        prev_ref[:, sec * GD_QK:(sec + 1) * GD_QK] = r[nsub * c - 8:nsub * c, :]


def _gdn(p_main, p_small, conv_w, conv0, alog_row, dtb_row, gd_norm, s0, bsz, t):
    step = CHUNK * _sub_chunks(t)
    nc = t // step
    rows = bsz * t
    rmap = lambda b, c: b * nc + c
    base = OFF_GD // GD_QK
    full = lambda shape: pl.BlockSpec(shape, lambda b, c: (0,) * len(shape))
    return pl.pallas_call(
        _gdn_kernel,
        grid=(bsz, nc),
        in_specs=[
            pl.BlockSpec((step, GD_QK), lambda b, c: (rmap(b, c), base)),
            pl.BlockSpec((step, GD_QK), lambda b, c: (rmap(b, c), base + 1)),
            pl.BlockSpec((step, GD_WIDTH), lambda b, c: (rmap(b, c), base + 2)),
            pl.BlockSpec((step, GD_WIDTH), lambda b, c: (rmap(b, c), base + 3)),
            pl.BlockSpec((step, SMALL_W), lambda b, c: (rmap(b, c), SMALL_BLK)),
            full((GD_CONV, GD_QKV)),
            pl.BlockSpec((1, GD_CONV - 1, GD_QKV), lambda b, c: (b, 0, 0)),
            full((1, SMALL_W)), full((1, SMALL_W)), full((1, GD_DV)),
            pl.BlockSpec((1, GD_HEADS, GD_DK, GD_DV), lambda b, c: (b, 0, 0, 0)),
        ],
        out_specs=[
            pl.BlockSpec((step, GD_WIDTH), lambda b, c: (rmap(b, c), 0)),
            pl.BlockSpec((1, GD_HEADS, GD_DK, GD_DV), lambda b, c: (b, 0, 0, 0)),
        ],
        out_shape=[jax.ShapeDtypeStruct((rows, GD_WIDTH), BF16), jax.ShapeDtypeStruct(s0.shape, F32)],
        scratch_shapes=[pltpu.VMEM((8, GD_QKV), F32)],
        compiler_params=_cparams(("parallel", "arbitrary")),
        name="gdn",
    )(p_main, p_main, p_main, p_main, p_small, conv_w, conv0, alog_row, dtb_row, gd_norm.reshape(1, GD_DV), s0)


def _merge_kernel(y0_ref, y1_ref, y2_ref, w_ref, g0_ref, g1_ref, g2_ref, o_ref):
    acc = _sigmoid(g0_ref[...]) * jnp.dot(y0_ref[...], w_ref[0], preferred_element_type=F32)
    acc = acc + _sigmoid(g1_ref[...]) * jnp.dot(y1_ref[...], w_ref[1], preferred_element_type=F32)
    acc = acc + _sigmoid(g2_ref[...]) * jnp.dot(y2_ref[...], w_ref[2], preferred_element_type=F32)
    o_ref[...] = acc.astype(o_ref.dtype)


def _merge(ys, w_branch, p_main, d):
    m = ys[0].shape[0]
    tm = _tile(m, 1024, 8)
    tn = _tile(d, 512, 128)
    gate0 = OFF_GATE
    assert gate0 % tn == 0
    gb = gate0 // tn
    nb = d // tn
    yspec = pl.BlockSpec((tm, BR_WIDTH), lambda i, j: (i, 0))
    gspec = lambda b: pl.BlockSpec((tm, tn), lambda i, j: (i, gb + b * nb + j))
    return pl.pallas_call(
        _merge_kernel,
        grid=(m // tm, nb),
        in_specs=[yspec, yspec, yspec, pl.BlockSpec((3, BR_WIDTH, tn), lambda i, j: (0, 0, j)),
                  gspec(0), gspec(1), gspec(2)],
        out_specs=pl.BlockSpec((tm, tn), lambda i, j: (i, j)),
        out_shape=jax.ShapeDtypeStruct((m, d), BF16),
        compiler_params=_cparams(("parallel", "arbitrary")),
        name="merge",
    )(ys[0], ys[1], ys[2], w_branch, p_main, p_main, p_main)


def _attn_kernel(q_ref, k_ref, v_ref, o_ref):
    for h in range(CA_HEADS):
        hs = slice(h * CA_HEAD_DIM, (h + 1) * CA_HEAD_DIM)
        s = _dg(q_ref[:, hs], k_ref[0, :, hs], 1, 1) * (CA_HEAD_DIM ** -0.5)
        s = s - jnp.max(s, axis=-1, keepdims=True)
        e = jnp.exp(s)
        pr = e / jnp.sum(e, axis=-1, keepdims=True)
        o = jnp.dot(pr.astype(BF16), v_ref[0, :, hs], preferred_element_type=F32)
        o_ref[:, hs] = o.astype(o_ref.dtype)


def _attention(q, mem_k, mem_v, bsz, t):
    tq = _tile(t, 512, 8)
    nt = t // tq
    n_mem = mem_k.shape[1]
    return pl.pallas_call(
        _attn_kernel,
        grid=(bsz, nt),
        in_specs=[
            pl.BlockSpec((tq, CA_WIDTH), lambda b, i: (b * nt + i, 0)),
            pl.BlockSpec((1, n_mem, CA_WIDTH), lambda b, i: (b, 0, 0)),
            pl.BlockSpec((1, n_mem, CA_WIDTH), lambda b, i: (b, 0, 0)),
        ],
        out_specs=pl.BlockSpec((tq, CA_WIDTH), lambda b, i: (b * nt + i, 0)),
        out_shape=jax.ShapeDtypeStruct((bsz * t, CA_WIDTH), BF16),
        compiler_params=_cparams(("parallel", "arbitrary")),
        name="mem_attention",
    )(q, mem_k, mem_v)


def _ffn_up_kernel(u_ref, ss_ref, wa_ref, wg_ref, cwa_ref, cwg_ref, c0a_ref, c0g_ref, act_ref, ta_ref, tg_ref,
                   carry_ref, *, tiles_per_seq, sub):
    i = pl.program_id(0)
    j = pl.program_id(1)
    tm, tn = act_ref.shape

    @pl.when(i == 0)
    def _():
        carry_ref[j] = jnp.zeros(carry_ref.shape[1:], F32)

    first = (i % tiles_per_seq) == 0
    pad = jnp.zeros((8 - (FFN_CONV - 1), tn), F32)
    carried = carry_ref[j]
    prev_a = jnp.where(first, jnp.concatenate([pad, c0a_ref[0]], axis=0), carried[0:8])
    prev_g = jnp.where(first, jnp.concatenate([pad, c0g_ref[0]], axis=0), carried[8:16])
    rstd = _row_rstd(ss_ref, u_ref.shape[1])

    def up(r):
        ur = u_ref[r * sub:(r + 1) * sub, :]
        sc = rstd[r * sub:(r + 1) * sub]
        return (jnp.dot(ur, wa_ref[...], preferred_element_type=F32) * sc,
                jnp.dot(ur, wg_ref[...], preferred_element_type=F32) * sc)

    nxt = up(0)
    for r in range(tm // sub):
        rows = slice(r * sub, (r + 1) * sub)
        za, zg = nxt
        if r + 1 < tm // sub:
            nxt = up(r + 1)
        fa =(_shift_rows(za, prev_a, 2) * cwa_ref[0:1] + _shift_rows(za, prev_a, 1) * cwa_ref[1:2]
              + za * cwa_ref[2:3])
        fg = (_shift_rows(zg, prev_g, 2) * cwg_ref[0:1] + _shift_rows(zg, prev_g, 1) * cwg_ref[1:2]
              + zg * cwg_ref[2:3])
        act_ref[rows, :] = (_silu(fg) * fa).astype(act_ref.dtype)
        prev_a = za[sub - 8:sub]
        prev_g = zg[sub - 8:sub]
    carry_ref[j] = jnp.concatenate([prev_a, prev_g], axis=0)
    ta_ref[0] = prev_a[8 - (FFN_CONV - 1):8]
    tg_ref[0] = prev_g[8 - (FFN_CONV - 1):8]


def _ffn_up_act(u, row_ss, w_up, conv0, conv_w, bsz, t):
    m, d = u.shape
    f = w_up.shape[1] // 2
    tm = _tile(t, FFN_ROW_TILE, 8)
    sub = _tile(tm, FFN_SUB_ROWS, 8)
    tn = _tile(f, 512, 128)
    nj = f // tn
    tps = t // tm
    act, ta, tg = pl.pallas_call(
        functools.partial(_ffn_up_kernel, tiles_per_seq=tps, sub=sub),
        grid=(m // tm, nj),
        in_specs=[
            pl.BlockSpec((tm, d), lambda i, j: (i, 0)),
            pl.BlockSpec((tm, 128), lambda i, j: (i, 0)),
            pl.BlockSpec((d, tn), lambda i, j: (0, j)),
            pl.BlockSpec((d, tn), lambda i, j: (0, nj + j)),
            pl.BlockSpec((FFN_CONV, tn), lambda i, j: (0, j)),
            pl.BlockSpec((FFN_CONV, tn), lambda i, j: (0, nj + j)),
            pl.BlockSpec((1, FFN_CONV - 1, tn), lambda i, j: (i // tps, 0, j)),
            pl.BlockSpec((1, FFN_CONV - 1, tn), lambda i, j: (i // tps, 0, nj + j)),
        ],
        out_specs=[
            pl.BlockSpec((tm, tn), lambda i, j: (i, j)),
            pl.BlockSpec((1, FFN_CONV - 1, tn), lambda i, j: (i // tps, 0, j)),
            pl.BlockSpec((1, FFN_CONV - 1, tn), lambda i, j: (i // tps, 0, j)),
        ],
        out_shape=[jax.ShapeDtypeStruct((m, f), BF16),
                   jax.ShapeDtypeStruct((bsz, FFN_CONV - 1, f), F32),
                   jax.ShapeDtypeStruct((bsz, FFN_CONV - 1, f), F32)],
        scratch_shapes=[pltpu.VMEM((nj, 16, tn), F32)],
        compiler_params=_cparams(("arbitrary", "arbitrary")),
        name="ffn_up_conv_act",
    )(u, row_ss, w_up, w_up, conv_w, conv_w, conv0, conv0)
    return act, jnp.concatenate([ta, tg], axis=-1)


def _ffn_up_short_kernel(u_ref, ss_ref, wa_ref, wg_ref, cwa_ref, cwg_ref, c0a_ref, c0g_ref, act_ref, ta_ref, tg_ref,
                         *, t, sub):
    tm, tn = act_ref.shape
    pad = jnp.zeros((8 - (FFN_CONV - 1), tn), F32)
    rstd = _row_rstd(ss_ref, u_ref.shape[1])

    def up(r):
        ur = u_ref[r * sub:(r + 1) * sub, :]
        sc = rstd[r * sub:(r + 1) * sub]
        return (jnp.dot(ur, wa_ref[...], preferred_element_type=F32) * sc,
                jnp.dot(ur, wg_ref[...], preferred_element_type=F32) * sc)

    nxt = up(0)
    for r in range(tm // sub):
        za_all, zg_all = nxt
        if r + 1 < tm // sub:
            nxt = up(r + 1)
        for q in range(sub // t):
            s = r * (sub // t) + q
            za = za_all[q * t:(q + 1) * t]
            zg = zg_all[q * t:(q + 1) * t]
            prev_a = jnp.concatenate([pad, c0a_ref[s]], axis=0)
            prev_g = jnp.concatenate([pad, c0g_ref[s]], axis=0)
            fa = (_shift_rows(za, prev_a, 2) * cwa_ref[0:1] + _shift_rows(za, prev_a, 1) * cwa_ref[1:2]
                  + za * cwa_ref[2:3])
            fg = (_shift_rows(zg, prev_g, 2) * cwg_ref[0:1] + _shift_rows(zg, prev_g, 1) * cwg_ref[1:2]
                  + zg * cwg_ref[2:3])
            act_ref[s * t:(s + 1) * t, :] = (_silu(fg) * fa).astype(act_ref.dtype)
            ta_ref[s] = za[t - (FFN_CONV - 1):t]
            tg_ref[s] = zg[t - (FFN_CONV - 1):t]


def _ffn_up_act_short(u, row_ss, w_up, conv0, conv_w, bsz, t):
    m, d = u.shape
    f = w_up.shape[1] // 2
    spt = _tile(bsz, max(1, FFN_ROW_TILE // t), 1)
    tm = spt * t
    sub = t * _tile(spt, max(1, FFN_SHORT_SUB_ROWS // t), 1)
    tn = _tile(f, 512, 128)
    nj = f // tn
    act, ta, tg = pl.pallas_call(
        functools.partial(_ffn_up_short_kernel, t=t, sub=sub),
        grid=(m // tm, nj),
        in_specs=[
            pl.BlockSpec((tm, d), lambda i, j: (i, 0)),
            pl.BlockSpec((tm, 128), lambda i, j: (i, 0)),
            pl.BlockSpec((d, tn), lambda i, j: (0, j)),
            pl.BlockSpec((d, tn), lambda i, j: (0, nj + j)),
            pl.BlockSpec((FFN_CONV, tn), lambda i, j: (0, j)),
            pl.BlockSpec((FFN_CONV, tn), lambda i, j: (0, nj + j)),
            pl.BlockSpec((spt, FFN_CONV - 1, tn), lambda i, j: (i, 0, j)),
            pl.BlockSpec((spt, FFN_CONV - 1, tn), lambda i, j: (i, 0, nj + j)),
        ],
        out_specs=[
            pl.BlockSpec((tm, tn), lambda i, j: (i, j)),
            pl.BlockSpec((spt, FFN_CONV - 1, tn), lambda i, j: (i, 0, j)),
            pl.BlockSpec((spt, FFN_CONV - 1, tn), lambda i, j: (i, 0, j)),
        ],
        out_shape=[jax.ShapeDtypeStruct((m, f), BF16),
                   jax.ShapeDtypeStruct((bsz, FFN_CONV - 1, f), F32),
                   jax.ShapeDtypeStruct((bsz, FFN_CONV - 1, f), F32)],
        compiler_params=_cparams(("parallel", "arbitrary")),
        name="ffn_up_conv_act_short",
    )(u, row_ss, w_up, w_up, conv_w, conv_w, conv0, conv0)
    return act, jnp.concatenate([ta, tg], axis=-1)


def _prep_layer(p, d):
    w_in = p["w_in"]
    o_ml = 0
    o_if = ML_MAIN
    o_rw = o_if + 2 * ML_HEADS
    o_gd = o_rw + RW_COLS
    o_ba = o_gd + GD_MAIN
    o_gate = o_ba + 2 * GD_HEADS
    w_main = jnp.concatenate(
        [w_in[:, o_ml:o_ml + ML_MAIN], w_in[:, o_gd:o_gd + GD_MAIN], w_in[:, o_gate:]], axis=1).astype(BF16)
    n_small = 2 * ML_HEADS + 2 * GD_HEADS
    w_rw = jnp.concatenate(
        [w_in[:, o_rw:o_rw + RW_COLS], w_in[:, o_if:o_if + 2 * ML_HEADS], w_in[:, o_ba:o_ba + 2 * GD_HEADS],
         jnp.zeros((d, RW_SLAB - RW_COLS - n_small), F32)], axis=1).astype(BF16)
    zrow = jnp.zeros((SMALL_W,), F32)
    q = dict(p)
    q.update(
        w_main=w_main, w_rw=w_rw,
        ml_bias_row=zrow.at[SM_ML_I:SM_ML_I + 2 * ML_HEADS].set(p["ml_b_if"]).reshape(1, SMALL_W),
        gd_alog_row=zrow.at[SM_GD_A:SM_GD_A + GD_HEADS].set(p["gd_a_log"]).reshape(1, SMALL_W),
        gd_dtb_row=zrow.at[SM_GD_A:SM_GD_A + GD_HEADS].set(p["gd_dt_bias"]).reshape(1, SMALL_W),
        w_branch_b=p["w_branch"].astype(BF16), w_out_b=p["w_out"].astype(BF16),
        w_ca_q_b=p["w_ca_q"].astype(BF16), w_ca_kv_b=p["w_ca_kv"].astype(BF16),
        w_ca_o_b=p["w_ca_o"].astype(BF16), w_up_b=p["w_up"].astype(BF16), w_down_b=p["w_down"].astype(BF16))
    return q


def _layer(h, mixed, mem_k, mem_v, st, p, next_gain, bsz, t):
    d = h.shape[1]
    if mixed is None:
        u, ss = _rmsnorm(h, p["g_mix"], BF16), None
    else:
        u, ss = mixed
    p_main = _matmul(u, p["w_main"], row_ss=ss)
    p_rw = p_small = _matmul(u, p["w_rw"], row_ss=ss)

    y_ml, ml_c, ml_n, ml_m = _mlstm(p_main, p_small, p["ml_bias_row"], p["ml_norm"],
                                    st["ml_C"], st["ml_n"], st["ml_m"], bsz, t)
    y_rw, rw_s = _rwkv(p_rw, st["rw_shift"], st["rw_S"], p, bsz, t)
    y_gd, gd_s = _gdn(p_main, p_small, p["gd_conv_w"], st["gd_conv"], p["gd_alog_row"], p["gd_dtb_row"],
                      p["gd_norm"], st["gd_S"], bsz, t)
    merged = _merge((y_ml, y_rw, y_gd), p["w_branch_b"], p_main, d)
    h, u, ss = _matmul(merged, p["w_out_b"], residual=h, next_gain=p["g_ca"])

    q = _matmul(u, p["w_ca_q_b"], out_dtype=BF16, row_ss=ss)
    o = _attention(q, mem_k, mem_v, bsz, t)
    h, u, ss = _matmul(o, p["w_ca_o_b"], residual=h, next_gain=p["g_ffn"])

    if t >= FFN_FUSE_MIN_T:
        act, ffn_conv = _ffn_up_act(u, ss, p["w_up_b"], st["ffn_conv"], p["ffn_conv_w"], bsz, t)
    else:
        act, ffn_conv = _ffn_up_act_short(u, ss, p["w_up_b"], st["ffn_conv"], p["ffn_conv_w"], bsz, t)
    if next_gain is None:
        h, mixed_next = _matmul(act, p["w_down_b"], residual=h), None
    else:
        h, u, ss = _matmul(act, p["w_down_b"], residual=h, next_gain=next_gain)
        mixed_next = (u, ss)

    gd0 = OFF_GD
    new_st = dict(
        ml_C=ml_c, ml_n=ml_n, ml_m=ml_m.reshape(bsz, ML_HEADS), rw_S=rw_s,
        rw_shift=p_rw.reshape(bsz, t, -1)[:, t - 1, :RW_COLS],
        gd_S=gd_s,
        gd_conv=p_main.reshape(bsz, t, -1)[:, t - (GD_CONV - 1):, gd0:gd0 + GD_QKV],
        ffn_conv=ffn_conv)
    return h, mixed_next, new_st


def _zero_state(bsz, d_ff2):
    return dict(
        ml_C=jnp.zeros((bsz, ML_HEADS, ML_DQK, ML_DV), F32), ml_n=jnp.zeros((bsz, ML_HEADS, ML_DQK), F32),
        ml_m=jnp.zeros((bsz, ML_HEADS), F32), rw_S=jnp.zeros((bsz, RW_HEADS, RW_N, RW_N), F32),
        rw_shift=jnp.zeros((bsz, RW_COLS), F32), gd_S=jnp.zeros((bsz, GD_HEADS, GD_DK, GD_DV), F32),
        gd_conv=jnp.zeros((bsz, GD_CONV - 1, GD_QKV), F32), ffn_conv=jnp.zeros((bsz, FFN_CONV - 1, d_ff2), F32))


def kernel(x_prompt, x_sample, cache_mem_k, cache_mem_v, state_mlstm_C, state_mlstm_n, state_mlstm_m, state_rwkv_S, state_rwkv_shift, state_gdn_S, state_gdn_conv, state_ffn_conv, mem_prompt, g_mix, w_in, ml_b_if, ml_norm, rw_mu, rw_w0, rw_w2, rw_a0, rw_a2, rw_g2, rw_k_k, rw_k_a, rw_r_k, rw_ln, gd_conv_w, gd_a_log, gd_dt_bias, gd_norm, w_branch, w_out, g_ca, g_mem, w_ca_q, w_ca_kv, w_ca_o, g_ffn, w_up, ffn_conv_w, w_down, g_final):
    bp, tp, d = x_prompt.shape
    bs, ts, _ = x_sample.shape
    depth = w_in.shape[0]
    n_mem = mem_prompt.shape[1]
    assert tp % CHUNK == 0 and ts % CHUNK == 0
    stacked = dict(g_mix=g_mix, w_in=w_in, ml_b_if=ml_b_if, ml_norm=ml_norm, rw_mu=rw_mu, rw_w0=rw_w0,
                   rw_w2=rw_w2, rw_a0=rw_a0, rw_a2=rw_a2, rw_g2=rw_g2, rw_k_k=rw_k_k, rw_k_a=rw_k_a,
                   rw_r_k=rw_r_k, rw_ln=rw_ln, gd_conv_w=gd_conv_w, gd_a_log=gd_a_log, gd_dt_bias=gd_dt_bias,
                   gd_norm=gd_norm, w_branch=w_branch, w_out=w_out, g_ca=g_ca, g_mem=g_mem, w_ca_q=w_ca_q,
                   w_ca_kv=w_ca_kv, w_ca_o=w_ca_o, g_ffn=g_ffn, w_up=w_up, ffn_conv_w=ffn_conv_w,
                   w_down=w_down)
    keys = ("ml_C", "ml_n", "ml_m", "rw_S", "rw_shift", "gd_S", "gd_conv", "ffn_conv")
    new_p = {k: [] for k in keys}
    new_s = {k: [] for k in keys}
    mem_k_list, mem_v_list = [], []
    hp = x_prompt.reshape(bp * tp, d)
    hs = x_sample.reshape(bs * ts, d)
    mem2d = mem_prompt.reshape(bp * n_mem, d)
    mixed_p = mixed_s = None
    for l in range(depth):
        p = _prep_layer({k: v[l] for k, v in stacked.items()}, d)
        next_gain = g_mix[l + 1] if l + 1 < depth else None
        kv = _matmul(_rmsnorm(mem2d, p["g_mem"], BF16), p["w_ca_kv_b"])
        mk = kv[:, :CA_WIDTH].reshape(bp, n_mem, CA_WIDTH)
        mv = kv[:, CA_WIDTH:].reshape(bp, n_mem, CA_WIDTH)
        hp, mixed_p, stp = _layer(hp, mixed_p, mk.astype(BF16), mv.astype(BF16), _zero_state(bp, w_up.shape[2]),
                                  p, next_gain, bp, tp)
        mem_k_list.append(mk.reshape(bp, n_mem, CA_HEADS, CA_HEAD_DIM))
        mem_v_list.append(mv.reshape(bp, n_mem, CA_HEADS, CA_HEAD_DIM))
        st_in = dict(ml_C=state_mlstm_C[l], ml_n=state_mlstm_n[l], ml_m=state_mlstm_m[l],
                     rw_S=state_rwkv_S[l], rw_shift=state_rwkv_shift[l], gd_S=state_gdn_S[l],
                     gd_conv=state_gdn_conv[l], ffn_conv=state_ffn_conv[l])
        ck = cache_mem_k[l].reshape(bs, n_mem, CA_WIDTH).astype(BF16)
        cv = cache_mem_v[l].reshape(bs, n_mem, CA_WIDTH).astype(BF16)
        hs, mixed_s, sts = _layer(hs, mixed_s, ck, cv, st_in, p, next_gain, bs, ts)
        for k in keys:
            new_p[k].append(stp[k])
            new_s[k].append(sts[k])
    y_prompt = _rmsnorm(hp, g_final, F32).reshape(bp, tp, d)
    y_sample = _rmsnorm(hs, g_final, F32).reshape(bs, ts, d)
    outs = [y_prompt, y_sample, jnp.stack(mem_k_list), jnp.stack(mem_v_list)]
    outs += [jnp.stack(new_p[k]) for k in keys]
    outs += [jnp.stack(new_s[k]) for k in keys]
    return tuple(outs)
```

```python
import functools

import jax
import jax.numpy as jnp
from jax import lax
from jax.experimental import pallas as pl
from jax.experimental.pallas import tpu as pltpu

F32 = jnp.float32
BF16 = jnp.bfloat16

EPS = 1e-6
CHUNK = 64
SUB_CHUNKS = 4

ML_HEADS, ML_DQK, ML_DV = 4, 128, 256
ML_QK = ML_HEADS * ML_DQK
ML_WIDTH = ML_HEADS * ML_DV
ML_MAIN = 2 * ML_QK + 2 * ML_WIDTH
ML_PASSES = 3

RW_HEADS, RW_N = 16, 64
RW_WIDTH = RW_HEADS * RW_N
RW_W_RANK, RW_A_RANK, RW_G_RANK = 64, 64, 128
RW_COLS = 3 * RW_WIDTH + RW_W_RANK + RW_A_RANK + RW_G_RANK
RW_GN_EPS = 64e-5
RW_GROUP = 4
RW_PASSES = 1

GD_HEADS, GD_DK, GD_DV = 8, 128, 128
GD_QK = GD_HEADS * GD_DK
GD_WIDTH = GD_HEADS * GD_DV
GD_QKV = 2 * GD_QK + GD_WIDTH
GD_CONV = 4
GD_MAIN = GD_QKV + GD_WIDTH
GD_GROUP = 4
GD_PASSES = 1

BR_WIDTH = 1024
CA_HEADS, CA_HEAD_DIM = 4, 256
CA_WIDTH = CA_HEADS * CA_HEAD_DIM
FFN_CONV = 3
FFN_FUSE_MIN_T = 512
FFN_ROW_TILE = 1024
FFN_SHORT_SUB_ROWS = 256
FFN_SUB_ROWS = 128

SMALL_W = 128
SM_ML_I, SM_ML_F, SM_GD_B, SM_GD_A = 0, ML_HEADS, 2 * ML_HEADS, 2 * ML_HEADS + GD_HEADS
RW_SLAB = 4096
SMALL_BLK = RW_COLS // SMALL_W
OFF_ML = 0
OFF_GD = OFF_ML + ML_MAIN
OFF_GATE = OFF_GD + GD_MAIN
assert RW_COLS % SMALL_W == 0 and RW_SLAB >= RW_COLS + SMALL_W and RW_SLAB % 512 == 0
assert OFF_ML % ML_WIDTH == 0 and OFF_GD % GD_QK == 0

V7X_VMEM_LIMIT = 56 * 1024 * 1024
V7X_MXU_COLS = 256
TRI_BLOCK = 16


def _cparams(sem):
    return pltpu.CompilerParams(dimension_semantics=sem, vmem_limit_bytes=V7X_VMEM_LIMIT)


def _tile(dim, pref, quantum):
    if dim <= pref:
        return dim
    t = (pref // quantum) * quantum
    while t > quantum and dim % t:
        t -= quantum
    assert dim % t == 0, (dim, pref, quantum)
    return t


def _sub_chunks(t):
    n = SUB_CHUNKS
    while (t // CHUNK) % n:
        n -= 1
    return n


def _split2(a):
    hi = a.astype(BF16)
    lo = (a - hi.astype(F32)).astype(BF16)
    return hi, lo


def _dg(a, b, ca, cb):
    return lax.dot_general(a, b, (((ca,), (cb,)), ((), ())), preferred_element_type=F32)


def _dot3(a, b, ca=1, cb=0):
    ah, al = _split2(a)
    bh, bl = _split2(b)
    return _dg(ah, bh, ca, cb) + (_dg(al, bh, ca, cb) + _dg(ah, bl, ca, cb))


def _dotp(a, b, ca, cb, passes):
    if passes == 1:
        return _dg(a.astype(BF16), b.astype(BF16), ca, cb)
    return _dot3(a, b, ca, cb)


def _tri_masks(c):
    row = lax.broadcasted_iota(jnp.int32, (c, c), 0)
    col = lax.broadcasted_iota(jnp.int32, (c, c), 1)
    return row, col


def _cumsum_rows(x):
    c = x.shape[0]
    row, col = _tri_masks(c)
    tri = jnp.where(row >= col, 1.0, 0.0).astype(BF16)
    x0 = x.astype(BF16)
    r1 = x - x0.astype(F32)
    x1 = r1.astype(BF16)
    x2 = (r1 - x1.astype(F32)).astype(BF16)
    return _dg(tri, x0, 1, 0) + (_dg(tri, x1, 1, 0) + _dg(tri, x2, 1, 0))


def _tri_factor_many(lows, c, passes):
    mm = lambda a, b: _dotp(a, b, 1, 0, passes)
    row, col = _tri_masks(lows[0].shape[0])
    same = (row // TRI_BLOCK) == (col // TRI_BLOCK)
    eye = jnp.where(row == col, 1.0, 0.0).astype(F32)
    ps = [jnp.where(same, -low, 0.0) for low in lows]
    offs = [jnp.where(same, 0.0, low) for low in lows]
    xs = [eye + p for p in ps]
    steps = 1
    while steps * 2 < TRI_BLOCK:
        ps = [mm(p, p) for p in ps]
        xs = [x + mm(x, p) for x, p in zip(xs, ps)]
        steps *= 2
    ms = [mm(x, off) for x, off in zip(xs, offs)]
    nblk = c // TRI_BLOCK
    terms = []
    pws = ms
    k = 1
    while k < nblk:
        terms.append(pws)
        k *= 2
        if k < nblk:
            pws = [mm(pw, pw) for pw in pws]
    return xs, terms


def _tri_apply_many(factors, rhss, passes):
    mm = lambda a, b: _dotp(a, b, 1, 0, passes)
    xs, terms = factors
    us = xs if rhss is None else [mm(x, rhs) for x, rhs in zip(xs, rhss)]
    for i in range(len(terms) - 1, 0, -1):
        us = [u + mm(t, u) for t, u in zip(terms[i], us)]
    if terms:
        us = [u - mm(t, u) for t, u in zip(terms[0], us)]
    return us


def _head_sums(x, hw):
    assert 2 * hw == 128
    lane = lax.broadcasted_iota(jnp.int32, (x.shape[0], 128), 1)
    lo = lane < hw
    out = []
    for s in range(x.shape[1] // 128):
        xs = x[:, s * 128:(s + 1) * 128]
        s_lo = jnp.sum(jnp.where(lo, xs, 0.0), axis=-1, keepdims=True)
        s_hi = jnp.sum(jnp.where(lo, 0.0, xs), axis=-1, keepdims=True)
        out.append(jnp.where(lo, s_lo, s_hi))
    return jnp.concatenate(out, axis=1)


def _softplus(x):
    return jnp.maximum(x, 0.0) + jnp.log1p(jnp.exp(-jnp.abs(x)))


def _sigmoid(x):
    return 1.0 / (1.0 + jnp.exp(-x))


def _silu(x):
    return x * _sigmoid(x)


def _shift_rows(x, prev8, k):
    xr = pltpu.roll(x, k, 0)
    pr = pltpu.roll(prev8, k, 0)
    row = lax.broadcasted_iota(jnp.int32, (8, x.shape[1]), 0)
    head = jnp.where(row < k, pr, xr[0:8])
    if x.shape[0] == 8:
        return head
    return jnp.concatenate([head, xr[8:]], axis=0)


def _rmsnorm_kernel(x_ref, g_ref, o_ref):
    x = x_ref[...]
    y = x * lax.rsqrt(jnp.mean(x * x, axis=-1, keepdims=True) + EPS)
    o_ref[...] = (y * g_ref[...]).astype(o_ref.dtype)


def _rmsnorm(x, g, out_dtype):
    m, d = x.shape
    tr = _tile(m, 256, 8)
    return pl.pallas_call(
        _rmsnorm_kernel,
        grid=(m // tr,),
        in_specs=[pl.BlockSpec((tr, d), lambda i: (i, 0)), pl.BlockSpec((1, d), lambda i: (0, 0))],
        out_specs=pl.BlockSpec((tr, d), lambda i: (i, 0)),
        out_shape=jax.ShapeDtypeStruct((m, d), out_dtype),
        compiler_params=_cparams(("parallel",)),
        name="rmsnorm",
    )(x, g.reshape(1, d))


def _mm_kernel(a_ref, w_ref, o_ref):
    o_ref[...] = jnp.dot(a_ref[...], w_ref[...], preferred_element_type=F32).astype(o_ref.dtype)


def _mm_res_kernel(a_ref, w_ref, r_ref, o_ref):
    o_ref[...] = r_ref[...] + jnp.dot(a_ref[...], w_ref[...], preferred_element_type=F32)


def _row_rstd(ss_ref, d):
    return lax.rsqrt(jnp.sum(ss_ref[...], axis=-1, keepdims=True) * (1.0 / d) + EPS)


def _mm_scaled_kernel(a_ref, w_ref, ss_ref, o_ref, *, d):
    acc = jnp.dot(a_ref[...], w_ref[...], preferred_element_type=F32)
    o_ref[...] = (acc * _row_rstd(ss_ref, d)).astype(o_ref.dtype)


def _mm_res_norm_kernel(a_ref, w_ref, r_ref, g_ref, o_ref, hg_ref, ss_ref):
    h = r_ref[...] + jnp.dot(a_ref[...], w_ref[...], preferred_element_type=F32)
    o_ref[...] = h
    hg_ref[...] = (h * g_ref[...]).astype(hg_ref.dtype)
    h2 = h * h
    part = h2[:, 0:128]
    for s in range(1, h.shape[1] // 128):
        part = part + h2[:, s * 128:(s + 1) * 128]

    @pl.when(pl.program_id(1) == 0)
    def _():
        ss_ref[...] = part

    @pl.when(pl.program_id(1) != 0)
    def _():
        ss_ref[...] = ss_ref[...] + part


def _mm_tiles(m, k, n, has_residual):
    tm = _tile(m, 1024 if k <= 4096 else 512, 8)
    tn = _tile(n, 1024 if (k <= 4096 and not has_residual) else 512, V7X_MXU_COLS if n % V7X_MXU_COLS == 0 else 128)
    return tm, tn


def _matmul(a, w, residual=None, out_dtype=F32, row_ss=None, next_gain=None):
    m, k = a.shape
    n = w.shape[1]
    tm, tn = _mm_tiles(m, k, n, residual is not None)
    in_specs = [pl.BlockSpec((tm, k), lambda i, j: (i, 0)), pl.BlockSpec((k, tn), lambda i, j: (0, j))]
    args = [a, w]
    body = _mm_kernel
    out_specs = pl.BlockSpec((tm, tn), lambda i, j: (i, j))
    out_shape = jax.ShapeDtypeStruct((m, n), out_dtype)
    if residual is not None:
        in_specs.append(pl.BlockSpec((tm, tn), lambda i, j: (i, j)))
        args.append(residual)
        body = _mm_res_kernel
        if next_gain is not None:
            in_specs.append(pl.BlockSpec((1, tn), lambda i, j: (0, j)))
            args.append(next_gain.reshape(1, n))
            body = _mm_res_norm_kernel
            out_specs = [out_specs, pl.BlockSpec((tm, tn), lambda i, j: (i, j)),
                         pl.BlockSpec((tm, 128), lambda i, j: (i, 0))]
            out_shape = [out_shape, jax.ShapeDtypeStruct((m, n), BF16), jax.ShapeDtypeStruct((m, 128), F32)]
    elif row_ss is not None:
        in_specs.append(pl.BlockSpec((tm, 128), lambda i, j: (i, 0)))
        args.append(row_ss)
        body = functools.partial(_mm_scaled_kernel, d=k)
    return pl.pallas_call(
        body,
        grid=(m // tm, n // tn),
        in_specs=in_specs,
        out_specs=out_specs,
        out_shape=out_shape,
        compiler_params=_cparams(("parallel", "arbitrary")),
        name="matmul",
    )(*args)


def _mlstm_front(q_all, k_all, v_all, pre):
    n = len(q_all)
    c = q_all[0].shape[0]
    cum = [_cumsum_rows(-_softplus(-p_)) for p_ in pre]
    pre_t = [p_.T for p_ in pre]
    cum_t = [z.T for z in cum]
    row, col = _tri_masks(c)
    causal = row >= col
    it = [(j, h) for j in range(n) for h in range(ML_HEADS)]
    q = [q_all[j][:, h * ML_DQK:(h + 1) * ML_DQK] for j, h in it]
    k = [k_all[j][:, h * ML_DQK:(h + 1) * ML_DQK] * (ML_DQK ** -0.5) for j, h in it]
    v = [v_all[j][:, h * ML_DV:(h + 1) * ML_DV] for j, h in it]
    b_col = [cum[j][:, SM_ML_F + h:SM_ML_F + h + 1] for j, h in it]
    b_row = [cum_t[j][SM_ML_F + h:SM_ML_F + h + 1, :] for j, h in it]
    i_col = [pre[j][:, SM_ML_I + h:SM_ML_I + h + 1] for j, h in it]
    i_row = [pre_t[j][SM_ML_I + h:SM_ML_I + h + 1, :] for j, h in it]
    ix = range(len(it))
    qk = [_dotp(q[x], k[x], 1, 1, ML_PASSES) for x in ix]
    dmat = [jnp.where(causal, b_col[x] - b_row[x] + i_row[x], -jnp.inf) for x in ix]
    m_loc = [jnp.max(dmat[x], axis=-1, keepdims=True) for x in ix]
    pmat = [jnp.exp(dmat[x] - m_loc[x]) * qk[x] for x in ix]
    m_end = [m_loc[x][c - 1:c, :] for x in ix]
    kw = [k[x] * jnp.exp(b_col[x][c - 1:c, :] - b_col[x] + i_col[x] - m_end[x]) for x in ix]
    pv = [_dotp(pmat[x], v[x], 1, 0, ML_PASSES) for x in ix]
    kv = [_dotp(kw[x].T, v[x], 1, 0, ML_PASSES) for x in ix]
    ps = [jnp.sum(pmat[x], axis=-1, keepdims=True) for x in ix]
    ks = [jnp.sum(kw[x], axis=0, keepdims=True) for x in ix]
    per_chunk = lambda z: [z[j * ML_HEADS:(j + 1) * ML_HEADS] for j in range(n)]
    names = ("q", "b_col", "m_loc", "m_end", "pv", "ps", "kv", "ks")
    cols = [per_chunk(z) for z in (q, b_col, m_loc, m_end, pv, ps, kv, ks)]
    return [dict(zip(names, [col_[j] for col_ in cols])) for j in range(n)]


def _mlstm_kernel(q_ref, k_ref, v_ref, og_ref, sm_ref, bias_ref, norm_ref, c0_ref, n0_ref, m0_ref,
                  y_ref, c_ref, n_ref, m_ref):
    @pl.when(pl.program_id(1) == 0)
    def _():
        c_ref[...] = c0_ref[...]
        n_ref[...] = n0_ref[...]
        m_ref[...] = m0_ref[...]

    c = CHUNK
    nsub = q_ref.shape[0] // c
    hs = range(ML_HEADS)
    chunk_rows = [slice(k * c, (k + 1) * c) for k in range(nsub)]
    fronts = _mlstm_front([q_ref[r, :] for r in chunk_rows], [k_ref[r, :] for r in chunk_rows],
                          [v_ref[r, :] for r in chunk_rows], [sm_ref[r, :] + bias_ref[...] for r in chunk_rows])
    m_old = [m_ref[0, :, h:h + 1] for h in hs]
    c_old = [c_ref[0, h] for h in hs]
    n_old = [n_ref[0, h:h + 1, :] for h in hs]
    for k in range(nsub):
        f = fronts[k]
        rows = slice(k * c, (k + 1) * c)
        qc = [_dotp(f["q"][h], c_old[h], 1, 0, ML_PASSES) for h in hs]
        mt = [jnp.maximum(f["b_col"][h] + m_old[h], f["m_loc"][h]) for h in hs]
        e_loc = [jnp.exp(f["m_loc"][h] - mt[h]) for h in hs]
        inter = [jnp.exp(f["b_col"][h] + m_old[h] - mt[h]) for h in hs]
        num = [inter[h] * qc[h] + e_loc[h] * f["pv"][h] for h in hs]
        den = [inter[h] * jnp.sum(f["q"][h] * n_old[h], axis=-1, keepdims=True) + e_loc[h] * f["ps"][h] for h in hs]
        hh = [num[h] / jnp.maximum(jnp.abs(den[h]), jnp.exp(-mt[h])) for h in hs]
        m_new = [mt[h][c - 1:c, :] for h in hs]
        scale = [jnp.exp(f["m_end"][h] - m_new[h]) for h in hs]
        dec = [jnp.exp(f["b_col"][h][c - 1:c, :] + m_old[h] - m_new[h]) for h in hs]
        c_old = [dec[h] * c_old[h] + scale[h] * f["kv"][h] for h in hs]
        n_old = [dec[h] * n_old[h] + scale[h] * f["ks"][h] for h in hs]
        m_old = m_new
        for h in hs:
            hn = hh[h] * lax.rsqrt(jnp.mean(hh[h] * hh[h], axis=-1, keepdims=True) + EPS)
            og = og_ref[rows, h * ML_DV:(h + 1) * ML_DV]
            y = hn * norm_ref[:, h * ML_DV:(h + 1) * ML_DV] * _sigmoid(og)
            y_ref[rows, h * ML_DV:(h + 1) * ML_DV] = y.astype(y_ref.dtype)
    for h in hs:
        c_ref[0, h] = c_old[h]
        n_ref[0, h:h + 1, :] = n_old[h]
        m_ref[0, :, h:h + 1] = m_old[h]


def _mlstm(p_main, p_small, bias_row, ml_norm, c0, n0, m0, bsz, t):
    step = CHUNK * _sub_chunks(t)
    nc = t // step
    rows = bsz * t
    rmap = lambda b, c: b * nc + c
    return pl.pallas_call(
        _mlstm_kernel,
        grid=(bsz, nc),
        in_specs=[
            pl.BlockSpec((step, ML_QK), lambda b, c: (rmap(b, c), OFF_ML // ML_QK)),
            pl.BlockSpec((step, ML_QK), lambda b, c: (rmap(b, c), OFF_ML // ML_QK + 1)),
            pl.BlockSpec((step, ML_WIDTH), lambda b, c: (rmap(b, c), OFF_ML // ML_WIDTH + 1)),
            pl.BlockSpec((step, ML_WIDTH), lambda b, c: (rmap(b, c), OFF_ML // ML_WIDTH + 2)),
            pl.BlockSpec((step, SMALL_W), lambda b, c: (rmap(b, c), SMALL_BLK)),
            pl.BlockSpec((1, SMALL_W), lambda b, c: (0, 0)),
            pl.BlockSpec((1, ML_WIDTH), lambda b, c: (0, 0)),
            pl.BlockSpec((1, ML_HEADS, ML_DQK, ML_DV), lambda b, c: (b, 0, 0, 0)),
            pl.BlockSpec((1, ML_HEADS, ML_DQK), lambda b, c: (b, 0, 0)),
            pl.BlockSpec((1, 1, ML_HEADS), lambda b, c: (b, 0, 0)),
        ],
        out_specs=[
            pl.BlockSpec((step, ML_WIDTH), lambda b, c: (rmap(b, c), 0)),
            pl.BlockSpec((1, ML_HEADS, ML_DQK, ML_DV), lambda b, c: (b, 0, 0, 0)),
            pl.BlockSpec((1, ML_HEADS, ML_DQK), lambda b, c: (b, 0, 0)),
            pl.BlockSpec((1, 1, ML_HEADS), lambda b, c: (b, 0, 0)),
        ],
        out_shape=[
            jax.ShapeDtypeStruct((rows, ML_WIDTH), BF16),
            jax.ShapeDtypeStruct(c0.shape, F32),
            jax.ShapeDtypeStruct(n0.shape, F32),
            jax.ShapeDtypeStruct((bsz, 1, ML_HEADS), F32),
        ],
        compiler_params=_cparams(("parallel", "arbitrary")),
        name="mlstm",
    )(p_main, p_main, p_main, p_main, p_small, bias_row, ml_norm.reshape(1, ML_WIDTH),
      c0, n0, m0.reshape(bsz, 1, ML_HEADS))


def _rwkv_front(x, prev8, mu_ref, w0_ref, w2_ref, a0_ref, a2_ref, g2_ref, kk_ref, ka_ref, rk_ref):
    c = x.shape[0]
    w = RW_WIDTH
    xprev = _shift_rows(x, prev8, 1)
    xm = x + (xprev - x) * mu_ref[...]
    rr = xm[:, 0:w]
    rk = xm[:, w:2 * w]
    rv = xm[:, 2 * w:3 * w]
    xw = xm[:, 3 * w:3 * w + RW_W_RANK]
    xa = xm[:, 3 * w + RW_W_RANK:3 * w + RW_W_RANK + RW_A_RANK]
    xg = xm[:, 3 * w + RW_W_RANK + RW_A_RANK:]

    w_pre = w0_ref[...] + _dot3(jnp.tanh(xw), w2_ref[...])
    lw = -jnp.exp(-_softplus(-w_pre) - 0.5)
    a = _sigmoid(a0_ref[...] + _dot3(xa, a2_ref[...]))
    g_out = _dot3(_sigmoid(xg), g2_ref[...])
    kk_raw = rk * kk_ref[...]
    kmod = rk * (1.0 + (a - 1.0) * ka_ref[...])
    bon = rr * kmod * rk_ref[...]

    lc = _cumsum_rows(lw)
    l_end = lc[c - 1:c, :]
    l_mid = lc[c // 2 - 1:c // 2, :]
    lcc = lc - l_mid
    p_mid = jnp.exp(l_mid)
    p_in = jnp.exp(lcc)
    p_prev = jnp.exp(lcc - lw)
    p_inv = jnp.exp(-lcc)
    p_end = jnp.exp(l_end - lc)
    p_all = jnp.exp(l_end)

    kkn = kk_raw * lax.rsqrt(_head_sums(kk_raw * kk_raw, RW_N) + 1e-6)
    bb = kkn * a
    kkp = kkn * p_prev
    rp = rr * p_in
    kd = kmod * p_inv
    bd = bb * p_inv
    k_end = kmod * p_end
    b_end = bb * p_end
    return dict(kkp=kkp, rp=rp, kd=kd, bd=bd, k_end=k_end, b_end=b_end, rv=rv, p_mid=p_mid, p_all=p_all,
                bonus=_head_sums(bon, RW_N) * rv, g_out=g_out)


def _rw_groups():
    gw = RW_GROUP * RW_N
    return [slice(g * gw, (g + 1) * gw) for g in range(RW_HEADS // RW_GROUP)]


def _rw_masks(c):
    shape = (RW_GROUP * c, RW_GROUP * RW_N)
    head_eq = lax.broadcasted_iota(jnp.int32, shape, 0) // c == lax.broadcasted_iota(jnp.int32, shape, 1) // RW_N
    tn_ = lax.broadcasted_iota(jnp.int32, (c, RW_GROUP * RW_N), 0)
    sn_ = lax.broadcasted_iota(jnp.int32, (c, RW_GROUP * RW_N), 1) % RW_N
    return head_eq, tn_ > sn_, tn_ >= sn_


def _rw_blockdiag(z, head_eq):
    return jnp.where(head_eq, jnp.concatenate([z] * RW_GROUP, axis=0), 0.0)


def _rw_rowsum(z, c):
    out = z[0:c]
    for i in range(1, RW_GROUP):
        out = out + z[i * c:(i + 1) * c]
    return out


def _rwkv_mid(f, masks):
    kkp, rp, kd, bd, rv = f["kkp"], f["rp"], f["kd"], f["bd"], f["rv"]
    c = rv.shape[0]
    head_eq, strict, incl = masks
    groups = _rw_groups()
    nt = lambda x, y: _dotp(x, y, 1, 1, RW_PASSES)
    x2 = [jnp.concatenate([kkp[:, cs], rp[:, cs]], axis=0) for cs in groups]
    ab_b = [nt(x, _rw_blockdiag(bd[:, cs], head_eq)) for x, cs in zip(x2, groups)]
    ab_k = [nt(x, _rw_blockdiag(kd[:, cs], head_eq)) for x, cs in zip(x2, groups)]
    a_b = [_rw_blockdiag(jnp.where(strict, z[0:c], 0.0), head_eq) for z in ab_b]
    return dict(
        x2=x2, v_bd=[_rw_blockdiag(rv[:, cs], head_eq) for cs in groups],
        a_b=a_b,
        a_k=[jnp.where(strict, z[0:c], 0.0) for z in ab_k],
        r_k=[jnp.where(incl, z[c:], 0.0) for z in ab_k],
        r_b=[jnp.where(incl, z[c:], 0.0) for z in ab_b])


def _rwkv_chain(f, g, s_olds, ln_ref, masks):
    k_end, b_end, rv, p_mid, p_all = f["k_end"], f["b_end"], f["rv"], f["p_mid"], f["p_all"]
    c = rv.shape[0]
    groups = _rw_groups()
    head_eq = masks[0]
    nt = lambda x, y: _dotp(x, y, 1, 1, RW_PASSES)
    nn = lambda x, y: _dotp(x, y, 1, 0, RW_PASSES)
    v_bd = g["v_bd"]
    ab_s = [nt(x, _rw_blockdiag(s * p_mid[:, cs], head_eq))
            for x, s, cs in zip(g["x2"], s_olds, groups)]
    rhs = [_rw_blockdiag(z[0:c] + nn(ak, v), head_eq) for z, ak, v in zip(ab_s, g["a_k"], v_bd)]
    us = [nn(t_, r_) for t_, r_ in zip(g["t_inv"], rhs)]
    ygs = [z[c:] + nn(rk_, v) - nn(rb_, u) for z, rk_, v, rb_, u in zip(ab_s, g["r_k"], v_bd, g["r_b"], us)]
    upds = [nn(jnp.concatenate([rv[:, cs], -_rw_rowsum(u, c)], axis=0).T,
               jnp.concatenate([k_end[:, cs], b_end[:, cs]], axis=0)) for cs, u in zip(groups, us)]
    s_news = [s * p_all[:, cs] + _rw_rowsum(jnp.where(head_eq, upd, 0.0), c)
              for cs, s, upd in zip(groups, s_olds, upds)]

    yh = jnp.concatenate(ygs, axis=1)
    mu_ = _head_sums(yh, RW_N) * (1.0 / RW_N)
    yc = yh - mu_
    var = _head_sums(yc * yc, RW_N) * (1.0 / RW_N)
    yn = yc * lax.rsqrt(var + RW_GN_EPS)
    return (yn * ln_ref[...] + f["bonus"]) * f["g_out"], s_news


def _rwkv_kernel(p_ref, shift_ref, mu_ref, w0_ref, w2_ref, a0_ref, a2_ref, g2_ref, kk_ref, ka_ref,
                 rk_ref, ln_ref, s0_ref, y_ref, s_ref, prev_ref):
    @pl.when(pl.program_id(1) == 0)
    def _():
        s_ref[...] = s0_ref[...]
        prev_ref[...] = jnp.broadcast_to(shift_ref[0], prev_ref.shape)

    c = CHUNK
    nsub = p_ref.shape[0] // c
    masks = _rw_masks(c)
    fronts = []
    for k in range(nsub):
        last8 = prev_ref[...] if k == 0 else p_ref[k * c - 8:k * c, :]
        fronts.append(_rwkv_front(p_ref[k * c:(k + 1) * c, :], last8, mu_ref, w0_ref, w2_ref, a0_ref, a2_ref,
                                  g2_ref, kk_ref, ka_ref, rk_ref))
    mids = [_rwkv_mid(f, masks) for f in fronts]
    lows = [low for g in mids for low in g["a_b"]]
    t_inv = _tri_apply_many(_tri_factor_many(lows, c, RW_PASSES), None, RW_PASSES)
    groups = _rw_groups()
    for k, g in enumerate(mids):
        g["t_inv"] = t_inv[k * len(groups):(k + 1) * len(groups)]
    states = [s_ref[0, :, cs] for cs in groups]
    for k in range(nsub):
        out, states = _rwkv_chain(fronts[k], mids[k], states, ln_ref, masks)
        y_ref[k * c:(k + 1) * c, :] = out.astype(y_ref.dtype)
    prev_ref[...] = p_ref[nsub * c - 8:nsub * c, :]
    for cs, s in zip(groups, states):
        s_ref[0, :, cs] = s


def _rwkv(p_rw, shift0, s0, p, bsz, t):
    step = CHUNK * _sub_chunks(t)
    nc = t // step
    rows = bsz * t
    rmap = lambda b, c: b * nc + c
    full = lambda shape: pl.BlockSpec(shape, lambda b, c: (0,) * len(shape))
    w = RW_WIDTH
    s_nat = s0.transpose(0, 2, 1, 3).reshape(bsz, RW_N, w)
    y, s = pl.pallas_call(
        _rwkv_kernel,
        grid=(bsz, nc),
        in_specs=[
            pl.BlockSpec((step, RW_COLS), lambda b, c: (rmap(b, c), 0)),
            pl.BlockSpec((1, 1, RW_COLS), lambda b, c: (b, 0, 0)),
            full((1, RW_COLS)), full((1, w)), full((RW_W_RANK, w)), full((1, w)), full((RW_A_RANK, w)),
            full((RW_G_RANK, w)), full((1, w)), full((1, w)), full((1, w)), full((1, w)),
            pl.BlockSpec((1, RW_N, w), lambda b, c: (b, 0, 0)),
        ],
        out_specs=[
            pl.BlockSpec((step, w), lambda b, c: (rmap(b, c), 0)),
            pl.BlockSpec((1, RW_N, w), lambda b, c: (b, 0, 0)),
        ],
        out_shape=[jax.ShapeDtypeStruct((rows, w), BF16), jax.ShapeDtypeStruct(s_nat.shape, F32)],
        scratch_shapes=[pltpu.VMEM((8, RW_COLS), F32)],
        compiler_params=_cparams(("parallel", "arbitrary")),
        name="rwkv7",
    )(p_rw, shift0.reshape(bsz, 1, RW_COLS), p["rw_mu"].reshape(1, RW_COLS), p["rw_w0"].reshape(1, w),
      p["rw_w2"], p["rw_a0"].reshape(1, w), p["rw_a2"], p["rw_g2"], p["rw_k_k"].reshape(1, w),
      p["rw_k_a"].reshape(1, w), p["rw_r_k"].reshape(1, w), p["rw_ln"].reshape(1, w), s_nat)
    return y, s.reshape(bsz, RW_N, RW_HEADS, RW_N).transpose(0, 2, 1, 3)


def _gdn_front(xs, prevs, sm, cw_ref, alog_ref, dtb_ref):
    c = xs[0].shape[0]
    acts = []
    for sec, (x, prev8) in enumerate(zip(xs, prevs)):
        cs = slice(sec * GD_QK, (sec + 1) * GD_QK)
        y = _shift_rows(x, prev8, GD_CONV - 1) * cw_ref[0:1, cs]
        for i in range(1, GD_CONV - 1):
            y = y + _shift_rows(x, prev8, GD_CONV - 1 - i) * cw_ref[i:i + 1, cs]
        y = y + x * cw_ref[GD_CONV - 1:GD_CONV, cs]
        acts.append(_silu(y))
    qa, ka, va = acts

    beta = _sigmoid(sm)
    g = -jnp.exp(alog_ref[...]) * _softplus(sm + dtb_ref[...])
    gc = _cumsum_rows(g)
    gc_t = gc.T

    gr = GD_GROUP * c
    gk = GD_GROUP * GD_DK
    r4 = lax.broadcasted_iota(jnp.int32, (gr, gr), 0)
    c4 = lax.broadcasted_iota(jnp.int32, (gr, gr), 1)
    t_minus_s = jnp.where((r4 // c) == (c4 // c), (r4 % c) - (c4 % c), -1)
    m_strict = t_minus_s > 0
    m_incl = t_minus_s >= 0
    head_eq = (lax.broadcasted_iota(jnp.int32, (gr, gk), 0) // c
               == lax.broadcasted_iota(jnp.int32, (gr, gk), 1) // GD_DK)

    def rep(z):
        return jnp.concatenate([z] * GD_GROUP, axis=0)

    def stack(parts):
        return jnp.concatenate(parts, axis=0)

    kn = []
    qn = []
    for h in range(GD_HEADS):
        ks = slice(h * GD_DK, (h + 1) * GD_DK)
        q = qa[:, ks]
        qn.append(q * lax.rsqrt(jnp.sum(q * q, axis=-1, keepdims=True) + 1e-6) * (GD_DK ** -0.5))
        k = ka[:, ks]
        kn.append(k * lax.rsqrt(jnp.sum(k * k, axis=-1, keepdims=True) + 1e-6))
    groups = [range(g0, g0 + GD_GROUP) for g0 in range(0, GD_HEADS, GD_GROUP)]
    k_nat = [jnp.concatenate([kn[h] for h in hs], axis=1) for hs in groups]
    q_nat = [jnp.concatenate([qn[h] for h in hs], axis=1) for hs in groups]
    k_bd = [jnp.where(head_eq, rep(kk_), 0.0) for kk_ in k_nat]
    v_st = [stack([va[:, h * GD_DV:(h + 1) * GD_DV] for h in hs]) for hs in groups]
    b_col = [stack([beta[:, SM_GD_B + h:SM_GD_B + h + 1] for h in hs]) for hs in groups]
    g_col = [stack([gc[:, SM_GD_A + h:SM_GD_A + h + 1] for h in hs]) for hs in groups]
    g_row = [jnp.concatenate([gc_t[SM_GD_A + h:SM_GD_A + h + 1, :] for h in hs], axis=1) for hs in groups]

    dec = [jnp.exp(jnp.where(m_incl, gc_ - gr_, -jnp.inf)) for gc_, gr_ in zip(g_col, g_row)]
    kq = [_dotp(stack([kk_, qq_]), kb, 1, 1, GD_PASSES) for kk_, qq_, kb in zip(k_nat, q_nat, k_bd)]
    low = [jnp.where(m_strict, b * rep(z[0:c]) * d, 0.0) for b, z, d in zip(b_col, kq, dec)]
    gl = [gc[c - 1:c, SM_GD_A + h:SM_GD_A + h + 1] for h in range(GD_HEADS)]
    return dict(
        kq_lhs=[stack([kn[h], qn[h]]) for h in range(GD_HEADS)], v_st=v_st, b_col=b_col,
        eg=[jnp.exp(gc_) for gc_ in g_col],
        low=low,
        qkt=[rep(z[c:]) * d for z, d in zip(kq, dec)],
        kw_t=[(kn[h] * jnp.exp(gl[h] - gc[:, SM_GD_A + h:SM_GD_A + h + 1])).T for h in range(GD_HEADS)],
        s_decay=[jnp.exp(gl[h]) for h in range(GD_HEADS)])


def _gdn_chain(f, s_old):
    c = f["kq_lhs"][0].shape[0] // 2
    stack = lambda parts: jnp.concatenate(parts, axis=0)
    groups = [range(g0, g0 + GD_GROUP) for g0 in range(0, GD_HEADS, GD_GROUP)]
    kqs = [_dotp(f["kq_lhs"][h], s_old[h], 1, 0, GD_PASSES) for h in range(GD_HEADS)]
    ks = [stack([kqs[h][0:c] for h in hs]) for hs in groups]
    qs = [stack([kqs[h][c:] for h in hs]) for hs in groups]
    rhs = [b * (v - e * z) for b, v, e, z in zip(f["b_col"], f["v_st"], f["eg"], ks)]
    us = [_dotp(t_, r_, 1, 0, GD_PASSES) for t_, r_ in zip(f["t_inv"], rhs)]
    os_ = [e * z + _dotp(qk, u, 1, 0, GD_PASSES) for e, z, qk, u in zip(f["eg"], qs, f["qkt"], us)]
    outs = []
    s_new = []
    for gi, hs in enumerate(groups):
        for i, h in enumerate(hs):
            rs = slice(i * c, (i + 1) * c)
            s_new.append(f["s_decay"][h] * s_old[h] + _dotp(f["kw_t"][h], us[gi][rs], 1, 0, GD_PASSES))
            outs.append(os_[gi][rs])
    return outs, s_new


def _gdn_kernel(q_ref, k_ref, v_ref, z_ref, sm_ref, cw_ref, conv0_ref, alog_ref, dtb_ref, norm_ref, s0_ref,
                y_ref, s_ref, prev_ref):
    @pl.when(pl.program_id(1) == 0)
    def _():
        s_ref[...] = s0_ref[...]
        prev_ref[...] = jnp.zeros(prev_ref.shape, F32)
        prev_ref[8 - (GD_CONV - 1):8, :] = conv0_ref[0]

    c = CHUNK
    nsub = q_ref.shape[0] // c
    refs = (q_ref, k_ref, v_ref)
    fronts = []
    for k in range(nsub):
        xs = [r[k * c:(k + 1) * c, :] for r in refs]
        if k == 0:
            prevs = [prev_ref[:, sec * GD_QK:(sec + 1) * GD_QK] for sec in range(3)]
        else:
            prevs = [r[k * c - 8:k * c, :] for r in refs]
        fronts.append(_gdn_front(xs, prevs, sm_ref[k * c:(k + 1) * c, :], cw_ref, alog_ref, dtb_ref))
    lows = [low for f in fronts for low in f["low"]]
    t_inv = _tri_apply_many(_tri_factor_many(lows, c, GD_PASSES), None, GD_PASSES)
    per = len(fronts[0]["low"])
    for k, f in enumerate(fronts):
        f["t_inv"] = t_inv[k * per:(k + 1) * per]
    states = [s_ref[0, h] for h in range(GD_HEADS)]
    for k in range(nsub):
        outs, states = _gdn_chain(fronts[k], states)
        for h in range(GD_HEADS):
            o_h = outs[h]
            og = o_h * lax.rsqrt(jnp.mean(o_h * o_h, axis=-1, keepdims=True) + EPS) * norm_ref[...]
            vs = slice(h * GD_DV, (h + 1) * GD_DV)
            y_ref[k * c:(k + 1) * c, vs] = (og * _silu(z_ref[k * c:(k + 1) * c, vs])).astype(y_ref.dtype)
    for h in range(GD_HEADS):
        s_ref[0, h] = states[h]
    for sec, r in enumerate(refs):
        prev_ref[:, sec * GD_QK:(sec + 1) * GD_QK] = r[nsub * c - 8:nsub * c, :]


def _gdn(p_main, p_small, conv_w, conv0, alog_row, dtb_row, gd_norm, s0, bsz, t):
    step = CHUNK * _sub_chunks(t)
    nc = t // step
    rows = bsz * t
    rmap = lambda b, c: b * nc + c
    base = OFF_GD // GD_QK
    full = lambda shape: pl.BlockSpec(shape, lambda b, c: (0,) * len(shape))
    return pl.pallas_call(
        _gdn_kernel,
        grid=(bsz, nc),
        in_specs=[
            pl.BlockSpec((step, GD_QK), lambda b, c: (rmap(b, c), base)),
            pl.BlockSpec((step, GD_QK), lambda b, c: (rmap(b, c), base + 1)),
            pl.BlockSpec((step, GD_WIDTH), lambda b, c: (rmap(b, c), base + 2)),
            pl.BlockSpec((step, GD_WIDTH), lambda b, c: (rmap(b, c), base + 3)),
            pl.BlockSpec((step, SMALL_W), lambda b, c: (rmap(b, c), SMALL_BLK)),
            full((GD_CONV, GD_QKV)),
            pl.BlockSpec((1, GD_CONV - 1, GD_QKV), lambda b, c: (b, 0, 0)),
            full((1, SMALL_W)), full((1, SMALL_W)), full((1, GD_DV)),
            pl.BlockSpec((1, GD_HEADS, GD_DK, GD_DV), lambda b, c: (b, 0, 0, 0)),
        ],
        out_specs=[
            pl.BlockSpec((step, GD_WIDTH), lambda b, c: (rmap(b, c), 0)),
            pl.BlockSpec((1, GD_HEADS, GD_DK, GD_DV), lambda b, c: (b, 0, 0, 0)),
        ],
        out_shape=[jax.ShapeDtypeStruct((rows, GD_WIDTH), BF16), jax.ShapeDtypeStruct(s0.shape, F32)],
        scratch_shapes=[pltpu.VMEM((8, GD_QKV), F32)],
        compiler_params=_cparams(("parallel", "arbitrary")),
        name="gdn",
    )(p_main, p_main, p_main, p_main, p_small, conv_w, conv0, alog_row, dtb_row, gd_norm.reshape(1, GD_DV), s0)


def _merge_kernel(y0_ref, y1_ref, y2_ref, w_ref, g0_ref, g1_ref, g2_ref, o_ref):
    acc = _sigmoid(g0_ref[...]) * jnp.dot(y0_ref[...], w_ref[0], preferred_element_type=F32)
    acc = acc + _sigmoid(g1_ref[...]) * jnp.dot(y1_ref[...], w_ref[1], preferred_element_type=F32)
    acc = acc + _sigmoid(g2_ref[...]) * jnp.dot(y2_ref[...], w_ref[2], preferred_element_type=F32)
    o_ref[...] = acc.astype(o_ref.dtype)


def _merge(ys, w_branch, p_main, d):
    m = ys[0].shape[0]
    tm = _tile(m, 1024, 8)
    tn = _tile(d, 512, 128)
    gate0 = OFF_GATE
    assert gate0 % tn == 0
    gb = gate0 // tn
    nb = d // tn
    yspec = pl.BlockSpec((tm, BR_WIDTH), lambda i, j: (i, 0))
    gspec = lambda b: pl.BlockSpec((tm, tn), lambda i, j: (i, gb + b * nb + j))
    return pl.pallas_call(
        _merge_kernel,
        grid=(m // tm, nb),
        in_specs=[yspec, yspec, yspec, pl.BlockSpec((3, BR_WIDTH, tn), lambda i, j: (0, 0, j)),
                  gspec(0), gspec(1), gspec(2)],
        out_specs=pl.BlockSpec((tm, tn), lambda i, j: (i, j)),
        out_shape=jax.ShapeDtypeStruct((m, d), BF16),
        compiler_params=_cparams(("parallel", "arbitrary")),
        name="merge",
    )(ys[0], ys[1], ys[2], w_branch, p_main, p_main, p_main)


def _attn_kernel(q_ref, k_ref, v_ref, o_ref):
    for h in range(CA_HEADS):
        hs = slice(h * CA_HEAD_DIM, (h + 1) * CA_HEAD_DIM)
        s = _dg(q_ref[:, hs], k_ref[0, :, hs], 1, 1) * (CA_HEAD_DIM ** -0.5)
        s = s - jnp.max(s, axis=-1, keepdims=True)
        e = jnp.exp(s)
        pr = e / jnp.sum(e, axis=-1, keepdims=True)
        o = jnp.dot(pr.astype(BF16), v_ref[0, :, hs], preferred_element_type=F32)
        o_ref[:, hs] = o.astype(o_ref.dtype)


def _attention(q, mem_k, mem_v, bsz, t):
    tq = _tile(t, 512, 8)
    nt = t // tq
    n_mem = mem_k.shape[1]
    return pl.pallas_call(
        _attn_kernel,
        grid=(bsz, nt),
        in_specs=[
            pl.BlockSpec((tq, CA_WIDTH), lambda b, i: (b * nt + i, 0)),
            pl.BlockSpec((1, n_mem, CA_WIDTH), lambda b, i: (b, 0, 0)),
            pl.BlockSpec((1, n_mem, CA_WIDTH), lambda b, i: (b, 0, 0)),
        ],
        out_specs=pl.BlockSpec((tq, CA_WIDTH), lambda b, i: (b * nt + i, 0)),
        out_shape=jax.ShapeDtypeStruct((bsz * t, CA_WIDTH), BF16),
        compiler_params=_cparams(("parallel", "arbitrary")),
        name="mem_attention",
    )(q, mem_k, mem_v)


def _ffn_up_kernel(u_ref, ss_ref, wa_ref, wg_ref, cwa_ref, cwg_ref, c0a_ref, c0g_ref, act_ref, ta_ref, tg_ref,
                   carry_ref, *, tiles_per_seq, sub):
    i = pl.program_id(0)
    j = pl.program_id(1)
    tm, tn = act_ref.shape

    @pl.when(i == 0)
    def _():
        carry_ref[j] = jnp.zeros(carry_ref.shape[1:], F32)

    first = (i % tiles_per_seq) == 0
    pad = jnp.zeros((8 - (FFN_CONV - 1), tn), F32)
    carried = carry_ref[j]
    prev_a = jnp.where(first, jnp.concatenate([pad, c0a_ref[0]], axis=0), carried[0:8])
    prev_g = jnp.where(first, jnp.concatenate([pad, c0g_ref[0]], axis=0), carried[8:16])
    rstd = _row_rstd(ss_ref, u_ref.shape[1])

    def up(r):
        ur = u_ref[r * sub:(r + 1) * sub, :]
        sc = rstd[r * sub:(r + 1) * sub]
        return (jnp.dot(ur, wa_ref[...], preferred_element_type=F32) * sc,
                jnp.dot(ur, wg_ref[...], preferred_element_type=F32) * sc)

    nxt = up(0)
    for r in range(tm // sub):
        rows = slice(r * sub, (r + 1) * sub)
        za, zg = nxt
        if r + 1 < tm // sub:
            nxt = up(r + 1)
        fa =(_shift_rows(za, prev_a, 2) * cwa_ref[0:1] + _shift_rows(za, prev_a, 1) * cwa_ref[1:2]
              + za * cwa_ref[2:3])
        fg = (_shift_rows(zg, prev_g, 2) * cwg_ref[0:1] + _shift_rows(zg, prev_g, 1) * cwg_ref[1:2]
              + zg * cwg_ref[2:3])
        act_ref[rows, :] = (_silu(fg) * fa).astype(act_ref.dtype)
        prev_a = za[sub - 8:sub]
        prev_g = zg[sub - 8:sub]
    carry_ref[j] = jnp.concatenate([prev_a, prev_g], axis=0)
    ta_ref[0] = prev_a[8 - (FFN_CONV - 1):8]
    tg_ref[0] = prev_g[8 - (FFN_CONV - 1):8]


def _ffn_up_act(u, row_ss, w_up, conv0, conv_w, bsz, t):
    m, d = u.shape
    f = w_up.shape[1] // 2
    tm = _tile(t, FFN_ROW_TILE, 8)
    sub = _tile(tm, FFN_SUB_ROWS, 8)
    tn = _tile(f, 512, 128)
    nj = f // tn
    tps = t // tm
    act, ta, tg = pl.pallas_call(
        functools.partial(_ffn_up_kernel, tiles_per_seq=tps, sub=sub),
        grid=(m // tm, nj),
        in_specs=[
            pl.BlockSpec((tm, d), lambda i, j: (i, 0)),
            pl.BlockSpec((tm, 128), lambda i, j: (i, 0)),
            pl.BlockSpec((d, tn), lambda i, j: (0, j)),
            pl.BlockSpec((d, tn), lambda i, j: (0, nj + j)),
            pl.BlockSpec((FFN_CONV, tn), lambda i, j: (0, j)),
            pl.BlockSpec((FFN_CONV, tn), lambda i, j: (0, nj + j)),
            pl.BlockSpec((1, FFN_CONV - 1, tn), lambda i, j: (i // tps, 0, j)),
            pl.BlockSpec((1, FFN_CONV - 1, tn), lambda i, j: (i // tps, 0, nj + j)),
        ],
        out_specs=[
            pl.BlockSpec((tm, tn), lambda i, j: (i, j)),
            pl.BlockSpec((1, FFN_CONV - 1, tn), lambda i, j: (i // tps, 0, j)),
            pl.BlockSpec((1, FFN_CONV - 1, tn), lambda i, j: (i // tps, 0, j)),
        ],
        out_shape=[jax.ShapeDtypeStruct((m, f), BF16),
                   jax.ShapeDtypeStruct((bsz, FFN_CONV - 1, f), F32),
                   jax.ShapeDtypeStruct((bsz, FFN_CONV - 1, f), F32)],
        scratch_shapes=[pltpu.VMEM((nj, 16, tn), F32)],
        compiler_params=_cparams(("arbitrary", "arbitrary")),
        name="ffn_up_conv_act",
    )(u, row_ss, w_up, w_up, conv_w, conv_w, conv0, conv0)
    return act, jnp.concatenate([ta, tg], axis=-1)


def _ffn_up_short_kernel(u_ref, ss_ref, wa_ref, wg_ref, cwa_ref, cwg_ref, c0a_ref, c0g_ref, act_ref, ta_ref, tg_ref,
                         *, t, sub):
    tm, tn = act_ref.shape
    pad = jnp.zeros((8 - (FFN_CONV - 1), tn), F32)
    rstd = _row_rstd(ss_ref, u_ref.shape[1])

    def up(r):
        ur = u_ref[r * sub:(r + 1) * sub, :]
        sc = rstd[r * sub:(r + 1) * sub]
        return (jnp.dot(ur, wa_ref[...], preferred_element_type=F32) * sc,
                jnp.dot(ur, wg_ref[...], preferred_element_type=F32) * sc)

    nxt = up(0)
    for r in range(tm // sub):
        za_all, zg_all = nxt
        if r + 1 < tm // sub:
            nxt = up(r + 1)
        for q in range(sub // t):
            s = r * (sub // t) + q
            za = za_all[q * t:(q + 1) * t]
            zg = zg_all[q * t:(q + 1) * t]
            prev_a = jnp.concatenate([pad, c0a_ref[s]], axis=0)
            prev_g = jnp.concatenate([pad, c0g_ref[s]], axis=0)
            fa = (_shift_rows(za, prev_a, 2) * cwa_ref[0:1] + _shift_rows(za, prev_a, 1) * cwa_ref[1:2]
                  + za * cwa_ref[2:3])
            fg = (_shift_rows(zg, prev_g, 2) * cwg_ref[0:1] + _shift_rows(zg, prev_g, 1) * cwg_ref[1:2]
                  + zg * cwg_ref[2:3])
            act_ref[s * t:(s + 1) * t, :] = (_silu(fg) * fa).astype(act_ref.dtype)
            ta_ref[s] = za[t - (FFN_CONV - 1):t]
            tg_ref[s] = zg[t - (FFN_CONV - 1):t]


def _ffn_up_act_short(u, row_ss, w_up, conv0, conv_w, bsz, t):
    m, d = u.shape
    f = w_up.shape[1] // 2
    spt = _tile(bsz, max(1, FFN_ROW_TILE // t), 1)
    tm = spt * t
    sub = t * _tile(spt, max(1, FFN_SHORT_SUB_ROWS // t), 1)
    tn = _tile(f, 512, 128)
    nj = f // tn
    act, ta, tg = pl.pallas_call(
        functools.partial(_ffn_up_short_kernel, t=t, sub=sub),
        grid=(m // tm, nj),
        in_specs=[
            pl.BlockSpec((tm, d), lambda i, j: (i, 0)),
            pl.BlockSpec((tm, 128), lambda i, j: (i, 0)),
            pl.BlockSpec((d, tn), lambda i, j: (0, j)),
            pl.BlockSpec((d, tn), lambda i, j: (0, nj + j)),
            pl.BlockSpec((FFN_CONV, tn), lambda i, j: (0, j)),
            pl.BlockSpec((FFN_CONV, tn), lambda i, j: (0, nj + j)),
            pl.BlockSpec((spt, FFN_CONV - 1, tn), lambda i, j: (i, 0, j)),
            pl.BlockSpec((spt, FFN_CONV - 1, tn), lambda i, j: (i, 0, nj + j)),
        ],
        out_specs=[
            pl.BlockSpec((tm, tn), lambda i, j: (i, j)),
            pl.BlockSpec((spt, FFN_CONV - 1, tn), lambda i, j: (i, 0, j)),
            pl.BlockSpec((spt, FFN_CONV - 1, tn), lambda i, j: (i, 0, j)),
        ],
        out_shape=[jax.ShapeDtypeStruct((m, f), BF16),
                   jax.ShapeDtypeStruct((bsz, FFN_CONV - 1, f), F32),
                   jax.ShapeDtypeStruct((bsz, FFN_CONV - 1, f), F32)],
        compiler_params=_cparams(("parallel", "arbitrary")),
        name="ffn_up_conv_act_short",
    )(u, row_ss, w_up, w_up, conv_w, conv_w, conv0, conv0)
    return act, jnp.concatenate([ta, tg], axis=-1)


def _prep_layer(p, d):
    w_in = p["w_in"]
    o_ml = 0
    o_if = ML_MAIN
    o_rw = o_if + 2 * ML_HEADS
    o_gd = o_rw + RW_COLS
    o_ba = o_gd + GD_MAIN
    o_gate = o_ba + 2 * GD_HEADS
    w_main = jnp.concatenate(
        [w_in[:, o_ml:o_ml + ML_MAIN], w_in[:, o_gd:o_gd + GD_MAIN], w_in[:, o_gate:]], axis=1).astype(BF16)
    n_small = 2 * ML_HEADS + 2 * GD_HEADS
    w_rw = jnp.concatenate(
        [w_in[:, o_rw:o_rw + RW_COLS], w_in[:, o_if:o_if + 2 * ML_HEADS], w_in[:, o_ba:o_ba + 2 * GD_HEADS],
         jnp.zeros((d, RW_SLAB - RW_COLS - n_small), F32)], axis=1).astype(BF16)
    zrow = jnp.zeros((SMALL_W,), F32)
    q = dict(p)
    q.update(
        w_main=w_main, w_rw=w_rw,
        ml_bias_row=zrow.at[SM_ML_I:SM_ML_I + 2 * ML_HEADS].set(p["ml_b_if"]).reshape(1, SMALL_W),
        gd_alog_row=zrow.at[SM_GD_A:SM_GD_A + GD_HEADS].set(p["gd_a_log"]).reshape(1, SMALL_W),
        gd_dtb_row=zrow.at[SM_GD_A:SM_GD_A + GD_HEADS].set(p["gd_dt_bias"]).reshape(1, SMALL_W),
        w_branch_b=p["w_branch"].astype(BF16), w_out_b=p["w_out"].astype(BF16),
        w_ca_q_b=p["w_ca_q"].astype(BF16), w_ca_kv_b=p["w_ca_kv"].astype(BF16),
        w_ca_o_b=p["w_ca_o"].astype(BF16), w_up_b=p["w_up"].astype(BF16), w_down_b=p["w_down"].astype(BF16))
    return q


def _layer(h, mixed, mem_k, mem_v, st, p, next_gain, bsz, t):
    d = h.shape[1]
    if mixed is None:
        u, ss = _rmsnorm(h, p["g_mix"], BF16), None
    else:
        u, ss = mixed
    p_main = _matmul(u, p["w_main"], row_ss=ss)
    p_rw = p_small = _matmul(u, p["w_rw"], row_ss=ss)

    y_ml, ml_c, ml_n, ml_m = _mlstm(p_main, p_small, p["ml_bias_row"], p["ml_norm"],
                                    st["ml_C"], st["ml_n"], st["ml_m"], bsz, t)
    y_rw, rw_s = _rwkv(p_rw, st["rw_shift"], st["rw_S"], p, bsz, t)
    y_gd, gd_s = _gdn(p_main, p_small, p["gd_conv_w"], st["gd_conv"], p["gd_alog_row"], p["gd_dtb_row"],
                      p["gd_norm"], st["gd_S"], bsz, t)
    merged = _merge((y_ml, y_rw, y_gd), p["w_branch_b"], p_main, d)
    h, u, ss = _matmul(merged, p["w_out_b"], residual=h, next_gain=p["g_ca"])

    q = _matmul(u, p["w_ca_q_b"], out_dtype=BF16, row_ss=ss)
    o = _attention(q, mem_k, mem_v, bsz, t)
    h, u, ss = _matmul(o, p["w_ca_o_b"], residual=h, next_gain=p["g_ffn"])

    if t >= FFN_FUSE_MIN_T:
        act, ffn_conv = _ffn_up_act(u, ss, p["w_up_b"], st["ffn_conv"], p["ffn_conv_w"], bsz, t)
    else:
        act, ffn_conv = _ffn_up_act_short(u, ss, p["w_up_b"], st["ffn_conv"], p["ffn_conv_w"], bsz, t)
    if next_gain is None:
        h, mixed_next = _matmul(act, p["w_down_b"], residual=h), None
    else:
        h, u, ss = _matmul(act, p["w_down_b"], residual=h, next_gain=next_gain)
        mixed_next = (u, ss)

    gd0 = OFF_GD
    new_st = dict(
        ml_C=ml_c, ml_n=ml_n, ml_m=ml_m.reshape(bsz, ML_HEADS), rw_S=rw_s,
        rw_shift=p_rw.reshape(bsz, t, -1)[:, t - 1, :RW_COLS],
        gd_S=gd_s,
        gd_conv=p_main.reshape(bsz, t, -1)[:, t - (GD_CONV - 1):, gd0:gd0 + GD_QKV],
        ffn_conv=ffn_conv)
    return h, mixed_next, new_st


def _zero_state(bsz, d_ff2):
    return dict(
        ml_C=jnp.zeros((bsz, ML_HEADS, ML_DQK, ML_DV), F32), ml_n=jnp.zeros((bsz, ML_HEADS, ML_DQK), F32),
        ml_m=jnp.zeros((bsz, ML_HEADS), F32), rw_S=jnp.zeros((bsz, RW_HEADS, RW_N, RW_N), F32),
        rw_shift=jnp.zeros((bsz, RW_COLS), F32), gd_S=jnp.zeros((bsz, GD_HEADS, GD_DK, GD_DV), F32),
        gd_conv=jnp.zeros((bsz, GD_CONV - 1, GD_QKV), F32), ffn_conv=jnp.zeros((bsz, FFN_CONV - 1, d_ff2), F32))


def kernel(x_prompt, x_sample, cache_mem_k, cache_mem_v, state_mlstm_C, state_mlstm_n, state_mlstm_m, state_rwkv_S, state_rwkv_shift, state_gdn_S, state_gdn_conv, state_ffn_conv, mem_prompt, g_mix, w_in, ml_b_if, ml_norm, rw_mu, rw_w0, rw_w2, rw_a0, rw_a2, rw_g2, rw_k_k, rw_k_a, rw_r_k, rw_ln, gd_conv_w, gd_a_log, gd_dt_bias, gd_norm, w_branch, w_out, g_ca, g_mem, w_ca_q, w_ca_kv, w_ca_o, g_ffn, w_up, ffn_conv_w, w_down, g_final):
    bp, tp, d = x_prompt.shape
    bs, ts, _ = x_sample.shape
    depth = w_in.shape[0]
    n_mem = mem_prompt.shape[1]
    assert tp % CHUNK == 0 and ts % CHUNK == 0
    stacked = dict(g_mix=g_mix, w_in=w_in, ml_b_if=ml_b_if, ml_norm=ml_norm, rw_mu=rw_mu, rw_w0=rw_w0,
                   rw_w2=rw_w2, rw_a0=rw_a0, rw_a2=rw_a2, rw_g2=rw_g2, rw_k_k=rw_k_k, rw_k_a=rw_k_a,
                   rw_r_k=rw_r_k, rw_ln=rw_ln, gd_conv_w=gd_conv_w, gd_a_log=gd_a_log, gd_dt_bias=gd_dt_bias,
                   gd_norm=gd_norm, w_branch=w_branch, w_out=w_out, g_ca=g_ca, g_mem=g_mem, w_ca_q=w_ca_q,
                   w_ca_kv=w_ca_kv, w_ca_o=w_ca_o, g_ffn=g_ffn, w_up=w_up, ffn_conv_w=ffn_conv_w,
                   w_down=w_down)
    keys = ("ml_C", "ml_n", "ml_m", "rw_S", "rw_shift", "gd_S", "gd_conv", "ffn_conv")
    new_p = {k: [] for k in keys}
    new_s = {k: [] for k in keys}
    mem_k_list, mem_v_list = [], []
    hp = x_prompt.reshape(bp * tp, d)
    hs = x_sample.reshape(bs * ts, d)
    mem2d = mem_prompt.reshape(bp * n_mem, d)
    mixed_p = mixed_s = None
    for l in range(depth):
        p = _prep_layer({k: v[l] for k, v in stacked.items()}, d)
        next_gain = g_mix[l + 1] if l + 1 < depth else None
        kv = _matmul(_rmsnorm(mem2d, p["g_mem"], BF16), p["w_ca_kv_b"])
        mk = kv[:, :CA_WIDTH].reshape(bp, n_mem, CA_WIDTH)
        mv = kv[:, CA_WIDTH:].reshape(bp, n_mem, CA_WIDTH)
        hp, mixed_p, stp = _layer(hp, mixed_p, mk.astype(BF16), mv.astype(BF16), _zero_state(bp, w_up.shape[2]),
                                  p, next_gain, bp, tp)
        mem_k_list.append(mk.reshape(bp, n_mem, CA_HEADS, CA_HEAD_DIM))
        mem_v_list.append(mv.reshape(bp, n_mem, CA_HEADS, CA_HEAD_DIM))
        st_in = dict(ml_C=state_mlstm_C[l], ml_n=state_mlstm_n[l], ml_m=state_mlstm_m[l],
                     rw_S=state_rwkv_S[l], rw_shift=state_rwkv_shift[l], gd_S=state_gdn_S[l],
                     gd_conv=state_gdn_conv[l], ffn_conv=state_ffn_conv[l])
        ck = cache_mem_k[l].reshape(bs, n_mem, CA_WIDTH).astype(BF16)
        cv = cache_mem_v[l].reshape(bs, n_mem, CA_WIDTH).astype(BF16)
        hs, mixed_s, sts = _layer(hs, mixed_s, ck, cv, st_in, p, next_gain, bs, ts)
        for k in keys:
            new_p[k].append(stp[k])
            new_s[k].append(sts[k])
    y_prompt = _rmsnorm(hp, g_final, F32).reshape(bp, tp, d)
    y_sample = _rmsnorm(hs, g_final, F32).reshape(bs, ts, d)
    outs = [y_prompt, y_sample, jnp.stack(mem_k_list), jnp.stack(mem_v_list)]
    outs += [jnp.stack(new_p[k]) for k in keys]
    outs += [jnp.stack(new_s[k]) for k in keys]
    return tuple(outs)
```

```python
import functools

import jax
import jax.numpy as jnp
from jax import lax
from jax.experimental import pallas as pl
from jax.experimental.pallas import tpu as pltpu

F32 = jnp.float32
BF16 = jnp.bfloat16

EPS = 1e-6
CHUNK = 64
SUB_CHUNKS = 4

ML_HEADS, ML_DQK, ML_DV = 4, 128, 256
ML_QK = ML_HEADS * ML_DQK
ML_WIDTH = ML_HEADS * ML_DV
ML_MAIN = 2 * ML_QK + 2 * ML_WIDTH
ML_PASSES = 3

RW_HEADS, RW_N = 16, 64
RW_WIDTH = RW_HEADS * RW_N
RW_W_RANK, RW_A_RANK, RW_G_RANK = 64, 64, 128
RW_COLS = 3 * RW_WIDTH + RW_W_RANK + RW_A_RANK + RW_G_RANK
RW_GN_EPS = 64e-5
RW_GROUP = 4
RW_PASSES = 1

GD_HEADS, GD_DK, GD_DV = 8, 128, 128
GD_QK = GD_HEADS * GD_DK
GD_WIDTH = GD_HEADS * GD_DV
GD_QKV = 2 * GD_QK + GD_WIDTH
GD_CONV = 4
GD_MAIN = GD_QKV + GD_WIDTH
GD_GROUP = 4
GD_PASSES = 1

BR_WIDTH = 1024
CA_HEADS, CA_HEAD_DIM = 4, 256
CA_WIDTH = CA_HEADS * CA_HEAD_DIM
FFN_CONV = 3
FFN_FUSE_MIN_T = 512
FFN_ROW_TILE = 1024
FFN_SHORT_SUB_ROWS = 256
FFN_SUB_ROWS = 128

SMALL_W = 128
SM_ML_I, SM_ML_F, SM_GD_B, SM_GD_A = 0, ML_HEADS, 2 * ML_HEADS, 2 * ML_HEADS + GD_HEADS
RW_SLAB = 3584
SMALL_BLK = RW_COLS // SMALL_W
OFF_ML = 0
OFF_GD = OFF_ML + ML_MAIN
OFF_GATE = OFF_GD + GD_MAIN
assert RW_COLS % SMALL_W == 0 and RW_SLAB >= RW_COLS + SMALL_W and RW_SLAB % 512 == 0
assert OFF_ML % ML_WIDTH == 0 and OFF_GD % GD_QK == 0

V7X_VMEM_LIMIT = 56 * 1024 * 1024
V7X_MXU_COLS = 256
TRI_BLOCK = 16


def _cparams(sem):
    return pltpu.CompilerParams(dimension_semantics=sem, vmem_limit_bytes=V7X_VMEM_LIMIT)


def _tile(dim, pref, quantum):
    if dim <= pref:
        return dim
    t = (pref // quantum) * quantum
    while t > quantum and dim % t:
        t -= quantum
    assert dim % t == 0, (dim, pref, quantum)
    return t


def _sub_chunks(t):
    n = SUB_CHUNKS
    while (t // CHUNK) % n:
        n -= 1
    return n


def _split2(a):
    hi = a.astype(BF16)
    lo = (a - hi.astype(F32)).astype(BF16)
    return hi, lo


def _dg(a, b, ca, cb):
    return lax.dot_general(a, b, (((ca,), (cb,)), ((), ())), preferred_element_type=F32)


def _dot3(a, b, ca=1, cb=0):
    ah, al = _split2(a)
    bh, bl = _split2(b)
    return _dg(ah, bh, ca, cb) + (_dg(al, bh, ca, cb) + _dg(ah, bl, ca, cb))


def _dotp(a, b, ca, cb, passes):
    if passes == 1:
        return _dg(a.astype(BF16), b.astype(BF16), ca, cb)
    return _dot3(a, b, ca, cb)


def _tri_masks(c):
    row = lax.broadcasted_iota(jnp.int32, (c, c), 0)
    col = lax.broadcasted_iota(jnp.int32, (c, c), 1)
    return row, col


def _cumsum_rows(x):
    c = x.shape[0]
    row, col = _tri_masks(c)
    tri = jnp.where(row >= col, 1.0, 0.0).astype(BF16)
    x0 = x.astype(BF16)
    r1 = x - x0.astype(F32)
    x1 = r1.astype(BF16)
    x2 = (r1 - x1.astype(F32)).astype(BF16)
    return _dg(tri, x0, 1, 0) + (_dg(tri, x1, 1, 0) + _dg(tri, x2, 1, 0))


def _tri_factor_many(lows, c, passes):
    mm = lambda a, b: _dotp(a, b, 1, 0, passes)
    row, col = _tri_masks(lows[0].shape[0])
    same = (row // TRI_BLOCK) == (col // TRI_BLOCK)
    eye = jnp.where(row == col, 1.0, 0.0).astype(F32)
    ps = [jnp.where(same, -low, 0.0) for low in lows]
    offs = [jnp.where(same, 0.0, low) for low in lows]
    xs = [eye + p for p in ps]
    steps = 1
    while steps * 2 < TRI_BLOCK:
        ps = [mm(p, p) for p in ps]
        xs = [x + mm(x, p) for x, p in zip(xs, ps)]
        steps *= 2
    ms = [mm(x, off) for x, off in zip(xs, offs)]
    nblk = c // TRI_BLOCK
    terms = []
    pws = ms
    k = 1
    while k < nblk:
        terms.append(pws)
        k *= 2
        if k < nblk:
            pws = [mm(pw, pw) for pw in pws]
    return xs, terms


def _tri_apply_many(factors, rhss, passes):
    mm = lambda a, b: _dotp(a, b, 1, 0, passes)
    xs, terms = factors
    us = xs if rhss is None else [mm(x, rhs) for x, rhs in zip(xs, rhss)]
    for i in range(len(terms) - 1, 0, -1):
        us = [u + mm(t, u) for t, u in zip(terms[i], us)]
    if terms:
        us = [u - mm(t, u) for t, u in zip(terms[0], us)]
    return us


def _head_sums(x, hw):
    assert 2 * hw == 128
    lane = lax.broadcasted_iota(jnp.int32, (x.shape[0], 128), 1)
    lo = lane < hw
    out = []
    for s in range(x.shape[1] // 128):
        xs = x[:, s * 128:(s + 1) * 128]
        s_lo = jnp.sum(jnp.where(lo, xs, 0.0), axis=-1, keepdims=True)
        s_hi = jnp.sum(jnp.where(lo, 0.0, xs), axis=-1, keepdims=True)
        out.append(jnp.where(lo, s_lo, s_hi))
    return jnp.concatenate(out, axis=1)


def _softplus(x):
    return jnp.maximum(x, 0.0) + jnp.log1p(jnp.exp(-jnp.abs(x)))


def _sigmoid(x):
    return 1.0 / (1.0 + jnp.exp(-x))


def _silu(x):
    return x * _sigmoid(x)


def _shift_rows(x, prev8, k):
    xr = pltpu.roll(x, k, 0)
    pr = pltpu.roll(prev8, k, 0)
    row = lax.broadcasted_iota(jnp.int32, (8, x.shape[1]), 0)
    head = jnp.where(row < k, pr, xr[0:8])
    if x.shape[0] == 8:
        return head
    return jnp.concatenate([head, xr[8:]], axis=0)


def _rmsnorm_kernel(x_ref, g_ref, o_ref):
    x = x_ref[...]
    y = x * lax.rsqrt(jnp.mean(x * x, axis=-1, keepdims=True) + EPS)
    o_ref[...] = (y * g_ref[...]).astype(o_ref.dtype)


def _rmsnorm(x, g, out_dtype):
    m, d = x.shape
    tr = _tile(m, 256, 8)
    return pl.pallas_call(
        _rmsnorm_kernel,
        grid=(m // tr,),
        in_specs=[pl.BlockSpec((tr, d), lambda i: (i, 0)), pl.BlockSpec((1, d), lambda i: (0, 0))],
        out_specs=pl.BlockSpec((tr, d), lambda i: (i, 0)),
        out_shape=jax.ShapeDtypeStruct((m, d), out_dtype),
        compiler_params=_cparams(("parallel",)),
        name="rmsnorm",
    )(x, g.reshape(1, d))


def _mm_kernel(a_ref, w_ref, o_ref):
    o_ref[...] = jnp.dot(a_ref[...], w_ref[...], preferred_element_type=F32).astype(o_ref.dtype)


def _mm_res_kernel(a_ref, w_ref, r_ref, o_ref):
    o_ref[...] = r_ref[...] + jnp.dot(a_ref[...], w_ref[...], preferred_element_type=F32)


def _row_rstd(ss_ref, d):
    return lax.rsqrt(jnp.sum(ss_ref[...], axis=-1, keepdims=True) * (1.0 / d) + EPS)


def _mm_scaled_kernel(a_ref, w_ref, ss_ref, o_ref, *, d):
    acc = jnp.dot(a_ref[...], w_ref[...], preferred_element_type=F32)
    o_ref[...] = (acc * _row_rstd(ss_ref, d)).astype(o_ref.dtype)


def _mm_res_norm_kernel(a_ref, w_ref, r_ref, g_ref, o_ref, hg_ref, ss_ref):
    h = r_ref[...] + jnp.dot(a_ref[...], w_ref[...], preferred_element_type=F32)
    o_ref[...] = h
    hg_ref[...] = (h * g_ref[...]).astype(hg_ref.dtype)
    h2 = h * h
    part = h2[:, 0:128]
    for s in range(1, h.shape[1] // 128):
        part = part + h2[:, s * 128:(s + 1) * 128]

    @pl.when(pl.program_id(1) == 0)
    def _():
        ss_ref[...] = part

    @pl.when(pl.program_id(1) != 0)
    def _():
        ss_ref[...] = ss_ref[...] + part


def _mm_tiles(m, k, n, has_residual):
    tm = _tile(m, 1024 if k <= 4096 else 512, 8)
    tn = _tile(n, 1024 if (k <= 4096 and not has_residual) else 512, V7X_MXU_COLS if n % V7X_MXU_COLS == 0 else 128)
    return tm, tn


def _matmul(a, w, residual=None, out_dtype=F32, row_ss=None, next_gain=None):
    m, k = a.shape
    n = w.shape[1]
    tm, tn = _mm_tiles(m, k, n, residual is not None)
    in_specs = [pl.BlockSpec((tm, k), lambda i, j: (i, 0)), pl.BlockSpec((k, tn), lambda i, j: (0, j))]
    args = [a, w]
    body = _mm_kernel
    out_specs = pl.BlockSpec((tm, tn), lambda i, j: (i, j))
    out_shape = jax.ShapeDtypeStruct((m, n), out_dtype)
    if residual is not None:
        in_specs.append(pl.BlockSpec((tm, tn), lambda i, j: (i, j)))
        args.append(residual)
        body = _mm_res_kernel
        if next_gain is not None:
            in_specs.append(pl.BlockSpec((1, tn), lambda i, j: (0, j)))
            args.append(next_gain.reshape(1, n))
            body = _mm_res_norm_kernel
            out_specs = [out_specs, pl.BlockSpec((tm, tn), lambda i, j: (i, j)),
                         pl.BlockSpec((tm, 128), lambda i, j: (i, 0))]
            out_shape = [out_shape, jax.ShapeDtypeStruct((m, n), BF16), jax.ShapeDtypeStruct((m, 128), F32)]
    elif row_ss is not None:
        in_specs.append(pl.BlockSpec((tm, 128), lambda i, j: (i, 0)))
        args.append(row_ss)
        body = functools.partial(_mm_scaled_kernel, d=k)
    return pl.pallas_call(
        body,
        grid=(m // tm, n // tn),
        in_specs=in_specs,
        out_specs=out_specs,
        out_shape=out_shape,
        compiler_params=_cparams(("parallel", "arbitrary")),
        name="matmul",
    )(*args)


def _mlstm_front(q_all, k_all, v_all, pre):
    n = len(q_all)
    c = q_all[0].shape[0]
    cum = [_cumsum_rows(-_softplus(-p_)) for p_ in pre]
    pre_t = [p_.T for p_ in pre]
    cum_t = [z.T for z in cum]
    row, col = _tri_masks(c)
    causal = row >= col
    it = [(j, h) for j in range(n) for h in range(ML_HEADS)]
    q = [q_all[j][:, h * ML_DQK:(h + 1) * ML_DQK] for j, h in it]
    k = [k_all[j][:, h * ML_DQK:(h + 1) * ML_DQK] * (ML_DQK ** -0.5) for j, h in it]
    v = [v_all[j][:, h * ML_DV:(h + 1) * ML_DV] for j, h in it]
    b_col = [cum[j][:, SM_ML_F + h:SM_ML_F + h + 1] for j, h in it]
    b_row = [cum_t[j][SM_ML_F + h:SM_ML_F + h + 1, :] for j, h in it]
    i_col = [pre[j][:, SM_ML_I + h:SM_ML_I + h + 1] for j, h in it]
    i_row = [pre_t[j][SM_ML_I + h:SM_ML_I + h + 1, :] for j, h in it]
    ix = range(len(it))
    qk = [_dotp(q[x], k[x], 1, 1, ML_PASSES) for x in ix]
    dmat = [jnp.where(causal, b_col[x] - b_row[x] + i_row[x], -jnp.inf) for x in ix]
    m_loc = [jnp.max(dmat[x], axis=-1, keepdims=True) for x in ix]
    pmat = [jnp.exp(dmat[x] - m_loc[x]) * qk[x] for x in ix]
    m_end = [m_loc[x][c - 1:c, :] for x in ix]
    kw = [k[x] * jnp.exp(b_col[x][c - 1:c, :] - b_col[x] + i_col[x] - m_end[x]) for x in ix]
    pv = [_dotp(pmat[x], v[x], 1, 0, ML_PASSES) for x in ix]
    kv = [_dotp(kw[x].T, v[x], 1, 0, ML_PASSES) for x in ix]
    ps = [jnp.sum(pmat[x], axis=-1, keepdims=True) for x in ix]
    ks = [jnp.sum(kw[x], axis=0, keepdims=True) for x in ix]
    per_chunk = lambda z: [z[j * ML_HEADS:(j + 1) * ML_HEADS] for j in range(n)]
    names = ("q", "b_col", "m_loc", "m_end", "pv", "ps", "kv", "ks")
    cols = [per_chunk(z) for z in (q, b_col, m_loc, m_end, pv, ps, kv, ks)]
    return [dict(zip(names, [col_[j] for col_ in cols])) for j in range(n)]


def _mlstm_kernel(q_ref, k_ref, v_ref, og_ref, sm_ref, bias_ref, norm_ref, c0_ref, n0_ref, m0_ref,
                  y_ref, c_ref, n_ref, m_ref):
    @pl.when(pl.program_id(1) == 0)
    def _():
        c_ref[...] = c0_ref[...]
        n_ref[...] = n0_ref[...]
        m_ref[...] = m0_ref[...]

    c = CHUNK
    nsub = q_ref.shape[0] // c
    hs = range(ML_HEADS)
    chunk_rows = [slice(k * c, (k + 1) * c) for k in range(nsub)]
    fronts = _mlstm_front([q_ref[r, :] for r in chunk_rows], [k_ref[r, :] for r in chunk_rows],
                          [v_ref[r, :] for r in chunk_rows], [sm_ref[r, :] + bias_ref[...] for r in chunk_rows])
    m_old = [m_ref[0, :, h:h + 1] for h in hs]
    c_old = [c_ref[0, h] for h in hs]
    n_old = [n_ref[0, h:h + 1, :] for h in hs]
    for k in range(nsub):
        f = fronts[k]
        rows = slice(k * c, (k + 1) * c)
        qc = [_dotp(f["q"][h], c_old[h], 1, 0, ML_PASSES) for h in hs]
        mt = [jnp.maximum(f["b_col"][h] + m_old[h], f["m_loc"][h]) for h in hs]
        e_loc = [jnp.exp(f["m_loc"][h] - mt[h]) for h in hs]
        inter = [jnp.exp(f["b_col"][h] + m_old[h] - mt[h]) for h in hs]
        num = [inter[h] * qc[h] + e_loc[h] * f["pv"][h] for h in hs]
        den = [inter[h] * jnp.sum(f["q"][h] * n_old[h], axis=-1, keepdims=True) + e_loc[h] * f["ps"][h] for h in hs]
        hh = [num[h] / jnp.maximum(jnp.abs(den[h]), jnp.exp(-mt[h])) for h in hs]
        m_new = [mt[h][c - 1:c, :] for h in hs]
        scale = [jnp.exp(f["m_end"][h] - m_new[h]) for h in hs]
        dec = [jnp.exp(f["b_col"][h][c - 1:c, :] + m_old[h] - m_new[h]) for h in hs]
        c_old = [dec[h] * c_old[h] + scale[h] * f["kv"][h] for h in hs]
        n_old = [dec[h] * n_old[h] + scale[h] * f["ks"][h] for h in hs]
        m_old = m_new
        for h in hs:
            hn = hh[h] * lax.rsqrt(jnp.mean(hh[h] * hh[h], axis=-1, keepdims=True) + EPS)
            og = og_ref[rows, h * ML_DV:(h + 1) * ML_DV]
            y = hn * norm_ref[:, h * ML_DV:(h + 1) * ML_DV] * _sigmoid(og)
            y_ref[rows, h * ML_DV:(h + 1) * ML_DV] = y.astype(y_ref.dtype)
    for h in hs:
        c_ref[0, h] = c_old[h]
        n_ref[0, h:h + 1, :] = n_old[h]
        m_ref[0, :, h:h + 1] = m_old[h]


def _mlstm(p_main, p_small, bias_row, ml_norm, c0, n0, m0, bsz, t):
    step = CHUNK * _sub_chunks(t)
    nc = t // step
    rows = bsz * t
    rmap = lambda b, c: b * nc + c
    return pl.pallas_call(
        _mlstm_kernel,
        grid=(bsz, nc),
        in_specs=[
            pl.BlockSpec((step, ML_QK), lambda b, c: (rmap(b, c), OFF_ML // ML_QK)),
            pl.BlockSpec((step, ML_QK), lambda b, c: (rmap(b, c), OFF_ML // ML_QK + 1)),
            pl.BlockSpec((step, ML_WIDTH), lambda b, c: (rmap(b, c), OFF_ML // ML_WIDTH + 1)),
            pl.BlockSpec((step, ML_WIDTH), lambda b, c: (rmap(b, c), OFF_ML // ML_WIDTH + 2)),
            pl.BlockSpec((step, SMALL_W), lambda b, c: (rmap(b, c), SMALL_BLK)),
            pl.BlockSpec((1, SMALL_W), lambda b, c: (0, 0)),
            pl.BlockSpec((1, ML_WIDTH), lambda b, c: (0, 0)),
            pl.BlockSpec((1, ML_HEADS, ML_DQK, ML_DV), lambda b, c: (b, 0, 0, 0)),
            pl.BlockSpec((1, ML_HEADS, ML_DQK), lambda b, c: (b, 0, 0)),
            pl.BlockSpec((1, 1, ML_HEADS), lambda b, c: (b, 0, 0)),
        ],
        out_specs=[
            pl.BlockSpec((step, ML_WIDTH), lambda b, c: (rmap(b, c), 0)),
            pl.BlockSpec((1, ML_HEADS, ML_DQK, ML_DV), lambda b, c: (b, 0, 0, 0)),
            pl.BlockSpec((1, ML_HEADS, ML_DQK), lambda b, c: (b, 0, 0)),
            pl.BlockSpec((1, 1, ML_HEADS), lambda b, c: (b, 0, 0)),
        ],
        out_shape=[
            jax.ShapeDtypeStruct((rows, ML_WIDTH), BF16),
            jax.ShapeDtypeStruct(c0.shape, F32),
            jax.ShapeDtypeStruct(n0.shape, F32),
            jax.ShapeDtypeStruct((bsz, 1, ML_HEADS), F32),
        ],
        compiler_params=_cparams(("parallel", "arbitrary")),
        name="mlstm",
    )(p_main, p_main, p_main, p_main, p_small, bias_row, ml_norm.reshape(1, ML_WIDTH),
      c0, n0, m0.reshape(bsz, 1, ML_HEADS))


def _rwkv_front(x, prev8, mu_ref, w0_ref, w2_ref, a0_ref, a2_ref, g2_ref, kk_ref, ka_ref, rk_ref):
    c = x.shape[0]
    w = RW_WIDTH
    xprev = _shift_rows(x, prev8, 1)
    xm = x + (xprev - x) * mu_ref[...]
    rr = xm[:, 0:w]
    rk = xm[:, w:2 * w]
    rv = xm[:, 2 * w:3 * w]
    xw = xm[:, 3 * w:3 * w + RW_W_RANK]
    xa = xm[:, 3 * w + RW_W_RANK:3 * w + RW_W_RANK + RW_A_RANK]
    xg = xm[:, 3 * w + RW_W_RANK + RW_A_RANK:]

    w_pre = w0_ref[...] + _dot3(jnp.tanh(xw), w2_ref[...])
    lw = -jnp.exp(-_softplus(-w_pre) - 0.5)
    a = _sigmoid(a0_ref[...] + _dot3(xa, a2_ref[...]))
    g_out = _dot3(_sigmoid(xg), g2_ref[...])
    kk_raw = rk * kk_ref[...]
    kmod = rk * (1.0 + (a - 1.0) * ka_ref[...])
    bon = rr * kmod * rk_ref[...]

    lc = _cumsum_rows(lw)
    l_end = lc[c - 1:c, :]
    l_mid = lc[c // 2 - 1:c // 2, :]
    lcc = lc - l_mid
    p_mid = jnp.exp(l_mid)
    p_in = jnp.exp(lcc)
    p_prev = jnp.exp(lcc - lw)
    p_inv = jnp.exp(-lcc)
    p_end = jnp.exp(l_end - lc)
    p_all = jnp.exp(l_end)

    kkn = kk_raw * lax.rsqrt(_head_sums(kk_raw * kk_raw, RW_N) + 1e-6)
    bb = kkn * a
    kkp = kkn * p_prev
    rp = rr * p_in
    kd = kmod * p_inv
    bd = bb * p_inv
    k_end = kmod * p_end
    b_end = bb * p_end
    return dict(kkp=kkp, rp=rp, kd=kd, bd=bd, k_end=k_end, b_end=b_end, rv=rv, p_mid=p_mid, p_all=p_all,
                bonus=_head_sums(bon, RW_N) * rv, g_out=g_out)


def _rw_groups():
    gw = RW_GROUP * RW_N
    return [slice(g * gw, (g + 1) * gw) for g in range(RW_HEADS // RW_GROUP)]


def _rw_masks(c):
    shape = (RW_GROUP * c, RW_GROUP * RW_N)
    head_eq = lax.broadcasted_iota(jnp.int32, shape, 0) // c == lax.broadcasted_iota(jnp.int32, shape, 1) // RW_N
    tn_ = lax.broadcasted_iota(jnp.int32, (c, RW_GROUP * RW_N), 0)
    sn_ = lax.broadcasted_iota(jnp.int32, (c, RW_GROUP * RW_N), 1) % RW_N
    return head_eq, tn_ > sn_, tn_ >= sn_


def _rw_blockdiag(z, head_eq):
    return jnp.where(head_eq, jnp.concatenate([z] * RW_GROUP, axis=0), 0.0)


def _rw_rowsum(z, c):
    out = z[0:c]
    for i in range(1, RW_GROUP):
        out = out + z[i * c:(i + 1) * c]
    return out


def _rwkv_mid(f, masks):
    kkp, rp, kd, bd, rv = f["kkp"], f["rp"], f["kd"], f["bd"], f["rv"]
    c = rv.shape[0]
    head_eq, strict, incl = masks
    groups = _rw_groups()
    nt = lambda x, y: _dotp(x, y, 1, 1, RW_PASSES)
    x2 = [jnp.concatenate([kkp[:, cs], rp[:, cs]], axis=0) for cs in groups]
    ab_b = [nt(x, _rw_blockdiag(bd[:, cs], head_eq)) for x, cs in zip(x2, groups)]
    ab_k = [nt(x, _rw_blockdiag(kd[:, cs], head_eq)) for x, cs in zip(x2, groups)]
    a_b = [_rw_blockdiag(jnp.where(strict, z[0:c], 0.0), head_eq) for z in ab_b]
    return dict(
        x2=x2, v_bd=[_rw_blockdiag(rv[:, cs], head_eq) for cs in groups],
        a_b=a_b,
        a_k=[jnp.where(strict, z[0:c], 0.0) for z in ab_k],
        r_k=[jnp.where(incl, z[c:], 0.0) for z in ab_k],
        r_b=[jnp.where(incl, z[c:], 0.0) for z in ab_b])


def _rwkv_chain(f, g, s_olds, ln_ref, masks):
    k_end, b_end, rv, p_mid, p_all = f["k_end"], f["b_end"], f["rv"], f["p_mid"], f["p_all"]
    c = rv.shape[0]
    groups = _rw_groups()
    head_eq = masks[0]
    nt = lambda x, y: _dotp(x, y, 1, 1, RW_PASSES)
    nn = lambda x, y: _dotp(x, y, 1, 0, RW_PASSES)
    v_bd = g["v_bd"]
    ab_s = [nt(x, _rw_blockdiag(s * p_mid[:, cs], head_eq))
            for x, s, cs in zip(g["x2"], s_olds, groups)]
    rhs = [_rw_blockdiag(z[0:c] + nn(ak, v), head_eq) for z, ak, v in zip(ab_s, g["a_k"], v_bd)]
    us = [nn(t_, r_) for t_, r_ in zip(g["t_inv"], rhs)]
    ygs = [z[c:] + nn(rk_, v) - nn(rb_, u) for z, rk_, v, rb_, u in zip(ab_s, g["r_k"], v_bd, g["r_b"], us)]
    upds = [nn(jnp.concatenate([rv[:, cs], -_rw_rowsum(u, c)], axis=0).T,
               jnp.concatenate([k_end[:, cs], b_end[:, cs]], axis=0)) for cs, u in zip(groups, us)]
    s_news = [s * p_all[:, cs] + _rw_rowsum(jnp.where(head_eq, upd, 0.0), c)
              for cs, s, upd in zip(groups, s_olds, upds)]

    yh = jnp.concatenate(ygs, axis=1)
    mu_ = _head_sums(yh, RW_N) * (1.0 / RW_N)
    yc = yh - mu_
    var = _head_sums(yc * yc, RW_N) * (1.0 / RW_N)
    yn = yc * lax.rsqrt(var + RW_GN_EPS)
    return (yn * ln_ref[...] + f["bonus"]) * f["g_out"], s_news


def _rwkv_kernel(p_ref, shift_ref, mu_ref, w0_ref, w2_ref, a0_ref, a2_ref, g2_ref, kk_ref, ka_ref,
                 rk_ref, ln_ref, s0_ref, y_ref, s_ref, prev_ref):
    @pl.when(pl.program_id(1) == 0)
    def _():
        s_ref[...] = s0_ref[...]
        prev_ref[...] = jnp.broadcast_to(shift_ref[0], prev_ref.shape)

    c = CHUNK
    nsub = p_ref.shape[0] // c
    masks = _rw_masks(c)
    fronts = []
    for k in range(nsub):
        last8 = prev_ref[...] if k == 0 else p_ref[k * c - 8:k * c, :]
        fronts.append(_rwkv_front(p_ref[k * c:(k + 1) * c, :], last8, mu_ref, w0_ref, w2_ref, a0_ref, a2_ref,
                                  g2_ref, kk_ref, ka_ref, rk_ref))
    mids = [_rwkv_mid(f, masks) for f in fronts]
    lows = [low for g in mids for low in g["a_b"]]
    t_inv = _tri_apply_many(_tri_factor_many(lows, c, RW_PASSES), None, RW_PASSES)
    groups = _rw_groups()
    for k, g in enumerate(mids):
        g["t_inv"] = t_inv[k * len(groups):(k + 1) * len(groups)]
    states = [s_ref[0, :, cs] for cs in groups]
    for k in range(nsub):
        out, states = _rwkv_chain(fronts[k], mids[k], states, ln_ref, masks)
        y_ref[k * c:(k + 1) * c, :] = out.astype(y_ref.dtype)
    prev_ref[...] = p_ref[nsub * c - 8:nsub * c, :]
    for cs, s in zip(groups, states):
        s_ref[0, :, cs] = s


def _rwkv(p_rw, shift0, s0, p, bsz, t):
    step = CHUNK * _sub_chunks(t)
    nc = t // step
    rows = bsz * t
    rmap = lambda b, c: b * nc + c
    full = lambda shape: pl.BlockSpec(shape, lambda b, c: (0,) * len(shape))
    w = RW_WIDTH
    s_nat = s0.transpose(0, 2, 1, 3).reshape(bsz, RW_N, w)
    y, s = pl.pallas_call(
        _rwkv_kernel,
        grid=(bsz, nc),
        in_specs=[
            pl.BlockSpec((step, RW_COLS), lambda b, c: (rmap(b, c), 0)),
            pl.BlockSpec((1, 1, RW_COLS), lambda b, c: (b, 0, 0)),
            full((1, RW_COLS)), full((1, w)), full((RW_W_RANK, w)), full((1, w)), full((RW_A_RANK, w)),
            full((RW_G_RANK, w)), full((1, w)), full((1, w)), full((1, w)), full((1, w)),
            pl.BlockSpec((1, RW_N, w), lambda b, c: (b, 0, 0)),
        ],
        out_specs=[
            pl.BlockSpec((step, w), lambda b, c: (rmap(b, c), 0)),
            pl.BlockSpec((1, RW_N, w), lambda b, c: (b, 0, 0)),
        ],
        out_shape=[jax.ShapeDtypeStruct((rows, w), BF16), jax.ShapeDtypeStruct(s_nat.shape, F32)],
        scratch_shapes=[pltpu.VMEM((8, RW_COLS), F32)],
        compiler_params=_cparams(("parallel", "arbitrary")),
        name="rwkv7",
    )(p_rw, shift0.reshape(bsz, 1, RW_COLS), p["rw_mu"].reshape(1, RW_COLS), p["rw_w0"].reshape(1, w),
      p["rw_w2"], p["rw_a0"].reshape(1, w), p["rw_a2"], p["rw_g2"], p["rw_k_k"].reshape(1, w),
      p["rw_k_a"].reshape(1, w), p["rw_r_k"].reshape(1, w), p["rw_ln"].reshape(1, w), s_nat)
    return y, s.reshape(bsz, RW_N, RW_HEADS, RW_N).transpose(0, 2, 1, 3)


def _gdn_front(xs, prevs, sm, cw_ref, alog_ref, dtb_ref):
    c = xs[0].shape[0]
    acts = []
    for sec, (x, prev8) in enumerate(zip(xs, prevs)):
        cs = slice(sec * GD_QK, (sec + 1) * GD_QK)
        y = _shift_rows(x, prev8, GD_CONV - 1) * cw_ref[0:1, cs]
        for i in range(1, GD_CONV - 1):
            y = y + _shift_rows(x, prev8, GD_CONV - 1 - i) * cw_ref[i:i + 1, cs]
        y = y + x * cw_ref[GD_CONV - 1:GD_CONV, cs]
        acts.append(_silu(y))
    qa, ka, va = acts

    beta = _sigmoid(sm)
    g = -jnp.exp(alog_ref[...]) * _softplus(sm + dtb_ref[...])
    gc = _cumsum_rows(g)
    gc_t = gc.T

    gr = GD_GROUP * c
    gk = GD_GROUP * GD_DK
    r4 = lax.broadcasted_iota(jnp.int32, (gr, gr), 0)
    c4 = lax.broadcasted_iota(jnp.int32, (gr, gr), 1)
    t_minus_s = jnp.where((r4 // c) == (c4 // c), (r4 % c) - (c4 % c), -1)
    m_strict = t_minus_s > 0
    m_incl = t_minus_s >= 0
    head_eq = (lax.broadcasted_iota(jnp.int32, (gr, gk), 0) // c
               == lax.broadcasted_iota(jnp.int32, (gr, gk), 1) // GD_DK)

    def rep(z):
        return jnp.concatenate([z] * GD_GROUP, axis=0)

    def stack(parts):
        return jnp.concatenate(parts, axis=0)

    kn = []
    qn = []
    for h in range(GD_HEADS):
        ks = slice(h * GD_DK, (h + 1) * GD_DK)
        q = qa[:, ks]
        qn.append(q * lax.rsqrt(jnp.sum(q * q, axis=-1, keepdims=True) + 1e-6) * (GD_DK ** -0.5))
        k = ka[:, ks]
        kn.append(k * lax.rsqrt(jnp.sum(k * k, axis=-1, keepdims=True) + 1e-6))
    groups = [range(g0, g0 + GD_GROUP) for g0 in range(0, GD_HEADS, GD_GROUP)]
    k_nat = [jnp.concatenate([kn[h] for h in hs], axis=1) for hs in groups]
    q_nat = [jnp.concatenate([qn[h] for h in hs], axis=1) for hs in groups]
    k_bd = [jnp.where(head_eq, rep(kk_), 0.0) for kk_ in k_nat]
    v_st = [stack([va[:, h * GD_DV:(h + 1) * GD_DV] for h in hs]) for hs in groups]
    b_col = [stack([beta[:, SM_GD_B + h:SM_GD_B + h + 1] for h in hs]) for hs in groups]
    g_col = [stack([gc[:, SM_GD_A + h:SM_GD_A + h + 1] for h in hs]) for hs in groups]
    g_row = [jnp.concatenate([gc_t[SM_GD_A + h:SM_GD_A + h + 1, :] for h in hs], axis=1) for hs in groups]

    dec = [jnp.exp(jnp.where(m_incl, gc_ - gr_, -jnp.inf)) for gc_, gr_ in zip(g_col, g_row)]
    kq = [_dotp(stack([kk_, qq_]), kb, 1, 1, GD_PASSES) for kk_, qq_, kb in zip(k_nat, q_nat, k_bd)]
    low = [jnp.where(m_strict, b * rep(z[0:c]) * d, 0.0) for b, z, d in zip(b_col, kq, dec)]
    gl = [gc[c - 1:c, SM_GD_A + h:SM_GD_A + h + 1] for h in range(GD_HEADS)]
    return dict(
        kq_lhs=[stack([kn[h], qn[h]]) for h in range(GD_HEADS)], v_st=v_st, b_col=b_col,
        eg=[jnp.exp(gc_) for gc_ in g_col],
        low=low,
        qkt=[rep(z[c:]) * d for z, d in zip(kq, dec)],
        kw_t=[(kn[h] * jnp.exp(gl[h] - gc[:, SM_GD_A + h:SM_GD_A + h + 1])).T for h in range(GD_HEADS)],
        s_decay=[jnp.exp(gl[h]) for h in range(GD_HEADS)])


def _gdn_chain(f, s_old):
    c = f["kq_lhs"][0].shape[0] // 2
    stack = lambda parts: jnp.concatenate(parts, axis=0)
    groups = [range(g0, g0 + GD_GROUP) for g0 in range(0, GD_HEADS, GD_GROUP)]
    kqs = [_dotp(f["kq_lhs"][h], s_old[h], 1, 0, GD_PASSES) for h in range(GD_HEADS)]
    ks = [stack([kqs[h][0:c] for h in hs]) for hs in groups]
    qs = [stack([kqs[h][c:] for h in hs]) for hs in groups]
    rhs = [b * (v - e * z) for b, v, e, z in zip(f["b_col"], f["v_st"], f["eg"], ks)]
    us = [_dotp(t_, r_, 1, 0, GD_PASSES) for t_, r_ in zip(f["t_inv"], rhs)]
    os_ = [e * z + _dotp(qk, u, 1, 0, GD_PASSES) for e, z, qk, u in zip(f["eg"], qs, f["qkt"], us)]
    outs = []
    s_new = []
    for gi, hs in enumerate(groups):
        for i, h in enumerate(hs):
            rs = slice(i * c, (i + 1) * c)
            s_new.append(f["s_decay"][h] * s_old[h] + _dotp(f["kw_t"][h], us[gi][rs], 1, 0, GD_PASSES))
            outs.append(os_[gi][rs])
    return outs, s_new


def _gdn_kernel(q_ref, k_ref, v_ref, z_ref, sm_ref, cw_ref, conv0_ref, alog_ref, dtb_ref, norm_ref, s0_ref,
                y_ref, s_ref, prev_ref):
    @pl.when(pl.program_id(1) == 0)
    def _():
        s_ref[...] = s0_ref[...]
        prev_ref[...] = jnp.zeros(prev_ref.shape, F32)
        prev_ref[8 - (GD_CONV - 1):8, :] = conv0_ref[0]

    c = CHUNK
    nsub = q_ref.shape[0] // c
    refs = (q_ref, k_ref, v_ref)
    fronts = []
    for k in range(nsub):
        xs = [r[k * c:(k + 1) * c, :] for r in refs]
        if k == 0:
            prevs = [prev_ref[:, sec * GD_QK:(sec + 1) * GD_QK] for sec in range(3)]
        else:
            prevs = [r[k * c - 8:k * c, :] for r in refs]
        fronts.append(_gdn_front(xs, prevs, sm_ref[k * c:(k + 1) * c, :], cw_ref, alog_ref, dtb_ref))
    lows = [low for f in fronts for low in f["low"]]
    t_inv = _tri_apply_many(_tri_factor_many(lows, c, GD_PASSES), None, GD_PASSES)
    per = len(fronts[0]["low"])
    for k, f in enumerate(fronts):
        f["t_inv"] = t_inv[k * per:(k + 1) * per]
    states = [s_ref[0, h] for h in range(GD_HEADS)]
    for k in range(nsub):
        outs, states = _gdn_chain(fronts[k], states)
        for h in range(GD_HEADS):
            o_h = outs[h]
            og = o_h * lax.rsqrt(jnp.mean(o_h * o_h, axis=-1, keepdims=True) + EPS) * norm_ref[...]
            vs = slice(h * GD_DV, (h + 1) * GD_DV)
            y_ref[k * c:(k + 1) * c, vs] = (og * _silu(z_ref[k * c:(k + 1) * c, vs])).astype(y_ref.dtype)
    for h in range(GD_HEADS):
        s_ref[0, h] = states[h]
    for sec, r in enumerate(refs):
        prev_ref[:, sec * GD_QK:(sec + 1) * GD_QK] = r[nsub * c - 8:nsub * c, :]


def _gdn(p_main, p_small, conv_w, conv0, alog_row, dtb_row, gd_norm, s0, bsz, t):
    step = CHUNK * _sub_chunks(t)
    nc = t // step
    rows = bsz * t
    rmap = lambda b, c: b * nc + c
    base = OFF_GD // GD_QK
    full = lambda shape: pl.BlockSpec(shape, lambda b, c: (0,) * len(shape))
    return pl.pallas_call(
        _gdn_kernel,
        grid=(bsz, nc),
        in_specs=[
            pl.BlockSpec((step, GD_QK), lambda b, c: (rmap(b, c), base)),
            pl.BlockSpec((step, GD_QK), lambda b, c: (rmap(b, c), base + 1)),
            pl.BlockSpec((step, GD_WIDTH), lambda b, c: (rmap(b, c), base + 2)),
            pl.BlockSpec((step, GD_WIDTH), lambda b, c: (rmap(b, c), base + 3)),
            pl.BlockSpec((step, SMALL_W), lambda b, c: (rmap(b, c), SMALL_BLK)),
            full((GD_CONV, GD_QKV)),
            pl.BlockSpec((1, GD_CONV - 1, GD_QKV), lambda b, c: (b, 0, 0)),
            full((1, SMALL_W)), full((1, SMALL_W)), full((1, GD_DV)),
            pl.BlockSpec((1, GD_HEADS, GD_DK, GD_DV), lambda b, c: (b, 0, 0, 0)),
        ],
        out_specs=[
            pl.BlockSpec((step, GD_WIDTH), lambda b, c: (rmap(b, c), 0)),
            pl.BlockSpec((1, GD_HEADS, GD_DK, GD_DV), lambda b, c: (b, 0, 0, 0)),
        ],
        out_shape=[jax.ShapeDtypeStruct((rows, GD_WIDTH), BF16), jax.ShapeDtypeStruct(s0.shape, F32)],
        scratch_shapes=[pltpu.VMEM((8, GD_QKV), F32)],
        compiler_params=_cparams(("parallel", "arbitrary")),
        name="gdn",
    )(p_main, p_main, p_main, p_main, p_small, conv_w, conv0, alog_row, dtb_row, gd_norm.reshape(1, GD_DV), s0)


def _merge_kernel(y0_ref, y1_ref, y2_ref, w_ref, g0_ref, g1_ref, g2_ref, o_ref):
    acc = _sigmoid(g0_ref[...]) * jnp.dot(y0_ref[...], w_ref[0], preferred_element_type=F32)
    acc = acc + _sigmoid(g1_ref[...]) * jnp.dot(y1_ref[...], w_ref[1], preferred_element_type=F32)
    acc = acc + _sigmoid(g2_ref[...]) * jnp.dot(y2_ref[...], w_ref[2], preferred_element_type=F32)
    o_ref[...] = acc.astype(o_ref.dtype)


def _merge(ys, w_branch, p_main, d):
    m = ys[0].shape[0]
    tm = _tile(m, 1024, 8)
    tn = _tile(d, 512, 128)
    gate0 = OFF_GATE
    assert gate0 % tn == 0
    gb = gate0 // tn
    nb = d // tn
    yspec = pl.BlockSpec((tm, BR_WIDTH), lambda i, j: (i, 0))
    gspec = lambda b: pl.BlockSpec((tm, tn), lambda i, j: (i, gb + b * nb + j))
    return pl.pallas_call(
        _merge_kernel,
        grid=(m // tm, nb),
        in_specs=[yspec, yspec, yspec, pl.BlockSpec((3, BR_WIDTH, tn), lambda i, j: (0, 0, j)),
                  gspec(0), gspec(1), gspec(2)],
        out_specs=pl.BlockSpec((tm, tn), lambda i, j: (i, j)),
        out_shape=jax.ShapeDtypeStruct((m, d), BF16),
        compiler_params=_cparams(("parallel", "arbitrary")),
        name="merge",
    )(ys[0], ys[1], ys[2], w_branch, p_main, p_main, p_main)


def _attn_kernel(q_ref, k_ref, v_ref, o_ref):
    for h in range(CA_HEADS):
        hs = slice(h * CA_HEAD_DIM, (h + 1) * CA_HEAD_DIM)
        s = _dg(q_ref[:, hs], k_ref[0, :, hs], 1, 1) * (CA_HEAD_DIM ** -0.5)
        s = s - jnp.max(s, axis=-1, keepdims=True)
        e = jnp.exp(s)
        pr = e / jnp.sum(e, axis=-1, keepdims=True)
        o = jnp.dot(pr.astype(BF16), v_ref[0, :, hs], preferred_element_type=F32)
        o_ref[:, hs] = o.astype(o_ref.dtype)


def _attention(q, mem_k, mem_v, bsz, t):
    tq = _tile(t, 512, 8)
    nt = t // tq
    n_mem = mem_k.shape[1]
    return pl.pallas_call(
        _attn_kernel,
        grid=(bsz, nt),
        in_specs=[
            pl.BlockSpec((tq, CA_WIDTH), lambda b, i: (b * nt + i, 0)),
            pl.BlockSpec((1, n_mem, CA_WIDTH), lambda b, i: (b, 0, 0)),
            pl.BlockSpec((1, n_mem, CA_WIDTH), lambda b, i: (b, 0, 0)),
        ],
        out_specs=pl.BlockSpec((tq, CA_WIDTH), lambda b, i: (b * nt + i, 0)),
        out_shape=jax.ShapeDtypeStruct((bsz * t, CA_WIDTH), BF16),
        compiler_params=_cparams(("parallel", "arbitrary")),
        name="mem_attention",
    )(q, mem_k, mem_v)


def _ffn_up_kernel(u_ref, ss_ref, wa_ref, wg_ref, cwa_ref, cwg_ref, c0a_ref, c0g_ref, act_ref, ta_ref, tg_ref,
                   carry_ref, *, tiles_per_seq, sub):
    i = pl.program_id(0)
    j = pl.program_id(1)
    tm, tn = act_ref.shape

    @pl.when(i == 0)
    def _():
        carry_ref[j] = jnp.zeros(carry_ref.shape[1:], F32)

    first = (i % tiles_per_seq) == 0
    pad = jnp.zeros((8 - (FFN_CONV - 1), tn), F32)
    carried = carry_ref[j]
    prev_a = jnp.where(first, jnp.concatenate([pad, c0a_ref[0]], axis=0), carried[0:8])
    prev_g = jnp.where(first, jnp.concatenate([pad, c0g_ref[0]], axis=0), carried[8:16])
    rstd = _row_rstd(ss_ref, u_ref.shape[1])

    def up(r):
        ur = u_ref[r * sub:(r + 1) * sub, :]
        sc = rstd[r * sub:(r + 1) * sub]
        return (jnp.dot(ur, wa_ref[...], preferred_element_type=F32) * sc,
                jnp.dot(ur, wg_ref[...], preferred_element_type=F32) * sc)

    nxt = up(0)
    for r in range(tm // sub):
        rows = slice(r * sub, (r + 1) * sub)
        za, zg = nxt
        if r + 1 < tm // sub:
            nxt = up(r + 1)
        fa =(_shift_rows(za, prev_a, 2) * cwa_ref[0:1] + _shift_rows(za, prev_a, 1) * cwa_ref[1:2]
              + za * cwa_ref[2:3])
        fg = (_shift_rows(zg, prev_g, 2) * cwg_ref[0:1] + _shift_rows(zg, prev_g, 1) * cwg_ref[1:2]
              + zg * cwg_ref[2:3])
        act_ref[rows, :] = (_silu(fg) * fa).astype(act_ref.dtype)
        prev_a = za[sub - 8:sub]
        prev_g = zg[sub - 8:sub]
    carry_ref[j] = jnp.concatenate([prev_a, prev_g], axis=0)
    ta_ref[0] = prev_a[8 - (FFN_CONV - 1):8]
    tg_ref[0] = prev_g[8 - (FFN_CONV - 1):8]


def _ffn_up_act(u, row_ss, w_up, conv0, conv_w, bsz, t):
    m, d = u.shape
    f = w_up.shape[1] // 2
    tm = _tile(t, FFN_ROW_TILE, 8)
    sub = _tile(tm, FFN_SUB_ROWS, 8)
    tn = _tile(f, 512, 128)
    nj = f // tn
    tps = t // tm
    act, ta, tg = pl.pallas_call(
        functools.partial(_ffn_up_kernel, tiles_per_seq=tps, sub=sub),
        grid=(m // tm, nj),
        in_specs=[
            pl.BlockSpec((tm, d), lambda i, j: (i, 0)),
            pl.BlockSpec((tm, 128), lambda i, j: (i, 0)),
            pl.BlockSpec((d, tn), lambda i, j: (0, j)),
            pl.BlockSpec((d, tn), lambda i, j: (0, nj + j)),
            pl.BlockSpec((FFN_CONV, tn), lambda i, j: (0, j)),
            pl.BlockSpec((FFN_CONV, tn), lambda i, j: (0, nj + j)),
            pl.BlockSpec((1, FFN_CONV - 1, tn), lambda i, j: (i // tps, 0, j)),
            pl.BlockSpec((1, FFN_CONV - 1, tn), lambda i, j: (i // tps, 0, nj + j)),
        ],
        out_specs=[
            pl.BlockSpec((tm, tn), lambda i, j: (i, j)),
            pl.BlockSpec((1, FFN_CONV - 1, tn), lambda i, j: (i // tps, 0, j)),
            pl.BlockSpec((1, FFN_CONV - 1, tn), lambda i, j: (i // tps, 0, j)),
        ],
        out_shape=[jax.ShapeDtypeStruct((m, f), BF16),
                   jax.ShapeDtypeStruct((bsz, FFN_CONV - 1, f), F32),
                   jax.ShapeDtypeStruct((bsz, FFN_CONV - 1, f), F32)],
        scratch_shapes=[pltpu.VMEM((nj, 16, tn), F32)],
        compiler_params=_cparams(("arbitrary", "arbitrary")),
        name="ffn_up_conv_act",
    )(u, row_ss, w_up, w_up, conv_w, conv_w, conv0, conv0)
    return act, jnp.concatenate([ta, tg], axis=-1)


def _ffn_up_short_kernel(u_ref, ss_ref, wa_ref, wg_ref, cwa_ref, cwg_ref, c0a_ref, c0g_ref, act_ref, ta_ref, tg_ref,
                         *, t, sub):
    tm, tn = act_ref.shape
    pad = jnp.zeros((8 - (FFN_CONV - 1), tn), F32)
    rstd = _row_rstd(ss_ref, u_ref.shape[1])

    def up(r):
        ur = u_ref[r * sub:(r + 1) * sub, :]
        sc = rstd[r * sub:(r + 1) * sub]
        return (jnp.dot(ur, wa_ref[...], preferred_element_type=F32) * sc,
                jnp.dot(ur, wg_ref[...], preferred_element_type=F32) * sc)

    nxt = up(0)
    for r in range(tm // sub):
        za_all, zg_all = nxt
        if r + 1 < tm // sub:
            nxt = up(r + 1)
        for q in range(sub // t):
            s = r * (sub // t) + q
            za = za_all[q * t:(q + 1) * t]
            zg = zg_all[q * t:(q + 1) * t]
            prev_a = jnp.concatenate([pad, c0a_ref[s]], axis=0)
            prev_g = jnp.concatenate([pad, c0g_ref[s]], axis=0)
            fa = (_shift_rows(za, prev_a, 2) * cwa_ref[0:1] + _shift_rows(za, prev_a, 1) * cwa_ref[1:2]
                  + za * cwa_ref[2:3])
            fg = (_shift_rows(zg, prev_g, 2) * cwg_ref[0:1] + _shift_rows(zg, prev_g, 1) * cwg_ref[1:2]
                  + zg * cwg_ref[2:3])
            act_ref[s * t:(s + 1) * t, :] = (_silu(fg) * fa).astype(act_ref.dtype)
            ta_ref[s] = za[t - (FFN_CONV - 1):t]
            tg_ref[s] = zg[t - (FFN_CONV - 1):t]


def _ffn_up_act_short(u, row_ss, w_up, conv0, conv_w, bsz, t):
    m, d = u.shape
    f = w_up.shape[1] // 2
    spt = _tile(bsz, max(1, FFN_ROW_TILE // t), 1)
    tm = spt * t
    sub = t * _tile(spt, max(1, FFN_SHORT_SUB_ROWS // t), 1)
    tn = _tile(f, 512, 128)
    nj = f // tn
    act, ta, tg = pl.pallas_call(
        functools.partial(_ffn_up_short_kernel, t=t, sub=sub),
        grid=(m // tm, nj),
        in_specs=[
            pl.BlockSpec((tm, d), lambda i, j: (i, 0)),
            pl.BlockSpec((tm, 128), lambda i, j: (i, 0)),
            pl.BlockSpec((d, tn), lambda i, j: (0, j)),
            pl.BlockSpec((d, tn), lambda i, j: (0, nj + j)),
            pl.BlockSpec((FFN_CONV, tn), lambda i, j: (0, j)),
            pl.BlockSpec((FFN_CONV, tn), lambda i, j: (0, nj + j)),
            pl.BlockSpec((spt, FFN_CONV - 1, tn), lambda i, j: (i, 0, j)),
            pl.BlockSpec((spt, FFN_CONV - 1, tn), lambda i, j: (i, 0, nj + j)),
        ],
        out_specs=[
            pl.BlockSpec((tm, tn), lambda i, j: (i, j)),
            pl.BlockSpec((spt, FFN_CONV - 1, tn), lambda i, j: (i, 0, j)),
            pl.BlockSpec((spt, FFN_CONV - 1, tn), lambda i, j: (i, 0, j)),
        ],
        out_shape=[jax.ShapeDtypeStruct((m, f), BF16),
                   jax.ShapeDtypeStruct((bsz, FFN_CONV - 1, f), F32),
                   jax.ShapeDtypeStruct((bsz, FFN_CONV - 1, f), F32)],
        compiler_params=_cparams(("parallel", "arbitrary")),
        name="ffn_up_conv_act_short",
    )(u, row_ss, w_up, w_up, conv_w, conv_w, conv0, conv0)
    return act, jnp.concatenate([ta, tg], axis=-1)


def _prep_layer(p, d):
    w_in = p["w_in"]
    o_ml = 0
    o_if = ML_MAIN
    o_rw = o_if + 2 * ML_HEADS
    o_gd = o_rw + RW_COLS
    o_ba = o_gd + GD_MAIN
    o_gate = o_ba + 2 * GD_HEADS
    w_main = jnp.concatenate(
        [w_in[:, o_ml:o_ml + ML_MAIN], w_in[:, o_gd:o_gd + GD_MAIN], w_in[:, o_gate:]], axis=1).astype(BF16)
    n_small = 2 * ML_HEADS + 2 * GD_HEADS
    w_rw = jnp.concatenate(
        [w_in[:, o_rw:o_rw + RW_COLS], w_in[:, o_if:o_if + 2 * ML_HEADS], w_in[:, o_ba:o_ba + 2 * GD_HEADS],
         jnp.zeros((d, RW_SLAB - RW_COLS - n_small), F32)], axis=1).astype(BF16)
    zrow = jnp.zeros((SMALL_W,), F32)
    q = dict(p)
    q.update(
        w_main=w_main, w_rw=w_rw,
        ml_bias_row=zrow.at[SM_ML_I:SM_ML_I + 2 * ML_HEADS].set(p["ml_b_if"]).reshape(1, SMALL_W),
        gd_alog_row=zrow.at[SM_GD_A:SM_GD_A + GD_HEADS].set(p["gd_a_log"]).reshape(1, SMALL_W),
        gd_dtb_row=zrow.at[SM_GD_A:SM_GD_A + GD_HEADS].set(p["gd_dt_bias"]).reshape(1, SMALL_W),
        w_branch_b=p["w_branch"].astype(BF16), w_out_b=p["w_out"].astype(BF16),
        w_ca_q_b=p["w_ca_q"].astype(BF16), w_ca_kv_b=p["w_ca_kv"].astype(BF16),
        w_ca_o_b=p["w_ca_o"].astype(BF16), w_up_b=p["w_up"].astype(BF16), w_down_b=p["w_down"].astype(BF16))
    return q


def _layer(h, mixed, mem_k, mem_v, st, p, next_gain, bsz, t):
    d = h.shape[1]
    if mixed is None:
        u, ss = _rmsnorm(h, p["g_mix"], BF16), None
    else:
        u, ss = mixed
    p_main = _matmul(u, p["w_main"], row_ss=ss)
    p_rw = p_small = _matmul(u, p["w_rw"], row_ss=ss)

    y_ml, ml_c, ml_n, ml_m = _mlstm(p_main, p_small, p["ml_bias_row"], p["ml_norm"],
                                    st["ml_C"], st["ml_n"], st["ml_m"], bsz, t)
    y_rw, rw_s = _rwkv(p_rw, st["rw_shift"], st["rw_S"], p, bsz, t)
    y_gd, gd_s = _gdn(p_main, p_small, p["gd_conv_w"], st["gd_conv"], p["gd_alog_row"], p["gd_dtb_row"],
                      p["gd_norm"], st["gd_S"], bsz, t)
    merged = _merge((y_ml, y_rw, y_gd), p["w_branch_b"], p_main, d)
    h, u, ss = _matmul(merged, p["w_out_b"], residual=h, next_gain=p["g_ca"])

    q = _matmul(u, p["w_ca_q_b"], out_dtype=BF16, row_ss=ss)
    o = _attention(q, mem_k, mem_v, bsz, t)
    h, u, ss = _matmul(o, p["w_ca_o_b"], residual=h, next_gain=p["g_ffn"])

    if t >= FFN_FUSE_MIN_T:
        act, ffn_conv = _ffn_up_act(u, ss, p["w_up_b"], st["ffn_conv"], p["ffn_conv_w"], bsz, t)
    else:
        act, ffn_conv = _ffn_up_act_short(u, ss, p["w_up_b"], st["ffn_conv"], p["ffn_conv_w"], bsz, t)
    if next_gain is None:
        h, mixed_next = _matmul(act, p["w_down_b"], residual=h), None
    else:
        h, u, ss = _matmul(act, p["w_down_b"], residual=h, next_gain=next_gain)
        mixed_next = (u, ss)

    gd0 = OFF_GD
    new_st = dict(
        ml_C=ml_c, ml_n=ml_n, ml_m=ml_m.reshape(bsz, ML_HEADS), rw_S=rw_s,
        rw_shift=p_rw.reshape(bsz, t, -1)[:, t - 1, :RW_COLS],
        gd_S=gd_s,
        gd_conv=p_main.reshape(bsz, t, -1)[:, t - (GD_CONV - 1):, gd0:gd0 + GD_QKV],
        ffn_conv=ffn_conv)
    return h, mixed_next, new_st


def _zero_state(bsz, d_ff2):
    return dict(
        ml_C=jnp.zeros((bsz, ML_HEADS, ML_DQK, ML_DV), F32), ml_n=jnp.zeros((bsz, ML_HEADS, ML_DQK), F32),
        ml_m=jnp.zeros((bsz, ML_HEADS), F32), rw_S=jnp.zeros((bsz, RW_HEADS, RW_N, RW_N), F32),
        rw_shift=jnp.zeros((bsz, RW_COLS), F32), gd_S=jnp.zeros((bsz, GD_HEADS, GD_DK, GD_DV), F32),
        gd_conv=jnp.zeros((bsz, GD_CONV - 1, GD_QKV), F32), ffn_conv=jnp.zeros((bsz, FFN_CONV - 1, d_ff2), F32))


def kernel(x_prompt, x_sample, cache_mem_k, cache_mem_v, state_mlstm_C, state_mlstm_n, state_mlstm_m, state_rwkv_S, state_rwkv_shift, state_gdn_S, state_gdn_conv, state_ffn_conv, mem_prompt, g_mix, w_in, ml_b_if, ml_norm, rw_mu, rw_w0, rw_w2, rw_a0, rw_a2, rw_g2, rw_k_k, rw_k_a, rw_r_k, rw_ln, gd_conv_w, gd_a_log, gd_dt_bias, gd_norm, w_branch, w_out, g_ca, g_mem, w_ca_q, w_ca_kv, w_ca_o, g_ffn, w_up, ffn_conv_w, w_down, g_final):
    bp, tp, d = x_prompt.shape
    bs, ts, _ = x_sample.shape
    depth = w_in.shape[0]
    n_mem = mem_prompt.shape[1]
    assert tp % CHUNK == 0 and ts % CHUNK == 0
    stacked = dict(g_mix=g_mix, w_in=w_in, ml_b_if=ml_b_if, ml_norm=ml_norm, rw_mu=rw_mu, rw_w0=rw_w0,
                   rw_w2=rw_w2, rw_a0=rw_a0, rw_a2=rw_a2, rw_g2=rw_g2, rw_k_k=rw_k_k, rw_k_a=rw_k_a,
                   rw_r_k=rw_r_k, rw_ln=rw_ln, gd_conv_w=gd_conv_w, gd_a_log=gd_a_log, gd_dt_bias=gd_dt_bias,
                   gd_norm=gd_norm, w_branch=w_branch, w_out=w_out, g_ca=g_ca, g_mem=g_mem, w_ca_q=w_ca_q,
                   w_ca_kv=w_ca_kv, w_ca_o=w_ca_o, g_ffn=g_ffn, w_up=w_up, ffn_conv_w=ffn_conv_w,
                   w_down=w_down)
    keys = ("ml_C", "ml_n", "ml_m", "rw_S", "rw_shift", "gd_S", "gd_conv", "ffn_conv")
    new_p = {k: [] for k in keys}
    new_s = {k: [] for k in keys}
    mem_k_list, mem_v_list = [], []
    hp = x_prompt.reshape(bp * tp, d)
    hs = x_sample.reshape(bs * ts, d)
    mem2d = mem_prompt.reshape(bp * n_mem, d)
    mixed_p = mixed_s = None
    for l in range(depth):
        p = _prep_layer({k: v[l] for k, v in stacked.items()}, d)
        next_gain = g_mix[l + 1] if l + 1 < depth else None
        kv = _matmul(_rmsnorm(mem2d, p["g_mem"], BF16), p["w_ca_kv_b"])
        mk = kv[:, :CA_WIDTH].reshape(bp, n_mem, CA_WIDTH)
        mv = kv[:, CA_WIDTH:].reshape(bp, n_mem, CA_WIDTH)
        hp, mixed_p, stp = _layer(hp, mixed_p, mk.astype(BF16), mv.astype(BF16), _zero_state(bp, w_up.shape[2]),
                                  p, next_gain, bp, tp)
        mem_k_list.append(mk.reshape(bp, n_mem, CA_HEADS, CA_HEAD_DIM))
        mem_v_list.append(mv.reshape(bp, n_mem, CA_HEADS, CA_HEAD_DIM))
        st_in = dict(ml_C=state_mlstm_C[l], ml_n=state_mlstm_n[l], ml_m=state_mlstm_m[l],
                     rw_S=state_rwkv_S[l], rw_shift=state_rwkv_shift[l], gd_S=state_gdn_S[l],
                     gd_conv=state_gdn_conv[l], ffn_conv=state_ffn_conv[l])
        ck = cache_mem_k[l].reshape(bs, n_mem, CA_WIDTH).astype(BF16)
        cv = cache_mem_v[l].reshape(bs, n_mem, CA_WIDTH).astype(BF16)
        hs, mixed_s, sts = _layer(hs, mixed_s, ck, cv, st_in, p, next_gain, bs, ts)
        for k in keys:
            new_p[k].append(stp[k])
            new_s[k].append(sts[k])
    y_prompt = _rmsnorm(hp, g_final, F32).reshape(bp, tp, d)
    y_sample = _rmsnorm(hs, g_final, F32).reshape(bs, ts, d)
    outs = [y_prompt, y_sample, jnp.stack(mem_k_list), jnp.stack(mem_v_list)]
    outs += [jnp.stack(new_p[k]) for k in keys]
    outs += [jnp.stack(new_s[k]) for k in keys]
    return tuple(outs)
```
